```python
import math
import jax, jax.numpy as jnp
from jax import lax
import numpy as np

D_MODEL = 2048
BATCH = 2
SEQ = 8192
DEPTH = 2

MEM_LEN = 256
NORM_EPS = 1e-6
Q_BLOCK = 128

A_WIDTH = D_MODEL // 2
A_HEAD_DIM = 64
A_HEADS = A_WIDTH // A_HEAD_DIM
A_DECAY_LORA = max(32, int(round(1.8 * A_WIDTH ** 0.5 / 32)) * 32)
A_AAA_LORA = max(32, int(round(1.8 * A_WIDTH ** 0.5 / 32)) * 32)
A_GATE_LORA = max(32, int(round(0.6 * A_WIDTH ** 0.8 / 32)) * 32)
A_LN_EPS = 1e-5 * (A_HEAD_DIM / 8) ** 2
A_COLS = 3 * A_WIDTH + A_DECAY_LORA + A_AAA_LORA + A_GATE_LORA

B_WIDTH = D_MODEL // 2
B_HEAD_DIM = 64
B_HEADS = B_WIDTH // B_HEAD_DIM
B_GROUPS = 4
B_STATE = 128
B_CONV = 4
B_CHUNK = 128
B_CONV_CH = B_WIDTH + 2 * B_GROUPS * B_STATE
B_COLS = B_WIDTH + B_CONV_CH + B_HEADS
AB_COLS = A_COLS + B_COLS

C_V_DIM = 128
C_HEADS = (D_MODEL // 2) // C_V_DIM
C_NOPE = 128
C_ROPE = 64
C_QK = C_NOPE + C_ROPE
C_Q_LORA = D_MODEL // 4
C_KV_LORA = 512
ROPE_THETA = 10000.0
C_COLS = C_Q_LORA + C_KV_LORA + C_ROPE

D_HEAD_DIM = 128
D_HEADS = (D_MODEL // 2) // D_HEAD_DIM
D_KV_HEADS = D_HEADS // 4
D_IDX_HEADS = D_HEADS
D_IDX_DIM = 64
D_TOPK_MAX = 256
D_COLS = (D_HEADS * D_HEAD_DIM + 2 * D_KV_HEADS * D_HEAD_DIM
          + D_IDX_HEADS * D_IDX_DIM + D_IDX_DIM + D_IDX_HEADS)
CD_COLS = C_COLS + D_COLS
CD_OUT = C_HEADS * C_V_DIM + D_HEADS * D_HEAD_DIM

REL_BUCKETS = 32
REL_MAX_DIST = 128

X_HEADS = 4
X_HEAD_DIM = 128
X_WIDTH = X_HEADS * X_HEAD_DIM

FF_DENSE = 256 * ((8 * D_MODEL // 3 + 255) // 256)
N_EXPERTS = 8
TOP_K = 2
FF_EXPERT = FF_DENSE // TOP_K

kernel_name = 'hybrid_rwkv7_ssd_mla_dsa_moe_block'


def rms_norm(x, g):
    xf = x.astype(jnp.float32)
    y = xf * lax.rsqrt(jnp.mean(xf * xf, axis=-1, keepdims=True) + NORM_EPS)
    return (y * g.astype(jnp.float32)).astype(x.dtype)


def split_cols(u, sizes):
    cuts = [int(c) for c in np.cumsum(sizes)[:-1]]
    return jnp.split(u, cuts, axis=-1)


def token_shift(u):
    return jnp.concatenate([jnp.zeros_like(u[:, :1]), u[:, :-1]], axis=1)


def rope(t, pos):
    half = t.shape[-1] // 2
    freqs = ROPE_THETA ** (-jnp.arange(half, dtype=jnp.float32) / half)
    ang = pos.astype(jnp.float32)[:, None] * freqs[None, :]
    cos = jnp.cos(ang)[None, :, None, :]
    sin = jnp.sin(ang)[None, :, None, :]
    tf = t.astype(jnp.float32)
    t1, t2 = tf[..., :half], tf[..., half:]
    return jnp.concatenate([t1 * cos - t2 * sin, t1 * sin + t2 * cos], axis=-1).astype(t.dtype)


def rel_bucket(dist):
    n = jnp.maximum(dist, 0)
    max_exact = REL_BUCKETS // 2
    nf = jnp.maximum(n, 1).astype(jnp.float32)
    large = max_exact + (jnp.log(nf / max_exact) / math.log(REL_MAX_DIST / max_exact)
                         * (REL_BUCKETS - max_exact)).astype(jnp.int32)
    large = jnp.minimum(large, REL_BUCKETS - 1)
    return jnp.where(n < max_exact, n, large)


def rwkv7_scan(r, w, k, v, a_vec, b_vec):
    bsz, _, nh, n = r.shape

    def step(state, inp):
        r_t, w_t, k_t, v_t, a_t, b_t = inp
        sa = jnp.einsum('bhij,bhj->bhi', state, a_t)
        state = (state * w_t[:, :, None, :] + sa[..., None] * b_t[:, :, None, :]
                 + v_t[..., None] * k_t[:, :, None, :])
        return state, jnp.einsum('bhij,bhj->bhi', state, r_t)

    xs = tuple(jnp.moveaxis(t, 1, 0) for t in (r, w, k, v, a_vec, b_vec))
    s0 = jnp.zeros((bsz, nh, n, n), jnp.float32)
    _, y = lax.scan(step, s0, xs)
    return jnp.moveaxis(y, 0, 1)


def rwkv7_time_mix(u, mu, w0, w2, a0, a2, g2, k_k, k_a, r_k, ln_g, ln_b):
    bsz, s, _ = u.shape
    u = u.astype(jnp.float32)
    u = u + (token_shift(u) - u) * mu
    r, k, v, dl, al, gl = split_cols(u, [A_WIDTH, A_WIDTH, A_WIDTH, A_DECAY_LORA, A_AAA_LORA, A_GATE_LORA])
    logw = -jax.nn.softplus(-(w0 + jnp.tanh(dl) @ w2)) - 0.5
    decay = jnp.exp(-jnp.exp(logw))
    a = jax.nn.sigmoid(a0 + al @ a2)
    g = jax.nn.sigmoid(gl) @ g2
    heads = lambda t: t.reshape(bsz, s, A_HEADS, A_HEAD_DIM)
    kk = heads(k * k_k)
    kk = kk * lax.rsqrt(jnp.maximum(jnp.sum(kk * kk, axis=-1, keepdims=True), 1e-24))
    k = k * (1.0 + (a - 1.0) * k_a)
    rh, kh, vh = heads(r), heads(k), heads(v)
    y = rwkv7_scan(rh, heads(decay), kh, vh, -kk, kk * heads(a))
    mean = jnp.mean(y, axis=-1, keepdims=True)
    var = jnp.mean(jnp.square(y - mean), axis=-1, keepdims=True)
    y = ((y - mean) * lax.rsqrt(var + A_LN_EPS)).reshape(bsz, s, A_WIDTH) * ln_g + ln_b
    bonus = jnp.sum(rh * kh * r_k, axis=-1, keepdims=True) * vh
    return (y + bonus.reshape(bsz, s, A_WIDTH)) * g


def causal_depthwise_conv(u, w, b):
    ch = u.shape[-1]
    out = lax.conv_general_dilated(u, w[:, None, :].astype(u.dtype), window_strides=(1,),
                                   padding=[(B_CONV - 1, 0)],
                                   dimension_numbers=('NWC', 'WIO', 'NWC'),
                                   feature_group_count=ch)
    return out + b


def ssd_chunked(xs, dt, a, bm, cm):
    bsz, s, nh, p = xs.shape
    g, n = bm.shape[2], bm.shape[3]
    rep = nh // g
    nc = s // B_CHUNK
    xdt = (xs * dt[..., None]).reshape(bsz, nc, B_CHUNK, g, rep, p)
    da = (dt * a).reshape(bsz, nc, B_CHUNK, g, rep)
    bc = bm.reshape(bsz, nc, B_CHUNK, g, n)
    cc = cm.reshape(bsz, nc, B_CHUNK, g, n)
    cum = jnp.cumsum(da, axis=2)
    causal = jnp.tril(jnp.ones((B_CHUNK, B_CHUNK), bool))
    seg = cum[:, :, :, None] - cum[:, :, None, :]
    decay_in = jnp.exp(jnp.where(causal[:, :, None, None], seg, -jnp.inf))
    cb = jnp.einsum('bclgn,bcsgn->bclsg', cc, bc)
    y_diag = jnp.einsum('bclsgr,bcsgrp->bclgrp', cb[..., None] * decay_in, xdt)
    decay_to_end = jnp.exp(cum[:, :, -1:] - cum)
    states = jnp.einsum('bclgn,bclgrp->bcgrpn', bc, decay_to_end[..., None] * xdt)
    chunk_decay = jnp.exp(cum[:, :, -1])

    def step(carry, inp):
        st, dec = inp
        return carry * dec[..., None, None] + st, carry

    init = jnp.zeros((bsz, g, rep, p, n), jnp.float32)
    _, prev = lax.scan(step, init, (jnp.moveaxis(states, 1, 0), jnp.moveaxis(chunk_decay, 1, 0)))
    prev = jnp.moveaxis(prev, 0, 1)
    y_off = jnp.einsum('bclgn,bcgrpn->bclgrp', cc, prev) * jnp.exp(cum)[..., None]
    return (y_diag + y_off).reshape(bsz, s, nh, p)


def mamba2_ssd_mix(u, conv_w, conv_b, dt_bias, a_log, d_skip, norm_g):
    bsz, s, _ = u.shape
    u = u.astype(jnp.float32)
    z, xbc, dt = split_cols(u, [B_WIDTH, B_CONV_CH, B_HEADS])
    xbc = jax.nn.silu(causal_depthwise_conv(xbc, conv_w, conv_b))
    xs, bm, cm = split_cols(xbc, [B_WIDTH, B_GROUPS * B_STATE, B_GROUPS * B_STATE])
    dt = jax.nn.softplus(dt + dt_bias)
    a = -jnp.exp(a_log.astype(jnp.float32))
    xs4 = xs.reshape(bsz, s, B_HEADS, B_HEAD_DIM)
    y = ssd_chunked(xs4, dt, a, bm.reshape(bsz, s, B_GROUPS, B_STATE), cm.reshape(bsz, s, B_GROUPS, B_STATE))
    y = (y + d_skip[:, None] * xs4).reshape(bsz, s, B_WIDTH) * jax.nn.silu(z)
    yg = y.reshape(bsz, s, B_GROUPS, B_WIDTH // B_GROUPS)
    yg = yg * lax.rsqrt(jnp.mean(yg * yg, axis=-1, keepdims=True) + NORM_EPS)
    return yg.reshape(bsz, s, B_WIDTH) * norm_g


def causal_block_attention(q, k, v, scale):
    bsz, s, nh, dk = q.shape
    nb = s // Q_BLOCK
    kpos = jnp.arange(s)
    qb = jnp.swapaxes(q.reshape(bsz, nb, Q_BLOCK, nh, dk), 0, 1)

    def block(args):
        q_i, blk = args
        qpos = blk * Q_BLOCK + jnp.arange(Q_BLOCK)
        logits = jnp.einsum('bqhd,bshd->bhqs', q_i, k).astype(jnp.float32) * scale
        logits = jnp.where(kpos[None, :] <= qpos[:, None], logits, -jnp.inf)
        p = jax.nn.softmax(logits, axis=-1).astype(v.dtype)
        return jnp.einsum('bhqs,bshd->bqhd', p, v)

    out = lax.map(block, (qb, jnp.arange(nb)))
    return jnp.swapaxes(out, 0, 1).reshape(bsz, s, nh * v.shape[-1])


def mla_attention(u, q_norm, wq_b, kv_norm, wkv_b, q_gain, k_gain):
    bsz, s, _ = u.shape
    q_lat, kv_lat, k_pe = split_cols(u, [C_Q_LORA, C_KV_LORA, C_ROPE])
    q = (rms_norm(q_lat, q_norm) @ wq_b).reshape(bsz, s, C_HEADS, C_QK)
    kv = (rms_norm(kv_lat, kv_norm) @ wkv_b).reshape(bsz, s, C_HEADS, C_NOPE + C_V_DIM)
    k_nope, v = kv[..., :C_NOPE], kv[..., C_NOPE:]
    k = jnp.concatenate([k_nope, jnp.broadcast_to(k_pe[:, :, None, :], (bsz, s, C_HEADS, C_ROPE))], axis=-1)
    q = rms_norm(q, q_gain)
    k = rms_norm(k, k_gain)
    pos = jnp.arange(s)
    q = jnp.concatenate([q[..., :C_NOPE], rope(q[..., C_NOPE:], pos)], axis=-1)
    k = jnp.concatenate([k[..., :C_NOPE], rope(k[..., C_NOPE:], pos)], axis=-1)
    return causal_block_attention(q, k, v, C_QK ** -0.5)


def dsa_attention(u, q_gain, k_gain, ik_gain, rel_bias):
    bsz, s, _ = u.shape
    dq, dk, dv, iq, ik, iw = split_cols(u, [D_HEADS * D_HEAD_DIM, D_KV_HEADS * D_HEAD_DIM, D_KV_HEADS * D_HEAD_DIM,
                                           D_IDX_HEADS * D_IDX_DIM, D_IDX_DIM, D_IDX_HEADS])
    rep = D_HEADS // D_KV_HEADS
    q = rms_norm(dq.reshape(bsz, s, D_HEADS, D_HEAD_DIM), q_gain)
    k = rms_norm(dk.reshape(bsz, s, D_KV_HEADS, D_HEAD_DIM), k_gain)
    v = dv.reshape(bsz, s, D_KV_HEADS, D_HEAD_DIM)
    kv = jnp.concatenate([k, v], axis=-1)
    iq = iq.reshape(bsz, s, D_IDX_HEADS, D_IDX_DIM)
    ik = rms_norm(ik, ik_gain)
    iw = iw * D_IDX_HEADS ** -0.5
    topk = min(D_TOPK_MAX, s // 4)
    nb = s // Q_BLOCK
    kpos = jnp.arange(s)
    to_blocks = lambda t: jnp.swapaxes(t.reshape((bsz, nb, Q_BLOCK) + t.shape[2:]), 0, 1)

    def block(args):
        q_i, iq_i, iw_i, blk = args
        qpos = blk * Q_BLOCK + jnp.arange(Q_BLOCK)
        dots = jnp.einsum('bqhd,bsd->bqhs', iq_i, ik).astype(jnp.float32) * D_IDX_DIM ** -0.5
        score = jnp.einsum('bqh,bqhs->bqs', iw_i.astype(jnp.float32), jax.nn.relu(dots))
        score = jnp.where(kpos[None, None, :] <= qpos[None, :, None], score, -jnp.inf)
        _, idx = lax.top_k(score, topk)
        kv_sel = jax.vmap(lambda kv_b, idx_b: kv_b[idx_b])(kv, idx)
        k_sel, v_sel = kv_sel[..., :D_HEAD_DIM], kv_sel[..., D_HEAD_DIM:]
        qg = q_i.reshape(bsz, Q_BLOCK, D_KV_HEADS, rep, D_HEAD_DIM)
        logits = jnp.einsum('bqgrd,bqkgd->bqgrk', qg, k_sel).astype(jnp.float32) * D_HEAD_DIM ** -0.5
        bias = rel_bias.astype(jnp.float32)[rel_bucket(qpos[None, :, None] - idx)]
        bias = jnp.transpose(bias.reshape(bsz, Q_BLOCK, topk, D_KV_HEADS, rep), (0, 1, 3, 4, 2))
        valid = (idx <= qpos[None, :, None])[:, :, None, None, :]
        logits = jnp.where(valid, logits + bias, -jnp.inf)
        p = jax.nn.softmax(logits, axis=-1).astype(v_sel.dtype)
        o = jnp.einsum('bqgrk,bqkgd->bqgrd', p, v_sel)
        return o.reshape(bsz, Q_BLOCK, D_HEADS * D_HEAD_DIM)

    out = lax.map(block, (to_blocks(q), to_blocks(iq), to_blocks(iw), jnp.arange(nb)))
    return jnp.swapaxes(out, 0, 1).reshape(bsz, s, D_HEADS * D_HEAD_DIM)


def memory_cross_attention(hq, mkv, wq, wk, wv, wo, q_gain, k_gain):
    bsz, s, _ = hq.shape
    m = mkv.shape[1]
    q = rms_norm((hq @ wq).reshape(bsz, s, X_HEADS, X_HEAD_DIM), q_gain)
    k = rms_norm((mkv @ wk).reshape(bsz, m, X_HEADS, X_HEAD_DIM), k_gain)
    v = (mkv @ wv).reshape(bsz, m, X_HEADS, X_HEAD_DIM)
    logits = jnp.einsum('bshd,bmhd->bhsm', q, k).astype(jnp.float32) * X_HEAD_DIM ** -0.5
    p = jax.nn.softmax(logits, axis=-1).astype(v.dtype)
    o = jnp.einsum('bhsm,bmhd->bshd', p, v).reshape(bsz, s, X_WIDTH)
    return o @ wo


def swiglu(h, w_gate, w_up, w_down):
    return (jax.nn.silu(h @ w_gate) * (h @ w_up)) @ w_down


def moe_swiglu(h, router, w_gate, w_up, w_down):
    bsz, s, d = h.shape
    t = h.reshape(bsz * s, d)
    logits = (t @ router).astype(jnp.float32)
    top_v, top_i = lax.top_k(logits, TOP_K)
    top_w = jax.nn.softmax(top_v, axis=-1)
    gates = jnp.sum(jax.nn.one_hot(top_i, N_EXPERTS, dtype=jnp.float32) * top_w[..., None], axis=1)
    out = jnp.zeros_like(t)
    for e in range(N_EXPERTS):
        out = out + gates[:, e:e + 1].astype(t.dtype) * swiglu(t, w_gate[e], w_up[e], w_down[e])
    return out.reshape(bsz, s, d)


def setup_inputs(seed: int = 0) -> dict:
    key = jax.random.key(seed)
    ks = iter(jax.random.split(key, 80))
    n_even = (DEPTH + 1) // 2
    n_odd = DEPTH // 2
    d = D_MODEL

    def nrm(shape, scale):
        return jax.random.normal(next(ks), shape, jnp.float32) * scale

    def gain(shape):
        return 1.0 + nrm(shape, 0.05)

    def unif(shape, lo, hi):
        return jax.random.uniform(next(ks), shape, jnp.float32, lo, hi)

    dt0 = jnp.exp(unif((n_even, B_HEADS), math.log(1e-3), math.log(1e-1)))
    return {
        'x': nrm((BATCH, SEQ, d), 1.0),
        'mem': nrm((BATCH, MEM_LEN, d), 1.0),
        'rel_bias': nrm((REL_BUCKETS, D_HEADS), 0.3),
        'norm_mix': gain((DEPTH, d)),
        'norm_mem_q': gain((DEPTH, d)),
        'norm_mem_kv': gain((DEPTH, d)),
        'norm_ffn': gain((DEPTH, d)),
        'xa_wq': nrm((DEPTH, d, X_WIDTH), d ** -0.5),
        'xa_wk': nrm((DEPTH, d, X_WIDTH), d ** -0.5),
        'xa_wv': nrm((DEPTH, d, X_WIDTH), d ** -0.5),
        'xa_wo': nrm((DEPTH, X_WIDTH, d), X_WIDTH ** -0.5),
        'xa_q_gain': gain((DEPTH, X_HEAD_DIM)),
        'xa_k_gain': gain((DEPTH, X_HEAD_DIM)),
        'ab_w_in': nrm((n_even, d, AB_COLS), d ** -0.5),
        'ab_w_out': nrm((n_even, A_WIDTH + B_WIDTH, d), (A_WIDTH + B_WIDTH) ** -0.5),
        'a_shift_mu': unif((n_even, A_COLS), 0.0, 1.0),
        'a_w0': unif((n_even, A_WIDTH), -6.0, 1.0),
        'a_w2': nrm((n_even, A_DECAY_LORA, A_WIDTH), 0.5 * A_DECAY_LORA ** -0.5),
        'a_a0': nrm((n_even, A_WIDTH), 0.1),
        'a_a2': nrm((n_even, A_AAA_LORA, A_WIDTH), 0.5 * A_AAA_LORA ** -0.5),
        'a_g2': nrm((n_even, A_GATE_LORA, A_WIDTH), A_GATE_LORA ** -0.5),
        'a_k_k': 0.85 + nrm((n_even, A_WIDTH), 0.05),
        'a_k_a': 1.0 + nrm((n_even, A_WIDTH), 0.05),
        'a_r_k': nrm((n_even, A_HEADS, A_HEAD_DIM), 0.1),
        'a_ln_g': gain((n_even, A_WIDTH)),
        'a_ln_b': nrm((n_even, A_WIDTH), 0.02),
        'b_conv_w': nrm((n_even, B_CONV, B_CONV_CH), B_CONV ** -0.5),
        'b_conv_b': nrm((n_even, B_CONV_CH), 0.02),
        'b_dt_bias': dt0 + jnp.log(-jnp.expm1(-dt0)),
        'b_a_log': jnp.log(unif((n_even, B_HEADS), 1.0, 16.0)),
        'b_d': gain((n_even, B_HEADS)),
        'b_norm_g': gain((n_even, B_WIDTH)),
        'ffn_w_gate': nrm((n_even, d, FF_DENSE), d ** -0.5),
        'ffn_w_up': nrm((n_even, d, FF_DENSE), d ** -0.5),
        'ffn_w_down': nrm((n_even, FF_DENSE, d), FF_DENSE ** -0.5),
        'cd_w_in': nrm((n_odd, d, CD_COLS), d ** -0.5),
        'cd_w_out': nrm((n_odd, CD_OUT, d), CD_OUT ** -0.5),
        'c_q_norm': gain((n_odd, C_Q_LORA)),
        'c_wq_b': nrm((n_odd, C_Q_LORA, C_HEADS * C_QK), C_Q_LORA ** -0.5),
        'c_kv_norm': gain((n_odd, C_KV_LORA)),
        'c_wkv_b': nrm((n_odd, C_KV_LORA, C_HEADS * (C_NOPE + C_V_DIM)), C_KV_LORA ** -0.5),
        'c_q_gain': gain((n_odd, C_QK)),
        'c_k_gain': gain((n_odd, C_QK)),
        'd_q_gain': gain((n_odd, D_HEAD_DIM)),
        'd_k_gain': gain((n_odd, D_HEAD_DIM)),
        'd_ik_gain': gain((n_odd, D_IDX_DIM)),
        'moe_router': nrm((n_odd, d, N_EXPERTS), d ** -0.5),
        'moe_w_gate': nrm((n_odd, N_EXPERTS, d, FF_EXPERT), d ** -0.5),
        'moe_w_up': nrm((n_odd, N_EXPERTS, d, FF_EXPERT), d ** -0.5),
        'moe_w_down': nrm((n_odd, N_EXPERTS, FF_EXPERT, d), FF_EXPERT ** -0.5),
    }


def reference(x, mem, rel_bias, norm_mix, norm_mem_q, norm_mem_kv, norm_ffn,
              xa_wq, xa_wk, xa_wv, xa_wo, xa_q_gain, xa_k_gain,
              ab_w_in, ab_w_out, a_shift_mu, a_w0, a_w2, a_a0, a_a2, a_g2, a_k_k, a_k_a, a_r_k,
              a_ln_g, a_ln_b, b_conv_w, b_conv_b, b_dt_bias, b_a_log, b_d, b_norm_g,
              ffn_w_gate, ffn_w_up, ffn_w_down,
              cd_w_in, cd_w_out, c_q_norm, c_wq_b, c_kv_norm, c_wkv_b, c_q_gain, c_k_gain,
              d_q_gain, d_k_gain, d_ik_gain,
              moe_router, moe_w_gate, moe_w_up, moe_w_down):
    h = x
    for layer in range(DEPTH):
        i = layer // 2
        hn = rms_norm(h, norm_mix[layer])
        if layer % 2 == 0:
            u = hn @ ab_w_in[i]
            ua, ub = split_cols(u, [A_COLS, B_COLS])
            ya = rwkv7_time_mix(ua, a_shift_mu[i], a_w0[i], a_w2[i], a_a0[i], a_a2[i], a_g2[i],
                                a_k_k[i], a_k_a[i], a_r_k[i], a_ln_g[i], a_ln_b[i])
            yb = mamba2_ssd_mix(ub, b_conv_w[i], b_conv_b[i], b_dt_bias[i], b_a_log[i], b_d[i], b_norm_g[i])
            mix = jnp.concatenate([ya, yb], axis=-1).astype(h.dtype) @ ab_w_out[i]
        else:
            u = hn @ cd_w_in[i]
            uc, ud = split_cols(u, [C_COLS, D_COLS])
            yc = mla_attention(uc, c_q_norm[i], c_wq_b[i], c_kv_norm[i], c_wkv_b[i], c_q_gain[i], c_k_gain[i])
            yd = dsa_attention(ud, d_q_gain[i], d_k_gain[i], d_ik_gain[i], rel_bias)
            mix = jnp.concatenate([yc, yd], axis=-1).astype(h.dtype) @ cd_w_out[i]
        h = h + mix
        h = h + memory_cross_attention(rms_norm(h, norm_mem_q[layer]), rms_norm(mem, norm_mem_kv[layer]),
                                       xa_wq[layer], xa_wk[layer], xa_wv[layer], xa_wo[layer],
                                       xa_q_gain[layer], xa_k_gain[layer])
        hn = rms_norm(h, norm_ffn[layer])
        if layer % 2 == 0:
            h = h + swiglu(hn, ffn_w_gate[i], ffn_w_up[i], ffn_w_down[i])
        else:
            h = h + moe_swiglu(hn, moe_router[i], moe_w_gate[i], moe_w_up[i], moe_w_down[i])
    return h
```

```python
import functools
import math

import numpy as np
import jax
import jax.numpy as jnp
from jax import lax
from jax.experimental import pallas as pl
from jax.experimental.pallas import tpu as pltpu

F32 = jnp.float32
BF16 = jnp.bfloat16
I32 = jnp.int32
HIGHEST = lax.Precision.HIGHEST

V7X_VMEM_BYTES = 64 * 1024 * 1024
VMEM_LIMIT = V7X_VMEM_BYTES - 8 * 1024 * 1024
LANES = 128

NORM_EPS = 1e-6
D_MODEL = 2048
HEAD64 = 64

A_WIDTH = 1024
A_LORA = (64, 64, 160)
A_LORA_PAD = 384
A_LN_EPS = 1e-5 * (HEAD64 / 8) ** 2
B_WIDTH = 1024
B_HEADS = 16
B_GROUPS = 4
B_STATE = 128
B_CONV = 4
B_CHUNK = 128
B_CONV_CH = B_WIDTH + 2 * B_GROUPS * B_STATE
AB_R, AB_K, AB_V, AB_Z, AB_XBC, AB_LORA, AB_DT, AB_COLS_PAD = 0, 1024, 2048, 3072, 4096, 6144, 6528, 6656

C_HEADS = 8
C_NOPE = 128
C_ROPE = 64
C_QK = C_NOPE + C_ROPE
C_V = 128
C_LORA = 512
ROPE_THETA = 10000.0
D_HEADS = 8
D_KV = 2
D_DIM = 128
D_IDX_DIM = 64
D_TOPK_MAX = 256
Q_BLOCK = 128
REL_BUCKETS = 32
REL_MAX_DIST = 128
CD_QLAT, CD_KVLAT, CD_DQ, CD_DK, CD_DV, CD_IQ, CD_PEIK, CD_IW, CD_COLS_PAD = (
    0, 512, 1024, 2048, 2304, 2560, 3072, 3200, 3328)

X_HEADS = 4
X_DIM = 128
N_EXPERTS = 8

NEG_BIG = -1e30
INT_MIN = -2 ** 31


def _cparams(sem):
    return pltpu.CompilerParams(dimension_semantics=sem, vmem_limit_bytes=VMEM_LIMIT)


def _rms(x, g, eps=NORM_EPS):
    return x * lax.rsqrt(jnp.mean(x * x, axis=-1, keepdims=True) + eps) * g


def _softplus(x):
    return jnp.maximum(x, 0.0) + jnp.log(1.0 + jnp.exp(-jnp.abs(x)))


def _silu(x):
    return x * jax.nn.sigmoid(x)


def _half_sum_bcast(x):
    left = lax.broadcasted_iota(I32, x.shape, 1) < HEAD64
    s0 = jnp.sum(jnp.where(left, x, 0.0), axis=1, keepdims=True)
    s1 = jnp.sum(jnp.where(left, 0.0, x), axis=1, keepdims=True)
    return jnp.where(left, s0, s1)


def _mm_kernel(*refs, n_x, has_norm, has_res):
    x_refs = refs[:n_x]
    pos = n_x
    g_ref = refs[pos] if has_norm else None
    pos += int(has_norm)
    w_refs = refs[pos:pos + n_x]
    pos += n_x
    res_ref = refs[pos] if has_res else None
    pos += int(has_res)
    o_ref = refs[pos]
    xn_refs = refs[pos + 1:]

    @pl.when(pl.program_id(1) == 0)
    def _():
        for x_ref, xn_ref in zip(x_refs, xn_refs):
            x = x_ref[...].astype(F32)
            if has_norm:
                x = _rms(x, g_ref[...])
            xn_ref[...] = x.astype(BF16)

    acc = None
    for xn_ref, w_ref in zip(xn_refs, w_refs):
        d = jnp.dot(xn_ref[...], w_ref[...], preferred_element_type=F32)
        acc = d if acc is None else acc + d
    if has_res:
        acc = acc + res_ref[...]
    o_ref[...] = acc


def _matmul(xs, ws, *, gain=None, res=None, tm=512, tn=512):
    m = xs[0][0].shape[0]
    n = ws[0].shape[1]
    tm = min(tm, m)
    tn = min(tn, n)
    assert m % tm == 0 and n % tn == 0
    in_specs, args, scratch = [], [], []
    for arr, cb, width in xs:
        in_specs.append(pl.BlockSpec((tm, width), lambda i, j, cb=cb: (i, cb)))
        args.append(arr)
        scratch.append(pltpu.VMEM((tm, width), BF16))
    if gain is not None:
        in_specs.append(pl.BlockSpec((1, gain.shape[-1]), lambda i, j: (0, 0)))
        args.append(gain.reshape(1, -1))
    for (arr, cb, width), w in zip(xs, ws):
        assert w.shape[0] == width
        in_specs.append(pl.BlockSpec((width, tn), lambda i, j: (0, j)))
        args.append(w)
    if res is not None:
        in_specs.append(pl.BlockSpec((tm, tn), lambda i, j: (i, j)))
        args.append(res)
    return pl.pallas_call(
        functools.partial(_mm_kernel, n_x=len(xs), has_norm=gain is not None, has_res=res is not None),
        grid=(m // tm, n // tn),
        in_specs=in_specs,
        out_specs=pl.BlockSpec((tm, tn), lambda i, j: (i, j)),
        out_shape=jax.ShapeDtypeStruct((m, n), F32),
        scratch_shapes=scratch,
        compiler_params=_cparams(("parallel", "arbitrary")),
        name="matmul",
    )(*args)


def _rwkv_pre_kernel(rkv_ref, lora_ref, rkvp_ref, lorap_ref, mu_rkv_ref, mu_lora_ref,
                     w0_ref, a0_ref, kk_ref, ka_ref, w2_ref, a2_ref, g2_ref,
                     r_o, w_o, k_o, v_o, na_o, nb_o, g_o, *, tiles_per_seq):
    first = (pl.program_id(0) % tiles_per_seq) == 0
    tm = rkv_ref.shape[0]

    def shift_mix(x, prev_rows, mu):
        prev_last = jnp.where(first, 0.0, prev_rows[7:8, :])
        xs = pltpu.roll(x, 1, axis=0)
        row = lax.broadcasted_iota(I32, x.shape, 0)
        xs = jnp.where(row == 0, prev_last, xs)
        return x + (xs - x) * mu

    lo = shift_mix(lora_ref[...], lorap_ref[...], mu_lora_ref[...])
    lane = lax.broadcasted_iota(I32, lo.shape, 1)
    act = jnp.where(lane < A_LORA[0], jnp.tanh(lo),
                    jnp.where(lane < A_LORA[0] + A_LORA[1], lo, jax.nn.sigmoid(lo))).astype(BF16)
    dw = jnp.dot(act, w2_ref[...], preferred_element_type=F32)
    da = jnp.dot(act, a2_ref[...], preferred_element_type=F32)
    g_o[...] = jnp.dot(act, g2_ref[...], preferred_element_type=F32)

    for p in range(A_WIDTH // LANES):
        sl = slice(LANES * p, LANES * (p + 1))

        def mixed(off):
            s2 = slice(off + LANES * p, off + LANES * (p + 1))
            return shift_mix(rkv_ref[:, s2], rkvp_ref[:, s2], mu_rkv_ref[:, s2])

        r_o[:, sl] = mixed(AB_R)
        v_o[:, sl] = mixed(AB_V)
        kx = mixed(AB_K)
        logw = -_softplus(-(w0_ref[:, sl] + dw[:, sl])) - 0.5
        w_o[:, sl] = jnp.exp(-jnp.exp(logw))
        a = jax.nn.sigmoid(a0_ref[:, sl] + da[:, sl])
        kk = kx * kk_ref[:, sl]
        kk = kk * lax.rsqrt(jnp.maximum(_half_sum_bcast(kk * kk), 1e-24))
        k_o[:, sl] = kx * (1.0 + (a - 1.0) * ka_ref[:, sl])
        na_o[:, sl] = -kk
        nb_o[:, sl] = kk * a


def _rwkv_pre(u, seq, mu_rkv, mu_lora, w0, a0, k_k, k_a, w2p, a2p, g2p, tm=256):
    m = u.shape[0]
    tm = min(tm, seq)
    row = lambda w: pl.BlockSpec((1, w), lambda i: (0, 0))
    full = lambda a: pl.BlockSpec(a.shape, lambda i: (0, 0))
    prev = lambda i: jnp.maximum(i * (tm // 8) - 1, 0)
    out = jax.ShapeDtypeStruct((m, A_WIDTH), F32)
    return pl.pallas_call(
        functools.partial(_rwkv_pre_kernel, tiles_per_seq=seq // tm),
        grid=(m // tm,),
        in_specs=[
            pl.BlockSpec((tm, 3 * A_WIDTH), lambda i: (i, 0)),
            pl.BlockSpec((tm, A_LORA_PAD), lambda i: (i, AB_LORA // A_LORA_PAD)),
            pl.BlockSpec((8, 3 * A_WIDTH), lambda i: (prev(i), 0)),
            pl.BlockSpec((8, A_LORA_PAD), lambda i: (prev(i), AB_LORA // A_LORA_PAD)),
            row(3 * A_WIDTH), row(A_LORA_PAD), row(A_WIDTH), row(A_WIDTH), row(A_WIDTH), row(A_WIDTH),
            full(w2p), full(a2p), full(g2p),
        ],
        out_specs=[pl.BlockSpec((tm, A_WIDTH), lambda i: (i, 0))] * 7,
        out_shape=[out] * 7,
        compiler_params=_cparams(("parallel",)),
        name="rwkv_pre",
    )(u, u, u, u, mu_rkv, mu_lora, w0, a0, k_k, k_a, w2p, a2p, g2p)


RWKV_CHUNK = 64


def _rwkv_scan_kernel(r_ref, w_ref, k_ref, v_ref, a_ref, b_ref, y_ref, s_ref, vt_ref, yt_ref):
    npairs = s_ref.shape[0]

    @pl.when(pl.program_id(1) == 0)
    def _():
        s_ref[...] = jnp.zeros_like(s_ref)

    lane = lax.broadcasted_iota(I32, (HEAD64, LANES), 1)
    left = lane < HEAD64
    lane64 = lane & (HEAD64 - 1)

    def pair_transpose(x):
        xt = jnp.concatenate([x, x], axis=0).T
        return jnp.where(left, xt[0:HEAD64], xt[HEAD64:2 * HEAD64])

    for p in range(npairs):
        vt_ref[p] = pair_transpose(v_ref[:, LANES * p:LANES * (p + 1)])
    yt_ref[...] = jnp.zeros_like(yt_ref)

    def step8(t8, carry):
        rows8 = pl.ds(pl.multiple_of(t8 * 8, 8), 8)
        for p in range(npairs):
            sl = slice(LANES * p, LANES * (p + 1))
            a8, w8, b8, k8, r8 = (ref[rows8, sl] for ref in (a_ref, w_ref, b_ref, k_ref, r_ref))
            s = s_ref[p]
            vt = vt_ref[p]
            yt = yt_ref[p]
            for j in range(8):
                sel = lane64 == t8 * 8 + j
                row = lambda x8: x8[j:j + 1, :]
                sa = _half_sum_bcast(s * row(a8))
                vc = _half_sum_bcast(jnp.where(sel, vt, 0.0))
                s = s * row(w8) + sa * row(b8) + vc * row(k8)
                yt = jnp.where(sel, _half_sum_bcast(s * row(r8)), yt)
            s_ref[p] = s
            yt_ref[p] = yt
        return carry

    lax.fori_loop(0, RWKV_CHUNK // 8, step8, 0)

    for p in range(npairs):
        y_ref[:, LANES * p:LANES * (p + 1)] = pair_transpose(yt_ref[p])


def _rwkv_scan(r, w, k, v, na, nb, batch):
    m = r.shape[0]
    seq = m // batch
    nchunk = seq // RWKV_CHUNK
    npairs = A_WIDTH // LANES
    spec = pl.BlockSpec((RWKV_CHUNK, A_WIDTH), lambda b, c: (b * nchunk + c, 0))
    pair_scratch = pltpu.VMEM((npairs, HEAD64, LANES), F32)
    return pl.pallas_call(
        _rwkv_scan_kernel,
        grid=(batch, nchunk),
        in_specs=[spec] * 6,
        out_specs=spec,
        out_shape=jax.ShapeDtypeStruct((m, A_WIDTH), F32),
        scratch_shapes=[pair_scratch, pair_scratch, pair_scratch],
        compiler_params=_cparams(("parallel", "arbitrary")),
        name="rwkv_scan",
    )(r, w, k, v, na, nb)


def _rwkv_post_kernel(y_ref, r_ref, k_ref, v_ref, g_ref, lng_ref, lnb_ref, rk_ref, o_ref):
    for p in range(A_WIDTH // LANES):
        sl = slice(LANES * p, LANES * (p + 1))
        y = y_ref[:, sl]
        mean = _half_sum_bcast(y) * (1.0 / HEAD64)
        d = y - mean
        var = _half_sum_bcast(d * d) * (1.0 / HEAD64)
        yn = d * lax.rsqrt(var + A_LN_EPS) * lng_ref[:, sl] + lnb_ref[:, sl]
        bonus = _half_sum_bcast(r_ref[:, sl] * k_ref[:, sl] * rk_ref[:, sl]) * v_ref[:, sl]
        o_ref[:, sl] = (yn + bonus) * g_ref[:, sl]


def _rwkv_post(y, r, k, v, g, ln_g, ln_b, r_k, tm=256):
    m = y.shape[0]
    tm = min(tm, m)
    spec = pl.BlockSpec((tm, A_WIDTH), lambda i: (i, 0))
    row = pl.BlockSpec((1, A_WIDTH), lambda i: (0, 0))
    return pl.pallas_call(
        _rwkv_post_kernel,
        grid=(m // tm,),
        in_specs=[spec] * 5 + [row] * 3,
        out_specs=spec,
        out_shape=jax.ShapeDtypeStruct((m, A_WIDTH), F32),
        compiler_params=_cparams(("parallel",)),
        name="rwkv_post",
    )(y, r, k, v, g, ln_g, ln_b, r_k)


def _ssd_kernel(z_ref, xbc_ref, dt_ref, cw_ref, cb_ref, dtb_ref, alog_ref, dskip_ref, ng_ref,
                expand_ref, o_ref, st_ref, tail_ref):
    lc = B_CHUNK

    @pl.when(pl.program_id(1) == 0)
    def _():
        st_ref[...] = jnp.zeros_like(st_ref)
        tail_ref[...] = jnp.zeros_like(tail_ref)

    x = xbc_ref[...]
    tail = tail_ref[...]
    row8 = lax.broadcasted_iota(I32, tail.shape, 0)
    conv = cb_ref[...] + cw_ref[B_CONV - 1:B_CONV, :] * x
    for j in range(1, B_CONV):
        xs = pltpu.roll(x, j, axis=0)
        top = jnp.where(row8 < j, pltpu.roll(tail, j, axis=0), xs[0:8])
        xs = jnp.concatenate([top, xs[8:]], axis=0)
        conv = conv + cw_ref[B_CONV - 1 - j:B_CONV - j, :] * xs
    tail_ref[...] = x[lc - 8:lc]
    act = _silu(conv)
    xs_in = act[:, 0:B_WIDTH]
    bm = act[:, B_WIDTH:B_WIDTH + B_GROUPS * B_STATE].astype(BF16)
    cm = act[:, B_WIDTH + B_GROUPS * B_STATE:].astype(BF16)

    dt = _softplus(dt_ref[...] + dtb_ref[...])
    a_neg = -jnp.exp(alog_ref[...])
    da = dt * a_neg
    ri = lax.broadcasted_iota(I32, (lc, lc), 0)
    ci = lax.broadcasted_iota(I32, (lc, lc), 1)
    causal = ci <= ri
    tri = causal.astype(F32)
    cum = jnp.dot(tri, da, precision=HIGHEST, preferred_element_type=F32)
    cum_t = jnp.dot(da.T, (ri <= ci).astype(F32), precision=HIGHEST,
                    preferred_element_type=F32)
    expand = expand_ref[...]
    widen = lambda t: jnp.dot(t, expand, precision=HIGHEST, preferred_element_type=F32)
    dt_full = widen(dt)
    ecum_full = widen(jnp.exp(cum))
    dte_full = widen(jnp.exp(cum[lc - 1:lc, :] - cum))
    xdt = xs_in * dt_full
    xdt_b = xdt.astype(BF16)
    xdte_b = (xdt * dte_full).astype(BF16)
    left = lax.broadcasted_iota(I32, (lc, LANES), 1) < HEAD64

    ys = []
    for g in range(B_GROUPS):
        gs = slice(B_STATE * g, B_STATE * (g + 1))
        cm_g = cm[:, gs]
        bm_g = bm[:, gs]
        cb = lax.dot_general(cm_g, bm_g, (((1,), (1,)), ((), ())), preferred_element_type=F32)
        bm_t = bm_g.T
        pairs_per_group = B_HEADS // B_GROUPS // 2
        for q in range(pairs_per_group):
            p = g * pairs_per_group + q
            sl = slice(LANES * p, LANES * (p + 1))
            yd = []
            for h in (2 * p, 2 * p + 1):
                seg = cum[:, h:h + 1] - cum_t[h:h + 1, :]
                dec = jnp.where(causal, jnp.exp(jnp.minimum(seg, 0.0)), 0.0)
                yd.append(jnp.dot((cb * dec).astype(BF16), xdt_b[:, sl], preferred_element_type=F32))
            y_diag = jnp.where(left, yd[0], yd[1])
            st = st_ref[p]
            y_off = jnp.dot(cm_g, st.astype(BF16), preferred_element_type=F32) * ecum_full[:, sl]
            ys.append(y_diag + y_off)
            st_ref[p] = st * ecum_full[lc - 1:lc, sl] + jnp.dot(bm_t, xdte_b[:, sl],
                                                                preferred_element_type=F32)
    y = jnp.concatenate(ys, axis=1)
    y = (y + dskip_ref[...] * xs_in) * _silu(z_ref[...])
    gw = B_WIDTH // B_GROUPS
    for g in range(B_GROUPS):
        gs = slice(gw * g, gw * (g + 1))
        yg = y[:, gs]
        o_ref[:, gs] = yg * lax.rsqrt(jnp.mean(yg * yg, axis=-1, keepdims=True) + NORM_EPS) * ng_ref[:, gs]


def _ssd(u, batch, conv_w, conv_b, dt_bias, a_log, d_skip, norm_g):
    m = u.shape[0]
    seq = m // batch
    nchunk = seq // B_CHUNK
    rows = lambda b, c: b * nchunk + c
    pad16 = lambda t: jnp.pad(t.reshape(1, -1), ((0, 0), (0, LANES - B_HEADS)))
    expand = (np.arange(LANES)[:, None] == (np.arange(B_WIDTH)[None, :] // HEAD64)).astype(np.float32)
    full = lambda a: pl.BlockSpec(a.shape, lambda b, c: (0,) * a.ndim)
    args = [conv_w, conv_b.reshape(1, -1), pad16(dt_bias), pad16(a_log),
            jnp.repeat(d_skip, HEAD64).reshape(1, -1), norm_g.reshape(1, -1), jnp.asarray(expand)]
    return pl.pallas_call(
        _ssd_kernel,
        grid=(batch, nchunk),
        in_specs=[
            pl.BlockSpec((B_CHUNK, B_WIDTH), lambda b, c: (rows(b, c), AB_Z // B_WIDTH)),
            pl.BlockSpec((B_CHUNK, B_CONV_CH), lambda b, c: (rows(b, c), AB_XBC // B_CONV_CH)),
            pl.BlockSpec((B_CHUNK, LANES), lambda b, c: (rows(b, c), AB_DT // LANES)),
        ] + [full(a) for a in args],
        out_specs=pl.BlockSpec((B_CHUNK, B_WIDTH), lambda b, c: (rows(b, c), 0)),
        out_shape=jax.ShapeDtypeStruct((m, B_WIDTH), F32),
        scratch_shapes=[pltpu.VMEM((B_HEADS // 2, B_STATE, LANES), F32),
                        pltpu.VMEM((8, B_CONV_CH), F32)],
        compiler_params=_cparams(("parallel", "arbitrary")),
        name="ssd",
    )(u, u, u, *args)


def _xattn_kernel(h_ref, gq_ref, wq_ref, kv_ref, qg_ref, kg_ref, wo_ref, o_ref):
    h = h_ref[...]
    hn = _rms(h, gq_ref[...]).astype(BF16)
    q = jnp.dot(hn, wq_ref[...], preferred_element_type=F32)
    outs = []
    for hd in range(X_HEADS):
        sl = slice(X_DIM * hd, X_DIM * (hd + 1))
        qh = _rms(q[:, sl], qg_ref[...]).astype(BF16)
        kh = _rms(kv_ref[:, sl], kg_ref[...]).astype(BF16)
        vh = kv_ref[:, X_HEADS * X_DIM + X_DIM * hd:X_HEADS * X_DIM + X_DIM * (hd + 1)].astype(BF16)
        s = lax.dot_general(qh, kh, (((1,), (1,)), ((), ())), preferred_element_type=F32) * X_DIM ** -0.5
        e = jnp.exp(s - jnp.max(s, axis=-1, keepdims=True))
        p = e / jnp.sum(e, axis=-1, keepdims=True)
        outs.append(jnp.dot(p.astype(BF16), vh, preferred_element_type=F32))
    o = jnp.concatenate(outs, axis=1).astype(BF16)
    o_ref[...] = h + jnp.dot(o, wo_ref[...], preferred_element_type=F32)


def _xattn(h, mem_kv, batch, gq, wq, q_gain, k_gain, wo, tm=512):
    m, d = h.shape
    seq = m // batch
    tm = min(tm, seq)
    mlen = mem_kv.shape[0] // batch
    nt = seq // tm
    full = lambda a: pl.BlockSpec(a.shape, lambda b, i: (0, 0))
    args = [gq.reshape(1, -1), wq, mem_kv, q_gain.reshape(1, -1), k_gain.reshape(1, -1), wo]
    specs = [full(a) for a in args]
    specs[2] = pl.BlockSpec((mlen, mem_kv.shape[1]), lambda b, i: (b, 0))
    return pl.pallas_call(
        _xattn_kernel,
        grid=(batch, nt),
        in_specs=[pl.BlockSpec((tm, d), lambda b, i: (b * nt + i, 0))] + specs,
        out_specs=pl.BlockSpec((tm, d), lambda b, i: (b * nt + i, 0)),
        out_shape=jax.ShapeDtypeStruct((m, d), F32),
        compiler_params=_cparams(("parallel", "parallel")),
        name="xattn",
    )(h, *args)


def _swiglu_kernel(h_ref, g_ref, wg_ref, wu_ref, wd_ref, o_ref, xn_ref, acc_ref):
    j = pl.program_id(1)

    @pl.when(j == 0)
    def _():
        xn_ref[...] = _rms(h_ref[...], g_ref[...]).astype(BF16)
        acc_ref[...] = jnp.zeros_like(acc_ref)

    xn = xn_ref[...]
    gate = jnp.dot(xn, wg_ref[...], preferred_element_type=F32)
    up = jnp.dot(xn, wu_ref[...], preferred_element_type=F32)
    acc_ref[...] += jnp.dot((_silu(gate) * up).astype(BF16), wd_ref[...], preferred_element_type=F32)

    @pl.when(j == pl.num_programs(1) - 1)
    def _():
        o_ref[...] = h_ref[...] + acc_ref[...]


def _swiglu(h, gain, wg, wu, wd, tm=512, tf=512):
    m, d = h.shape
    f = wg.shape[1]
    tm = min(tm, m)
    assert f % tf == 0
    return pl.pallas_call(
        _swiglu_kernel,
        grid=(m // tm, f // tf),
        in_specs=[
            pl.BlockSpec((tm, d), lambda i, j: (i, 0)),
            pl.BlockSpec((1, d), lambda i, j: (0, 0)),
            pl.BlockSpec((d, tf), lambda i, j: (0, j)),
            pl.BlockSpec((d, tf), lambda i, j: (0, j)),
            pl.BlockSpec((tf, d), lambda i, j: (j, 0)),
        ],
        out_specs=pl.BlockSpec((tm, d), lambda i, j: (i, 0)),
        out_shape=jax.ShapeDtypeStruct((m, d), F32),
        scratch_shapes=[pltpu.VMEM((tm, d), BF16), pltpu.VMEM((tm, d), F32)],
        compiler_params=_cparams(("parallel", "arbitrary")),
        name="swiglu",
    )(h, gain.reshape(1, -1), wg, wu, wd)


def _moe_kernel(h_ref, g_ref, router_ref, wg_ref, wu_ref, wd_ref, o_ref, xn_ref, acc_ref, gates_ref):
    e = pl.program_id(1)
    j = pl.program_id(2)

    @pl.when((e == 0) & (j == 0))
    def _():
        xn = _rms(h_ref[...], g_ref[...]).astype(BF16)
        xn_ref[...] = xn
        acc_ref[...] = jnp.zeros_like(acc_ref)
        logits = jnp.dot(xn, router_ref[...], preferred_element_type=F32)
        lane = lax.broadcasted_iota(I32, logits.shape, 1)
        logits = jnp.where(lane < N_EXPERTS, logits, -jnp.inf)
        m1 = jnp.max(logits, axis=-1, keepdims=True)
        i1 = jnp.min(jnp.where(logits == m1, lane, LANES), axis=-1, keepdims=True)
        rest = jnp.where(lane == i1, -jnp.inf, logits)
        m2 = jnp.max(rest, axis=-1, keepdims=True)
        i2 = jnp.min(jnp.where(rest == m2, lane, LANES), axis=-1, keepdims=True)
        e2 = jnp.exp(m2 - m1)
        w1 = 1.0 / (1.0 + e2)
        w2 = e2 / (1.0 + e2)
        gates_ref[...] = jnp.where(lane == i1, w1, 0.0) + jnp.where(lane == i2, w2, 0.0)

    xn = xn_ref[...]
    gates = gates_ref[...]
    lane = lax.broadcasted_iota(I32, gates.shape, 1)
    ge = jnp.sum(jnp.where(lane == e, gates, 0.0), axis=-1, keepdims=True)
    gate = jnp.dot(xn, wg_ref[0], preferred_element_type=F32)
    up = jnp.dot(xn, wu_ref[0], preferred_element_type=F32)
    act = (_silu(gate) * up * ge).astype(BF16)
    acc_ref[...] += jnp.dot(act, wd_ref[0], preferred_element_type=F32)

    @pl.when((e == pl.num_programs(1) - 1) & (j == pl.num_programs(2) - 1))
    def _():
        o_ref[...] = h_ref[...] + acc_ref[...]


def _moe(h, gain, router_p, wg, wu, wd, tm=512, tf=256):
    m, d = h.shape
    ne, _, f = wg.shape
    tm = min(tm, m)
    assert f % tf == 0
    return pl.pallas_call(
        _moe_kernel,
        grid=(m // tm, ne, f // tf),
        in_specs=[
            pl.BlockSpec((tm, d), lambda i, e, j: (i, 0)),
            pl.BlockSpec((1, d), lambda i, e, j: (0, 0)),
            pl.BlockSpec((d, LANES), lambda i, e, j: (0, 0)),
            pl.BlockSpec((1, d, tf), lambda i, e, j: (e, 0, j)),
            pl.BlockSpec((1, d, tf), lambda i, e, j: (e, 0, j)),
            pl.BlockSpec((1, tf, d), lambda i, e, j: (e, j, 0)),
        ],
        out_specs=pl.BlockSpec((tm, d), lambda i, e, j: (i, 0)),
        out_shape=jax.ShapeDtypeStruct((m, d), F32),
        scratch_shapes=[pltpu.VMEM((tm, d), BF16), pltpu.VMEM((tm, d), F32), pltpu.VMEM((tm, LANES), F32)],
        compiler_params=_cparams(("parallel", "arbitrary", "arbitrary")),
        name="moe",
    )(h, gain.reshape(1, -1), router_p, wg, wu, wd)


def _rope_pairs(x, cos, sin_signed):
    w = x.shape[1]
    lane = lax.broadcasted_iota(I32, x.shape, 1)
    partner = jnp.where((lane & 32) != 0, pltpu.roll(x, 32, axis=1), pltpu.roll(x, w - 32, axis=1))
    return x * cos + partner * sin_signed


def _mla_prep_kernel(ql_ref, kvl_ref, pe_ref, qn_ref, wq_ref, kvn_ref, wkv_ref, qgn_ref, qgr_ref,
                     kgn_ref, kgr_ref, cos_ref, sin_ref, red128_ref, red64_ref, exp128_ref, exp64_ref,
                     rep_ref, qn_o, qr_o, kn_o, kr_o, v_o):
    nn = C_HEADS * C_NOPE
    hi = lambda a, b: jnp.dot(a, b, precision=HIGHEST, preferred_element_type=F32)
    cos = cos_ref[...]
    sin = sin_ref[...]

    def head_norm(nope, rope_part, ss_extra):
        ss = hi(nope * nope, red128_ref[...]) + ss_extra
        rs = lax.rsqrt(ss * (1.0 / C_QK) + NORM_EPS)
        return nope * hi(rs, exp128_ref[...]), rope_part * hi(rs, exp64_ref[...])

    q = jnp.dot(_rms(ql_ref[...], qn_ref[...]).astype(BF16), wq_ref[...], preferred_element_type=F32)
    q_nope, q_rope = q[:, :nn], q[:, nn:]
    q_nope, q_rope = head_norm(q_nope, q_rope, hi(q_rope * q_rope, red64_ref[...]))
    qn_o[...] = (q_nope * qgn_ref[...]).astype(BF16)
    qr_o[...] = _rope_pairs(q_rope * qgr_ref[...], cos, sin).astype(BF16)

    kv = jnp.dot(_rms(kvl_ref[...], kvn_ref[...]).astype(BF16), wkv_ref[...], preferred_element_type=F32)
    v_o[...] = kv[:, nn:].astype(BF16)
    k_rope = hi(pe_ref[...], rep_ref[...])
    k_nope, k_rope = head_norm(kv[:, :nn], k_rope, hi(k_rope * k_rope, red64_ref[...]))
    kn_o[...] = (k_nope * kgn_ref[...]).astype(BF16)
    kr_o[...] = _rope_pairs(k_rope * kgr_ref[...], cos, sin).astype(BF16)


def _mla_prep(u, seq, q_norm, wq_p, kv_norm, wkv_p, q_gain, k_gain, tm=256):
    m = u.shape[0]
    tm = min(tm, seq)
    nn, nr = C_HEADS * C_NOPE, C_HEADS * C_ROPE
    half = C_ROPE // 2
    freqs = ROPE_THETA ** (-jnp.arange(half, dtype=F32) / half)
    ang = jnp.arange(seq, dtype=F32)[:, None] * freqs[None, :]
    cos = jnp.tile(jnp.cos(ang), (1, 2 * C_HEADS))
    sin = jnp.tile(jnp.concatenate([-jnp.sin(ang), jnp.sin(ang)], axis=1), (1, C_HEADS))
    heads = np.arange(LANES)[None, :]
    red128 = (np.arange(nn)[:, None] // C_NOPE == heads).astype(np.float32)
    red64 = (np.arange(nr)[:, None] // C_ROPE == heads).astype(np.float32)
    rep = ((np.arange(LANES)[:, None] == np.arange(nr)[None, :] % C_ROPE)
           & (np.arange(LANES)[:, None] < C_ROPE)).astype(np.float32)
    tile_gain = lambda g: jnp.tile(g, C_HEADS).reshape(1, -1)
    consts = [q_norm.reshape(1, -1), wq_p, kv_norm.reshape(1, -1), wkv_p,
              tile_gain(q_gain[:C_NOPE]), tile_gain(q_gain[C_NOPE:]),
              tile_gain(k_gain[:C_NOPE]), tile_gain(k_gain[C_NOPE:])]
    mats = [jnp.asarray(a) for a in (red128, red64, red128.T.copy(), red64.T.copy(), rep)]
    full = lambda a: pl.BlockSpec(a.shape, lambda i: (0, 0))
    nt = seq // tm
    tab = pl.BlockSpec((tm, nr), lambda i: (i % nt, 0))
    out = lambda w: jax.ShapeDtypeStruct((m, w), BF16)
    ospec = lambda w: pl.BlockSpec((tm, w), lambda i: (i, 0))
    return pl.pallas_call(
        _mla_prep_kernel,
        grid=(m // tm,),
        in_specs=[
            pl.BlockSpec((tm, C_LORA), lambda i: (i, CD_QLAT // C_LORA)),
            pl.BlockSpec((tm, C_LORA), lambda i: (i, CD_KVLAT // C_LORA)),
            pl.BlockSpec((tm, LANES), lambda i: (i, CD_PEIK // LANES)),
        ] + [full(a) for a in consts] + [tab, tab] + [full(a) for a in mats],
        out_specs=[ospec(nn), ospec(nr), ospec(nn), ospec(nr), ospec(nn)],
        out_shape=[out(nn), out(nr), out(nn), out(nr), out(nn)],
        compiler_params=_cparams(("parallel",)),
        name="mla_prep",
    )(u, u, u, *consts, cos, sin, *mats)


def _mla_attn_kernel(qn_ref, qr_ref, kn_ref, kr_ref, v_ref, o_ref, m_ref, l_ref, acc_ref, *, tq):
    h = pl.program_id(1)
    qi = pl.program_id(2)
    qn = qn_ref[...]
    lane = lax.broadcasted_iota(I32, qr_ref.shape, 1)
    qr = jnp.where((lane // HEAD64) == (h % 2), qr_ref[...], jnp.zeros_like(qr_ref[...]))
    m_ref[...] = jnp.full_like(m_ref, NEG_BIG)
    l_ref[...] = jnp.zeros_like(l_ref)
    acc_ref[...] = jnp.zeros_like(acc_ref)
    nt = (((1,), (1,)), ((), ()))

    def update(j, masked):
        rows = pl.ds(pl.multiple_of(j * tq, tq), tq)
        s = (lax.dot_general(qn, kn_ref[rows, :], nt, preferred_element_type=F32)
             + lax.dot_general(qr, kr_ref[rows, :], nt, preferred_element_type=F32)) * C_QK ** -0.5
        if masked:
            ri = lax.broadcasted_iota(I32, s.shape, 0)
            ci = lax.broadcasted_iota(I32, s.shape, 1)
            s = jnp.where(ci <= ri, s, NEG_BIG)
        m_old = m_ref[...]
        m_new = jnp.maximum(m_old, jnp.max(s, axis=-1, keepdims=True))
        alpha = jnp.exp(m_old - m_new)
        p = jnp.exp(s - m_new)
        l_ref[...] = alpha * l_ref[...] + jnp.sum(p, axis=-1, keepdims=True)
        acc_ref[...] = alpha * acc_ref[...] + jnp.dot(p.astype(BF16), v_ref[rows, :],
                                                      preferred_element_type=F32)
        m_ref[...] = m_new

    def body(j, carry):
        update(j, False)
        return carry

    lax.fori_loop(0, qi, body, 0)
    update(qi, True)
    o_ref[...] = acc_ref[...] / l_ref[...]


def _mla_attn(qn, qr, kn, kr, v, batch, tq=512):
    m = qn.shape[0]
    seq = m // batch
    tq = min(tq, seq)
    nq = seq // tq
    qspec = lambda col: pl.BlockSpec((tq, LANES), col)
    kspec = lambda col: pl.BlockSpec((seq, LANES), col)
    return pl.pallas_call(
        functools.partial(_mla_attn_kernel, tq=tq),
        grid=(batch, C_HEADS, nq),
        in_specs=[
            qspec(lambda b, h, i: (b * nq + i, h)),
            qspec(lambda b, h, i: (b * nq + i, h // 2)),
            kspec(lambda b, h, i: (b, h)),
            kspec(lambda b, h, i: (b, h // 2)),
            kspec(lambda b, h, i: (b, h)),
        ],
        out_specs=pl.BlockSpec((tq, C_V), lambda b, h, i: (b * nq + i, h)),
        out_shape=jax.ShapeDtypeStruct((m, C_HEADS * C_V), F32),
        scratch_shapes=[pltpu.VMEM((tq, 1), F32), pltpu.VMEM((tq, 1), F32), pltpu.VMEM((tq, C_V), F32)],
        compiler_params=_cparams(("parallel", "parallel", "arbitrary")),
        name="mla_attn",
    )(qn, qr, kn, kr, v)


def _rel_bucket_table(n):
    d = np.arange(n)
    max_exact = REL_BUCKETS // 2
    nf = np.maximum(d, 1).astype(np.float32)
    large = max_exact + (np.log(nf / np.float32(max_exact)) / np.float32(math.log(REL_MAX_DIST / max_exact))
                         * np.float32(REL_BUCKETS - max_exact)).astype(np.int32)
    large = np.minimum(large, REL_BUCKETS - 1)
    return np.where(d < max_exact, d, large).astype(np.int32)


def _near_bias_kernel(bucket_ref, rel_ref, o_ref):
    for var in range(2):
        bk = bucket_ref[var]
        for hd in range(D_HEADS):
            acc = jnp.zeros(bk.shape, F32)
            for b in range(REL_BUCKETS):
                acc = jnp.where(bk == b, rel_ref[b, hd], acc)
            o_ref[var, hd] = acc


def _near_bias(rel_bias):
    table = _rel_bucket_table(2 * Q_BLOCK)
    q = np.arange(Q_BLOCK)[:, None]
    j = np.arange(2 * Q_BLOCK)[None, :]
    dist0 = np.maximum(q - j, 0)
    dist1 = np.maximum(q - j + Q_BLOCK, 0)
    buckets = np.stack([table[dist0], table[dist1]]).astype(np.int32)
    return pl.pallas_call(
        _near_bias_kernel,
        in_specs=[pl.BlockSpec(memory_space=pltpu.VMEM), pl.BlockSpec(memory_space=pltpu.SMEM)],
        out_specs=pl.BlockSpec(memory_space=pltpu.VMEM),
        out_shape=jax.ShapeDtypeStruct((2, D_HEADS, Q_BLOCK, 2 * Q_BLOCK), F32),
        name="near_bias",
    )(jnp.asarray(buckets), rel_bias)


def _dsa_prep_kernel(dq_ref, dk_ref, dv_ref, iq_ref, peik_ref, iw_ref, qg_ref, kg_ref, ikg_ref,
                     q_o, k_o, v_o, iq_o, ika_o, ikb_o, iw_o):
    for hd in range(D_HEADS):
        sl = slice(D_DIM * hd, D_DIM * (hd + 1))
        q_o[:, sl] = _rms(dq_ref[:, sl], qg_ref[...]).astype(BF16)
    for hd in range(D_KV):
        sl = slice(D_DIM * hd, D_DIM * (hd + 1))
        k_o[:, sl] = _rms(dk_ref[:, sl], kg_ref[...]).astype(BF16)
    v_o[...] = dv_ref[...].astype(BF16)
    iq_o[...] = iq_ref[...].astype(BF16)
    x = peik_ref[...]
    right = lax.broadcasted_iota(I32, x.shape, 1) >= D_IDX_DIM
    x = jnp.where(right, x, 0.0)
    ms = jnp.sum(x * x, axis=-1, keepdims=True) * (1.0 / D_IDX_DIM)
    ik = (x * lax.rsqrt(ms + NORM_EPS) * ikg_ref[...]).astype(BF16)
    ikb_o[...] = ik
    ika_o[...] = pltpu.roll(ik.astype(F32), D_IDX_DIM, axis=1).astype(BF16)
    iw_o[...] = iw_ref[...] * D_HEADS ** -0.5


def _dsa_prep(u, q_gain, k_gain, ik_gain, tm=256):
    m = u.shape[0]
    tm = min(tm, m)
    blk = lambda w, off: pl.BlockSpec((tm, w), lambda i: (i, off // w))
    full = lambda a: pl.BlockSpec(a.shape, lambda i: (0, 0))
    ikg = jnp.concatenate([jnp.zeros((D_IDX_DIM,), F32), ik_gain]).reshape(1, -1)
    consts = [q_gain.reshape(1, -1), k_gain.reshape(1, -1), ikg]
    widths = [D_HEADS * D_DIM, D_KV * D_DIM, D_KV * D_DIM, D_HEADS * D_IDX_DIM, LANES, LANES, LANES]
    dtypes = [BF16] * 6 + [F32]
    return pl.pallas_call(
        _dsa_prep_kernel,
        grid=(m // tm,),
        in_specs=[blk(D_HEADS * D_DIM, CD_DQ), blk(D_KV * D_DIM, CD_DK), blk(D_KV * D_DIM, CD_DV),
                  blk(D_HEADS * D_IDX_DIM, CD_IQ), blk(LANES, CD_PEIK), blk(LANES, CD_IW)]
        + [full(a) for a in consts],
        out_specs=[pl.BlockSpec((tm, w), lambda i: (i, 0)) for w in widths],
        out_shape=[jax.ShapeDtypeStruct((m, w), dt) for w, dt in zip(widths, dtypes)],
        compiler_params=_cparams(("parallel",)),
        name="dsa_prep",
    )(u, u, u, u, u, u, *consts)


DSA_CK = 512


def _sort_key(score):
    bits = lax.bitcast_convert_type(score + 0.0, I32)
    return bits ^ ((bits >> 31) & 0x7FFFFFFF)


def _dsa_kernel(q_ref, iq_ref, iw_ref, k_ref, v_ref, ika_ref, ikb_ref, nbias_ref, fbias_ref, o_ref,
                keys_ref, keysn_ref, m_ref, l_ref, acc_ref, *, topk, pos_bits):
    qb = pl.program_id(1)
    q0 = qb * Q_BLOCK
    far_end = jnp.maximum(q0 - Q_BLOCK, 0)
    nfar = (far_end + DSA_CK - 1) // DSA_CK
    near0 = pl.multiple_of(far_end, Q_BLOCK)
    nt = (((1,), (1,)), ((), ()))
    npair = D_HEADS // 2

    iw = iw_ref[...]
    iw_cols = [iw[:, hd:hd + 1] for hd in range(D_HEADS)]

    def index_scores(rows):
        ka = ika_ref[rows, :]
        kb = ikb_ref[rows, :]
        score = None
        for p in range(npair):
            iqp = iq_ref[:, LANES * p:LANES * (p + 1)]
            for hd, kk in ((2 * p, ka), (2 * p + 1, kb)):
                d = lax.dot_general(iqp, kk, nt, preferred_element_type=F32) * D_IDX_DIM ** -0.5
                term = iw_cols[hd] * jnp.maximum(d, 0.0)
                score = term if score is None else score + term
        return score

    def far_scores(c, carry):
        rows = pl.ds(pl.multiple_of(c * DSA_CK, DSA_CK), DSA_CK)
        key = _sort_key(index_scores(rows))
        pos = c * DSA_CK + lax.broadcasted_iota(I32, key.shape, 1)
        keys_ref[c] = jnp.where(pos < far_end, key, INT_MIN)
        return carry

    lax.fori_loop(0, nfar, far_scores, 0)
    near_rows = pl.ds(near0, 2 * Q_BLOCK)
    keyn = _sort_key(index_scores(near_rows))
    posn = near0 + lax.broadcasted_iota(I32, keyn.shape, 1)
    qpos = q0 + lax.broadcasted_iota(I32, keyn.shape, 0)
    keysn_ref[...] = jnp.where(posn <= qpos, keyn, INT_MIN)

    def count(pred_far, pred_near):
        def body(c, acc):
            hit = pred_far(c, keys_ref[c]).astype(F32)
            part = hit[:, 0:LANES]
            for s in range(1, DSA_CK // LANES):
                part = part + hit[:, LANES * s:LANES * (s + 1)]
            return acc + part
        acc = lax.fori_loop(0, nfar, body, jnp.zeros((Q_BLOCK, LANES), F32))
        hit = pred_near(keysn_ref[...]).astype(F32)
        acc = acc + hit[:, 0:LANES] + hit[:, LANES:2 * LANES]
        return jnp.sum(acc, axis=-1, keepdims=True)

    count_ge = lambda cand: count(lambda c, kk: kk >= cand, lambda kk: kk >= cand)
    kf = float(topk)
    thr = jnp.where(count_ge(jnp.zeros((Q_BLOCK, 1), I32)) >= kf, 0, INT_MIN).astype(I32)

    def thr_bit(i, thr):
        cand = thr | (1 << (30 - i))
        return jnp.where(count_ge(cand) >= kf, cand, thr)

    thr = lax.fori_loop(0, 31, thr_bit, thr)
    need = kf - count(lambda c, kk: kk > thr, lambda kk: kk > thr)

    def tie_count(x):
        far = lambda c, kk: (kk == thr) & (c * DSA_CK + lax.broadcasted_iota(I32, kk.shape, 1) < x)
        near = lambda kk: (kk == thr) & (posn < x)
        return count(far, near)

    def cut_bit(i, cut):
        cand = cut | (1 << (pos_bits - 1 - i))
        return jnp.where(tie_count(cand) < need, cand, cut)

    cut = lax.fori_loop(0, pos_bits, cut_bit, jnp.zeros((Q_BLOCK, 1), I32))

    def selected(kk, pos):
        return ((kk > thr) | ((kk == thr) & (pos <= cut))) & (kk != INT_MIN)

    m_ref[...] = jnp.full_like(m_ref, NEG_BIG)
    l_ref[...] = jnp.zeros_like(l_ref)
    acc_ref[...] = jnp.zeros_like(acc_ref)

    def attend(rows, sel, bias_of):
        kc = k_ref[rows, :]
        vc = v_ref[rows, :]
        for hd in range(D_HEADS):
            g = hd // (D_HEADS // D_KV)
            gs = slice(D_DIM * g, D_DIM * (g + 1))
            s = lax.dot_general(q_ref[:, D_DIM * hd:D_DIM * (hd + 1)], kc[:, gs], nt,
                                preferred_element_type=F32) * D_DIM ** -0.5 + bias_of(hd)
            s = jnp.where(sel, s, NEG_BIG)
            m_old = m_ref[hd]
            m_new = jnp.maximum(m_old, jnp.max(s, axis=-1, keepdims=True))
            alpha = jnp.exp(m_old - m_new)
            p = jnp.where(sel, jnp.exp(s - m_new), 0.0)
            l_ref[hd] = alpha * l_ref[hd] + jnp.sum(p, axis=-1, keepdims=True)
            acc_ref[hd] = alpha * acc_ref[hd] + jnp.dot(p.astype(BF16), vc[:, gs],
                                                        preferred_element_type=F32)
            m_ref[hd] = m_new

    def far_attend(c, carry):
        rows = pl.ds(pl.multiple_of(c * DSA_CK, DSA_CK), DSA_CK)
        kk = keys_ref[c]
        pos = c * DSA_CK + lax.broadcasted_iota(I32, kk.shape, 1)
        attend(rows, selected(kk, pos), lambda hd: fbias_ref[hd])
        return carry

    lax.fori_loop(0, nfar, far_attend, 0)
    attend(near_rows, selected(keysn_ref[...], posn), lambda hd: nbias_ref[0, hd])
    for hd in range(D_HEADS):
        o_ref[:, D_DIM * hd:D_DIM * (hd + 1)] = acc_ref[hd] / l_ref[hd]


def _dsa(q, k, v, iq, ika, ikb, iw, near_bias, far_bias, batch):
    m = q.shape[0]
    seq = m // batch
    nqb = seq // Q_BLOCK
    topk = min(D_TOPK_MAX, seq // 4)
    pos_bits = int(seq).bit_length()
    nck = max(seq // DSA_CK, 1)
    qspec = lambda w: pl.BlockSpec((Q_BLOCK, w), lambda b, i: (b * nqb + i, 0))
    kspec = lambda w: pl.BlockSpec((seq, w), lambda b, i: (b, 0))
    return pl.pallas_call(
        functools.partial(_dsa_kernel, topk=topk, pos_bits=pos_bits),
        grid=(batch, nqb),
        in_specs=[qspec(D_HEADS * D_DIM), qspec(D_HEADS * D_IDX_DIM), qspec(LANES),
                  kspec(D_KV * D_DIM), kspec(D_KV * D_DIM), kspec(LANES), kspec(LANES),
                  pl.BlockSpec((1, D_HEADS, Q_BLOCK, 2 * Q_BLOCK), lambda b, i: (jnp.minimum(i, 1), 0, 0, 0)),
                  pl.BlockSpec(memory_space=pltpu.SMEM)],
        out_specs=qspec(D_HEADS * D_DIM),
        out_shape=jax.ShapeDtypeStruct((m, D_HEADS * D_DIM), F32),
        scratch_shapes=[pltpu.VMEM((nck, Q_BLOCK, DSA_CK), I32), pltpu.VMEM((Q_BLOCK, 2 * Q_BLOCK), I32),
                        pltpu.VMEM((D_HEADS, Q_BLOCK, 1), F32), pltpu.VMEM((D_HEADS, Q_BLOCK, 1), F32),
                        pltpu.VMEM((D_HEADS, Q_BLOCK, D_DIM), F32)],
        compiler_params=_cparams(("parallel", "arbitrary")),
        name="dsa",
    )(q, iq, iw, k, v, ika, ikb, near_bias, far_bias)


def _pad_cols(w, width):
    return jnp.pad(w, ((0, 0), (0, width - w.shape[1])))


def _pack_ab_in(w):
    a_cols = 3 * A_WIDTH + sum(A_LORA)
    wa, wb = w[:, :a_cols], w[:, a_cols:]
    rkv, lora = wa[:, :3 * A_WIDTH], wa[:, 3 * A_WIDTH:]
    z, xbc, dt = wb[:, :B_WIDTH], wb[:, B_WIDTH:B_WIDTH + B_CONV_CH], wb[:, B_WIDTH + B_CONV_CH:]
    return jnp.concatenate([rkv, z, xbc, _pad_cols(lora, A_LORA_PAD), _pad_cols(dt, LANES)], axis=1).astype(BF16)


def _pack_cd_in(w):
    c_cols = 2 * C_LORA + C_ROPE
    wc, wd = w[:, :c_cols], w[:, c_cols:]
    q_lat, kv_lat, k_pe = wc[:, :C_LORA], wc[:, C_LORA:2 * C_LORA], wc[:, 2 * C_LORA:]
    sizes = [D_HEADS * D_DIM, D_KV * D_DIM, D_KV * D_DIM, D_HEADS * D_IDX_DIM, D_IDX_DIM, D_HEADS]
    cuts = np.cumsum(sizes)[:-1]
    dq, dk, dv, iq, ik, iw = jnp.split(wd, [int(c) for c in cuts], axis=1)
    return jnp.concatenate([q_lat, kv_lat, dq, dk, dv, iq, k_pe, ik, _pad_cols(iw, LANES)], axis=1).astype(BF16)


def _pack_lora(w2, a2, g2):
    out, off = [], 0
    for w in (w2, a2, g2):
        out.append(jnp.pad(w, ((off, A_LORA_PAD - off - w.shape[0]), (0, 0))).astype(BF16))
        off += w.shape[0]
    return out


def _pack_mla_q(wq_b):
    w = wq_b.reshape(C_LORA, C_HEADS, C_QK)
    return jnp.concatenate([w[:, :, :C_NOPE].reshape(C_LORA, -1), w[:, :, C_NOPE:].reshape(C_LORA, -1)],
                           axis=1).astype(BF16)


def _pack_mla_kv(wkv_b):
    w = wkv_b.reshape(C_LORA, C_HEADS, C_NOPE + C_V)
    return jnp.concatenate([w[:, :, :C_NOPE].reshape(C_LORA, -1), w[:, :, C_NOPE:].reshape(C_LORA, -1)],
                           axis=1).astype(BF16)


def _layer0_mix(h, batch, norm_g, ab_w_in, ab_w_out, a_shift_mu, a_w0, a_w2, a_a0, a_a2, a_g2, a_k_k, a_k_a,
                a_r_k, a_ln_g, a_ln_b, b_conv_w, b_conv_b, b_dt_bias, b_a_log, b_d, b_norm_g):
    seq = h.shape[0] // batch
    u = _matmul([(h, 0, D_MODEL)], [_pack_ab_in(ab_w_in)], gain=norm_g)
    row = lambda t: t.reshape(1, -1)
    w2p, a2p, g2p = _pack_lora(a_w2, a_a2, a_g2)
    mu_rkv = row(a_shift_mu[:3 * A_WIDTH])
    mu_lora = _pad_cols(row(a_shift_mu[3 * A_WIDTH:]), A_LORA_PAD)
    r, w, k, v, na, nb, g = _rwkv_pre(u, seq, mu_rkv, mu_lora, row(a_w0), row(a_a0), row(a_k_k), row(a_k_a),
                                      w2p, a2p, g2p)
    y = _rwkv_scan(r, w, k, v, na, nb, batch)
    ya = _rwkv_post(y, r, k, v, g, row(a_ln_g), row(a_ln_b), row(a_r_k))
    yb = _ssd(u, batch, b_conv_w, b_conv_b, b_dt_bias, b_a_log, b_d, b_norm_g)
    w_out = ab_w_out.astype(BF16)
    return _matmul([(ya, 0, A_WIDTH), (yb, 0, B_WIDTH)], [w_out[:A_WIDTH], w_out[A_WIDTH:]], res=h)


def _layer1_mix(h, batch, norm_g, rel_bias, cd_w_in, cd_w_out, c_q_norm, c_wq_b, c_kv_norm, c_wkv_b,
                c_q_gain, c_k_gain, d_q_gain, d_k_gain, d_ik_gain):
    seq = h.shape[0] // batch
    u = _matmul([(h, 0, D_MODEL)], [_pack_cd_in(cd_w_in)], gain=norm_g, tn=256)
    qn, qr, kn, kr, v = _mla_prep(u, seq, c_q_norm, _pack_mla_q(c_wq_b), c_kv_norm, _pack_mla_kv(c_wkv_b),
                                  c_q_gain, c_k_gain)
    yc = _mla_attn(qn, qr, kn, kr, v, batch)
    dq, dk, dv, iq, ika, ikb, iw = _dsa_prep(u, d_q_gain, d_k_gain, d_ik_gain)
    yd = _dsa(dq, dk, dv, iq, ika, ikb, iw, _near_bias(rel_bias), rel_bias[REL_BUCKETS - 1], batch)
    w_out = cd_w_out.astype(BF16)
    half = C_HEADS * C_V
    return _matmul([(yc, 0, half), (yd, 0, D_HEADS * D_DIM)], [w_out[:half], w_out[half:]], res=h)


def _memory_attention(h, mem2, batch, gq, gkv, wq, wk, wv, wo, q_gain, k_gain):
    wkv = jnp.concatenate([wk, wv], axis=1).astype(BF16)
    mem_kv = _matmul([(mem2, 0, D_MODEL)], [wkv], gain=gkv)
    return _xattn(h, mem_kv, batch, gq, wq.astype(BF16), q_gain, k_gain, wo.astype(BF16))


def kernel(x, mem, rel_bias, norm_mix, norm_mem_q, norm_mem_kv, norm_ffn, xa_wq, xa_wk, xa_wv, xa_wo, xa_q_gain, xa_k_gain, ab_w_in, ab_w_out, a_shift_mu, a_w0, a_w2, a_a0, a_a2, a_g2, a_k_k, a_k_a, a_r_k, a_ln_g, a_ln_b, b_conv_w, b_conv_b, b_dt_bias, b_a_log, b_d, b_norm_g, ffn_w_gate, ffn_w_up, ffn_w_down, cd_w_in, cd_w_out, c_q_norm, c_wq_b, c_kv_norm, c_wkv_b, c_q_gain, c_k_gain, d_q_gain, d_k_gain, d_ik_gain, moe_router, moe_w_gate, moe_w_up, moe_w_down):
    batch, seq, d = x.shape
    h = x.reshape(batch * seq, d)
    mem2 = mem.reshape(-1, d)
    depth = norm_mix.shape[0]
    for layer in range(depth):
        i = layer // 2
        if layer % 2 == 0:
            h = _layer0_mix(h, batch, norm_mix[layer], ab_w_in[i], ab_w_out[i], a_shift_mu[i], a_w0[i], a_w2[i],
                            a_a0[i], a_a2[i], a_g2[i], a_k_k[i], a_k_a[i], a_r_k[i], a_ln_g[i], a_ln_b[i],
                            b_conv_w[i], b_conv_b[i], b_dt_bias[i], b_a_log[i], b_d[i], b_norm_g[i])
        else:
            h = _layer1_mix(h, batch, norm_mix[layer], rel_bias, cd_w_in[i], cd_w_out[i], c_q_norm[i], c_wq_b[i],
                            c_kv_norm[i], c_wkv_b[i], c_q_gain[i], c_k_gain[i], d_q_gain[i], d_k_gain[i],
                            d_ik_gain[i])
        h = _memory_attention(h, mem2, batch, norm_mem_q[layer], norm_mem_kv[layer], xa_wq[layer], xa_wk[layer],
                              xa_wv[layer], xa_wo[layer], xa_q_gain[layer], xa_k_gain[layer])
        if layer % 2 == 0:
            h = _swiglu(h, norm_ffn[layer], ffn_w_gate[i].astype(BF16), ffn_w_up[i].astype(BF16),
                        ffn_w_down[i].astype(BF16))
        else:
            router_p = _pad_cols(moe_router[i], LANES).astype(BF16)
            h = _moe(h, norm_ffn[layer], router_p, moe_w_gate[i].astype(BF16), moe_w_up[i].astype(BF16),
                     moe_w_down[i].astype(BF16))
    return h.reshape(batch, seq, d)
```

```python
import functools
import math

import numpy as np
import jax
import jax.numpy as jnp
from jax import lax
from jax.experimental import pallas as pl
from jax.experimental.pallas import tpu as pltpu

F32 = jnp.float32
BF16 = jnp.bfloat16
I32 = jnp.int32
HIGHEST = lax.Precision.HIGHEST

V7X_VMEM_BYTES = 64 * 1024 * 1024
VMEM_LIMIT = V7X_VMEM_BYTES - 8 * 1024 * 1024
LANES = 128

NORM_EPS = 1e-6
D_MODEL = 2048
HEAD64 = 64

A_WIDTH = 1024
A_LORA = (64, 64, 160)
A_LORA_PAD = 384
A_LN_EPS = 1e-5 * (HEAD64 / 8) ** 2
B_WIDTH = 1024
B_HEADS = 16
B_GROUPS = 4
B_STATE = 128
B_CONV = 4
B_CHUNK = 128
B_CONV_CH = B_WIDTH + 2 * B_GROUPS * B_STATE
AB_R, AB_K, AB_V, AB_Z, AB_XBC, AB_LORA, AB_DT, AB_COLS_PAD = 0, 1024, 2048, 3072, 4096, 6144, 6528, 6656

C_HEADS = 8
C_NOPE = 128
C_ROPE = 64
C_QK = C_NOPE + C_ROPE
C_V = 128
C_LORA = 512
ROPE_THETA = 10000.0
D_HEADS = 8
D_KV = 2
D_DIM = 128
D_IDX_DIM = 64
D_TOPK_MAX = 256
Q_BLOCK = 128
REL_BUCKETS = 32
REL_MAX_DIST = 128
CD_QLAT, CD_KVLAT, CD_DQ, CD_DK, CD_DV, CD_IQ, CD_PEIK, CD_IW, CD_COLS_PAD = (
    0, 512, 1024, 2048, 2304, 2560, 3072, 3200, 3328)

X_HEADS = 4
X_DIM = 128
N_EXPERTS = 8

NEG_BIG = -1e30
INT_MIN = -2 ** 31


def _cparams(sem):
    return pltpu.CompilerParams(dimension_semantics=sem, vmem_limit_bytes=VMEM_LIMIT)


def _rms(x, g, eps=NORM_EPS):
    return x * lax.rsqrt(jnp.mean(x * x, axis=-1, keepdims=True) + eps) * g


def _softplus(x):
    return jnp.maximum(x, 0.0) + jnp.log(1.0 + jnp.exp(-jnp.abs(x)))


def _silu(x):
    return x * jax.nn.sigmoid(x)


def _half_sum_bcast(x):
    left = lax.broadcasted_iota(I32, x.shape, 1) < HEAD64
    s0 = jnp.sum(jnp.where(left, x, 0.0), axis=1, keepdims=True)
    s1 = jnp.sum(jnp.where(left, 0.0, x), axis=1, keepdims=True)
    return jnp.where(left, s0, s1)


def _mm_kernel(*refs, n_x, has_norm, has_res):
    x_refs = refs[:n_x]
    pos = n_x
    g_ref = refs[pos] if has_norm else None
    pos += int(has_norm)
    w_refs = refs[pos:pos + n_x]
    pos += n_x
    res_ref = refs[pos] if has_res else None
    pos += int(has_res)
    o_ref = refs[pos]
    xn_refs = refs[pos + 1:]

    @pl.when(pl.program_id(1) == 0)
    def _():
        for x_ref, xn_ref in zip(x_refs, xn_refs):
            x = x_ref[...].astype(F32)
            if has_norm:
                x = _rms(x, g_ref[...])
            xn_ref[...] = x.astype(BF16)

    acc = None
    for xn_ref, w_ref in zip(xn_refs, w_refs):
        d = jnp.dot(xn_ref[...], w_ref[...], preferred_element_type=F32)
        acc = d if acc is None else acc + d
    if has_res:
        acc = acc + res_ref[...]
    o_ref[...] = acc


def _matmul(xs, ws, *, gain=None, res=None, tm=512, tn=512):
    m = xs[0][0].shape[0]
    n = ws[0].shape[1]
    tm = min(tm, m)
    tn = min(tn, n)
    assert m % tm == 0 and n % tn == 0
    in_specs, args, scratch = [], [], []
    for arr, cb, width in xs:
        in_specs.append(pl.BlockSpec((tm, width), lambda i, j, cb=cb: (i, cb)))
        args.append(arr)
        scratch.append(pltpu.VMEM((tm, width), BF16))
    if gain is not None:
        in_specs.append(pl.BlockSpec((1, gain.shape[-1]), lambda i, j: (0, 0)))
        args.append(gain.reshape(1, -1))
    for (arr, cb, width), w in zip(xs, ws):
        assert w.shape[0] == width
        in_specs.append(pl.BlockSpec((width, tn), lambda i, j: (0, j)))
        args.append(w)
    if res is not None:
        in_specs.append(pl.BlockSpec((tm, tn), lambda i, j: (i, j)))
        args.append(res)
    return pl.pallas_call(
        functools.partial(_mm_kernel, n_x=len(xs), has_norm=gain is not None, has_res=res is not None),
        grid=(m // tm, n // tn),
        in_specs=in_specs,
        out_specs=pl.BlockSpec((tm, tn), lambda i, j: (i, j)),
        out_shape=jax.ShapeDtypeStruct((m, n), F32),
        scratch_shapes=scratch,
        compiler_params=_cparams(("parallel", "arbitrary")),
        name="matmul",
    )(*args)


def _rwkv_pre_kernel(rkv_ref, lora_ref, rkvp_ref, lorap_ref, mu_rkv_ref, mu_lora_ref,
                     w0_ref, a0_ref, kk_ref, ka_ref, w2_ref, a2_ref, g2_ref,
                     r_o, w_o, k_o, v_o, na_o, nb_o, g_o, *, tiles_per_seq):
    first = (pl.program_id(0) % tiles_per_seq) == 0
    tm = rkv_ref.shape[0]

    def shift_mix(x, prev_rows, mu):
        prev_last = jnp.where(first, 0.0, prev_rows[7:8, :])
        xs = pltpu.roll(x, 1, axis=0)
        row = lax.broadcasted_iota(I32, x.shape, 0)
        xs = jnp.where(row == 0, prev_last, xs)
        return x + (xs - x) * mu

    lo = shift_mix(lora_ref[...], lorap_ref[...], mu_lora_ref[...])
    lane = lax.broadcasted_iota(I32, lo.shape, 1)
    act = jnp.where(lane < A_LORA[0], jnp.tanh(lo),
                    jnp.where(lane < A_LORA[0] + A_LORA[1], lo, jax.nn.sigmoid(lo))).astype(BF16)
    dw = jnp.dot(act, w2_ref[...], preferred_element_type=F32)
    da = jnp.dot(act, a2_ref[...], preferred_element_type=F32)
    g_o[...] = jnp.dot(act, g2_ref[...], preferred_element_type=F32)

    for p in range(A_WIDTH // LANES):
        sl = slice(LANES * p, LANES * (p + 1))

        def mixed(off):
            s2 = slice(off + LANES * p, off + LANES * (p + 1))
            return shift_mix(rkv_ref[:, s2], rkvp_ref[:, s2], mu_rkv_ref[:, s2])

        r_o[:, sl] = mixed(AB_R)
        v_o[:, sl] = mixed(AB_V)
        kx = mixed(AB_K)
        logw = -_softplus(-(w0_ref[:, sl] + dw[:, sl])) - 0.5
        w_o[:, sl] = jnp.exp(-jnp.exp(logw))
        a = jax.nn.sigmoid(a0_ref[:, sl] + da[:, sl])
        kk = kx * kk_ref[:, sl]
        kk = kk * lax.rsqrt(jnp.maximum(_half_sum_bcast(kk * kk), 1e-24))
        k_o[:, sl] = kx * (1.0 + (a - 1.0) * ka_ref[:, sl])
        na_o[:, sl] = -kk
        nb_o[:, sl] = kk * a


def _rwkv_pre(u, seq, mu_rkv, mu_lora, w0, a0, k_k, k_a, w2p, a2p, g2p, tm=256):
    m = u.shape[0]
    tm = min(tm, seq)
    row = lambda w: pl.BlockSpec((1, w), lambda i: (0, 0))
    full = lambda a: pl.BlockSpec(a.shape, lambda i: (0, 0))
    prev = lambda i: jnp.maximum(i * (tm // 8) - 1, 0)
    out = jax.ShapeDtypeStruct((m, A_WIDTH), F32)
    return pl.pallas_call(
        functools.partial(_rwkv_pre_kernel, tiles_per_seq=seq // tm),
        grid=(m // tm,),
        in_specs=[
            pl.BlockSpec((tm, 3 * A_WIDTH), lambda i: (i, 0)),
            pl.BlockSpec((tm, A_LORA_PAD), lambda i: (i, AB_LORA // A_LORA_PAD)),
            pl.BlockSpec((8, 3 * A_WIDTH), lambda i: (prev(i), 0)),
            pl.BlockSpec((8, A_LORA_PAD), lambda i: (prev(i), AB_LORA // A_LORA_PAD)),
            row(3 * A_WIDTH), row(A_LORA_PAD), row(A_WIDTH), row(A_WIDTH), row(A_WIDTH), row(A_WIDTH),
            full(w2p), full(a2p), full(g2p),
        ],
        out_specs=[pl.BlockSpec((tm, A_WIDTH), lambda i: (i, 0))] * 7,
        out_shape=[out] * 7,
        compiler_params=_cparams(("parallel",)),
        name="rwkv_pre",
    )(u, u, u, u, mu_rkv, mu_lora, w0, a0, k_k, k_a, w2p, a2p, g2p)


RWKV_CHUNK = 64
RWKV_PAIR_GROUP = 8


def _rwkv_scan_kernel(r_ref, w_ref, k_ref, v_ref, a_ref, b_ref, y_ref, s_ref, vt_ref, yt_ref):
    npairs = s_ref.shape[0]

    @pl.when(pl.program_id(1) == 0)
    def _():
        s_ref[...] = jnp.zeros_like(s_ref)

    lane = lax.broadcasted_iota(I32, (HEAD64, LANES), 1)
    left = lane < HEAD64
    lane64 = lane & (HEAD64 - 1)

    def pair_transpose(x):
        xt = jnp.concatenate([x, x], axis=0).T
        return jnp.where(left, xt[0:HEAD64], xt[HEAD64:2 * HEAD64])

    for p in range(npairs):
        vt_ref[p] = pair_transpose(v_ref[:, LANES * p:LANES * (p + 1)])
    yt_ref[...] = jnp.zeros_like(yt_ref)

    ri = lax.broadcasted_iota(I32, (2 * LANES, LANES), 0)
    ci = lax.broadcasted_iota(I32, (2 * LANES, LANES), 1)
    ones_blk = (((ri // HEAD64) & 1) == (ci // HEAD64)).astype(BF16)

    def half_sum_mxu(x, two_terms):
        hi = x.astype(BF16)
        if not two_terms:
            return jnp.dot(hi, ones_blk[0:LANES], preferred_element_type=F32)
        lo = (x - hi.astype(F32)).astype(BF16)
        return jnp.dot(jnp.concatenate([hi, lo], axis=1), ones_blk, preferred_element_type=F32)

    def step8(t8, carry):
        rows8 = pl.ds(pl.multiple_of(t8 * 8, 8), 8)
        unroll = (LANES - t8 * 8) & (LANES - 1)
        for g0 in range(0, npairs, RWKV_PAIR_GROUP):
            group = range(g0, min(g0 + RWKV_PAIR_GROUP, npairs))
            rows = {p: [ref[rows8, LANES * p:LANES * (p + 1)] for ref in (a_ref, w_ref, b_ref, k_ref, r_ref)]
                    for p in group}
            s = {p: s_ref[p] for p in group}
            vt8 = {p: pltpu.roll(vt_ref[p], unroll, axis=1) for p in group}
            for j in range(8):
                sel = lane64 == t8 * 8 + j
                for p in group:
                    a_row, w_row, b_row, k_row, r_row = (x8[j:j + 1, :] for x8 in rows[p])
                    sa = half_sum_mxu(s[p] * a_row, True)
                    vc = jnp.where(left, vt8[p][:, j:j + 1], vt8[p][:, HEAD64 + j:HEAD64 + j + 1])
                    s[p] = s[p] * w_row + sa * b_row + vc * k_row
                    yt_ref[p] = jnp.where(sel, half_sum_mxu(s[p] * r_row, False), yt_ref[p])
            for p in group:
                s_ref[p] = s[p]
        return carry

    lax.fori_loop(0, RWKV_CHUNK // 8, step8, 0)

    for p in range(npairs):
        y_ref[:, LANES * p:LANES * (p + 1)] = pair_transpose(yt_ref[p])


def _rwkv_scan(r, w, k, v, na, nb, batch):
    m = r.shape[0]
    seq = m // batch
    nchunk = seq // RWKV_CHUNK
    npairs = A_WIDTH // LANES
    spec = pl.BlockSpec((RWKV_CHUNK, A_WIDTH), lambda b, c: (b * nchunk + c, 0))
    pair_scratch = pltpu.VMEM((npairs, HEAD64, LANES), F32)
    return pl.pallas_call(
        _rwkv_scan_kernel,
        grid=(batch, nchunk),
        in_specs=[spec] * 6,
        out_specs=spec,
        out_shape=jax.ShapeDtypeStruct((m, A_WIDTH), F32),
        scratch_shapes=[pair_scratch, pair_scratch, pair_scratch],
        compiler_params=_cparams(("parallel", "arbitrary")),
        name="rwkv_scan",
    )(r, w, k, v, na, nb)


def _rwkv_post_kernel(y_ref, r_ref, k_ref, v_ref, g_ref, lng_ref, lnb_ref, rk_ref, o_ref):
    for p in range(A_WIDTH // LANES):
        sl = slice(LANES * p, LANES * (p + 1))
        y = y_ref[:, sl]
        mean = _half_sum_bcast(y) * (1.0 / HEAD64)
        d = y - mean
        var = _half_sum_bcast(d * d) * (1.0 / HEAD64)
        yn = d * lax.rsqrt(var + A_LN_EPS) * lng_ref[:, sl] + lnb_ref[:, sl]
        bonus = _half_sum_bcast(r_ref[:, sl] * k_ref[:, sl] * rk_ref[:, sl]) * v_ref[:, sl]
        o_ref[:, sl] = (yn + bonus) * g_ref[:, sl]


def _rwkv_post(y, r, k, v, g, ln_g, ln_b, r_k, tm=256):
    m = y.shape[0]
    tm = min(tm, m)
    spec = pl.BlockSpec((tm, A_WIDTH), lambda i: (i, 0))
    row = pl.BlockSpec((1, A_WIDTH), lambda i: (0, 0))
    return pl.pallas_call(
        _rwkv_post_kernel,
        grid=(m // tm,),
        in_specs=[spec] * 5 + [row] * 3,
        out_specs=spec,
        out_shape=jax.ShapeDtypeStruct((m, A_WIDTH), F32),
        compiler_params=_cparams(("parallel",)),
        name="rwkv_post",
    )(y, r, k, v, g, ln_g, ln_b, r_k)


def _ssd_kernel(z_ref, xbc_ref, dt_ref, cw_ref, cb_ref, dtb_ref, alog_ref, dskip_ref, ng_ref,
                expand_ref, o_ref, st_ref, tail_ref):
    lc = B_CHUNK

    @pl.when(pl.program_id(1) == 0)
    def _():
        st_ref[...] = jnp.zeros_like(st_ref)
        tail_ref[...] = jnp.zeros_like(tail_ref)

    x = xbc_ref[...]
    tail = tail_ref[...]
    row8 = lax.broadcasted_iota(I32, tail.shape, 0)
    conv = cb_ref[...] + cw_ref[B_CONV - 1:B_CONV, :] * x
    for j in range(1, B_CONV):
        xs = pltpu.roll(x, j, axis=0)
        top = jnp.where(row8 < j, pltpu.roll(tail, j, axis=0), xs[0:8])
        xs = jnp.concatenate([top, xs[8:]], axis=0)
        conv = conv + cw_ref[B_CONV - 1 - j:B_CONV - j, :] * xs
    tail_ref[...] = x[lc - 8:lc]
    act = _silu(conv)
    xs_in = act[:, 0:B_WIDTH]
    bm = act[:, B_WIDTH:B_WIDTH + B_GROUPS * B_STATE].astype(BF16)
    cm = act[:, B_WIDTH + B_GROUPS * B_STATE:].astype(BF16)

    dt = _softplus(dt_ref[...] + dtb_ref[...])
    a_neg = -jnp.exp(alog_ref[...])
    da = dt * a_neg
    ri = lax.broadcasted_iota(I32, (lc, lc), 0)
    ci = lax.broadcasted_iota(I32, (lc, lc), 1)
    causal = ci <= ri
    tri = causal.astype(F32)
    cum = jnp.dot(tri, da, precision=HIGHEST, preferred_element_type=F32)
    cum_t = jnp.dot(da.T, (ri <= ci).astype(F32), precision=HIGHEST,
                    preferred_element_type=F32)
    expand = expand_ref[...]
    widen = lambda t: jnp.dot(t, expand, precision=HIGHEST, preferred_element_type=F32)
    dt_full = widen(dt)
    ecum_full = widen(jnp.exp(cum))
    dte_full = widen(jnp.exp(cum[lc - 1:lc, :] - cum))
    xdt = xs_in * dt_full
    xdt_b = xdt.astype(BF16)
    xdte_b = (xdt * dte_full).astype(BF16)
    left = lax.broadcasted_iota(I32, (lc, LANES), 1) < HEAD64

    ys = []
    for g in range(B_GROUPS):
        gs = slice(B_STATE * g, B_STATE * (g + 1))
        cm_g = cm[:, gs]
        bm_g = bm[:, gs]
        cb = lax.dot_general(cm_g, bm_g, (((1,), (1,)), ((), ())), preferred_element_type=F32)
        bm_t = bm_g.T
        pairs_per_group = B_HEADS // B_GROUPS // 2
        for q in range(pairs_per_group):
            p = g * pairs_per_group + q
            sl = slice(LANES * p, LANES * (p + 1))
            yd = []
            for h in (2 * p, 2 * p + 1):
                seg = cum[:, h:h + 1] - cum_t[h:h + 1, :]
                dec = jnp.where(causal, jnp.exp(jnp.minimum(seg, 0.0)), 0.0)
                yd.append(jnp.dot((cb * dec).astype(BF16), xdt_b[:, sl], preferred_element_type=F32))
            y_diag = jnp.where(left, yd[0], yd[1])
            st = st_ref[p]
            y_off = jnp.dot(cm_g, st.astype(BF16), preferred_element_type=F32) * ecum_full[:, sl]
            ys.append(y_diag + y_off)
            st_ref[p] = st * ecum_full[lc - 1:lc, sl] + jnp.dot(bm_t, xdte_b[:, sl],
                                                                preferred_element_type=F32)
    y = jnp.concatenate(ys, axis=1)
    y = (y + dskip_ref[...] * xs_in) * _silu(z_ref[...])
    gw = B_WIDTH // B_GROUPS
    for g in range(B_GROUPS):
        gs = slice(gw * g, gw * (g + 1))
        yg = y[:, gs]
        o_ref[:, gs] = yg * lax.rsqrt(jnp.mean(yg * yg, axis=-1, keepdims=True) + NORM_EPS) * ng_ref[:, gs]


def _ssd(u, batch, conv_w, conv_b, dt_bias, a_log, d_skip, norm_g):
    m = u.shape[0]
    seq = m // batch
    nchunk = seq // B_CHUNK
    rows = lambda b, c: b * nchunk + c
    pad16 = lambda t: jnp.pad(t.reshape(1, -1), ((0, 0), (0, LANES - B_HEADS)))
    expand = (np.arange(LANES)[:, None] == (np.arange(B_WIDTH)[None, :] // HEAD64)).astype(np.float32)
    full = lambda a: pl.BlockSpec(a.shape, lambda b, c: (0,) * a.ndim)
    args = [conv_w, conv_b.reshape(1, -1), pad16(dt_bias), pad16(a_log),
            jnp.repeat(d_skip, HEAD64).reshape(1, -1), norm_g.reshape(1, -1), jnp.asarray(expand)]
    return pl.pallas_call(
        _ssd_kernel,
        grid=(batch, nchunk),
        in_specs=[
            pl.BlockSpec((B_CHUNK, B_WIDTH), lambda b, c: (rows(b, c), AB_Z // B_WIDTH)),
            pl.BlockSpec((B_CHUNK, B_CONV_CH), lambda b, c: (rows(b, c), AB_XBC // B_CONV_CH)),
            pl.BlockSpec((B_CHUNK, LANES), lambda b, c: (rows(b, c), AB_DT // LANES)),
        ] + [full(a) for a in args],
        out_specs=pl.BlockSpec((B_CHUNK, B_WIDTH), lambda b, c: (rows(b, c), 0)),
        out_shape=jax.ShapeDtypeStruct((m, B_WIDTH), F32),
        scratch_shapes=[pltpu.VMEM((B_HEADS // 2, B_STATE, LANES), F32),
                        pltpu.VMEM((8, B_CONV_CH), F32)],
        compiler_params=_cparams(("parallel", "arbitrary")),
        name="ssd",
    )(u, u, u, *args)


def _xattn_kernel(h_ref, gq_ref, wq_ref, kv_ref, qg_ref, kg_ref, wo_ref, o_ref):
    h = h_ref[...]
    hn = _rms(h, gq_ref[...]).astype(BF16)
    q = jnp.dot(hn, wq_ref[...], preferred_element_type=F32)
    outs = []
    for hd in range(X_HEADS):
        sl = slice(X_DIM * hd, X_DIM * (hd + 1))
        qh = _rms(q[:, sl], qg_ref[...]).astype(BF16)
        kh = _rms(kv_ref[:, sl], kg_ref[...]).astype(BF16)
        vh = kv_ref[:, X_HEADS * X_DIM + X_DIM * hd:X_HEADS * X_DIM + X_DIM * (hd + 1)].astype(BF16)
        s = lax.dot_general(qh, kh, (((1,), (1,)), ((), ())), preferred_element_type=F32) * X_DIM ** -0.5
        e = jnp.exp(s - jnp.max(s, axis=-1, keepdims=True))
        p = e / jnp.sum(e, axis=-1, keepdims=True)
        outs.append(jnp.dot(p.astype(BF16), vh, preferred_element_type=F32))
    o = jnp.concatenate(outs, axis=1).astype(BF16)
    o_ref[...] = h + jnp.dot(o, wo_ref[...], preferred_element_type=F32)


def _xattn(h, mem_kv, batch, gq, wq, q_gain, k_gain, wo, tm=512):
    m, d = h.shape
    seq = m // batch
    tm = min(tm, seq)
    mlen = mem_kv.shape[0] // batch
    nt = seq // tm
    full = lambda a: pl.BlockSpec(a.shape, lambda b, i: (0, 0))
    args = [gq.reshape(1, -1), wq, mem_kv, q_gain.reshape(1, -1), k_gain.reshape(1, -1), wo]
    specs = [full(a) for a in args]
    specs[2] = pl.BlockSpec((mlen, mem_kv.shape[1]), lambda b, i: (b, 0))
    return pl.pallas_call(
        _xattn_kernel,
        grid=(batch, nt),
        in_specs=[pl.BlockSpec((tm, d), lambda b, i: (b * nt + i, 0))] + specs,
        out_specs=pl.BlockSpec((tm, d), lambda b, i: (b * nt + i, 0)),
        out_shape=jax.ShapeDtypeStruct((m, d), F32),
        compiler_params=_cparams(("parallel", "parallel")),
        name="xattn",
    )(h, *args)


def _swiglu_kernel(h_ref, g_ref, wg_ref, wu_ref, wd_ref, o_ref, xn_ref, acc_ref):
    j = pl.program_id(1)

    @pl.when(j == 0)
    def _():
        xn_ref[...] = _rms(h_ref[...], g_ref[...]).astype(BF16)
        acc_ref[...] = jnp.zeros_like(acc_ref)

    xn = xn_ref[...]
    gate = jnp.dot(xn, wg_ref[...], preferred_element_type=F32)
    up = jnp.dot(xn, wu_ref[...], preferred_element_type=F32)
    acc_ref[...] += jnp.dot((_silu(gate) * up).astype(BF16), wd_ref[...], preferred_element_type=F32)

    @pl.when(j == pl.num_programs(1) - 1)
    def _():
        o_ref[...] = h_ref[...] + acc_ref[...]


def _swiglu(h, gain, wg, wu, wd, tm=512, tf=512):
    m, d = h.shape
    f = wg.shape[1]
    tm = min(tm, m)
    assert f % tf == 0
    return pl.pallas_call(
        _swiglu_kernel,
        grid=(m // tm, f // tf),
        in_specs=[
            pl.BlockSpec((tm, d), lambda i, j: (i, 0)),
            pl.BlockSpec((1, d), lambda i, j: (0, 0)),
            pl.BlockSpec((d, tf), lambda i, j: (0, j)),
            pl.BlockSpec((d, tf), lambda i, j: (0, j)),
            pl.BlockSpec((tf, d), lambda i, j: (j, 0)),
        ],
        out_specs=pl.BlockSpec((tm, d), lambda i, j: (i, 0)),
        out_shape=jax.ShapeDtypeStruct((m, d), F32),
        scratch_shapes=[pltpu.VMEM((tm, d), BF16), pltpu.VMEM((tm, d), F32)],
        compiler_params=_cparams(("parallel", "arbitrary")),
        name="swiglu",
    )(h, gain.reshape(1, -1), wg, wu, wd)


def _moe_kernel(h_ref, g_ref, router_ref, wg_ref, wu_ref, wd_ref, o_ref, xn_ref, acc_ref, gates_ref):
    e = pl.program_id(1)
    j = pl.program_id(2)

    @pl.when((e == 0) & (j == 0))
    def _():
        xn = _rms(h_ref[...], g_ref[...]).astype(BF16)
        xn_ref[...] = xn
        acc_ref[...] = jnp.zeros_like(acc_ref)
        logits = jnp.dot(xn, router_ref[...], preferred_element_type=F32)
        lane = lax.broadcasted_iota(I32, logits.shape, 1)
        logits = jnp.where(lane < N_EXPERTS, logits, -jnp.inf)
        m1 = jnp.max(logits, axis=-1, keepdims=True)
        i1 = jnp.min(jnp.where(logits == m1, lane, LANES), axis=-1, keepdims=True)
        rest = jnp.where(lane == i1, -jnp.inf, logits)
        m2 = jnp.max(rest, axis=-1, keepdims=True)
        i2 = jnp.min(jnp.where(rest == m2, lane, LANES), axis=-1, keepdims=True)
        e2 = jnp.exp(m2 - m1)
        w1 = 1.0 / (1.0 + e2)
        w2 = e2 / (1.0 + e2)
        gates_ref[...] = jnp.where(lane == i1, w1, 0.0) + jnp.where(lane == i2, w2, 0.0)

    xn = xn_ref[...]
    gates = gates_ref[...]
    lane = lax.broadcasted_iota(I32, gates.shape, 1)
    ge = jnp.sum(jnp.where(lane == e, gates, 0.0), axis=-1, keepdims=True)
    gate = jnp.dot(xn, wg_ref[0], preferred_element_type=F32)
    up = jnp.dot(xn, wu_ref[0], preferred_element_type=F32)
    act = (_silu(gate) * up * ge).astype(BF16)
    acc_ref[...] += jnp.dot(act, wd_ref[0], preferred_element_type=F32)

    @pl.when((e == pl.num_programs(1) - 1) & (j == pl.num_programs(2) - 1))
    def _():
        o_ref[...] = h_ref[...] + acc_ref[...]


def _moe(h, gain, router_p, wg, wu, wd, tm=512, tf=256):
    m, d = h.shape
    ne, _, f = wg.shape
    tm = min(tm, m)
    assert f % tf == 0
    return pl.pallas_call(
        _moe_kernel,
        grid=(m // tm, ne, f // tf),
        in_specs=[
            pl.BlockSpec((tm, d), lambda i, e, j: (i, 0)),
            pl.BlockSpec((1, d), lambda i, e, j: (0, 0)),
            pl.BlockSpec((d, LANES), lambda i, e, j: (0, 0)),
            pl.BlockSpec((1, d, tf), lambda i, e, j: (e, 0, j)),
            pl.BlockSpec((1, d, tf), lambda i, e, j: (e, 0, j)),
            pl.BlockSpec((1, tf, d), lambda i, e, j: (e, j, 0)),
        ],
        out_specs=pl.BlockSpec((tm, d), lambda i, e, j: (i, 0)),
        out_shape=jax.ShapeDtypeStruct((m, d), F32),
        scratch_shapes=[pltpu.VMEM((tm, d), BF16), pltpu.VMEM((tm, d), F32), pltpu.VMEM((tm, LANES), F32)],
        compiler_params=_cparams(("parallel", "arbitrary", "arbitrary")),
        name="moe",
    )(h, gain.reshape(1, -1), router_p, wg, wu, wd)


def _rope_pairs(x, cos, sin_signed):
    w = x.shape[1]
    lane = lax.broadcasted_iota(I32, x.shape, 1)
    partner = jnp.where((lane & 32) != 0, pltpu.roll(x, 32, axis=1), pltpu.roll(x, w - 32, axis=1))
    return x * cos + partner * sin_signed


def _mla_prep_kernel(ql_ref, kvl_ref, pe_ref, qn_ref, wq_ref, kvn_ref, wkv_ref, qgn_ref, qgr_ref,
                     kgn_ref, kgr_ref, cos_ref, sin_ref, red128_ref, red64_ref, exp128_ref, exp64_ref,
                     rep_ref, qn_o, qr_o, kn_o, kr_o, v_o):
    nn = C_HEADS * C_NOPE
    hi = lambda a, b: jnp.dot(a, b, precision=HIGHEST, preferred_element_type=F32)
    cos = cos_ref[...]
    sin = sin_ref[...]

    def head_norm(nope, rope_part, ss_extra):
        ss = hi(nope * nope, red128_ref[...]) + ss_extra
        rs = lax.rsqrt(ss * (1.0 / C_QK) + NORM_EPS)
        return nope * hi(rs, exp128_ref[...]), rope_part * hi(rs, exp64_ref[...])

    q = jnp.dot(_rms(ql_ref[...], qn_ref[...]).astype(BF16), wq_ref[...], preferred_element_type=F32)
    q_nope, q_rope = q[:, :nn], q[:, nn:]
    q_nope, q_rope = head_norm(q_nope, q_rope, hi(q_rope * q_rope, red64_ref[...]))
    qn_o[...] = (q_nope * qgn_ref[...]).astype(BF16)
    qr_o[...] = _rope_pairs(q_rope * qgr_ref[...], cos, sin).astype(BF16)

    kv = jnp.dot(_rms(kvl_ref[...], kvn_ref[...]).astype(BF16), wkv_ref[...], preferred_element_type=F32)
    v_o[...] = kv[:, nn:].astype(BF16)
    k_rope = hi(pe_ref[...], rep_ref[...])
    k_nope, k_rope = head_norm(kv[:, :nn], k_rope, hi(k_rope * k_rope, red64_ref[...]))
    kn_o[...] = (k_nope * kgn_ref[...]).astype(BF16)
    kr_o[...] = _rope_pairs(k_rope * kgr_ref[...], cos, sin).astype(BF16)


def _mla_prep(u, seq, q_norm, wq_p, kv_norm, wkv_p, q_gain, k_gain, tm=256):
    m = u.shape[0]
    tm = min(tm, seq)
    nn, nr = C_HEADS * C_NOPE, C_HEADS * C_ROPE
    half = C_ROPE // 2
    freqs = ROPE_THETA ** (-jnp.arange(half, dtype=F32) / half)
    ang = jnp.arange(seq, dtype=F32)[:, None] * freqs[None, :]
    cos = jnp.tile(jnp.cos(ang), (1, 2 * C_HEADS))
    sin = jnp.tile(jnp.concatenate([-jnp.sin(ang), jnp.sin(ang)], axis=1), (1, C_HEADS))
    heads = np.arange(LANES)[None, :]
    red128 = (np.arange(nn)[:, None] // C_NOPE == heads).astype(np.float32)
    red64 = (np.arange(nr)[:, None] // C_ROPE == heads).astype(np.float32)
    rep = ((np.arange(LANES)[:, None] == np.arange(nr)[None, :] % C_ROPE)
           & (np.arange(LANES)[:, None] < C_ROPE)).astype(np.float32)
    tile_gain = lambda g: jnp.tile(g, C_HEADS).reshape(1, -1)
    consts = [q_norm.reshape(1, -1), wq_p, kv_norm.reshape(1, -1), wkv_p,
              tile_gain(q_gain[:C_NOPE]), tile_gain(q_gain[C_NOPE:]),
              tile_gain(k_gain[:C_NOPE]), tile_gain(k_gain[C_NOPE:])]
    mats = [jnp.asarray(a) for a in (red128, red64, red128.T.copy(), red64.T.copy(), rep)]
    full = lambda a: pl.BlockSpec(a.shape, lambda i: (0, 0))
    nt = seq // tm
    tab = pl.BlockSpec((tm, nr), lambda i: (i % nt, 0))
    out = lambda w: jax.ShapeDtypeStruct((m, w), BF16)
    ospec = lambda w: pl.BlockSpec((tm, w), lambda i: (i, 0))
    return pl.pallas_call(
        _mla_prep_kernel,
        grid=(m // tm,),
        in_specs=[
            pl.BlockSpec((tm, C_LORA), lambda i: (i, CD_QLAT // C_LORA)),
            pl.BlockSpec((tm, C_LORA), lambda i: (i, CD_KVLAT // C_LORA)),
            pl.BlockSpec((tm, LANES), lambda i: (i, CD_PEIK // LANES)),
        ] + [full(a) for a in consts] + [tab, tab] + [full(a) for a in mats],
        out_specs=[ospec(nn), ospec(nr), ospec(nn), ospec(nr), ospec(nn)],
        out_shape=[out(nn), out(nr), out(nn), out(nr), out(nn)],
        compiler_params=_cparams(("parallel",)),
        name="mla_prep",
    )(u, u, u, *consts, cos, sin, *mats)


def _mla_attn_kernel(qn_ref, qr_ref, kn_ref, kr_ref, v_ref, o_ref, m_ref, l_ref, acc_ref, *, tq):
    h = pl.program_id(1)
    qi = pl.program_id(2)
    lane = lax.broadcasted_iota(I32, qr_ref.shape, 1)
    qr = jnp.where((lane // HEAD64) == (h % 2), qr_ref[...], jnp.zeros_like(qr_ref[...]))
    q = jnp.concatenate([qn_ref[...], qr], axis=1)
    m_ref[...] = jnp.full_like(m_ref, NEG_BIG)
    l_ref[...] = jnp.zeros_like(l_ref)
    acc_ref[...] = jnp.zeros_like(acc_ref)
    nt = (((1,), (1,)), ((), ()))

    def update(j, masked):
        rows = pl.ds(pl.multiple_of(j * tq, tq), tq)
        kc = jnp.concatenate([kn_ref[rows, :], kr_ref[rows, :]], axis=1)
        s = lax.dot_general(q, kc, nt, preferred_element_type=F32) * C_QK ** -0.5
        if masked:
            ri = lax.broadcasted_iota(I32, s.shape, 0)
            ci = lax.broadcasted_iota(I32, s.shape, 1)
            s = jnp.where(ci <= ri, s, NEG_BIG)
        m_old = m_ref[...]
        m_new = jnp.maximum(m_old, jnp.max(s, axis=-1, keepdims=True))
        alpha = jnp.exp(m_old - m_new)
        p = jnp.exp(s - m_new)
        l_ref[...] = alpha * l_ref[...] + jnp.sum(p, axis=-1, keepdims=True)
        acc_ref[...] = alpha * acc_ref[...] + jnp.dot(p.astype(BF16), v_ref[rows, :],
                                                      preferred_element_type=F32)
        m_ref[...] = m_new

    def body(j, carry):
        update(j, False)
        return carry

    lax.fori_loop(0, qi, body, 0)
    update(qi, True)
    o_ref[...] = acc_ref[...] / l_ref[...]


def _mla_attn(qn, qr, kn, kr, v, batch, tq=512):
    m = qn.shape[0]
    seq = m // batch
    tq = min(tq, seq)
    nq = seq // tq
    qspec = lambda col: pl.BlockSpec((tq, LANES), col)
    kspec = lambda col: pl.BlockSpec((seq, LANES), col)
    return pl.pallas_call(
        functools.partial(_mla_attn_kernel, tq=tq),
        grid=(batch, C_HEADS, nq),
        in_specs=[
            qspec(lambda b, h, i: (b * nq + i, h)),
            qspec(lambda b, h, i: (b * nq + i, h // 2)),
            kspec(lambda b, h, i: (b, h)),
            kspec(lambda b, h, i: (b, h // 2)),
            kspec(lambda b, h, i: (b, h)),
        ],
        out_specs=pl.BlockSpec((tq, C_V), lambda b, h, i: (b * nq + i, h)),
        out_shape=jax.ShapeDtypeStruct((m, C_HEADS * C_V), F32),
        scratch_shapes=[pltpu.VMEM((tq, 1), F32), pltpu.VMEM((tq, 1), F32), pltpu.VMEM((tq, C_V), F32)],
        compiler_params=_cparams(("parallel", "parallel", "arbitrary")),
        name="mla_attn",
    )(qn, qr, kn, kr, v)


def _rel_bucket_table(n):
    d = np.arange(n)
    max_exact = REL_BUCKETS // 2
    nf = np.maximum(d, 1).astype(np.float32)
    large = max_exact + (np.log(nf / np.float32(max_exact)) / np.float32(math.log(REL_MAX_DIST / max_exact))
                         * np.float32(REL_BUCKETS - max_exact)).astype(np.int32)
    large = np.minimum(large, REL_BUCKETS - 1)
    return np.where(d < max_exact, d, large).astype(np.int32)


def _near_bias_kernel(bucket_ref, rel_ref, o_ref):
    for var in range(2):
        bk = bucket_ref[var]
        for hd in range(D_HEADS):
            acc = jnp.zeros(bk.shape, F32)
            for b in range(REL_BUCKETS):
                acc = jnp.where(bk == b, rel_ref[b, hd], acc)
            o_ref[var, hd] = acc - rel_ref[REL_BUCKETS - 1, hd]


def _near_bias(rel_bias):
    table = _rel_bucket_table(2 * Q_BLOCK)
    q = np.arange(Q_BLOCK)[:, None]
    j = np.arange(2 * Q_BLOCK)[None, :]
    dist0 = np.maximum(q - j, 0)
    dist1 = np.maximum(q - j + Q_BLOCK, 0)
    buckets = np.stack([table[dist0], table[dist1]]).astype(np.int32)
    return pl.pallas_call(
        _near_bias_kernel,
        in_specs=[pl.BlockSpec(memory_space=pltpu.VMEM), pl.BlockSpec(memory_space=pltpu.SMEM)],
        out_specs=pl.BlockSpec(memory_space=pltpu.VMEM),
        out_shape=jax.ShapeDtypeStruct((2, D_HEADS, Q_BLOCK, 2 * Q_BLOCK), F32),
        name="near_bias",
    )(jnp.asarray(buckets), rel_bias)


def _dsa_prep_kernel(dq_ref, dk_ref, dv_ref, iq_ref, peik_ref, iw_ref, qg_ref, kg_ref, ikg_ref,
                     q_o, k_o, v_o, iq_o, ika_o, ikb_o, iw_o):
    for hd in range(D_HEADS):
        sl = slice(D_DIM * hd, D_DIM * (hd + 1))
        q_o[:, sl] = _rms(dq_ref[:, sl], qg_ref[...]).astype(BF16)
    for hd in range(D_KV):
        sl = slice(D_DIM * hd, D_DIM * (hd + 1))
        k_o[:, sl] = _rms(dk_ref[:, sl], kg_ref[...]).astype(BF16)
    v_o[...] = dv_ref[...].astype(BF16)
    iq_o[...] = iq_ref[...].astype(BF16)
    x = peik_ref[...]
    right = lax.broadcasted_iota(I32, x.shape, 1) >= D_IDX_DIM
    x = jnp.where(right, x, 0.0)
    ms = jnp.sum(x * x, axis=-1, keepdims=True) * (1.0 / D_IDX_DIM)
    ik = (x * lax.rsqrt(ms + NORM_EPS) * ikg_ref[...]).astype(BF16)
    ikb_o[...] = ik
    ika_o[...] = pltpu.roll(ik.astype(F32), D_IDX_DIM, axis=1).astype(BF16)
    iw_o[...] = iw_ref[...] * D_HEADS ** -0.5


def _dsa_prep(u, q_gain, k_gain, ik_gain, tm=256):
    m = u.shape[0]
    tm = min(tm, m)
    blk = lambda w, off: pl.BlockSpec((tm, w), lambda i: (i, off // w))
    full = lambda a: pl.BlockSpec(a.shape, lambda i: (0, 0))
    ikg = jnp.concatenate([jnp.zeros((D_IDX_DIM,), F32), ik_gain]).reshape(1, -1)
    consts = [q_gain.reshape(1, -1), k_gain.reshape(1, -1), ikg]
    widths = [D_HEADS * D_DIM, D_KV * D_DIM, D_KV * D_DIM, D_HEADS * D_IDX_DIM, LANES, LANES, LANES]
    dtypes = [BF16] * 6 + [F32]
    return pl.pallas_call(
        _dsa_prep_kernel,
        grid=(m // tm,),
        in_specs=[blk(D_HEADS * D_DIM, CD_DQ), blk(D_KV * D_DIM, CD_DK), blk(D_KV * D_DIM, CD_DV),
                  blk(D_HEADS * D_IDX_DIM, CD_IQ), blk(LANES, CD_PEIK), blk(LANES, CD_IW)]
        + [full(a) for a in consts],
        out_specs=[pl.BlockSpec((tm, w), lambda i: (i, 0)) for w in widths],
        out_shape=[jax.ShapeDtypeStruct((m, w), dt) for w, dt in zip(widths, dtypes)],
        compiler_params=_cparams(("parallel",)),
        name="dsa_prep",
    )(u, u, u, u, u, u, *consts)


DSA_CK = 512


def _sort_key(score):
    bits = lax.bitcast_convert_type(score + 0.0, I32)
    return bits ^ ((bits >> 31) & 0x7FFFFFFF)


def _dsa_kernel(q_ref, iq_ref, iw_ref, k_ref, v_ref, ika_ref, ikb_ref, nbias_ref, o_ref,
                keys_ref, keysn_ref, m_ref, l_ref, acc_ref, qs_ref, iqs_ref, *, topk, pos_bits):
    qb = pl.program_id(1)
    q0 = qb * Q_BLOCK
    far_end = jnp.maximum(q0 - Q_BLOCK, 0)
    nfar = (far_end + DSA_CK - 1) // DSA_CK
    near0 = pl.multiple_of(far_end, Q_BLOCK)
    nt = (((1,), (1,)), ((), ()))
    npair = D_HEADS // 2
    rep = D_HEADS // D_KV

    for p in range(npair):
        iqs_ref[Q_BLOCK * p:Q_BLOCK * (p + 1), :] = iq_ref[:, LANES * p:LANES * (p + 1)]
    for hd in range(D_HEADS):
        qs_ref[hd // rep, Q_BLOCK * (hd % rep):Q_BLOCK * (hd % rep + 1), :] = q_ref[:, D_DIM * hd:D_DIM * (hd + 1)]

    iw = iw_ref[...]
    iw_cols = [iw[:, hd:hd + 1] for hd in range(D_HEADS)]

    def index_scores(rows):
        iqs = iqs_ref[...]
        score = None
        for parity, k_ref_ in ((0, ika_ref), (1, ikb_ref)):
            d = lax.dot_general(iqs, k_ref_[rows, :], nt, preferred_element_type=F32) * D_IDX_DIM ** -0.5
            for p in range(npair):
                term = iw_cols[2 * p + parity] * jnp.maximum(d[Q_BLOCK * p:Q_BLOCK * (p + 1)], 0.0)
                score = term if score is None else score + term
        return score

    def far_scores(c, carry):
        rows = pl.ds(pl.multiple_of(c * DSA_CK, DSA_CK), DSA_CK)
        key = _sort_key(index_scores(rows))
        pos = c * DSA_CK + lax.broadcasted_iota(I32, key.shape, 1)
        keys_ref[c] = jnp.where(pos < far_end, key, INT_MIN)
        return carry

    lax.fori_loop(0, nfar, far_scores, 0)
    near_rows = pl.ds(near0, 2 * Q_BLOCK)
    keyn = _sort_key(index_scores(near_rows))
    posn = near0 + lax.broadcasted_iota(I32, keyn.shape, 1)
    qpos = q0 + lax.broadcasted_iota(I32, keyn.shape, 0)
    keysn_ref[...] = jnp.where(posn <= qpos, keyn, INT_MIN)

    def count(pred_far, pred_near):
        def body(c, acc):
            hit = pred_far(c, keys_ref[c]).astype(F32)
            part = hit[:, 0:LANES]
            for s in range(1, DSA_CK // LANES):
                part = part + hit[:, LANES * s:LANES * (s + 1)]
            return acc + part
        acc = lax.fori_loop(0, nfar, body, jnp.zeros((Q_BLOCK, LANES), F32))
        hit = pred_near(keysn_ref[...]).astype(F32)
        acc = acc + hit[:, 0:LANES] + hit[:, LANES:2 * LANES]
        return jnp.sum(acc, axis=-1, keepdims=True)

    count_ge = lambda cand: count(lambda c, kk: kk >= cand, lambda kk: kk >= cand)
    kf = float(topk)
    thr = jnp.where(count_ge(jnp.zeros((Q_BLOCK, 1), I32)) >= kf, 0, INT_MIN).astype(I32)

    def thr_bit(i, thr):
        cand = thr | (1 << (30 - i))
        return jnp.where(count_ge(cand) >= kf, cand, thr)

    thr = lax.fori_loop(0, 31, thr_bit, thr)
    need = kf - count(lambda c, kk: kk > thr, lambda kk: kk > thr)
    surplus = (count_ge(thr) > kf) & (thr != INT_MIN)

    def tie_count(x):
        far = lambda c, kk: (kk == thr) & (c * DSA_CK + lax.broadcasted_iota(I32, kk.shape, 1) < x)
        near = lambda kk: (kk == thr) & (posn < x)
        return count(far, near)

    def cut_bit(i, cut):
        cand = cut | (1 << (pos_bits - 1 - i))
        return jnp.where(tie_count(cand) < need, cand, cut)

    take_all = jnp.full((Q_BLOCK, 1), 2 ** pos_bits - 1, I32)
    cut = lax.cond(jnp.max(surplus.astype(F32)) > 0.0,
                   lambda: lax.fori_loop(0, pos_bits, cut_bit, jnp.zeros((Q_BLOCK, 1), I32)),
                   lambda: take_all)

    def mask_bias(kk, pos):
        sel = ((kk > thr) | ((kk == thr) & (pos <= cut))) & (kk != INT_MIN)
        return jnp.where(sel, 0.0, NEG_BIG)

    m_ref[...] = jnp.full_like(m_ref, NEG_BIG)
    l_ref[...] = jnp.zeros_like(l_ref)
    acc_ref[...] = jnp.zeros_like(acc_ref)

    def attend(rows, bias_of):
        nk = bias_of(0).shape[-1]
        for g in range(D_KV):
            gs = slice(D_DIM * g, D_DIM * (g + 1))
            s = lax.dot_general(qs_ref[g], k_ref[rows, gs], nt, preferred_element_type=F32) * D_DIM ** -0.5
            s = (s.reshape(rep, Q_BLOCK, nk) + bias_of(g)).reshape(rep * Q_BLOCK, nk)
            m_old = m_ref[g]
            m_new = jnp.maximum(m_old, jnp.max(s, axis=-1, keepdims=True))
            alpha = jnp.exp(m_old - m_new)
            p = jnp.exp(s - m_new)
            l_ref[g] = alpha * l_ref[g] + jnp.sum(p, axis=-1, keepdims=True)
            acc_ref[g] = alpha * acc_ref[g] + jnp.dot(p.astype(BF16), v_ref[rows, gs],
                                                      preferred_element_type=F32)
            m_ref[g] = m_new

    def far_attend(c, carry):
        rows = pl.ds(pl.multiple_of(c * DSA_CK, DSA_CK), DSA_CK)
        kk = keys_ref[c]
        pos = c * DSA_CK + lax.broadcasted_iota(I32, kk.shape, 1)
        mb = mask_bias(kk, pos)
        attend(rows, lambda g: mb[None])
        return carry

    lax.fori_loop(0, nfar, far_attend, 0)
    mbn = mask_bias(keysn_ref[...], posn)
    attend(near_rows, lambda g: mbn[None] + nbias_ref[0, rep * g:rep * (g + 1)])
    for hd in range(D_HEADS):
        rs = slice(Q_BLOCK * (hd % rep), Q_BLOCK * (hd % rep + 1))
        o_ref[:, D_DIM * hd:D_DIM * (hd + 1)] = acc_ref[hd // rep, rs, :] / l_ref[hd // rep, rs, :]


def _dsa(q, k, v, iq, ika, ikb, iw, near_bias, batch):
    m = q.shape[0]
    seq = m // batch
    nqb = seq // Q_BLOCK
    topk = min(D_TOPK_MAX, seq // 4)
    pos_bits = int(seq).bit_length()
    nck = max(seq // DSA_CK, 1)
    grp = D_HEADS // D_KV
    qspec = lambda w: pl.BlockSpec((Q_BLOCK, w), lambda b, i: (b * nqb + i, 0))
    kspec = lambda w: pl.BlockSpec((seq, w), lambda b, i: (b, 0))
    return pl.pallas_call(
        functools.partial(_dsa_kernel, topk=topk, pos_bits=pos_bits),
        grid=(batch, nqb),
        in_specs=[qspec(D_HEADS * D_DIM), qspec(D_HEADS * D_IDX_DIM), qspec(LANES),
                  kspec(D_KV * D_DIM), kspec(D_KV * D_DIM), kspec(LANES), kspec(LANES),
                  pl.BlockSpec((1, D_HEADS, Q_BLOCK, 2 * Q_BLOCK), lambda b, i: (jnp.minimum(i, 1), 0, 0, 0))],
        out_specs=qspec(D_HEADS * D_DIM),
        out_shape=jax.ShapeDtypeStruct((m, D_HEADS * D_DIM), F32),
        scratch_shapes=[pltpu.VMEM((nck, Q_BLOCK, DSA_CK), I32), pltpu.VMEM((Q_BLOCK, 2 * Q_BLOCK), I32),
                        pltpu.VMEM((D_KV, grp * Q_BLOCK, 1), F32), pltpu.VMEM((D_KV, grp * Q_BLOCK, 1), F32),
                        pltpu.VMEM((D_KV, grp * Q_BLOCK, D_DIM), F32),
                        pltpu.VMEM((D_KV, grp * Q_BLOCK, D_DIM), BF16),
                        pltpu.VMEM((D_HEADS // 2 * Q_BLOCK, LANES), BF16)],
        compiler_params=_cparams(("parallel", "arbitrary")),
        name="dsa",
    )(q, iq, iw, k, v, ika, ikb, near_bias)


def _pad_cols(w, width):
    return jnp.pad(w, ((0, 0), (0, width - w.shape[1])))


def _pack_ab_in(w):
    a_cols = 3 * A_WIDTH + sum(A_LORA)
    wa, wb = w[:, :a_cols], w[:, a_cols:]
    rkv, lora = wa[:, :3 * A_WIDTH], wa[:, 3 * A_WIDTH:]
    z, xbc, dt = wb[:, :B_WIDTH], wb[:, B_WIDTH:B_WIDTH + B_CONV_CH], wb[:, B_WIDTH + B_CONV_CH:]
    return jnp.concatenate([rkv, z, xbc, _pad_cols(lora, A_LORA_PAD), _pad_cols(dt, LANES)], axis=1).astype(BF16)


def _pack_cd_in(w):
    c_cols = 2 * C_LORA + C_ROPE
    wc, wd = w[:, :c_cols], w[:, c_cols:]
    q_lat, kv_lat, k_pe = wc[:, :C_LORA], wc[:, C_LORA:2 * C_LORA], wc[:, 2 * C_LORA:]
    sizes = [D_HEADS * D_DIM, D_KV * D_DIM, D_KV * D_DIM, D_HEADS * D_IDX_DIM, D_IDX_DIM, D_HEADS]
    cuts = np.cumsum(sizes)[:-1]
    dq, dk, dv, iq, ik, iw = jnp.split(wd, [int(c) for c in cuts], axis=1)
    return jnp.concatenate([q_lat, kv_lat, dq, dk, dv, iq, k_pe, ik, _pad_cols(iw, LANES)], axis=1).astype(BF16)


def _pack_lora(w2, a2, g2):
    out, off = [], 0
    for w in (w2, a2, g2):
        out.append(jnp.pad(w, ((off, A_LORA_PAD - off - w.shape[0]), (0, 0))).astype(BF16))
        off += w.shape[0]
    return out


def _pack_mla_q(wq_b):
    w = wq_b.reshape(C_LORA, C_HEADS, C_QK)
    return jnp.concatenate([w[:, :, :C_NOPE].reshape(C_LORA, -1), w[:, :, C_NOPE:].reshape(C_LORA, -1)],
                           axis=1).astype(BF16)


def _pack_mla_kv(wkv_b):
    w = wkv_b.reshape(C_LORA, C_HEADS, C_NOPE + C_V)
    return jnp.concatenate([w[:, :, :C_NOPE].reshape(C_LORA, -1), w[:, :, C_NOPE:].reshape(C_LORA, -1)],
                           axis=1).astype(BF16)


def _layer0_mix(h, batch, norm_g, ab_w_in, ab_w_out, a_shift_mu, a_w0, a_w2, a_a0, a_a2, a_g2, a_k_k, a_k_a,
                a_r_k, a_ln_g, a_ln_b, b_conv_w, b_conv_b, b_dt_bias, b_a_log, b_d, b_norm_g):
    seq = h.shape[0] // batch
    u = _matmul([(h, 0, D_MODEL)], [_pack_ab_in(ab_w_in)], gain=norm_g)
    row = lambda t: t.reshape(1, -1)
    w2p, a2p, g2p = _pack_lora(a_w2, a_a2, a_g2)
    mu_rkv = row(a_shift_mu[:3 * A_WIDTH])
    mu_lora = _pad_cols(row(a_shift_mu[3 * A_WIDTH:]), A_LORA_PAD)
    r, w, k, v, na, nb, g = _rwkv_pre(u, seq, mu_rkv, mu_lora, row(a_w0), row(a_a0), row(a_k_k), row(a_k_a),
                                      w2p, a2p, g2p)
    y = _rwkv_scan(r, w, k, v, na, nb, batch)
    ya = _rwkv_post(y, r, k, v, g, row(a_ln_g), row(a_ln_b), row(a_r_k))
    yb = _ssd(u, batch, b_conv_w, b_conv_b, b_dt_bias, b_a_log, b_d, b_norm_g)
    w_out = ab_w_out.astype(BF16)
    return _matmul([(ya, 0, A_WIDTH), (yb, 0, B_WIDTH)], [w_out[:A_WIDTH], w_out[A_WIDTH:]], res=h)


def _layer1_mix(h, batch, norm_g, rel_bias, cd_w_in, cd_w_out, c_q_norm, c_wq_b, c_kv_norm, c_wkv_b,
                c_q_gain, c_k_gain, d_q_gain, d_k_gain, d_ik_gain):
    seq = h.shape[0] // batch
    u = _matmul([(h, 0, D_MODEL)], [_pack_cd_in(cd_w_in)], gain=norm_g, tn=256)
    qn, qr, kn, kr, v = _mla_prep(u, seq, c_q_norm, _pack_mla_q(c_wq_b), c_kv_norm, _pack_mla_kv(c_wkv_b),
                                  c_q_gain, c_k_gain)
    yc = _mla_attn(qn, qr, kn, kr, v, batch)
    dq, dk, dv, iq, ika, ikb, iw = _dsa_prep(u, d_q_gain, d_k_gain, d_ik_gain)
    yd = _dsa(dq, dk, dv, iq, ika, ikb, iw, _near_bias(rel_bias), batch)
    w_out = cd_w_out.astype(BF16)
    half = C_HEADS * C_V
    return _matmul([(yc, 0, half), (yd, 0, D_HEADS * D_DIM)], [w_out[:half], w_out[half:]], res=h)


def _memory_attention(h, mem2, batch, gq, gkv, wq, wk, wv, wo, q_gain, k_gain):
    wkv = jnp.concatenate([wk, wv], axis=1).astype(BF16)
    mem_kv = _matmul([(mem2, 0, D_MODEL)], [wkv], gain=gkv)
    return _xattn(h, mem_kv, batch, gq, wq.astype(BF16), q_gain, k_gain, wo.astype(BF16))


def kernel(x, mem, rel_bias, norm_mix, norm_mem_q, norm_mem_kv, norm_ffn, xa_wq, xa_wk, xa_wv, xa_wo, xa_q_gain, xa_k_gain, ab_w_in, ab_w_out, a_shift_mu, a_w0, a_w2, a_a0, a_a2, a_g2, a_k_k, a_k_a, a_r_k, a_ln_g, a_ln_b, b_conv_w, b_conv_b, b_dt_bias, b_a_log, b_d, b_norm_g, ffn_w_gate, ffn_w_up, ffn_w_down, cd_w_in, cd_w_out, c_q_norm, c_wq_b, c_kv_norm, c_wkv_b, c_q_gain, c_k_gain, d_q_gain, d_k_gain, d_ik_gain, moe_router, moe_w_gate, moe_w_up, moe_w_down):
    batch, seq, d = x.shape
    h = x.reshape(batch * seq, d)
    mem2 = mem.reshape(-1, d)
    depth = norm_mix.shape[0]
    for layer in range(depth):
        i = layer // 2
        if layer % 2 == 0:
            h = _layer0_mix(h, batch, norm_mix[layer], ab_w_in[i], ab_w_out[i], a_shift_mu[i], a_w0[i], a_w2[i],
                            a_a0[i], a_a2[i], a_g2[i], a_k_k[i], a_k_a[i], a_r_k[i], a_ln_g[i], a_ln_b[i],
                            b_conv_w[i], b_conv_b[i], b_dt_bias[i], b_a_log[i], b_d[i], b_norm_g[i])
        else:
            h = _layer1_mix(h, batch, norm_mix[layer], rel_bias, cd_w_in[i], cd_w_out[i], c_q_norm[i], c_wq_b[i],
                            c_kv_norm[i], c_wkv_b[i], c_q_gain[i], c_k_gain[i], d_q_gain[i], d_k_gain[i],
                            d_ik_gain[i])
        h = _memory_attention(h, mem2, batch, norm_mem_q[layer], norm_mem_kv[layer], xa_wq[layer], xa_wk[layer],
                              xa_wv[layer], xa_wo[layer], xa_q_gain[layer], xa_k_gain[layer])
        if layer % 2 == 0:
            h = _swiglu(h, norm_ffn[layer], ffn_w_gate[i].astype(BF16), ffn_w_up[i].astype(BF16),
                        ffn_w_down[i].astype(BF16))
        else:
            router_p = _pad_cols(moe_router[i], LANES).astype(BF16)
            h = _moe(h, norm_ffn[layer], router_p, moe_w_gate[i].astype(BF16), moe_w_up[i].astype(BF16),
                     moe_w_down[i].astype(BF16))
    return h.reshape(batch, seq, d)
```

```python
import functools
import math

import numpy as np
import jax
import jax.numpy as jnp
from jax import lax
from jax.experimental import pallas as pl
from jax.experimental.pallas import tpu as pltpu

F32 = jnp.float32
BF16 = jnp.bfloat16
I32 = jnp.int32
HIGHEST = lax.Precision.HIGHEST

V7X_VMEM_BYTES = 64 * 1024 * 1024
VMEM_LIMIT = V7X_VMEM_BYTES - 8 * 1024 * 1024
LANES = 128

NORM_EPS = 1e-6
D_MODEL = 2048
HEAD64 = 64

A_WIDTH = 1024
A_LORA = (64, 64, 160)
A_LORA_PAD = 384
A_LN_EPS = 1e-5 * (HEAD64 / 8) ** 2
B_WIDTH = 1024
B_HEADS = 16
B_GROUPS = 4
B_STATE = 128
B_CONV = 4
B_CHUNK = 128
B_CONV_CH = B_WIDTH + 2 * B_GROUPS * B_STATE
AB_R, AB_K, AB_V, AB_Z, AB_XBC, AB_LORA, AB_DT, AB_COLS_PAD = 0, 1024, 2048, 3072, 4096, 6144, 6528, 6656

C_HEADS = 8
C_NOPE = 128
C_ROPE = 64
C_QK = C_NOPE + C_ROPE
C_V = 128
C_LORA = 512
ROPE_THETA = 10000.0
D_HEADS = 8
D_KV = 2
D_DIM = 128
D_IDX_DIM = 64
D_TOPK_MAX = 256
Q_BLOCK = 128
REL_BUCKETS = 32
REL_MAX_DIST = 128
CD_QLAT, CD_KVLAT, CD_DQ, CD_DK, CD_DV, CD_IQ, CD_PEIK, CD_IW, CD_COLS_PAD = (
    0, 512, 1024, 2048, 2304, 2560, 3072, 3200, 3328)

X_HEADS = 4
X_DIM = 128
N_EXPERTS = 8

NEG_BIG = -1e30
INT_MIN = -2 ** 31


def _cparams(sem):
    return pltpu.CompilerParams(dimension_semantics=sem, vmem_limit_bytes=VMEM_LIMIT)


def _rms(x, g, eps=NORM_EPS):
    return x * lax.rsqrt(jnp.mean(x * x, axis=-1, keepdims=True) + eps) * g


def _softplus(x):
    return jnp.maximum(x, 0.0) + jnp.log(1.0 + jnp.exp(-jnp.abs(x)))


def _silu(x):
    return x * jax.nn.sigmoid(x)


def _online_softmax_step(s, m_ref, l_ref, acc_ref):
    cols = [s[:, LANES * c:LANES * (c + 1)] for c in range(s.shape[1] // LANES)]
    mx = cols[0]
    for c in cols[1:]:
        mx = jnp.maximum(mx, c)
    m_old = m_ref[...]
    m_new = jnp.maximum(m_old, jnp.max(mx, axis=-1, keepdims=True))
    alpha = jnp.exp(m_old - m_new)
    ps = [jnp.exp(c - m_new) for c in cols]
    rs = ps[0]
    for p in ps[1:]:
        rs = rs + p
    l_ref[...] = alpha * l_ref[...] + jnp.sum(rs, axis=-1, keepdims=True)
    acc_ref[...] = alpha * acc_ref[...]
    m_ref[...] = m_new
    return jnp.concatenate(ps, axis=1).astype(BF16)


def _half_sum_bcast(x):
    left = lax.broadcasted_iota(I32, x.shape, 1) < HEAD64
    s0 = jnp.sum(jnp.where(left, x, 0.0), axis=1, keepdims=True)
    s1 = jnp.sum(jnp.where(left, 0.0, x), axis=1, keepdims=True)
    return jnp.where(left, s0, s1)


def _mm_kernel(*refs, n_x, has_norm, has_res):
    x_refs = refs[:n_x]
    pos = n_x
    g_ref = refs[pos] if has_norm else None
    pos += int(has_norm)
    w_refs = refs[pos:pos + n_x]
    pos += n_x
    res_ref = refs[pos] if has_res else None
    pos += int(has_res)
    o_ref = refs[pos]
    xn_refs = refs[pos + 1:]

    @pl.when(pl.program_id(1) == 0)
    def _():
        for x_ref, xn_ref in zip(x_refs, xn_refs):
            x = x_ref[...].astype(F32)
            if has_norm:
                x = _rms(x, g_ref[...])
            xn_ref[...] = x.astype(BF16)

    acc = None
    for xn_ref, w_ref in zip(xn_refs, w_refs):
        d = jnp.dot(xn_ref[...], w_ref[...], preferred_element_type=F32)
        acc = d if acc is None else acc + d
    if has_res:
        acc = acc + res_ref[...]
    o_ref[...] = acc


def _matmul(xs, ws, *, gain=None, res=None, tm=512, tn=512):
    m = xs[0][0].shape[0]
    n = ws[0].shape[1]
    tm = min(tm, m)
    tn = min(tn, n)
    assert m % tm == 0 and n % tn == 0
    in_specs, args, scratch = [], [], []
    for arr, cb, width in xs:
        in_specs.append(pl.BlockSpec((tm, width), lambda i, j, cb=cb: (i, cb)))
        args.append(arr)
        scratch.append(pltpu.VMEM((tm, width), BF16))
    if gain is not None:
        in_specs.append(pl.BlockSpec((1, gain.shape[-1]), lambda i, j: (0, 0)))
        args.append(gain.reshape(1, -1))
    for (arr, cb, width), w in zip(xs, ws):
        assert w.shape[0] == width
        in_specs.append(pl.BlockSpec((width, tn), lambda i, j: (0, j)))
        args.append(w)
    if res is not None:
        in_specs.append(pl.BlockSpec((tm, tn), lambda i, j: (i, j)))
        args.append(res)
    return pl.pallas_call(
        functools.partial(_mm_kernel, n_x=len(xs), has_norm=gain is not None, has_res=res is not None),
        grid=(m // tm, n // tn),
        in_specs=in_specs,
        out_specs=pl.BlockSpec((tm, tn), lambda i, j: (i, j)),
        out_shape=jax.ShapeDtypeStruct((m, n), F32),
        scratch_shapes=scratch,
        compiler_params=_cparams(("parallel", "arbitrary")),
        name="matmul",
    )(*args)


def _rwkv_pre_kernel(rkv_ref, lora_ref, rkvp_ref, lorap_ref, mu_rkv_ref, mu_lora_ref,
                     w0_ref, a0_ref, kk_ref, ka_ref, w2_ref, a2_ref, g2_ref,
                     r_o, w_o, k_o, v_o, na_o, nb_o, g_o, *, tiles_per_seq):
    first = (pl.program_id(0) % tiles_per_seq) == 0
    tm = rkv_ref.shape[0]

    def shift_mix(x, prev_rows, mu):
        prev_last = jnp.where(first, 0.0, prev_rows[7:8, :])
        xs = pltpu.roll(x, 1, axis=0)
        row = lax.broadcasted_iota(I32, x.shape, 0)
        xs = jnp.where(row == 0, prev_last, xs)
        return x + (xs - x) * mu

    lo = shift_mix(lora_ref[...], lorap_ref[...], mu_lora_ref[...])
    lane = lax.broadcasted_iota(I32, lo.shape, 1)
    act = jnp.where(lane < A_LORA[0], jnp.tanh(lo),
                    jnp.where(lane < A_LORA[0] + A_LORA[1], lo, jax.nn.sigmoid(lo))).astype(BF16)
    dw = jnp.dot(act, w2_ref[...], preferred_element_type=F32)
    da = jnp.dot(act, a2_ref[...], preferred_element_type=F32)
    g_o[...] = jnp.dot(act, g2_ref[...], preferred_element_type=F32)

    for p in range(A_WIDTH // LANES):
        sl = slice(LANES * p, LANES * (p + 1))

        def mixed(off):
            s2 = slice(off + LANES * p, off + LANES * (p + 1))
            return shift_mix(rkv_ref[:, s2], rkvp_ref[:, s2], mu_rkv_ref[:, s2])

        r_o[:, sl] = mixed(AB_R)
        v_o[:, sl] = mixed(AB_V)
        kx = mixed(AB_K)
        logw = -_softplus(-(w0_ref[:, sl] + dw[:, sl])) - 0.5
        w_o[:, sl] = jnp.exp(-jnp.exp(logw))
        a = jax.nn.sigmoid(a0_ref[:, sl] + da[:, sl])
        kk = kx * kk_ref[:, sl]
        kk = kk * lax.rsqrt(jnp.maximum(_half_sum_bcast(kk * kk), 1e-24))
        k_o[:, sl] = kx * (1.0 + (a - 1.0) * ka_ref[:, sl])
        na_o[:, sl] = -kk
        nb_o[:, sl] = kk * a


def _rwkv_pre(u, seq, mu_rkv, mu_lora, w0, a0, k_k, k_a, w2p, a2p, g2p, tm=256):
    m = u.shape[0]
    tm = min(tm, seq)
    row = lambda w: pl.BlockSpec((1, w), lambda i: (0, 0))
    full = lambda a: pl.BlockSpec(a.shape, lambda i: (0, 0))
    prev = lambda i: jnp.maximum(i * (tm // 8) - 1, 0)
    out = jax.ShapeDtypeStruct((m, A_WIDTH), F32)
    return pl.pallas_call(
        functools.partial(_rwkv_pre_kernel, tiles_per_seq=seq // tm),
        grid=(m // tm,),
        in_specs=[
            pl.BlockSpec((tm, 3 * A_WIDTH), lambda i: (i, 0)),
            pl.BlockSpec((tm, A_LORA_PAD), lambda i: (i, AB_LORA // A_LORA_PAD)),
            pl.BlockSpec((8, 3 * A_WIDTH), lambda i: (prev(i), 0)),
            pl.BlockSpec((8, A_LORA_PAD), lambda i: (prev(i), AB_LORA // A_LORA_PAD)),
            row(3 * A_WIDTH), row(A_LORA_PAD), row(A_WIDTH), row(A_WIDTH), row(A_WIDTH), row(A_WIDTH),
            full(w2p), full(a2p), full(g2p),
        ],
        out_specs=[pl.BlockSpec((tm, A_WIDTH), lambda i: (i, 0))] * 7,
        out_shape=[out] * 7,
        compiler_params=_cparams(("parallel",)),
        name="rwkv_pre",
    )(u, u, u, u, mu_rkv, mu_lora, w0, a0, k_k, k_a, w2p, a2p, g2p)


RWKV_CHUNK = 64
RWKV_PAIR_GROUP = 8


def _rwkv_scan_kernel(r_ref, w_ref, k_ref, v_ref, a_ref, b_ref, y_ref, s_ref, vt_ref, yt_ref):
    npairs = s_ref.shape[0]

    @pl.when(pl.program_id(1) == 0)
    def _():
        s_ref[...] = jnp.zeros_like(s_ref)

    lane = lax.broadcasted_iota(I32, (HEAD64, LANES), 1)
    left = lane < HEAD64
    lane64 = lane & (HEAD64 - 1)

    def pair_transpose(x):
        xt = jnp.concatenate([x, x], axis=0).T
        return jnp.where(left, xt[0:HEAD64], xt[HEAD64:2 * HEAD64])

    for p in range(npairs):
        vt_ref[p] = pair_transpose(v_ref[:, LANES * p:LANES * (p + 1)])
    yt_ref[...] = jnp.zeros_like(yt_ref)

    ri = lax.broadcasted_iota(I32, (2 * LANES, LANES), 0)
    ci = lax.broadcasted_iota(I32, (2 * LANES, LANES), 1)
    ones_blk = (((ri // HEAD64) & 1) == (ci // HEAD64)).astype(BF16)

    def half_sum_mxu(x, two_terms):
        hi = x.astype(BF16)
        if not two_terms:
            return jnp.dot(hi, ones_blk[0:LANES], preferred_element_type=F32)
        lo = (x - hi.astype(F32)).astype(BF16)
        return jnp.dot(jnp.concatenate([hi, lo], axis=1), ones_blk, preferred_element_type=F32)

    def step8(t8, carry):
        rows8 = pl.ds(pl.multiple_of(t8 * 8, 8), 8)
        unroll = (LANES - t8 * 8) & (LANES - 1)
        for g0 in range(0, npairs, RWKV_PAIR_GROUP):
            group = range(g0, min(g0 + RWKV_PAIR_GROUP, npairs))
            rows = {p: [ref[rows8, LANES * p:LANES * (p + 1)] for ref in (a_ref, w_ref, b_ref, k_ref, r_ref)]
                    for p in group}
            s = {p: s_ref[p] for p in group}
            vt8 = {p: pltpu.roll(vt_ref[p], unroll, axis=1) for p in group}
            for j in range(8):
                sel = lane64 == t8 * 8 + j
                for p in group:
                    a_row, w_row, b_row, k_row, r_row = (x8[j:j + 1, :] for x8 in rows[p])
                    sa = half_sum_mxu(s[p] * a_row, True)
                    vc = jnp.where(left, vt8[p][:, j:j + 1], vt8[p][:, HEAD64 + j:HEAD64 + j + 1])
                    s[p] = s[p] * w_row + sa * b_row + vc * k_row
                    yt_ref[p] = jnp.where(sel, half_sum_mxu(s[p] * r_row, False), yt_ref[p])
            for p in group:
                s_ref[p] = s[p]
        return carry

    lax.fori_loop(0, RWKV_CHUNK // 8, step8, 0)

    for p in range(npairs):
        y_ref[:, LANES * p:LANES * (p + 1)] = pair_transpose(yt_ref[p])


def _rwkv_scan(r, w, k, v, na, nb, batch):
    m = r.shape[0]
    seq = m // batch
    nchunk = seq // RWKV_CHUNK
    npairs = A_WIDTH // LANES
    spec = pl.BlockSpec((RWKV_CHUNK, A_WIDTH), lambda b, c: (b * nchunk + c, 0))
    pair_scratch = pltpu.VMEM((npairs, HEAD64, LANES), F32)
    return pl.pallas_call(
        _rwkv_scan_kernel,
        grid=(batch, nchunk),
        in_specs=[spec] * 6,
        out_specs=spec,
        out_shape=jax.ShapeDtypeStruct((m, A_WIDTH), F32),
        scratch_shapes=[pair_scratch, pair_scratch, pair_scratch],
        compiler_params=_cparams(("parallel", "arbitrary")),
        name="rwkv_scan",
    )(r, w, k, v, na, nb)


def _rwkv_post_kernel(y_ref, r_ref, k_ref, v_ref, g_ref, lng_ref, lnb_ref, rk_ref, o_ref):
    for p in range(A_WIDTH // LANES):
        sl = slice(LANES * p, LANES * (p + 1))
        y = y_ref[:, sl]
        mean = _half_sum_bcast(y) * (1.0 / HEAD64)
        d = y - mean
        var = _half_sum_bcast(d * d) * (1.0 / HEAD64)
        yn = d * lax.rsqrt(var + A_LN_EPS) * lng_ref[:, sl] + lnb_ref[:, sl]
        bonus = _half_sum_bcast(r_ref[:, sl] * k_ref[:, sl] * rk_ref[:, sl]) * v_ref[:, sl]
        o_ref[:, sl] = (yn + bonus) * g_ref[:, sl]


def _rwkv_post(y, r, k, v, g, ln_g, ln_b, r_k, tm=256):
    m = y.shape[0]
    tm = min(tm, m)
    spec = pl.BlockSpec((tm, A_WIDTH), lambda i: (i, 0))
    row = pl.BlockSpec((1, A_WIDTH), lambda i: (0, 0))
    return pl.pallas_call(
        _rwkv_post_kernel,
        grid=(m // tm,),
        in_specs=[spec] * 5 + [row] * 3,
        out_specs=spec,
        out_shape=jax.ShapeDtypeStruct((m, A_WIDTH), F32),
        compiler_params=_cparams(("parallel",)),
        name="rwkv_post",
    )(y, r, k, v, g, ln_g, ln_b, r_k)


def _ssd_kernel(z_ref, xbc_ref, dt_ref, cw_ref, cb_ref, dtb_ref, alog_ref, dskip_ref, ng_ref,
                expand_ref, o_ref, st_ref, tail_ref):
    lc = B_CHUNK

    @pl.when(pl.program_id(1) == 0)
    def _():
        st_ref[...] = jnp.zeros_like(st_ref)
        tail_ref[...] = jnp.zeros_like(tail_ref)

    x = xbc_ref[...]
    tail = tail_ref[...]
    row8 = lax.broadcasted_iota(I32, tail.shape, 0)
    conv = cb_ref[...] + cw_ref[B_CONV - 1:B_CONV, :] * x
    for j in range(1, B_CONV):
        xs = pltpu.roll(x, j, axis=0)
        top = jnp.where(row8 < j, pltpu.roll(tail, j, axis=0), xs[0:8])
        xs = jnp.concatenate([top, xs[8:]], axis=0)
        conv = conv + cw_ref[B_CONV - 1 - j:B_CONV - j, :] * xs
    tail_ref[...] = x[lc - 8:lc]
    act = _silu(conv)
    xs_in = act[:, 0:B_WIDTH]
    bm = act[:, B_WIDTH:B_WIDTH + B_GROUPS * B_STATE].astype(BF16)
    cm = act[:, B_WIDTH + B_GROUPS * B_STATE:].astype(BF16)

    dt = _softplus(dt_ref[...] + dtb_ref[...])
    a_neg = -jnp.exp(alog_ref[...])
    da = dt * a_neg
    ri = lax.broadcasted_iota(I32, (lc, lc), 0)
    ci = lax.broadcasted_iota(I32, (lc, lc), 1)
    causal = ci <= ri
    tri = causal.astype(F32)
    cum = jnp.dot(tri, da, precision=HIGHEST, preferred_element_type=F32)
    cum_t = jnp.dot(da.T, (ri <= ci).astype(F32), precision=HIGHEST,
                    preferred_element_type=F32)
    expand = expand_ref[...]
    widen = lambda t: jnp.dot(t, expand, precision=HIGHEST, preferred_element_type=F32)
    dt_full = widen(dt)
    ecum_full = widen(jnp.exp(cum))
    dte_full = widen(jnp.exp(cum[lc - 1:lc, :] - cum))
    xdt = xs_in * dt_full
    xdt_b = xdt.astype(BF16)
    xdte_b = (xdt * dte_full).astype(BF16)
    left = lax.broadcasted_iota(I32, (lc, LANES), 1) < HEAD64

    ys = []
    for g in range(B_GROUPS):
        gs = slice(B_STATE * g, B_STATE * (g + 1))
        cm_g = cm[:, gs]
        bm_g = bm[:, gs]
        cb = lax.dot_general(cm_g, bm_g, (((1,), (1,)), ((), ())), preferred_element_type=F32)
        bm_t = bm_g.T
        pairs_per_group = B_HEADS // B_GROUPS // 2
        for q in range(pairs_per_group):
            p = g * pairs_per_group + q
            sl = slice(LANES * p, LANES * (p + 1))
            yd = []
            for h in (2 * p, 2 * p + 1):
                seg = cum[:, h:h + 1] - cum_t[h:h + 1, :]
                dec = jnp.where(causal, jnp.exp(jnp.minimum(seg, 0.0)), 0.0)
                yd.append(jnp.dot((cb * dec).astype(BF16), xdt_b[:, sl], preferred_element_type=F32))
            y_diag = jnp.where(left, yd[0], yd[1])
            st = st_ref[p]
            y_off = jnp.dot(cm_g, st.astype(BF16), preferred_element_type=F32) * ecum_full[:, sl]
            ys.append(y_diag + y_off)
            st_ref[p] = st * ecum_full[lc - 1:lc, sl] + jnp.dot(bm_t, xdte_b[:, sl],
                                                                preferred_element_type=F32)
    y = jnp.concatenate(ys, axis=1)
    y = (y + dskip_ref[...] * xs_in) * _silu(z_ref[...])
    gw = B_WIDTH // B_GROUPS
    for g in range(B_GROUPS):
        gs = slice(gw * g, gw * (g + 1))
        yg = y[:, gs]
        o_ref[:, gs] = yg * lax.rsqrt(jnp.mean(yg * yg, axis=-1, keepdims=True) + NORM_EPS) * ng_ref[:, gs]


def _ssd(u, batch, conv_w, conv_b, dt_bias, a_log, d_skip, norm_g):
    m = u.shape[0]
    seq = m // batch
    nchunk = seq // B_CHUNK
    rows = lambda b, c: b * nchunk + c
    pad16 = lambda t: jnp.pad(t.reshape(1, -1), ((0, 0), (0, LANES - B_HEADS)))
    expand = (np.arange(LANES)[:, None] == (np.arange(B_WIDTH)[None, :] // HEAD64)).astype(np.float32)
    full = lambda a: pl.BlockSpec(a.shape, lambda b, c: (0,) * a.ndim)
    args = [conv_w, conv_b.reshape(1, -1), pad16(dt_bias), pad16(a_log),
            jnp.repeat(d_skip, HEAD64).reshape(1, -1), norm_g.reshape(1, -1), jnp.asarray(expand)]
    return pl.pallas_call(
        _ssd_kernel,
        grid=(batch, nchunk),
        in_specs=[
            pl.BlockSpec((B_CHUNK, B_WIDTH), lambda b, c: (rows(b, c), AB_Z // B_WIDTH)),
            pl.BlockSpec((B_CHUNK, B_CONV_CH), lambda b, c: (rows(b, c), AB_XBC // B_CONV_CH)),
            pl.BlockSpec((B_CHUNK, LANES), lambda b, c: (rows(b, c), AB_DT // LANES)),
        ] + [full(a) for a in args],
        out_specs=pl.BlockSpec((B_CHUNK, B_WIDTH), lambda b, c: (rows(b, c), 0)),
        out_shape=jax.ShapeDtypeStruct((m, B_WIDTH), F32),
        scratch_shapes=[pltpu.VMEM((B_HEADS // 2, B_STATE, LANES), F32),
                        pltpu.VMEM((8, B_CONV_CH), F32)],
        compiler_params=_cparams(("parallel", "arbitrary")),
        name="ssd",
    )(u, u, u, *args)


def _xattn_kernel(h_ref, gq_ref, wq_ref, kv_ref, qg_ref, kg_ref, wo_ref, o_ref):
    h = h_ref[...]
    hn = _rms(h, gq_ref[...]).astype(BF16)
    q = jnp.dot(hn, wq_ref[...], preferred_element_type=F32)
    outs = []
    for hd in range(X_HEADS):
        sl = slice(X_DIM * hd, X_DIM * (hd + 1))
        qh = _rms(q[:, sl], qg_ref[...]).astype(BF16)
        kh = _rms(kv_ref[:, sl], kg_ref[...]).astype(BF16)
        vh = kv_ref[:, X_HEADS * X_DIM + X_DIM * hd:X_HEADS * X_DIM + X_DIM * (hd + 1)].astype(BF16)
        s = lax.dot_general(qh, kh, (((1,), (1,)), ((), ())), preferred_element_type=F32) * X_DIM ** -0.5
        e = jnp.exp(s - jnp.max(s, axis=-1, keepdims=True))
        p = e / jnp.sum(e, axis=-1, keepdims=True)
        outs.append(jnp.dot(p.astype(BF16), vh, preferred_element_type=F32))
    o = jnp.concatenate(outs, axis=1).astype(BF16)
    o_ref[...] = h + jnp.dot(o, wo_ref[...], preferred_element_type=F32)


def _xattn(h, mem_kv, batch, gq, wq, q_gain, k_gain, wo, tm=512):
    m, d = h.shape
    seq = m // batch
    tm = min(tm, seq)
    mlen = mem_kv.shape[0] // batch
    nt = seq // tm
    full = lambda a: pl.BlockSpec(a.shape, lambda b, i: (0, 0))
    args = [gq.reshape(1, -1), wq, mem_kv, q_gain.reshape(1, -1), k_gain.reshape(1, -1), wo]
    specs = [full(a) for a in args]
    specs[2] = pl.BlockSpec((mlen, mem_kv.shape[1]), lambda b, i: (b, 0))
    return pl.pallas_call(
        _xattn_kernel,
        grid=(batch, nt),
        in_specs=[pl.BlockSpec((tm, d), lambda b, i: (b * nt + i, 0))] + specs,
        out_specs=pl.BlockSpec((tm, d), lambda b, i: (b * nt + i, 0)),
        out_shape=jax.ShapeDtypeStruct((m, d), F32),
        compiler_params=_cparams(("parallel", "parallel")),
        name="xattn",
    )(h, *args)


def _swiglu_kernel(h_ref, g_ref, wg_ref, wu_ref, wd_ref, o_ref, xn_ref, acc_ref):
    j = pl.program_id(1)

    @pl.when(j == 0)
    def _():
        xn_ref[...] = _rms(h_ref[...], g_ref[...]).astype(BF16)
        acc_ref[...] = jnp.zeros_like(acc_ref)

    xn = xn_ref[...]
    gate = jnp.dot(xn, wg_ref[...], preferred_element_type=F32)
    up = jnp.dot(xn, wu_ref[...], preferred_element_type=F32)
    acc_ref[...] += jnp.dot((_silu(gate) * up).astype(BF16), wd_ref[...], preferred_element_type=F32)

    @pl.when(j == pl.num_programs(1) - 1)
    def _():
        o_ref[...] = h_ref[...] + acc_ref[...]


def _swiglu(h, gain, wg, wu, wd, tm=512, tf=512):
    m, d = h.shape
    f = wg.shape[1]
    tm = min(tm, m)
    assert f % tf == 0
    return pl.pallas_call(
        _swiglu_kernel,
        grid=(m // tm, f // tf),
        in_specs=[
            pl.BlockSpec((tm, d), lambda i, j: (i, 0)),
            pl.BlockSpec((1, d), lambda i, j: (0, 0)),
            pl.BlockSpec((d, tf), lambda i, j: (0, j)),
            pl.BlockSpec((d, tf), lambda i, j: (0, j)),
            pl.BlockSpec((tf, d), lambda i, j: (j, 0)),
        ],
        out_specs=pl.BlockSpec((tm, d), lambda i, j: (i, 0)),
        out_shape=jax.ShapeDtypeStruct((m, d), F32),
        scratch_shapes=[pltpu.VMEM((tm, d), BF16), pltpu.VMEM((tm, d), F32)],
        compiler_params=_cparams(("parallel", "arbitrary")),
        name="swiglu",
    )(h, gain.reshape(1, -1), wg, wu, wd)


def _moe_kernel(h_ref, g_ref, router_ref, wg_ref, wu_ref, wd_ref, o_ref, xn_ref, acc_ref, gates_ref):
    e = pl.program_id(1)
    j = pl.program_id(2)

    @pl.when((e == 0) & (j == 0))
    def _():
        xn = _rms(h_ref[...], g_ref[...]).astype(BF16)
        xn_ref[...] = xn
        acc_ref[...] = jnp.zeros_like(acc_ref)
        logits = jnp.dot(xn, router_ref[...], preferred_element_type=F32)
        lane = lax.broadcasted_iota(I32, logits.shape, 1)
        logits = jnp.where(lane < N_EXPERTS, logits, -jnp.inf)
        m1 = jnp.max(logits, axis=-1, keepdims=True)
        i1 = jnp.min(jnp.where(logits == m1, lane, LANES), axis=-1, keepdims=True)
        rest = jnp.where(lane == i1, -jnp.inf, logits)
        m2 = jnp.max(rest, axis=-1, keepdims=True)
        i2 = jnp.min(jnp.where(rest == m2, lane, LANES), axis=-1, keepdims=True)
        e2 = jnp.exp(m2 - m1)
        w1 = 1.0 / (1.0 + e2)
        w2 = e2 / (1.0 + e2)
        gates_ref[...] = jnp.where(lane == i1, w1, 0.0) + jnp.where(lane == i2, w2, 0.0)

    xn = xn_ref[...]
    gates = gates_ref[...]
    lane = lax.broadcasted_iota(I32, gates.shape, 1)
    ge = jnp.sum(jnp.where(lane == e, gates, 0.0), axis=-1, keepdims=True)
    gate = jnp.dot(xn, wg_ref[0], preferred_element_type=F32)
    up = jnp.dot(xn, wu_ref[0], preferred_element_type=F32)
    act = (_silu(gate) * up * ge).astype(BF16)
    acc_ref[...] += jnp.dot(act, wd_ref[0], preferred_element_type=F32)

    @pl.when((e == pl.num_programs(1) - 1) & (j == pl.num_programs(2) - 1))
    def _():
        o_ref[...] = h_ref[...] + acc_ref[...]


def _moe(h, gain, router_p, wg, wu, wd, tm=512, tf=256):
    m, d = h.shape
    ne, _, f = wg.shape
    tm = min(tm, m)
    assert f % tf == 0
    return pl.pallas_call(
        _moe_kernel,
        grid=(m // tm, ne, f // tf),
        in_specs=[
            pl.BlockSpec((tm, d), lambda i, e, j: (i, 0)),
            pl.BlockSpec((1, d), lambda i, e, j: (0, 0)),
            pl.BlockSpec((d, LANES), lambda i, e, j: (0, 0)),
            pl.BlockSpec((1, d, tf), lambda i, e, j: (e, 0, j)),
            pl.BlockSpec((1, d, tf), lambda i, e, j: (e, 0, j)),
            pl.BlockSpec((1, tf, d), lambda i, e, j: (e, j, 0)),
        ],
        out_specs=pl.BlockSpec((tm, d), lambda i, e, j: (i, 0)),
        out_shape=jax.ShapeDtypeStruct((m, d), F32),
        scratch_shapes=[pltpu.VMEM((tm, d), BF16), pltpu.VMEM((tm, d), F32), pltpu.VMEM((tm, LANES), F32)],
        compiler_params=_cparams(("parallel", "arbitrary", "arbitrary")),
        name="moe",
    )(h, gain.reshape(1, -1), router_p, wg, wu, wd)


def _rope_pairs(x, cos, sin_signed):
    w = x.shape[1]
    lane = lax.broadcasted_iota(I32, x.shape, 1)
    partner = jnp.where((lane & 32) != 0, pltpu.roll(x, 32, axis=1), pltpu.roll(x, w - 32, axis=1))
    return x * cos + partner * sin_signed


def _mla_prep_kernel(ql_ref, kvl_ref, pe_ref, qn_ref, wq_ref, kvn_ref, wkv_ref, qgn_ref, qgr_ref,
                     kgn_ref, kgr_ref, cos_ref, sin_ref, red128_ref, red64_ref, exp128_ref, exp64_ref,
                     rep_ref, qn_o, qr_o, kn_o, kr_o, v_o):
    nn = C_HEADS * C_NOPE
    hi = lambda a, b: jnp.dot(a, b, precision=HIGHEST, preferred_element_type=F32)
    cos = cos_ref[...]
    sin = sin_ref[...]

    def head_norm(nope, rope_part, ss_extra):
        ss = hi(nope * nope, red128_ref[...]) + ss_extra
        rs = lax.rsqrt(ss * (1.0 / C_QK) + NORM_EPS)
        return nope * hi(rs, exp128_ref[...]), rope_part * hi(rs, exp64_ref[...])

    q = jnp.dot(_rms(ql_ref[...], qn_ref[...]).astype(BF16), wq_ref[...], preferred_element_type=F32)
    q_nope, q_rope = q[:, :nn], q[:, nn:]
    q_nope, q_rope = head_norm(q_nope, q_rope, hi(q_rope * q_rope, red64_ref[...]))
    qn_o[...] = (q_nope * qgn_ref[...]).astype(BF16)
    qr_o[...] = _rope_pairs(q_rope * qgr_ref[...], cos, sin).astype(BF16)

    kv = jnp.dot(_rms(kvl_ref[...], kvn_ref[...]).astype(BF16), wkv_ref[...], preferred_element_type=F32)
    v_o[...] = kv[:, nn:].astype(BF16)
    k_rope = hi(pe_ref[...], rep_ref[...])
    k_nope, k_rope = head_norm(kv[:, :nn], k_rope, hi(k_rope * k_rope, red64_ref[...]))
    kn_o[...] = (k_nope * kgn_ref[...]).astype(BF16)
    kr_o[...] = _rope_pairs(k_rope * kgr_ref[...], cos, sin).astype(BF16)


def _mla_prep(u, seq, q_norm, wq_p, kv_norm, wkv_p, q_gain, k_gain, tm=256):
    m = u.shape[0]
    tm = min(tm, seq)
    nn, nr = C_HEADS * C_NOPE, C_HEADS * C_ROPE
    half = C_ROPE // 2
    freqs = ROPE_THETA ** (-jnp.arange(half, dtype=F32) / half)
    ang = jnp.arange(seq, dtype=F32)[:, None] * freqs[None, :]
    cos = jnp.tile(jnp.cos(ang), (1, 2 * C_HEADS))
    sin = jnp.tile(jnp.concatenate([-jnp.sin(ang), jnp.sin(ang)], axis=1), (1, C_HEADS))
    heads = np.arange(LANES)[None, :]
    red128 = (np.arange(nn)[:, None] // C_NOPE == heads).astype(np.float32)
    red64 = (np.arange(nr)[:, None] // C_ROPE == heads).astype(np.float32)
    rep = ((np.arange(LANES)[:, None] == np.arange(nr)[None, :] % C_ROPE)
           & (np.arange(LANES)[:, None] < C_ROPE)).astype(np.float32)
    tile_gain = lambda g: jnp.tile(g, C_HEADS).reshape(1, -1)
    consts = [q_norm.reshape(1, -1), wq_p, kv_norm.reshape(1, -1), wkv_p,
              tile_gain(q_gain[:C_NOPE]), tile_gain(q_gain[C_NOPE:]),
              tile_gain(k_gain[:C_NOPE]), tile_gain(k_gain[C_NOPE:])]
    mats = [jnp.asarray(a) for a in (red128, red64, red128.T.copy(), red64.T.copy(), rep)]
    full = lambda a: pl.BlockSpec(a.shape, lambda i: (0, 0))
    nt = seq // tm
    tab = pl.BlockSpec((tm, nr), lambda i: (i % nt, 0))
    out = lambda w: jax.ShapeDtypeStruct((m, w), BF16)
    ospec = lambda w: pl.BlockSpec((tm, w), lambda i: (i, 0))
    return pl.pallas_call(
        _mla_prep_kernel,
        grid=(m // tm,),
        in_specs=[
            pl.BlockSpec((tm, C_LORA), lambda i: (i, CD_QLAT // C_LORA)),
            pl.BlockSpec((tm, C_LORA), lambda i: (i, CD_KVLAT // C_LORA)),
            pl.BlockSpec((tm, LANES), lambda i: (i, CD_PEIK // LANES)),
        ] + [full(a) for a in consts] + [tab, tab] + [full(a) for a in mats],
        out_specs=[ospec(nn), ospec(nr), ospec(nn), ospec(nr), ospec(nn)],
        out_shape=[out(nn), out(nr), out(nn), out(nr), out(nn)],
        compiler_params=_cparams(("parallel",)),
        name="mla_prep",
    )(u, u, u, *consts, cos, sin, *mats)


def _mla_attn_kernel(qn_ref, qr_ref, kn_ref, kr_ref, v_ref, o_ref, m_ref, l_ref, acc_ref, *, tq):
    qi = pl.program_id(2)
    lane = lax.broadcasted_iota(I32, qr_ref.shape, 1)
    qr = qr_ref[...]
    qs = []
    for hh in range(2):
        qr_h = jnp.where((lane // HEAD64) == hh, qr, jnp.zeros_like(qr))
        qs.append(jnp.concatenate([qn_ref[:, C_NOPE * hh:C_NOPE * (hh + 1)], qr_h], axis=1))
    m_ref[...] = jnp.full_like(m_ref, NEG_BIG)
    l_ref[...] = jnp.zeros_like(l_ref)
    acc_ref[...] = jnp.zeros_like(acc_ref)
    nt = (((1,), (1,)), ((), ()))

    def update(j, masked):
        rows = pl.ds(pl.multiple_of(j * tq, tq), tq)
        kr = kr_ref[rows, :]
        ss = []
        for hh in range(2):
            kc = jnp.concatenate([kn_ref[rows, C_NOPE * hh:C_NOPE * (hh + 1)], kr], axis=1)
            ss.append(lax.dot_general(qs[hh], kc, nt, preferred_element_type=F32) * C_QK ** -0.5)
        for hh in range(2):
            s = ss[hh]
            if masked:
                ri = lax.broadcasted_iota(I32, s.shape, 0)
                ci = lax.broadcasted_iota(I32, s.shape, 1)
                s = jnp.where(ci <= ri, s, NEG_BIG)
            p = _online_softmax_step(s, m_ref.at[hh], l_ref.at[hh], acc_ref.at[hh])
            acc_ref[hh] += jnp.dot(p, v_ref[rows, C_V * hh:C_V * (hh + 1)], preferred_element_type=F32)

    def body(j, carry):
        update(j, False)
        return carry

    lax.fori_loop(0, qi, body, 0)
    update(qi, True)
    for hh in range(2):
        o_ref[:, C_V * hh:C_V * (hh + 1)] = acc_ref[hh] / l_ref[hh]


def _mla_attn(qn, qr, kn, kr, v, batch, tq=512):
    m = qn.shape[0]
    seq = m // batch
    tq = min(tq, seq)
    nq = seq // tq
    qspec = lambda w: pl.BlockSpec((tq, w), lambda b, p, i: (b * nq + i, p))
    kspec = lambda w: pl.BlockSpec((seq, w), lambda b, p, i: (b, p))
    return pl.pallas_call(
        functools.partial(_mla_attn_kernel, tq=tq),
        grid=(batch, C_HEADS // 2, nq),
        in_specs=[qspec(2 * C_NOPE), qspec(2 * C_ROPE), kspec(2 * C_NOPE), kspec(2 * C_ROPE), kspec(2 * C_V)],
        out_specs=qspec(2 * C_V),
        out_shape=jax.ShapeDtypeStruct((m, C_HEADS * C_V), F32),
        scratch_shapes=[pltpu.VMEM((2, tq, LANES), F32), pltpu.VMEM((2, tq, LANES), F32),
                        pltpu.VMEM((2, tq, C_V), F32)],
        compiler_params=_cparams(("parallel", "parallel", "arbitrary")),
        name="mla_attn",
    )(qn, qr, kn, kr, v)


def _rel_bucket_table(n):
    d = np.arange(n)
    max_exact = REL_BUCKETS // 2
    nf = np.maximum(d, 1).astype(np.float32)
    large = max_exact + (np.log(nf / np.float32(max_exact)) / np.float32(math.log(REL_MAX_DIST / max_exact))
                         * np.float32(REL_BUCKETS - max_exact)).astype(np.int32)
    large = np.minimum(large, REL_BUCKETS - 1)
    return np.where(d < max_exact, d, large).astype(np.int32)


def _near_bias_kernel(bucket_ref, rel_ref, o_ref):
    for var in range(2):
        bk = bucket_ref[var]
        for hd in range(D_HEADS):
            acc = jnp.zeros(bk.shape, F32)
            for b in range(REL_BUCKETS):
                acc = jnp.where(bk == b, rel_ref[b, hd], acc)
            o_ref[var, hd] = acc - rel_ref[REL_BUCKETS - 1, hd]


def _near_bias(rel_bias):
    table = _rel_bucket_table(2 * Q_BLOCK)
    q = np.arange(Q_BLOCK)[:, None]
    j = np.arange(2 * Q_BLOCK)[None, :]
    dist0 = np.maximum(q - j, 0)
    dist1 = np.maximum(q - j + Q_BLOCK, 0)
    buckets = np.stack([table[dist0], table[dist1]]).astype(np.int32)
    return pl.pallas_call(
        _near_bias_kernel,
        in_specs=[pl.BlockSpec(memory_space=pltpu.VMEM), pl.BlockSpec(memory_space=pltpu.SMEM)],
        out_specs=pl.BlockSpec(memory_space=pltpu.VMEM),
        out_shape=jax.ShapeDtypeStruct((2, D_HEADS, Q_BLOCK, 2 * Q_BLOCK), F32),
        name="near_bias",
    )(jnp.asarray(buckets), rel_bias)


def _dsa_prep_kernel(dq_ref, dk_ref, dv_ref, iq_ref, peik_ref, iw_ref, qg_ref, kg_ref, ikg_ref,
                     q_o, k_o, v_o, iq_o, ika_o, ikb_o, iw_o):
    for hd in range(D_HEADS):
        sl = slice(D_DIM * hd, D_DIM * (hd + 1))
        q_o[:, sl] = _rms(dq_ref[:, sl], qg_ref[...]).astype(BF16)
    for hd in range(D_KV):
        sl = slice(D_DIM * hd, D_DIM * (hd + 1))
        k_o[:, sl] = _rms(dk_ref[:, sl], kg_ref[...]).astype(BF16)
    v_o[...] = dv_ref[...].astype(BF16)
    iq_o[...] = iq_ref[...].astype(BF16)
    x = peik_ref[...]
    right = lax.broadcasted_iota(I32, x.shape, 1) >= D_IDX_DIM
    x = jnp.where(right, x, 0.0)
    ms = jnp.sum(x * x, axis=-1, keepdims=True) * (1.0 / D_IDX_DIM)
    ik = (x * lax.rsqrt(ms + NORM_EPS) * ikg_ref[...]).astype(BF16)
    ikb_o[...] = ik
    ika_o[...] = pltpu.roll(ik.astype(F32), D_IDX_DIM, axis=1).astype(BF16)
    iw_o[...] = iw_ref[...] * D_HEADS ** -0.5


def _dsa_prep(u, q_gain, k_gain, ik_gain, tm=256):
    m = u.shape[0]
    tm = min(tm, m)
    blk = lambda w, off: pl.BlockSpec((tm, w), lambda i: (i, off // w))
    full = lambda a: pl.BlockSpec(a.shape, lambda i: (0, 0))
    ikg = jnp.concatenate([jnp.zeros((D_IDX_DIM,), F32), ik_gain]).reshape(1, -1)
    consts = [q_gain.reshape(1, -1), k_gain.reshape(1, -1), ikg]
    widths = [D_HEADS * D_DIM, D_KV * D_DIM, D_KV * D_DIM, D_HEADS * D_IDX_DIM, LANES, LANES, LANES]
    dtypes = [BF16] * 6 + [F32]
    return pl.pallas_call(
        _dsa_prep_kernel,
        grid=(m // tm,),
        in_specs=[blk(D_HEADS * D_DIM, CD_DQ), blk(D_KV * D_DIM, CD_DK), blk(D_KV * D_DIM, CD_DV),
                  blk(D_HEADS * D_IDX_DIM, CD_IQ), blk(LANES, CD_PEIK), blk(LANES, CD_IW)]
        + [full(a) for a in consts],
        out_specs=[pl.BlockSpec((tm, w), lambda i: (i, 0)) for w in widths],
        out_shape=[jax.ShapeDtypeStruct((m, w), dt) for w, dt in zip(widths, dtypes)],
        compiler_params=_cparams(("parallel",)),
        name="dsa_prep",
    )(u, u, u, u, u, u, *consts)


DSA_CK = 512


def _sort_key(score):
    bits = lax.bitcast_convert_type(score + 0.0, I32)
    return bits ^ ((bits >> 31) & 0x7FFFFFFF)


def _dsa_kernel(q_ref, iq_ref, iw_ref, k_ref, v_ref, ika_ref, ikb_ref, nbias_ref, o_ref,
                keys_ref, m_ref, l_ref, acc_ref, qs_ref, iqs_ref, *, topk, pos_bits):
    qb = pl.program_id(1)
    q0 = qb * Q_BLOCK
    far_end = jnp.maximum(q0 - Q_BLOCK, 0)
    nfar = (far_end + DSA_CK - 1) // DSA_CK
    near0 = pl.multiple_of(far_end, Q_BLOCK)
    nt = (((1,), (1,)), ((), ()))
    npair = D_HEADS // 2
    rep = D_HEADS // D_KV

    for p in range(npair):
        iqs_ref[Q_BLOCK * p:Q_BLOCK * (p + 1), :] = iq_ref[:, LANES * p:LANES * (p + 1)]
    for hd in range(D_HEADS):
        qs_ref[hd // rep, Q_BLOCK * (hd % rep):Q_BLOCK * (hd % rep + 1), :] = q_ref[:, D_DIM * hd:D_DIM * (hd + 1)]

    iw = iw_ref[...]
    iw_cols = [iw[:, hd:hd + 1] for hd in range(D_HEADS)]

    def index_scores(rows):
        iqs = iqs_ref[...]
        score = None
        for parity, k_ref_ in ((0, ika_ref), (1, ikb_ref)):
            d = lax.dot_general(iqs, k_ref_[rows, :], nt, preferred_element_type=F32) * D_IDX_DIM ** -0.5
            for p in range(npair):
                term = iw_cols[2 * p + parity] * jnp.maximum(d[Q_BLOCK * p:Q_BLOCK * (p + 1)], 0.0)
                score = term if score is None else score + term
        return score

    def far_scores(c, carry):
        rows = pl.ds(pl.multiple_of(c * DSA_CK, DSA_CK), DSA_CK)
        key = _sort_key(index_scores(rows))
        pos = c * DSA_CK + lax.broadcasted_iota(I32, key.shape, 1)
        keys_ref[c] = jnp.where(pos < far_end, key, INT_MIN)
        return carry

    lax.fori_loop(0, nfar, far_scores, 0)
    near_rows = pl.ds(near0, 2 * Q_BLOCK)
    keyn = _sort_key(index_scores(near_rows))
    posn = near0 + lax.broadcasted_iota(I32, keyn.shape, 1)
    qpos = q0 + lax.broadcasted_iota(I32, keyn.shape, 0)
    keyn = jnp.where(posn <= qpos, keyn, INT_MIN)
    keys_ref[nfar] = jnp.concatenate([keyn, jnp.full((Q_BLOCK, DSA_CK - 2 * Q_BLOCK), INT_MIN, I32)], axis=1)

    def chunk_pos(c):
        base = jnp.where(c == nfar, near0, c * DSA_CK)
        return base + lax.broadcasted_iota(I32, (Q_BLOCK, DSA_CK), 1)

    def count(pred):
        def body(c, acc):
            hit = pred(c, keys_ref[c]).astype(F32)
            part = hit[:, 0:LANES]
            for s in range(1, DSA_CK // LANES):
                part = part + hit[:, LANES * s:LANES * (s + 1)]
            return acc + part
        acc = lax.fori_loop(0, nfar + 1, body, jnp.zeros((Q_BLOCK, LANES), F32))
        return jnp.sum(acc, axis=-1, keepdims=True)

    count_ge = lambda cand: count(lambda c, kk: kk >= cand)
    kf = float(topk)
    thr = jnp.where(count_ge(jnp.zeros((Q_BLOCK, 1), I32)) >= kf, 0, INT_MIN).astype(I32)

    def thr_bit(i, thr):
        cand = thr | (1 << (30 - i))
        return jnp.where(count_ge(cand) >= kf, cand, thr)

    thr = lax.fori_loop(0, 31, thr_bit, thr)
    need = kf - count(lambda c, kk: kk > thr)
    surplus = (count_ge(thr) > kf) & (thr != INT_MIN)

    def tie_count(x):
        return count(lambda c, kk: (kk == thr) & (chunk_pos(c) < x))

    def cut_bit(i, cut):
        cand = cut | (1 << (pos_bits - 1 - i))
        return jnp.where(tie_count(cand) < need, cand, cut)

    take_all = jnp.full((Q_BLOCK, 1), 2 ** pos_bits - 1, I32)
    cut = lax.cond(jnp.max(surplus.astype(F32)) > 0.0,
                   lambda: lax.fori_loop(0, pos_bits, cut_bit, jnp.zeros((Q_BLOCK, 1), I32)),
                   lambda: take_all)

    def mask_bias(kk, pos):
        sel = ((kk > thr) | ((kk == thr) & (pos <= cut))) & (kk != INT_MIN)
        return jnp.where(sel, 0.0, NEG_BIG)

    m_ref[...] = jnp.full_like(m_ref, NEG_BIG)
    l_ref[...] = jnp.zeros_like(l_ref)
    acc_ref[...] = jnp.zeros_like(acc_ref)

    def attend(rows, bias_of):
        nk = bias_of(0).shape[-1]
        for g in range(D_KV):
            gs = slice(D_DIM * g, D_DIM * (g + 1))
            s = lax.dot_general(qs_ref[g], k_ref[rows, gs], nt, preferred_element_type=F32) * D_DIM ** -0.5
            s = (s.reshape(rep, Q_BLOCK, nk) + bias_of(g)).reshape(rep * Q_BLOCK, nk)
            p = _online_softmax_step(s, m_ref.at[g], l_ref.at[g], acc_ref.at[g])
            acc_ref[g] += jnp.dot(p, v_ref[rows, gs], preferred_element_type=F32)

    def far_attend(c, carry):
        rows = pl.ds(pl.multiple_of(c * DSA_CK, DSA_CK), DSA_CK)
        kk = keys_ref[c]
        pos = c * DSA_CK + lax.broadcasted_iota(I32, kk.shape, 1)
        mb = mask_bias(kk, pos)
        attend(rows, lambda g: mb[None])
        return carry

    lax.fori_loop(0, nfar, far_attend, 0)
    mbn = mask_bias(keys_ref[nfar][:, 0:2 * Q_BLOCK], posn)
    attend(near_rows, lambda g: mbn[None] + nbias_ref[0, rep * g:rep * (g + 1)])
    for hd in range(D_HEADS):
        rs = slice(Q_BLOCK * (hd % rep), Q_BLOCK * (hd % rep + 1))
        o_ref[:, D_DIM * hd:D_DIM * (hd + 1)] = acc_ref[hd // rep, rs, :] / l_ref[hd // rep, rs, :]


def _dsa(q, k, v, iq, ika, ikb, iw, near_bias, batch):
    m = q.shape[0]
    seq = m // batch
    nqb = seq // Q_BLOCK
    topk = min(D_TOPK_MAX, seq // 4)
    pos_bits = int(seq).bit_length()
    nck = max(seq // DSA_CK, 1)
    grp = D_HEADS // D_KV
    qspec = lambda w: pl.BlockSpec((Q_BLOCK, w), lambda b, i: (b * nqb + i, 0))
    kspec = lambda w: pl.BlockSpec((seq, w), lambda b, i: (b, 0))
    return pl.pallas_call(
        functools.partial(_dsa_kernel, topk=topk, pos_bits=pos_bits),
        grid=(batch, nqb),
        in_specs=[qspec(D_HEADS * D_DIM), qspec(D_HEADS * D_IDX_DIM), qspec(LANES),
                  kspec(D_KV * D_DIM), kspec(D_KV * D_DIM), kspec(LANES), kspec(LANES),
                  pl.BlockSpec((1, D_HEADS, Q_BLOCK, 2 * Q_BLOCK), lambda b, i: (jnp.minimum(i, 1), 0, 0, 0))],
        out_specs=qspec(D_HEADS * D_DIM),
        out_shape=jax.ShapeDtypeStruct((m, D_HEADS * D_DIM), F32),
        scratch_shapes=[pltpu.VMEM((nck + 1, Q_BLOCK, DSA_CK), I32),
                        pltpu.VMEM((D_KV, grp * Q_BLOCK, LANES), F32), pltpu.VMEM((D_KV, grp * Q_BLOCK, LANES), F32),
                        pltpu.VMEM((D_KV, grp * Q_BLOCK, D_DIM), F32),
                        pltpu.VMEM((D_KV, grp * Q_BLOCK, D_DIM), BF16),
                        pltpu.VMEM((D_HEADS // 2 * Q_BLOCK, LANES), BF16)],
        compiler_params=_cparams(("parallel", "arbitrary")),
        name="dsa",
    )(q, iq, iw, k, v, ika, ikb, near_bias)


def _pad_cols(w, width):
    return jnp.pad(w, ((0, 0), (0, width - w.shape[1])))


def _pack_ab_in(w):
    a_cols = 3 * A_WIDTH + sum(A_LORA)
    wa, wb = w[:, :a_cols], w[:, a_cols:]
    rkv, lora = wa[:, :3 * A_WIDTH], wa[:, 3 * A_WIDTH:]
    z, xbc, dt = wb[:, :B_WIDTH], wb[:, B_WIDTH:B_WIDTH + B_CONV_CH], wb[:, B_WIDTH + B_CONV_CH:]
    return jnp.concatenate([rkv, z, xbc, _pad_cols(lora, A_LORA_PAD), _pad_cols(dt, LANES)], axis=1).astype(BF16)


def _pack_cd_in(w):
    c_cols = 2 * C_LORA + C_ROPE
    wc, wd = w[:, :c_cols], w[:, c_cols:]
    q_lat, kv_lat, k_pe = wc[:, :C_LORA], wc[:, C_LORA:2 * C_LORA], wc[:, 2 * C_LORA:]
    sizes = [D_HEADS * D_DIM, D_KV * D_DIM, D_KV * D_DIM, D_HEADS * D_IDX_DIM, D_IDX_DIM, D_HEADS]
    cuts = np.cumsum(sizes)[:-1]
    dq, dk, dv, iq, ik, iw = jnp.split(wd, [int(c) for c in cuts], axis=1)
    return jnp.concatenate([q_lat, kv_lat, dq, dk, dv, iq, k_pe, ik, _pad_cols(iw, LANES)], axis=1).astype(BF16)


def _pack_lora(w2, a2, g2):
    out, off = [], 0
    for w in (w2, a2, g2):
        out.append(jnp.pad(w, ((off, A_LORA_PAD - off - w.shape[0]), (0, 0))).astype(BF16))
        off += w.shape[0]
    return out


def _pack_mla_q(wq_b):
    w = wq_b.reshape(C_LORA, C_HEADS, C_QK)
    return jnp.concatenate([w[:, :, :C_NOPE].reshape(C_LORA, -1), w[:, :, C_NOPE:].reshape(C_LORA, -1)],
                           axis=1).astype(BF16)


def _pack_mla_kv(wkv_b):
    w = wkv_b.reshape(C_LORA, C_HEADS, C_NOPE + C_V)
    return jnp.concatenate([w[:, :, :C_NOPE].reshape(C_LORA, -1), w[:, :, C_NOPE:].reshape(C_LORA, -1)],
                           axis=1).astype(BF16)


def _layer0_mix(h, batch, norm_g, ab_w_in, ab_w_out, a_shift_mu, a_w0, a_w2, a_a0, a_a2, a_g2, a_k_k, a_k_a,
                a_r_k, a_ln_g, a_ln_b, b_conv_w, b_conv_b, b_dt_bias, b_a_log, b_d, b_norm_g):
    seq = h.shape[0] // batch
    u = _matmul([(h, 0, D_MODEL)], [_pack_ab_in(ab_w_in)], gain=norm_g)
    row = lambda t: t.reshape(1, -1)
    w2p, a2p, g2p = _pack_lora(a_w2, a_a2, a_g2)
    mu_rkv = row(a_shift_mu[:3 * A_WIDTH])
    mu_lora = _pad_cols(row(a_shift_mu[3 * A_WIDTH:]), A_LORA_PAD)
    r, w, k, v, na, nb, g = _rwkv_pre(u, seq, mu_rkv, mu_lora, row(a_w0), row(a_a0), row(a_k_k), row(a_k_a),
                                      w2p, a2p, g2p)
    y = _rwkv_scan(r, w, k, v, na, nb, batch)
    ya = _rwkv_post(y, r, k, v, g, row(a_ln_g), row(a_ln_b), row(a_r_k))
    yb = _ssd(u, batch, b_conv_w, b_conv_b, b_dt_bias, b_a_log, b_d, b_norm_g)
    w_out = ab_w_out.astype(BF16)
    return _matmul([(ya, 0, A_WIDTH), (yb, 0, B_WIDTH)], [w_out[:A_WIDTH], w_out[A_WIDTH:]], res=h)


def _layer1_mix(h, batch, norm_g, rel_bias, cd_w_in, cd_w_out, c_q_norm, c_wq_b, c_kv_norm, c_wkv_b,
                c_q_gain, c_k_gain, d_q_gain, d_k_gain, d_ik_gain):
    seq = h.shape[0] // batch
    u = _matmul([(h, 0, D_MODEL)], [_pack_cd_in(cd_w_in)], gain=norm_g, tn=256)
    qn, qr, kn, kr, v = _mla_prep(u, seq, c_q_norm, _pack_mla_q(c_wq_b), c_kv_norm, _pack_mla_kv(c_wkv_b),
                                  c_q_gain, c_k_gain)
    yc = _mla_attn(qn, qr, kn, kr, v, batch)
    dq, dk, dv, iq, ika, ikb, iw = _dsa_prep(u, d_q_gain, d_k_gain, d_ik_gain)
    yd = _dsa(dq, dk, dv, iq, ika, ikb, iw, _near_bias(rel_bias), batch)
    w_out = cd_w_out.astype(BF16)
    half = C_HEADS * C_V
    return _matmul([(yc, 0, half), (yd, 0, D_HEADS * D_DIM)], [w_out[:half], w_out[half:]], res=h)


def _memory_attention(h, mem2, batch, gq, gkv, wq, wk, wv, wo, q_gain, k_gain):
    wkv = jnp.concatenate([wk, wv], axis=1).astype(BF16)
    mem_kv = _matmul([(mem2, 0, D_MODEL)], [wkv], gain=gkv)
    return _xattn(h, mem_kv, batch, gq, wq.astype(BF16), q_gain, k_gain, wo.astype(BF16))


def kernel(x, mem, rel_bias, norm_mix, norm_mem_q, norm_mem_kv, norm_ffn, xa_wq, xa_wk, xa_wv, xa_wo, xa_q_gain, xa_k_gain, ab_w_in, ab_w_out, a_shift_mu, a_w0, a_w2, a_a0, a_a2, a_g2, a_k_k, a_k_a, a_r_k, a_ln_g, a_ln_b, b_conv_w, b_conv_b, b_dt_bias, b_a_log, b_d, b_norm_g, ffn_w_gate, ffn_w_up, ffn_w_down, cd_w_in, cd_w_out, c_q_norm, c_wq_b, c_kv_norm, c_wkv_b, c_q_gain, c_k_gain, d_q_gain, d_k_gain, d_ik_gain, moe_router, moe_w_gate, moe_w_up, moe_w_down):
    batch, seq, d = x.shape
    h = x.reshape(batch * seq, d)
    mem2 = mem.reshape(-1, d)
    depth = norm_mix.shape[0]
    for layer in range(depth):
        i = layer // 2
        if layer % 2 == 0:
            h = _layer0_mix(h, batch, norm_mix[layer], ab_w_in[i], ab_w_out[i], a_shift_mu[i], a_w0[i], a_w2[i],
                            a_a0[i], a_a2[i], a_g2[i], a_k_k[i], a_k_a[i], a_r_k[i], a_ln_g[i], a_ln_b[i],
                            b_conv_w[i], b_conv_b[i], b_dt_bias[i], b_a_log[i], b_d[i], b_norm_g[i])
        else:
            h = _layer1_mix(h, batch, norm_mix[layer], rel_bias, cd_w_in[i], cd_w_out[i], c_q_norm[i], c_wq_b[i],
                            c_kv_norm[i], c_wkv_b[i], c_q_gain[i], c_k_gain[i], d_q_gain[i], d_k_gain[i],
                            d_ik_gain[i])
        h = _memory_attention(h, mem2, batch, norm_mem_q[layer], norm_mem_kv[layer], xa_wq[layer], xa_wk[layer],
                              xa_wv[layer], xa_wo[layer], xa_q_gain[layer], xa_k_gain[layer])
        if layer % 2 == 0:
            h = _swiglu(h, norm_ffn[layer], ffn_w_gate[i].astype(BF16), ffn_w_up[i].astype(BF16),
                        ffn_w_down[i].astype(BF16))
        else:
            router_p = _pad_cols(moe_router[i], LANES).astype(BF16)
            h = _moe(h, norm_ffn[layer], router_p, moe_w_gate[i].astype(BF16), moe_w_up[i].astype(BF16),
                     moe_w_down[i].astype(BF16))
    return h.reshape(batch, seq, d)
```

```python
import functools
import math

import numpy as np
import jax
import jax.numpy as jnp
from jax import lax
from jax.experimental import pallas as pl
from jax.experimental.pallas import tpu as pltpu

F32 = jnp.float32
BF16 = jnp.bfloat16
I32 = jnp.int32
HIGHEST = lax.Precision.HIGHEST

V7X_VMEM_BYTES = 64 * 1024 * 1024
VMEM_LIMIT = V7X_VMEM_BYTES - 8 * 1024 * 1024
LANES = 128

NORM_EPS = 1e-6
D_MODEL = 2048
HEAD64 = 64

A_WIDTH = 1024
A_LORA = (64, 64, 160)
A_LORA_PAD = 384
A_LN_EPS = 1e-5 * (HEAD64 / 8) ** 2
B_WIDTH = 1024
B_HEADS = 16
B_GROUPS = 4
B_STATE = 128
B_CONV = 4
B_CHUNK = 128
B_CONV_CH = B_WIDTH + 2 * B_GROUPS * B_STATE
AB_R, AB_K, AB_V, AB_Z, AB_XBC, AB_LORA, AB_DT, AB_COLS_PAD = 0, 1024, 2048, 3072, 4096, 6144, 6528, 6656

C_HEADS = 8
C_NOPE = 128
C_ROPE = 64
C_QK = C_NOPE + C_ROPE
C_V = 128
C_LORA = 512
ROPE_THETA = 10000.0
D_HEADS = 8
D_KV = 2
D_DIM = 128
D_IDX_DIM = 64
D_TOPK_MAX = 256
Q_BLOCK = 128
REL_BUCKETS = 32
REL_MAX_DIST = 128
CD_QLAT, CD_KVLAT, CD_DQ, CD_DK, CD_DV, CD_IQ, CD_PEIK, CD_IW, CD_COLS_PAD = (
    0, 512, 1024, 2048, 2304, 2560, 3072, 3200, 3328)

X_HEADS = 4
X_DIM = 128
N_EXPERTS = 8

NEG_BIG = -1e30
INT_MIN = -2 ** 31


def _cparams(sem):
    return pltpu.CompilerParams(dimension_semantics=sem, vmem_limit_bytes=VMEM_LIMIT)


def _rms(x, g, eps=NORM_EPS):
    return x * lax.rsqrt(jnp.mean(x * x, axis=-1, keepdims=True) + eps) * g


def _softplus(x):
    return jnp.maximum(x, 0.0) + jnp.log(1.0 + jnp.exp(-jnp.abs(x)))


def _silu(x):
    return x * jax.nn.sigmoid(x)


def _online_softmax_step(s, m_ref, l_ref, acc_ref):
    cols = [s[:, LANES * c:LANES * (c + 1)] for c in range(s.shape[1] // LANES)]
    mx = cols[0]
    for c in cols[1:]:
        mx = jnp.maximum(mx, c)
    m_old = m_ref[...]
    m_new = jnp.maximum(m_old, jnp.max(mx, axis=-1, keepdims=True))
    alpha = jnp.exp(m_old - m_new)
    ps = [jnp.exp(c - m_new) for c in cols]
    rs = ps[0]
    for p in ps[1:]:
        rs = rs + p
    l_ref[...] = alpha * l_ref[...] + jnp.sum(rs, axis=-1, keepdims=True)
    acc_ref[...] = alpha * acc_ref[...]
    m_ref[...] = m_new
    return jnp.concatenate(ps, axis=1).astype(BF16)


def _half_sum_bcast(x):
    left = lax.broadcasted_iota(I32, x.shape, 1) < HEAD64
    s0 = jnp.sum(jnp.where(left, x, 0.0), axis=1, keepdims=True)
    s1 = jnp.sum(jnp.where(left, 0.0, x), axis=1, keepdims=True)
    return jnp.where(left, s0, s1)


def _mm_kernel(*refs, n_x, has_norm, has_res):
    x_refs = refs[:n_x]
    pos = n_x
    g_ref = refs[pos] if has_norm else None
    pos += int(has_norm)
    w_refs = refs[pos:pos + n_x]
    pos += n_x
    res_ref = refs[pos] if has_res else None
    pos += int(has_res)
    o_ref = refs[pos]
    xn_refs = refs[pos + 1:]

    @pl.when(pl.program_id(1) == 0)
    def _():
        for x_ref, xn_ref in zip(x_refs, xn_refs):
            x = x_ref[...].astype(F32)
            if has_norm:
                x = _rms(x, g_ref[...])
            xn_ref[...] = x.astype(BF16)

    acc = None
    for xn_ref, w_ref in zip(xn_refs, w_refs):
        d = jnp.dot(xn_ref[...], w_ref[...], preferred_element_type=F32)
        acc = d if acc is None else acc + d
    if has_res:
        acc = acc + res_ref[...]
    o_ref[...] = acc


def _matmul(xs, ws, *, gain=None, res=None, tm=512, tn=512):
    m = xs[0][0].shape[0]
    n = ws[0].shape[1]
    tm = min(tm, m)
    tn = min(tn, n)
    assert m % tm == 0 and n % tn == 0
    in_specs, args, scratch = [], [], []
    for arr, cb, width in xs:
        in_specs.append(pl.BlockSpec((tm, width), lambda i, j, cb=cb: (i, cb)))
        args.append(arr)
        scratch.append(pltpu.VMEM((tm, width), BF16))
    if gain is not None:
        in_specs.append(pl.BlockSpec((1, gain.shape[-1]), lambda i, j: (0, 0)))
        args.append(gain.reshape(1, -1))
    for (arr, cb, width), w in zip(xs, ws):
        assert w.shape[0] == width
        in_specs.append(pl.BlockSpec((width, tn), lambda i, j: (0, j)))
        args.append(w)
    if res is not None:
        in_specs.append(pl.BlockSpec((tm, tn), lambda i, j: (i, j)))
        args.append(res)
    return pl.pallas_call(
        functools.partial(_mm_kernel, n_x=len(xs), has_norm=gain is not None, has_res=res is not None),
        grid=(m // tm, n // tn),
        in_specs=in_specs,
        out_specs=pl.BlockSpec((tm, tn), lambda i, j: (i, j)),
        out_shape=jax.ShapeDtypeStruct((m, n), F32),
        scratch_shapes=scratch,
        compiler_params=_cparams(("parallel", "arbitrary")),
        name="matmul",
    )(*args)


def _rwkv_pre_kernel(rkv_ref, lora_ref, rkvp_ref, lorap_ref, mu_rkv_ref, mu_lora_ref,
                     w0_ref, a0_ref, kk_ref, ka_ref, w2_ref, a2_ref, g2_ref,
                     r_o, w_o, k_o, v_o, na_o, nb_o, g_o, *, tiles_per_seq):
    first = (pl.program_id(0) % tiles_per_seq) == 0
    tm = rkv_ref.shape[0]

    def shift_mix(x, prev_rows, mu):
        prev_last = jnp.where(first, 0.0, prev_rows[7:8, :])
        xs = pltpu.roll(x, 1, axis=0)
        row = lax.broadcasted_iota(I32, x.shape, 0)
        xs = jnp.where(row == 0, prev_last, xs)
        return x + (xs - x) * mu

    lo = shift_mix(lora_ref[...], lorap_ref[...], mu_lora_ref[...])
    lane = lax.broadcasted_iota(I32, lo.shape, 1)
    act = jnp.where(lane < A_LORA[0], jnp.tanh(lo),
                    jnp.where(lane < A_LORA[0] + A_LORA[1], lo, jax.nn.sigmoid(lo))).astype(BF16)
    dw = jnp.dot(act, w2_ref[...], preferred_element_type=F32)
    da = jnp.dot(act, a2_ref[...], preferred_element_type=F32)
    g_o[...] = jnp.dot(act, g2_ref[...], preferred_element_type=F32)

    for p in range(A_WIDTH // LANES):
        sl = slice(LANES * p, LANES * (p + 1))

        def mixed(off):
            s2 = slice(off + LANES * p, off + LANES * (p + 1))
            return shift_mix(rkv_ref[:, s2], rkvp_ref[:, s2], mu_rkv_ref[:, s2])

        r_o[:, sl] = mixed(AB_R)
        v_o[:, sl] = mixed(AB_V)
        kx = mixed(AB_K)
        logw = -_softplus(-(w0_ref[:, sl] + dw[:, sl])) - 0.5
        w_o[:, sl] = jnp.exp(-jnp.exp(logw))
        a = jax.nn.sigmoid(a0_ref[:, sl] + da[:, sl])
        kk = kx * kk_ref[:, sl]
        kk = kk * lax.rsqrt(jnp.maximum(_half_sum_bcast(kk * kk), 1e-24))
        k_o[:, sl] = kx * (1.0 + (a - 1.0) * ka_ref[:, sl])
        na_o[:, sl] = -kk
        nb_o[:, sl] = kk * a


def _rwkv_pre(u, seq, mu_rkv, mu_lora, w0, a0, k_k, k_a, w2p, a2p, g2p, tm=256):
    m = u.shape[0]
    tm = min(tm, seq)
    row = lambda w: pl.BlockSpec((1, w), lambda i: (0, 0))
    full = lambda a: pl.BlockSpec(a.shape, lambda i: (0, 0))
    prev = lambda i: jnp.maximum(i * (tm // 8) - 1, 0)
    out = jax.ShapeDtypeStruct((m, A_WIDTH), F32)
    return pl.pallas_call(
        functools.partial(_rwkv_pre_kernel, tiles_per_seq=seq // tm),
        grid=(m // tm,),
        in_specs=[
            pl.BlockSpec((tm, 3 * A_WIDTH), lambda i: (i, 0)),
            pl.BlockSpec((tm, A_LORA_PAD), lambda i: (i, AB_LORA // A_LORA_PAD)),
            pl.BlockSpec((8, 3 * A_WIDTH), lambda i: (prev(i), 0)),
            pl.BlockSpec((8, A_LORA_PAD), lambda i: (prev(i), AB_LORA // A_LORA_PAD)),
            row(3 * A_WIDTH), row(A_LORA_PAD), row(A_WIDTH), row(A_WIDTH), row(A_WIDTH), row(A_WIDTH),
            full(w2p), full(a2p), full(g2p),
        ],
        out_specs=[pl.BlockSpec((tm, A_WIDTH), lambda i: (i, 0))] * 7,
        out_shape=[out] * 7,
        compiler_params=_cparams(("parallel",)),
        name="rwkv_pre",
    )(u, u, u, u, mu_rkv, mu_lora, w0, a0, k_k, k_a, w2p, a2p, g2p)


RWKV_CHUNK = 64
RWKV_PAIR_GROUP = 8


def _rwkv_scan_kernel(r_ref, w_ref, k_ref, v_ref, a_ref, b_ref, y_ref, s_ref, vt_ref, yt_ref):
    npairs = s_ref.shape[0]

    @pl.when(pl.program_id(1) == 0)
    def _():
        s_ref[...] = jnp.zeros_like(s_ref)

    lane = lax.broadcasted_iota(I32, (HEAD64, LANES), 1)
    left = lane < HEAD64
    lane64 = lane & (HEAD64 - 1)

    def pair_transpose(x):
        xt = jnp.concatenate([x, x], axis=0).T
        return jnp.where(left, xt[0:HEAD64], xt[HEAD64:2 * HEAD64])

    def two_terms(x):
        hi = x.astype(BF16)
        return jnp.concatenate([hi, (x - hi.astype(F32)).astype(BF16)], axis=1)

    for p in range(npairs):
        vt_ref[p] = pair_transpose(v_ref[:, LANES * p:LANES * (p + 1)])
    yt_ref[...] = jnp.zeros_like(yt_ref)

    ri = lax.broadcasted_iota(I32, (2 * LANES, LANES), 0)
    ci = lax.broadcasted_iota(I32, (2 * LANES, LANES), 1)
    ones_blk = (((ri // HEAD64) & 1) == (ci // HEAD64)).astype(BF16)

    def half_sum_mxu(x, split):
        if not split:
            return jnp.dot(x.astype(BF16), ones_blk[0:LANES], preferred_element_type=F32)
        return jnp.dot(two_terms(x), ones_blk, preferred_element_type=F32)

    def step8(t8, carry):
        rows8 = pl.ds(pl.multiple_of(t8 * 8, 8), 8)
        unroll = (LANES - t8 * 8) & (LANES - 1)
        for g0 in range(0, npairs, RWKV_PAIR_GROUP):
            group = range(g0, min(g0 + RWKV_PAIR_GROUP, npairs))
            rows = {p: [ref[rows8, LANES * p:LANES * (p + 1)] for ref in (a_ref, w_ref, b_ref, k_ref, r_ref)]
                    for p in group}
            s = {p: s_ref[p] for p in group}
            vt8 = {p: pltpu.roll(vt_ref[p], unroll, axis=1) for p in group}
            for j in range(8):
                sel = lane64 == t8 * 8 + j
                for p in group:
                    a_row, w_row, b_row, k_row, r_row = (x8[j:j + 1, :] for x8 in rows[p])
                    sa = half_sum_mxu(s[p] * a_row, True)
                    vc = jnp.take_along_axis(vt8[p], jnp.where(left, j, HEAD64 + j), axis=1)
                    s[p] = s[p] * w_row + sa * b_row + vc * k_row
                    yt_ref[p] = jnp.where(sel, half_sum_mxu(s[p] * r_row, False), yt_ref[p])
            for p in group:
                s_ref[p] = s[p]
        return carry

    lax.fori_loop(0, RWKV_CHUNK // 8, step8, 0)

    for p in range(npairs):
        y_ref[:, LANES * p:LANES * (p + 1)] = pair_transpose(yt_ref[p])


def _rwkv_scan(r, w, k, v, na, nb, batch):
    m = r.shape[0]
    seq = m // batch
    nchunk = seq // RWKV_CHUNK
    npairs = A_WIDTH // LANES
    spec = pl.BlockSpec((RWKV_CHUNK, A_WIDTH), lambda b, c: (b * nchunk + c, 0))
    pair_scratch = pltpu.VMEM((npairs, HEAD64, LANES), F32)
    return pl.pallas_call(
        _rwkv_scan_kernel,
        grid=(batch, nchunk),
        in_specs=[spec] * 6,
        out_specs=spec,
        out_shape=jax.ShapeDtypeStruct((m, A_WIDTH), F32),
        scratch_shapes=[pair_scratch, pair_scratch, pair_scratch],
        compiler_params=_cparams(("parallel", "arbitrary")),
        name="rwkv_scan",
    )(r, w, k, v, na, nb)


def _rwkv_post_kernel(y_ref, r_ref, k_ref, v_ref, g_ref, lng_ref, lnb_ref, rk_ref, o_ref):
    for p in range(A_WIDTH // LANES):
        sl = slice(LANES * p, LANES * (p + 1))
        y = y_ref[:, sl]
        mean = _half_sum_bcast(y) * (1.0 / HEAD64)
        d = y - mean
        var = _half_sum_bcast(d * d) * (1.0 / HEAD64)
        yn = d * lax.rsqrt(var + A_LN_EPS) * lng_ref[:, sl] + lnb_ref[:, sl]
        bonus = _half_sum_bcast(r_ref[:, sl] * k_ref[:, sl] * rk_ref[:, sl]) * v_ref[:, sl]
        o_ref[:, sl] = (yn + bonus) * g_ref[:, sl]


def _rwkv_post(y, r, k, v, g, ln_g, ln_b, r_k, tm=256):
    m = y.shape[0]
    tm = min(tm, m)
    spec = pl.BlockSpec((tm, A_WIDTH), lambda i: (i, 0))
    row = pl.BlockSpec((1, A_WIDTH), lambda i: (0, 0))
    return pl.pallas_call(
        _rwkv_post_kernel,
        grid=(m // tm,),
        in_specs=[spec] * 5 + [row] * 3,
        out_specs=spec,
        out_shape=jax.ShapeDtypeStruct((m, A_WIDTH), F32),
        compiler_params=_cparams(("parallel",)),
        name="rwkv_post",
    )(y, r, k, v, g, ln_g, ln_b, r_k)


def _ssd_kernel(z_ref, xbc_ref, dt_ref, cw_ref, cb_ref, dtb_ref, alog_ref, dskip_ref, ng_ref,
                expand_ref, o_ref, st_ref, tail_ref):
    lc = B_CHUNK

    @pl.when(pl.program_id(1) == 0)
    def _():
        st_ref[...] = jnp.zeros_like(st_ref)
        tail_ref[...] = jnp.zeros_like(tail_ref)

    x = xbc_ref[...]
    tail = tail_ref[...]
    row8 = lax.broadcasted_iota(I32, tail.shape, 0)
    conv = cb_ref[...] + cw_ref[B_CONV - 1:B_CONV, :] * x
    for j in range(1, B_CONV):
        xs = pltpu.roll(x, j, axis=0)
        top = jnp.where(row8 < j, pltpu.roll(tail, j, axis=0), xs[0:8])
        xs = jnp.concatenate([top, xs[8:]], axis=0)
        conv = conv + cw_ref[B_CONV - 1 - j:B_CONV - j, :] * xs
    tail_ref[...] = x[lc - 8:lc]
    act = _silu(conv)
    xs_in = act[:, 0:B_WIDTH]
    bm = act[:, B_WIDTH:B_WIDTH + B_GROUPS * B_STATE].astype(BF16)
    cm = act[:, B_WIDTH + B_GROUPS * B_STATE:].astype(BF16)

    dt = _softplus(dt_ref[...] + dtb_ref[...])
    a_neg = -jnp.exp(alog_ref[...])
    da = dt * a_neg
    ri = lax.broadcasted_iota(I32, (lc, lc), 0)
    ci = lax.broadcasted_iota(I32, (lc, lc), 1)
    causal = ci <= ri
    tri = causal.astype(F32)
    cum = jnp.dot(tri, da, precision=HIGHEST, preferred_element_type=F32)
    cum_t = jnp.dot(da.T, (ri <= ci).astype(F32), precision=HIGHEST,
                    preferred_element_type=F32)
    expand = expand_ref[...]
    widen = lambda t: jnp.dot(t, expand, precision=HIGHEST, preferred_element_type=F32)
    dt_full = widen(dt)
    ecum_full = widen(jnp.exp(cum))
    dte_full = widen(jnp.exp(cum[lc - 1:lc, :] - cum))
    xdt = xs_in * dt_full
    xdt_b = xdt.astype(BF16)
    xdte_b = (xdt * dte_full).astype(BF16)
    left = lax.broadcasted_iota(I32, (lc, LANES), 1) < HEAD64

    ys = []
    for g in range(B_GROUPS):
        gs = slice(B_STATE * g, B_STATE * (g + 1))
        cm_g = cm[:, gs]
        bm_g = bm[:, gs]
        cb = lax.dot_general(cm_g, bm_g, (((1,), (1,)), ((), ())), preferred_element_type=F32)
        bm_t = bm_g.T
        pairs_per_group = B_HEADS // B_GROUPS // 2
        for q in range(pairs_per_group):
            p = g * pairs_per_group + q
            sl = slice(LANES * p, LANES * (p + 1))
            yd = []
            for h in (2 * p, 2 * p + 1):
                seg = cum[:, h:h + 1] - cum_t[h:h + 1, :]
                dec = jnp.where(causal, jnp.exp(jnp.minimum(seg, 0.0)), 0.0)
                yd.append(jnp.dot((cb * dec).astype(BF16), xdt_b[:, sl], preferred_element_type=F32))
            y_diag = jnp.where(left, yd[0], yd[1])
            st = st_ref[p]
            y_off = jnp.dot(cm_g, st.astype(BF16), preferred_element_type=F32) * ecum_full[:, sl]
            ys.append(y_diag + y_off)
            st_ref[p] = st * ecum_full[lc - 1:lc, sl] + jnp.dot(bm_t, xdte_b[:, sl],
                                                                preferred_element_type=F32)
    y = jnp.concatenate(ys, axis=1)
    y = (y + dskip_ref[...] * xs_in) * _silu(z_ref[...])
    gw = B_WIDTH // B_GROUPS
    for g in range(B_GROUPS):
        gs = slice(gw * g, gw * (g + 1))
        yg = y[:, gs]
        o_ref[:, gs] = yg * lax.rsqrt(jnp.mean(yg * yg, axis=-1, keepdims=True) + NORM_EPS) * ng_ref[:, gs]


def _ssd(u, batch, conv_w, conv_b, dt_bias, a_log, d_skip, norm_g):
    m = u.shape[0]
    seq = m // batch
    nchunk = seq // B_CHUNK
    rows = lambda b, c: b * nchunk + c
    pad16 = lambda t: jnp.pad(t.reshape(1, -1), ((0, 0), (0, LANES - B_HEADS)))
    expand = (np.arange(LANES)[:, None] == (np.arange(B_WIDTH)[None, :] // HEAD64)).astype(np.float32)
    full = lambda a: pl.BlockSpec(a.shape, lambda b, c: (0,) * a.ndim)
    args = [conv_w, conv_b.reshape(1, -1), pad16(dt_bias), pad16(a_log),
            jnp.repeat(d_skip, HEAD64).reshape(1, -1), norm_g.reshape(1, -1), jnp.asarray(expand)]
    return pl.pallas_call(
        _ssd_kernel,
        grid=(batch, nchunk),
        in_specs=[
            pl.BlockSpec((B_CHUNK, B_WIDTH), lambda b, c: (rows(b, c), AB_Z // B_WIDTH)),
            pl.BlockSpec((B_CHUNK, B_CONV_CH), lambda b, c: (rows(b, c), AB_XBC // B_CONV_CH)),
            pl.BlockSpec((B_CHUNK, LANES), lambda b, c: (rows(b, c), AB_DT // LANES)),
        ] + [full(a) for a in args],
        out_specs=pl.BlockSpec((B_CHUNK, B_WIDTH), lambda b, c: (rows(b, c), 0)),
        out_shape=jax.ShapeDtypeStruct((m, B_WIDTH), F32),
        scratch_shapes=[pltpu.VMEM((B_HEADS // 2, B_STATE, LANES), F32),
                        pltpu.VMEM((8, B_CONV_CH), F32)],
        compiler_params=_cparams(("parallel", "arbitrary")),
        name="ssd",
    )(u, u, u, *args)


def _xattn_kernel(h_ref, gq_ref, wq_ref, kv_ref, qg_ref, kg_ref, wo_ref, o_ref):
    h = h_ref[...]
    hn = _rms(h, gq_ref[...]).astype(BF16)
    q = jnp.dot(hn, wq_ref[...], preferred_element_type=F32)
    outs = []
    for hd in range(X_HEADS):
        sl = slice(X_DIM * hd, X_DIM * (hd + 1))
        qh = _rms(q[:, sl], qg_ref[...]).astype(BF16)
        kh = _rms(kv_ref[:, sl], kg_ref[...]).astype(BF16)
        vh = kv_ref[:, X_HEADS * X_DIM + X_DIM * hd:X_HEADS * X_DIM + X_DIM * (hd + 1)].astype(BF16)
        s = lax.dot_general(qh, kh, (((1,), (1,)), ((), ())), preferred_element_type=F32) * X_DIM ** -0.5
        e = jnp.exp(s - jnp.max(s, axis=-1, keepdims=True))
        p = e / jnp.sum(e, axis=-1, keepdims=True)
        outs.append(jnp.dot(p.astype(BF16), vh, preferred_element_type=F32))
    o = jnp.concatenate(outs, axis=1).astype(BF16)
    o_ref[...] = h + jnp.dot(o, wo_ref[...], preferred_element_type=F32)


def _xattn(h, mem_kv, batch, gq, wq, q_gain, k_gain, wo, tm=512):
    m, d = h.shape
    seq = m // batch
    tm = min(tm, seq)
    mlen = mem_kv.shape[0] // batch
    nt = seq // tm
    full = lambda a: pl.BlockSpec(a.shape, lambda b, i: (0, 0))
    args = [gq.reshape(1, -1), wq, mem_kv, q_gain.reshape(1, -1), k_gain.reshape(1, -1), wo]
    specs = [full(a) for a in args]
    specs[2] = pl.BlockSpec((mlen, mem_kv.shape[1]), lambda b, i: (b, 0))
    return pl.pallas_call(
        _xattn_kernel,
        grid=(batch, nt),
        in_specs=[pl.BlockSpec((tm, d), lambda b, i: (b * nt + i, 0))] + specs,
        out_specs=pl.BlockSpec((tm, d), lambda b, i: (b * nt + i, 0)),
        out_shape=jax.ShapeDtypeStruct((m, d), F32),
        compiler_params=_cparams(("parallel", "parallel")),
        name="xattn",
    )(h, *args)


def _swiglu_kernel(h_ref, g_ref, wg_ref, wu_ref, wd_ref, o_ref, xn_ref, acc_ref):
    j = pl.program_id(1)

    @pl.when(j == 0)
    def _():
        xn_ref[...] = _rms(h_ref[...], g_ref[...]).astype(BF16)
        acc_ref[...] = jnp.zeros_like(acc_ref)

    xn = xn_ref[...]
    gate = jnp.dot(xn, wg_ref[...], preferred_element_type=F32)
    up = jnp.dot(xn, wu_ref[...], preferred_element_type=F32)
    acc_ref[...] += jnp.dot((_silu(gate) * up).astype(BF16), wd_ref[...], preferred_element_type=F32)

    @pl.when(j == pl.num_programs(1) - 1)
    def _():
        o_ref[...] = h_ref[...] + acc_ref[...]


def _swiglu(h, gain, wg, wu, wd, tm=512, tf=512):
    m, d = h.shape
    f = wg.shape[1]
    tm = min(tm, m)
    assert f % tf == 0
    return pl.pallas_call(
        _swiglu_kernel,
        grid=(m // tm, f // tf),
        in_specs=[
            pl.BlockSpec((tm, d), lambda i, j: (i, 0)),
            pl.BlockSpec((1, d), lambda i, j: (0, 0)),
            pl.BlockSpec((d, tf), lambda i, j: (0, j)),
            pl.BlockSpec((d, tf), lambda i, j: (0, j)),
            pl.BlockSpec((tf, d), lambda i, j: (j, 0)),
        ],
        out_specs=pl.BlockSpec((tm, d), lambda i, j: (i, 0)),
        out_shape=jax.ShapeDtypeStruct((m, d), F32),
        scratch_shapes=[pltpu.VMEM((tm, d), BF16), pltpu.VMEM((tm, d), F32)],
        compiler_params=_cparams(("parallel", "arbitrary")),
        name="swiglu",
    )(h, gain.reshape(1, -1), wg, wu, wd)


MOE_TOPK = 2
MOE_ROWS = 512
MOE_TOKENS = 256


def _moe_route_kernel(h_ref, g_ref, router_ref, eid_o, gw_o):
    xn = _rms(h_ref[...], g_ref[...]).astype(BF16)
    logits = jnp.dot(xn, router_ref[...], preferred_element_type=F32)
    lane = lax.broadcasted_iota(I32, logits.shape, 1)
    logits = jnp.where(lane < N_EXPERTS, logits, -jnp.inf)
    m1 = jnp.max(logits, axis=-1, keepdims=True)
    i1 = jnp.min(jnp.where(logits == m1, lane, LANES), axis=-1, keepdims=True)
    rest = jnp.where(lane == i1, -jnp.inf, logits)
    m2 = jnp.max(rest, axis=-1, keepdims=True)
    i2 = jnp.min(jnp.where(rest == m2, lane, LANES), axis=-1, keepdims=True)
    e2 = jnp.exp(m2 - m1)
    eid_o[...] = jnp.where(lane == 0, i1, jnp.where(lane == 1, i2, 0))
    gw_o[...] = jnp.where(lane == 0, 1.0 / (1.0 + e2), jnp.where(lane == 1, e2 / (1.0 + e2), 0.0))


def _moe_route(h, gain, router_p, tm=512):
    m, d = h.shape
    tm = min(tm, m)
    spec = pl.BlockSpec((tm, LANES), lambda i: (i, 0))
    return pl.pallas_call(
        _moe_route_kernel,
        grid=(m // tm,),
        in_specs=[pl.BlockSpec((tm, d), lambda i: (i, 0)), pl.BlockSpec((1, d), lambda i: (0, 0)),
                  pl.BlockSpec((d, LANES), lambda i: (0, 0))],
        out_specs=[spec, spec],
        out_shape=[jax.ShapeDtypeStruct((m, LANES), I32), jax.ShapeDtypeStruct((m, LANES), F32)],
        compiler_params=_cparams(("parallel",)),
        name="moe_route",
    )(h, gain.reshape(1, -1), router_p)


def _moe_plan(eid, nblocks):
    e = eid.reshape(-1)
    onehot = (e[:, None] == jnp.arange(N_EXPERTS, dtype=I32)[None, :]).astype(I32)
    csum = jnp.cumsum(onehot, axis=0)
    rank = jnp.sum(onehot * csum, axis=1) - 1
    counts = csum[-1]
    padded = (counts + MOE_ROWS - 1) // MOE_ROWS * MOE_ROWS
    gend = jnp.cumsum(padded)
    dest = jnp.sum(onehot * (gend - padded)[None, :], axis=1) + rank
    nb_used = gend[-1] // MOE_ROWS
    blk = jnp.arange(nblocks, dtype=I32)
    blk_e = jnp.minimum(jnp.sum((blk[:, None] * MOE_ROWS >= gend[None, :]).astype(I32), axis=1), N_EXPERTS - 1)
    blk_e = jnp.where(blk < nb_used, blk_e, blk_e[jnp.maximum(nb_used - 1, 0)])
    return dest.astype(I32), blk_e.astype(I32), nb_used.astype(I32).reshape(1)


def _row_copy(src_ref, src_row, dst_ref, dst_row, sem):
    return pltpu.make_async_copy(src_ref.at[pl.ds(src_row, 1), :], dst_ref.at[pl.ds(dst_row, 1), :], sem)


def _moe_dispatch_kernel(dest_ref, h_ref, xs_in_ref, xs_ref, sem):
    del xs_in_ref
    base = pl.program_id(0) * MOE_TOKENS

    def issue(r, carry):
        for c in range(MOE_TOPK):
            _row_copy(h_ref, r, xs_ref, dest_ref[MOE_TOPK * (base + r) + c], sem).start()
        return carry

    def drain(r, carry):
        for c in range(MOE_TOPK):
            _row_copy(h_ref, 0, xs_ref, 0, sem).wait()
        return carry

    lax.fori_loop(0, MOE_TOKENS, issue, 0)
    lax.fori_loop(0, MOE_TOKENS, drain, 0)


def _moe_dispatch(h, dest, rows):
    m, d = h.shape
    return pl.pallas_call(
        _moe_dispatch_kernel,
        grid_spec=pltpu.PrefetchScalarGridSpec(
            num_scalar_prefetch=1,
            grid=(m // MOE_TOKENS,),
            in_specs=[pl.BlockSpec((MOE_TOKENS, d), lambda i, dest: (i, 0)), pl.BlockSpec(memory_space=pl.ANY)],
            out_specs=pl.BlockSpec(memory_space=pl.ANY),
            scratch_shapes=[pltpu.SemaphoreType.DMA(())],
        ),
        out_shape=jax.ShapeDtypeStruct((rows, d), F32),
        input_output_aliases={2: 0},
        compiler_params=_cparams(("arbitrary",)),
        name="moe_dispatch",
    )(dest, h, jnp.zeros((rows, d), F32))


def _moe_ffn_kernel(be_ref, nb_ref, x_ref, g_ref, wg_ref, wu_ref, wd_ref, y_ref, xn_ref, acc_ref):
    del be_ref
    j = pl.program_id(1)
    live = pl.program_id(0) < nb_ref[0]
    last = j == pl.num_programs(1) - 1

    @pl.when(live & (j == 0))
    def _():
        xn_ref[...] = _rms(x_ref[...], g_ref[...]).astype(BF16)
        acc_ref[...] = jnp.zeros_like(acc_ref)

    @pl.when(live)
    def _():
        xn = xn_ref[...]
        gate = jnp.dot(xn, wg_ref[0], preferred_element_type=F32)
        up = jnp.dot(xn, wu_ref[0], preferred_element_type=F32)
        acc_ref[...] += jnp.dot((_silu(gate) * up).astype(BF16), wd_ref[0], preferred_element_type=F32)

    @pl.when(live & last)
    def _():
        y_ref[...] = acc_ref[...]

    @pl.when(jnp.logical_not(live) & last)
    def _():
        y_ref[...] = jnp.zeros_like(y_ref)


def _moe_ffn(xs, blk_e, nb_used, gain, wg, wu, wd, tf=256):
    rows, d = xs.shape
    f = wg.shape[2]
    nj = f // tf
    assert f % tf == 0 and rows % MOE_ROWS == 0
    jx = lambda b, j, nb: jnp.where(b < nb[0], j, nj - 1)
    return pl.pallas_call(
        _moe_ffn_kernel,
        grid_spec=pltpu.PrefetchScalarGridSpec(
            num_scalar_prefetch=2,
            grid=(rows // MOE_ROWS, nj),
            in_specs=[
                pl.BlockSpec((MOE_ROWS, d), lambda b, j, be, nb: (jnp.minimum(b, jnp.maximum(nb[0] - 1, 0)), 0)),
                pl.BlockSpec((1, d), lambda b, j, be, nb: (0, 0)),
                pl.BlockSpec((1, d, tf), lambda b, j, be, nb: (be[b], 0, jx(b, j, nb))),
                pl.BlockSpec((1, d, tf), lambda b, j, be, nb: (be[b], 0, jx(b, j, nb))),
                pl.BlockSpec((1, tf, d), lambda b, j, be, nb: (be[b], jx(b, j, nb), 0)),
            ],
            out_specs=pl.BlockSpec((MOE_ROWS, d), lambda b, j, be, nb: (b, 0)),
            scratch_shapes=[pltpu.VMEM((MOE_ROWS, d), BF16), pltpu.VMEM((MOE_ROWS, d), F32)],
        ),
        out_shape=jax.ShapeDtypeStruct((rows, d), F32),
        compiler_params=_cparams(("parallel", "arbitrary")),
        name="moe_ffn",
    )(blk_e, nb_used, xs, gain.reshape(1, -1), wg, wu, wd)


def _moe_combine_kernel(dest_ref, h_ref, gw_ref, ys_ref, o_ref, ybuf_ref, sem):
    base = pl.program_id(0) * MOE_TOKENS

    def issue(r, carry):
        for c in range(MOE_TOPK):
            _row_copy(ys_ref, dest_ref[MOE_TOPK * (base + r) + c], ybuf_ref.at[c], r, sem).start()
        return carry

    def drain(r, carry):
        for c in range(MOE_TOPK):
            _row_copy(ys_ref, 0, ybuf_ref.at[c], 0, sem).wait()
        return carry

    lax.fori_loop(0, MOE_TOKENS, issue, 0)
    lax.fori_loop(0, MOE_TOKENS, drain, 0)
    gw = gw_ref[...]
    o_ref[...] = h_ref[...] + (gw[:, 0:1] * ybuf_ref[0] + gw[:, 1:2] * ybuf_ref[1])


def _moe_combine(h, gw, ys, dest):
    m, d = h.shape
    tok = lambda w: pl.BlockSpec((MOE_TOKENS, w), lambda i, dest: (i, 0))
    return pl.pallas_call(
        _moe_combine_kernel,
        grid_spec=pltpu.PrefetchScalarGridSpec(
            num_scalar_prefetch=1,
            grid=(m // MOE_TOKENS,),
            in_specs=[tok(d), tok(LANES), pl.BlockSpec(memory_space=pl.ANY)],
            out_specs=tok(d),
            scratch_shapes=[pltpu.VMEM((MOE_TOPK, MOE_TOKENS, d), F32), pltpu.SemaphoreType.DMA(())],
        ),
        out_shape=jax.ShapeDtypeStruct((m, d), F32),
        compiler_params=_cparams(("arbitrary",)),
        name="moe_combine",
    )(dest, h, gw, ys)


def _moe(h, gain, router_p, wg, wu, wd):
    m = h.shape[0]
    nblocks = MOE_TOPK * m // MOE_ROWS + N_EXPERTS
    eid, gw = _moe_route(h, gain, router_p)
    dest, blk_e, nb_used = _moe_plan(eid[:, :MOE_TOPK], nblocks)
    xs = _moe_dispatch(h, dest, nblocks * MOE_ROWS)
    ys = _moe_ffn(xs, blk_e, nb_used, gain, wg, wu, wd)
    return _moe_combine(h, gw, ys, dest)


def _rope_pairs(x, cos, sin_signed):
    w = x.shape[1]
    lane = lax.broadcasted_iota(I32, x.shape, 1)
    partner = jnp.where((lane & 32) != 0, pltpu.roll(x, 32, axis=1), pltpu.roll(x, w - 32, axis=1))
    return x * cos + partner * sin_signed


def _mla_prep_kernel(ql_ref, kvl_ref, pe_ref, qn_ref, wq_ref, kvn_ref, wkv_ref, qgn_ref, qgr_ref,
                     kgn_ref, kgr_ref, cos_ref, sin_ref, qn_o, qr_o, kn_o, kr_o, v_o):
    nn = C_HEADS * C_NOPE
    left = lax.broadcasted_iota(I32, (ql_ref.shape[0], LANES), 1) < C_ROPE

    def head_norm(nope_of, rope_of, gn_ref, gr_ref, n_o, r_o):
        for p in range(C_HEADS // 2):
            pr = slice(LANES * p, LANES * (p + 1))
            rope = rope_of(p)
            r2 = rope * rope
            rs = []
            for hh, ss_rope in ((0, jnp.sum(jnp.where(left, r2, 0.0), axis=-1, keepdims=True)),
                                (1, jnp.sum(jnp.where(left, 0.0, r2), axis=-1, keepdims=True))):
                hs = slice(C_NOPE * (2 * p + hh), C_NOPE * (2 * p + hh + 1))
                nope = nope_of(hs)
                ss = jnp.sum(nope * nope, axis=-1, keepdims=True) + ss_rope
                rs.append(lax.rsqrt(ss * (1.0 / C_QK) + NORM_EPS))
                n_o[:, hs] = (nope * rs[hh] * gn_ref[:, hs]).astype(BF16)
            rope = rope * jnp.where(left, rs[0], rs[1]) * gr_ref[:, pr]
            r_o[:, pr] = _rope_pairs(rope, cos_ref[:, pr], sin_ref[:, pr]).astype(BF16)

    q = jnp.dot(_rms(ql_ref[...], qn_ref[...]).astype(BF16), wq_ref[...], preferred_element_type=F32)
    head_norm(lambda hs: q[:, hs], lambda p: q[:, nn + LANES * p:nn + LANES * (p + 1)],
              qgn_ref, qgr_ref, qn_o, qr_o)

    kv = jnp.dot(_rms(kvl_ref[...], kvn_ref[...]).astype(BF16), wkv_ref[...], preferred_element_type=F32)
    v_o[...] = kv[:, nn:].astype(BF16)
    pe = pe_ref[...]
    pe_pair = jnp.where(left, pe, pltpu.roll(pe, C_ROPE, axis=1))
    head_norm(lambda hs: kv[:, hs], lambda p: pe_pair, kgn_ref, kgr_ref, kn_o, kr_o)


def _mla_prep(u, seq, q_norm, wq_p, kv_norm, wkv_p, q_gain, k_gain, tm=256):
    m = u.shape[0]
    tm = min(tm, seq)
    nn, nr = C_HEADS * C_NOPE, C_HEADS * C_ROPE
    half = C_ROPE // 2
    freqs = ROPE_THETA ** (-jnp.arange(half, dtype=F32) / half)
    ang = jnp.arange(seq, dtype=F32)[:, None] * freqs[None, :]
    cos = jnp.tile(jnp.cos(ang), (1, 2 * C_HEADS))
    sin = jnp.tile(jnp.concatenate([-jnp.sin(ang), jnp.sin(ang)], axis=1), (1, C_HEADS))
    tile_gain = lambda g: jnp.tile(g, C_HEADS).reshape(1, -1)
    consts = [q_norm.reshape(1, -1), wq_p, kv_norm.reshape(1, -1), wkv_p,
              tile_gain(q_gain[:C_NOPE]), tile_gain(q_gain[C_NOPE:]),
              tile_gain(k_gain[:C_NOPE]), tile_gain(k_gain[C_NOPE:])]
    full = lambda a: pl.BlockSpec(a.shape, lambda i: (0, 0))
    nt = seq // tm
    tab = pl.BlockSpec((tm, nr), lambda i: (i % nt, 0))
    out = lambda w: jax.ShapeDtypeStruct((m, w), BF16)
    ospec = lambda w: pl.BlockSpec((tm, w), lambda i: (i, 0))
    return pl.pallas_call(
        _mla_prep_kernel,
        grid=(m // tm,),
        in_specs=[
            pl.BlockSpec((tm, C_LORA), lambda i: (i, CD_QLAT // C_LORA)),
            pl.BlockSpec((tm, C_LORA), lambda i: (i, CD_KVLAT // C_LORA)),
            pl.BlockSpec((tm, LANES), lambda i: (i, CD_PEIK // LANES)),
        ] + [full(a) for a in consts] + [tab, tab],
        out_specs=[ospec(nn), ospec(nr), ospec(nn), ospec(nr), ospec(nn)],
        out_shape=[out(nn), out(nr), out(nn), out(nr), out(nn)],
        compiler_params=_cparams(("parallel",)),
        name="mla_prep",
    )(u, u, u, *consts, cos, sin)


def _mla_attn_kernel(qn_ref, qr_ref, kn_ref, kr_ref, v_ref, o_ref, m_ref, l_ref, acc_ref, *, tq):
    qi = pl.program_id(2)
    lane = lax.broadcasted_iota(I32, qr_ref.shape, 1)
    qr = qr_ref[...]
    qs = []
    for hh in range(2):
        qr_h = jnp.where((lane // HEAD64) == hh, qr, jnp.zeros_like(qr))
        qs.append(jnp.concatenate([qn_ref[:, C_NOPE * hh:C_NOPE * (hh + 1)], qr_h], axis=1))
    m_ref[...] = jnp.full_like(m_ref, NEG_BIG)
    l_ref[...] = jnp.zeros_like(l_ref)
    acc_ref[...] = jnp.zeros_like(acc_ref)
    nt = (((1,), (1,)), ((), ()))

    def update(j, masked):
        rows = pl.ds(pl.multiple_of(j * tq, tq), tq)
        kr = kr_ref[rows, :]
        ss = []
        for hh in range(2):
            kc = jnp.concatenate([kn_ref[rows, C_NOPE * hh:C_NOPE * (hh + 1)], kr], axis=1)
            ss.append(lax.dot_general(qs[hh], kc, nt, preferred_element_type=F32) * C_QK ** -0.5)
        for hh in range(2):
            s = ss[hh]
            if masked:
                ri = lax.broadcasted_iota(I32, s.shape, 0)
                ci = lax.broadcasted_iota(I32, s.shape, 1)
                s = jnp.where(ci <= ri, s, NEG_BIG)
            p = _online_softmax_step(s, m_ref.at[hh], l_ref.at[hh], acc_ref.at[hh])
            acc_ref[hh] += jnp.dot(p, v_ref[rows, C_V * hh:C_V * (hh + 1)], preferred_element_type=F32)

    def body(j, carry):
        update(j, False)
        return carry

    lax.fori_loop(0, qi, body, 0)
    update(qi, True)
    for hh in range(2):
        o_ref[:, C_V * hh:C_V * (hh + 1)] = acc_ref[hh] / l_ref[hh]


def _mla_attn(qn, qr, kn, kr, v, batch, tq=512):
    m = qn.shape[0]
    seq = m // batch
    tq = min(tq, seq)
    nq = seq // tq
    qspec = lambda w: pl.BlockSpec((tq, w), lambda b, p, i: (b * nq + i, p))
    kspec = lambda w: pl.BlockSpec((seq, w), lambda b, p, i: (b, p))
    return pl.pallas_call(
        functools.partial(_mla_attn_kernel, tq=tq),
        grid=(batch, C_HEADS // 2, nq),
        in_specs=[qspec(2 * C_NOPE), qspec(2 * C_ROPE), kspec(2 * C_NOPE), kspec(2 * C_ROPE), kspec(2 * C_V)],
        out_specs=qspec(2 * C_V),
        out_shape=jax.ShapeDtypeStruct((m, C_HEADS * C_V), F32),
        scratch_shapes=[pltpu.VMEM((2, tq, LANES), F32), pltpu.VMEM((2, tq, LANES), F32),
                        pltpu.VMEM((2, tq, C_V), F32)],
        compiler_params=_cparams(("parallel", "parallel", "arbitrary")),
        name="mla_attn",
    )(qn, qr, kn, kr, v)


def _rel_bucket_table(n):
    d = np.arange(n)
    max_exact = REL_BUCKETS // 2
    nf = np.maximum(d, 1).astype(np.float32)
    large = max_exact + (np.log(nf / np.float32(max_exact)) / np.float32(math.log(REL_MAX_DIST / max_exact))
                         * np.float32(REL_BUCKETS - max_exact)).astype(np.int32)
    large = np.minimum(large, REL_BUCKETS - 1)
    return np.where(d < max_exact, d, large).astype(np.int32)


def _near_bias_kernel(bucket_ref, rel_ref, o_ref):
    for var in range(2):
        bk = bucket_ref[var]
        for hd in range(D_HEADS):
            acc = jnp.zeros(bk.shape, F32)
            for b in range(REL_BUCKETS):
                acc = jnp.where(bk == b, rel_ref[b, hd], acc)
            o_ref[var, hd] = acc - rel_ref[REL_BUCKETS - 1, hd]


def _near_bias(rel_bias):
    table = _rel_bucket_table(2 * Q_BLOCK)
    q = np.arange(Q_BLOCK)[:, None]
    j = np.arange(2 * Q_BLOCK)[None, :]
    dist0 = np.maximum(q - j, 0)
    dist1 = np.maximum(q - j + Q_BLOCK, 0)
    buckets = np.stack([table[dist0], table[dist1]]).astype(np.int32)
    return pl.pallas_call(
        _near_bias_kernel,
        in_specs=[pl.BlockSpec(memory_space=pltpu.VMEM), pl.BlockSpec(memory_space=pltpu.SMEM)],
        out_specs=pl.BlockSpec(memory_space=pltpu.VMEM),
        out_shape=jax.ShapeDtypeStruct((2, D_HEADS, Q_BLOCK, 2 * Q_BLOCK), F32),
        name="near_bias",
    )(jnp.asarray(buckets), rel_bias)


def _dsa_prep_kernel(dq_ref, dk_ref, dv_ref, iq_ref, peik_ref, iw_ref, qg_ref, kg_ref, ikg_ref,
                     q_o, k_o, v_o, iq_o, ika_o, ikb_o, iw_o):
    for hd in range(D_HEADS):
        sl = slice(D_DIM * hd, D_DIM * (hd + 1))
        q_o[:, sl] = _rms(dq_ref[:, sl], qg_ref[...]).astype(BF16)
    for hd in range(D_KV):
        sl = slice(D_DIM * hd, D_DIM * (hd + 1))
        k_o[:, sl] = _rms(dk_ref[:, sl], kg_ref[...]).astype(BF16)
    v_o[...] = dv_ref[...].astype(BF16)
    iq_o[...] = iq_ref[...].astype(BF16)
    x = peik_ref[...]
    right = lax.broadcasted_iota(I32, x.shape, 1) >= D_IDX_DIM
    x = jnp.where(right, x, 0.0)
    ms = jnp.sum(x * x, axis=-1, keepdims=True) * (1.0 / D_IDX_DIM)
    ik = (x * lax.rsqrt(ms + NORM_EPS) * ikg_ref[...]).astype(BF16)
    ikb_o[...] = ik
    ika_o[...] = pltpu.roll(ik.astype(F32), D_IDX_DIM, axis=1).astype(BF16)
    iw_o[...] = iw_ref[...] * D_HEADS ** -0.5


def _dsa_prep(u, q_gain, k_gain, ik_gain, tm=256):
    m = u.shape[0]
    tm = min(tm, m)
    blk = lambda w, off: pl.BlockSpec((tm, w), lambda i: (i, off // w))
    full = lambda a: pl.BlockSpec(a.shape, lambda i: (0, 0))
    ikg = jnp.concatenate([jnp.zeros((D_IDX_DIM,), F32), ik_gain]).reshape(1, -1)
    consts = [q_gain.reshape(1, -1), k_gain.reshape(1, -1), ikg]
    widths = [D_HEADS * D_DIM, D_KV * D_DIM, D_KV * D_DIM, D_HEADS * D_IDX_DIM, LANES, LANES, LANES]
    dtypes = [BF16] * 6 + [F32]
    return pl.pallas_call(
        _dsa_prep_kernel,
        grid=(m // tm,),
        in_specs=[blk(D_HEADS * D_DIM, CD_DQ), blk(D_KV * D_DIM, CD_DK), blk(D_KV * D_DIM, CD_DV),
                  blk(D_HEADS * D_IDX_DIM, CD_IQ), blk(LANES, CD_PEIK), blk(LANES, CD_IW)]
        + [full(a) for a in consts],
        out_specs=[pl.BlockSpec((tm, w), lambda i: (i, 0)) for w in widths],
        out_shape=[jax.ShapeDtypeStruct((m, w), dt) for w, dt in zip(widths, dtypes)],
        compiler_params=_cparams(("parallel",)),
        name="dsa_prep",
    )(u, u, u, u, u, u, *consts)


DSA_CK = 512


def _sort_key(score):
    bits = lax.bitcast_convert_type(score + 0.0, I32)
    return bits ^ ((bits >> 31) & 0x7FFFFFFF)


def _dsa_kernel(q_ref, iq_ref, iw_ref, k_ref, v_ref, ika_ref, ikb_ref, nbias_ref, o_ref,
                keys_ref, m_ref, l_ref, acc_ref, qs_ref, iqs_ref, *, topk, pos_bits):
    qb = pl.program_id(1)
    q0 = qb * Q_BLOCK
    far_end = jnp.maximum(q0 - Q_BLOCK, 0)
    nfar = (far_end + DSA_CK - 1) // DSA_CK
    near0 = pl.multiple_of(far_end, Q_BLOCK)
    nt = (((1,), (1,)), ((), ()))
    npair = D_HEADS // 2
    rep = D_HEADS // D_KV

    for p in range(npair):
        iqs_ref[Q_BLOCK * p:Q_BLOCK * (p + 1), :] = iq_ref[:, LANES * p:LANES * (p + 1)]
    for hd in range(D_HEADS):
        qs_ref[hd // rep, Q_BLOCK * (hd % rep):Q_BLOCK * (hd % rep + 1), :] = q_ref[:, D_DIM * hd:D_DIM * (hd + 1)]

    iw = iw_ref[...]
    iw_cols = [iw[:, hd:hd + 1] for hd in range(D_HEADS)]

    def index_scores(rows):
        iqs = iqs_ref[...]
        score = None
        for parity, k_ref_ in ((0, ika_ref), (1, ikb_ref)):
            d = lax.dot_general(iqs, k_ref_[rows, :], nt, preferred_element_type=F32) * D_IDX_DIM ** -0.5
            for p in range(npair):
                term = iw_cols[2 * p + parity] * jnp.maximum(d[Q_BLOCK * p:Q_BLOCK * (p + 1)], 0.0)
                score = term if score is None else score + term
        return score

    def far_scores(c, carry):
        rows = pl.ds(pl.multiple_of(c * DSA_CK, DSA_CK), DSA_CK)
        key = _sort_key(index_scores(rows))
        pos = c * DSA_CK + lax.broadcasted_iota(I32, key.shape, 1)
        keys_ref[c] = jnp.where(pos < far_end, key, INT_MIN)
        return carry

    lax.fori_loop(0, nfar, far_scores, 0)
    near_rows = pl.ds(near0, 2 * Q_BLOCK)
    keyn = _sort_key(index_scores(near_rows))
    posn = near0 + lax.broadcasted_iota(I32, keyn.shape, 1)
    qpos = q0 + lax.broadcasted_iota(I32, keyn.shape, 0)
    keyn = jnp.where(posn <= qpos, keyn, INT_MIN)
    keys_ref[nfar] = jnp.concatenate([keyn, jnp.full((Q_BLOCK, DSA_CK - 2 * Q_BLOCK), INT_MIN, I32)], axis=1)

    def chunk_pos(c):
        base = jnp.where(c == nfar, near0, c * DSA_CK)
        return base + lax.broadcasted_iota(I32, (Q_BLOCK, DSA_CK), 1)

    def count(pred):
        def body(c, acc):
            hit = pred(c, keys_ref[c]).astype(F32)
            part = hit[:, 0:LANES]
            for s in range(1, DSA_CK // LANES):
                part = part + hit[:, LANES * s:LANES * (s + 1)]
            return acc + part
        acc = lax.fori_loop(0, nfar + 1, body, jnp.zeros((Q_BLOCK, LANES), F32))
        return jnp.sum(acc, axis=-1, keepdims=True)

    count_ge = lambda cand: count(lambda c, kk: kk >= cand)
    kf = float(topk)
    thr = jnp.where(count_ge(jnp.zeros((Q_BLOCK, 1), I32)) >= kf, 0, INT_MIN).astype(I32)

    def thr_bit(i, thr):
        cand = thr | (1 << (30 - i))
        return jnp.where(count_ge(cand) >= kf, cand, thr)

    thr = lax.fori_loop(0, 31, thr_bit, thr)
    need = kf - count(lambda c, kk: kk > thr)
    surplus = (count_ge(thr) > kf) & (thr != INT_MIN)

    def tie_count(x):
        return count(lambda c, kk: (kk == thr) & (chunk_pos(c) < x))

    def cut_bit(i, cut):
        cand = cut | (1 << (pos_bits - 1 - i))
        return jnp.where(tie_count(cand) < need, cand, cut)

    take_all = jnp.full((Q_BLOCK, 1), 2 ** pos_bits - 1, I32)
    cut = lax.cond(jnp.max(surplus.astype(F32)) > 0.0,
                   lambda: lax.fori_loop(0, pos_bits, cut_bit, jnp.zeros((Q_BLOCK, 1), I32)),
                   lambda: take_all)

    def mask_bias(kk, pos):
        sel = ((kk > thr) | ((kk == thr) & (pos <= cut))) & (kk != INT_MIN)
        return jnp.where(sel, 0.0, NEG_BIG)

    m_ref[...] = jnp.full_like(m_ref, NEG_BIG)
    l_ref[...] = jnp.zeros_like(l_ref)
    acc_ref[...] = jnp.zeros_like(acc_ref)

    def attend(rows, bias_of):
        nk = bias_of(0).shape[-1]
        for g in range(D_KV):
            gs = slice(D_DIM * g, D_DIM * (g + 1))
            s = lax.dot_general(qs_ref[g], k_ref[rows, gs], nt, preferred_element_type=F32) * D_DIM ** -0.5
            s = (s.reshape(rep, Q_BLOCK, nk) + bias_of(g)).reshape(rep * Q_BLOCK, nk)
            p = _online_softmax_step(s, m_ref.at[g], l_ref.at[g], acc_ref.at[g])
            acc_ref[g] += jnp.dot(p, v_ref[rows, gs], preferred_element_type=F32)

    def far_attend(c, carry):
        rows = pl.ds(pl.multiple_of(c * DSA_CK, DSA_CK), DSA_CK)
        kk = keys_ref[c]
        pos = c * DSA_CK + lax.broadcasted_iota(I32, kk.shape, 1)
        mb = mask_bias(kk, pos)
        attend(rows, lambda g: mb[None])
        return carry

    lax.fori_loop(0, nfar, far_attend, 0)
    mbn = mask_bias(keys_ref[nfar][:, 0:2 * Q_BLOCK], posn)
    attend(near_rows, lambda g: mbn[None] + nbias_ref[0, rep * g:rep * (g + 1)])
    for hd in range(D_HEADS):
        rs = slice(Q_BLOCK * (hd % rep), Q_BLOCK * (hd % rep + 1))
        o_ref[:, D_DIM * hd:D_DIM * (hd + 1)] = acc_ref[hd // rep, rs, :] / l_ref[hd // rep, rs, :]


def _dsa(q, k, v, iq, ika, ikb, iw, near_bias, batch):
    m = q.shape[0]
    seq = m // batch
    nqb = seq // Q_BLOCK
    topk = min(D_TOPK_MAX, seq // 4)
    pos_bits = int(seq).bit_length()
    nck = max(seq // DSA_CK, 1)
    grp = D_HEADS // D_KV
    qspec = lambda w: pl.BlockSpec((Q_BLOCK, w), lambda b, i: (b * nqb + i, 0))
    kspec = lambda w: pl.BlockSpec((seq, w), lambda b, i: (b, 0))
    return pl.pallas_call(
        functools.partial(_dsa_kernel, topk=topk, pos_bits=pos_bits),
        grid=(batch, nqb),
        in_specs=[qspec(D_HEADS * D_DIM), qspec(D_HEADS * D_IDX_DIM), qspec(LANES),
                  kspec(D_KV * D_DIM), kspec(D_KV * D_DIM), kspec(LANES), kspec(LANES),
                  pl.BlockSpec((1, D_HEADS, Q_BLOCK, 2 * Q_BLOCK), lambda b, i: (jnp.minimum(i, 1), 0, 0, 0))],
        out_specs=qspec(D_HEADS * D_DIM),
        out_shape=jax.ShapeDtypeStruct((m, D_HEADS * D_DIM), F32),
        scratch_shapes=[pltpu.VMEM((nck + 1, Q_BLOCK, DSA_CK), I32),
                        pltpu.VMEM((D_KV, grp * Q_BLOCK, LANES), F32), pltpu.VMEM((D_KV, grp * Q_BLOCK, LANES), F32),
                        pltpu.VMEM((D_KV, grp * Q_BLOCK, D_DIM), F32),
                        pltpu.VMEM((D_KV, grp * Q_BLOCK, D_DIM), BF16),
                        pltpu.VMEM((D_HEADS // 2 * Q_BLOCK, LANES), BF16)],
        compiler_params=_cparams(("parallel", "arbitrary")),
        name="dsa",
    )(q, iq, iw, k, v, ika, ikb, near_bias)


def _pad_cols(w, width):
    return jnp.pad(w, ((0, 0), (0, width - w.shape[1])))


def _pack_ab_in(w):
    a_cols = 3 * A_WIDTH + sum(A_LORA)
    wa, wb = w[:, :a_cols], w[:, a_cols:]
    rkv, lora = wa[:, :3 * A_WIDTH], wa[:, 3 * A_WIDTH:]
    z, xbc, dt = wb[:, :B_WIDTH], wb[:, B_WIDTH:B_WIDTH + B_CONV_CH], wb[:, B_WIDTH + B_CONV_CH:]
    return jnp.concatenate([rkv, z, xbc, _pad_cols(lora, A_LORA_PAD), _pad_cols(dt, LANES)], axis=1).astype(BF16)


def _pack_cd_in(w):
    c_cols = 2 * C_LORA + C_ROPE
    wc, wd = w[:, :c_cols], w[:, c_cols:]
    q_lat, kv_lat, k_pe = wc[:, :C_LORA], wc[:, C_LORA:2 * C_LORA], wc[:, 2 * C_LORA:]
    sizes = [D_HEADS * D_DIM, D_KV * D_DIM, D_KV * D_DIM, D_HEADS * D_IDX_DIM, D_IDX_DIM, D_HEADS]
    cuts = np.cumsum(sizes)[:-1]
    dq, dk, dv, iq, ik, iw = jnp.split(wd, [int(c) for c in cuts], axis=1)
    return jnp.concatenate([q_lat, kv_lat, dq, dk, dv, iq, k_pe, ik, _pad_cols(iw, LANES)], axis=1).astype(BF16)


def _pack_lora(w2, a2, g2):
    out, off = [], 0
    for w in (w2, a2, g2):
        out.append(jnp.pad(w, ((off, A_LORA_PAD - off - w.shape[0]), (0, 0))).astype(BF16))
        off += w.shape[0]
    return out


def _pack_mla_q(wq_b):
    w = wq_b.reshape(C_LORA, C_HEADS, C_QK)
    return jnp.concatenate([w[:, :, :C_NOPE].reshape(C_LORA, -1), w[:, :, C_NOPE:].reshape(C_LORA, -1)],
                           axis=1).astype(BF16)


def _pack_mla_kv(wkv_b):
    w = wkv_b.reshape(C_LORA, C_HEADS, C_NOPE + C_V)
    return jnp.concatenate([w[:, :, :C_NOPE].reshape(C_LORA, -1), w[:, :, C_NOPE:].reshape(C_LORA, -1)],
                           axis=1).astype(BF16)


def _layer0_mix(h, batch, norm_g, ab_w_in, ab_w_out, a_shift_mu, a_w0, a_w2, a_a0, a_a2, a_g2, a_k_k, a_k_a,
                a_r_k, a_ln_g, a_ln_b, b_conv_w, b_conv_b, b_dt_bias, b_a_log, b_d, b_norm_g):
    seq = h.shape[0] // batch
    u = _matmul([(h, 0, D_MODEL)], [_pack_ab_in(ab_w_in)], gain=norm_g)
    row = lambda t: t.reshape(1, -1)
    w2p, a2p, g2p = _pack_lora(a_w2, a_a2, a_g2)
    mu_rkv = row(a_shift_mu[:3 * A_WIDTH])
    mu_lora = _pad_cols(row(a_shift_mu[3 * A_WIDTH:]), A_LORA_PAD)
    r, w, k, v, na, nb, g = _rwkv_pre(u, seq, mu_rkv, mu_lora, row(a_w0), row(a_a0), row(a_k_k), row(a_k_a),
                                      w2p, a2p, g2p)
    y = _rwkv_scan(r, w, k, v, na, nb, batch)
    ya = _rwkv_post(y, r, k, v, g, row(a_ln_g), row(a_ln_b), row(a_r_k))
    yb = _ssd(u, batch, b_conv_w, b_conv_b, b_dt_bias, b_a_log, b_d, b_norm_g)
    w_out = ab_w_out.astype(BF16)
    return _matmul([(ya, 0, A_WIDTH), (yb, 0, B_WIDTH)], [w_out[:A_WIDTH], w_out[A_WIDTH:]], res=h)


def _layer1_mix(h, batch, norm_g, rel_bias, cd_w_in, cd_w_out, c_q_norm, c_wq_b, c_kv_norm, c_wkv_b,
                c_q_gain, c_k_gain, d_q_gain, d_k_gain, d_ik_gain):
    seq = h.shape[0] // batch
    u = _matmul([(h, 0, D_MODEL)], [_pack_cd_in(cd_w_in)], gain=norm_g, tn=256)
    qn, qr, kn, kr, v = _mla_prep(u, seq, c_q_norm, _pack_mla_q(c_wq_b), c_kv_norm, _pack_mla_kv(c_wkv_b),
                                  c_q_gain, c_k_gain)
    yc = _mla_attn(qn, qr, kn, kr, v, batch)
    dq, dk, dv, iq, ika, ikb, iw = _dsa_prep(u, d_q_gain, d_k_gain, d_ik_gain)
    yd = _dsa(dq, dk, dv, iq, ika, ikb, iw, _near_bias(rel_bias), batch)
    w_out = cd_w_out.astype(BF16)
    half = C_HEADS * C_V
    return _matmul([(yc, 0, half), (yd, 0, D_HEADS * D_DIM)], [w_out[:half], w_out[half:]], res=h)


def _memory_attention(h, mem2, batch, gq, gkv, wq, wk, wv, wo, q_gain, k_gain):
    wkv = jnp.concatenate([wk, wv], axis=1).astype(BF16)
    mem_kv = _matmul([(mem2, 0, D_MODEL)], [wkv], gain=gkv)
    return _xattn(h, mem_kv, batch, gq, wq.astype(BF16), q_gain, k_gain, wo.astype(BF16))


def kernel(x, mem, rel_bias, norm_mix, norm_mem_q, norm_mem_kv, norm_ffn, xa_wq, xa_wk, xa_wv, xa_wo, xa_q_gain, xa_k_gain, ab_w_in, ab_w_out, a_shift_mu, a_w0, a_w2, a_a0, a_a2, a_g2, a_k_k, a_k_a, a_r_k, a_ln_g, a_ln_b, b_conv_w, b_conv_b, b_dt_bias, b_a_log, b_d, b_norm_g, ffn_w_gate, ffn_w_up, ffn_w_down, cd_w_in, cd_w_out, c_q_norm, c_wq_b, c_kv_norm, c_wkv_b, c_q_gain, c_k_gain, d_q_gain, d_k_gain, d_ik_gain, moe_router, moe_w_gate, moe_w_up, moe_w_down):
    batch, seq, d = x.shape
    h = x.reshape(batch * seq, d)
    mem2 = mem.reshape(-1, d)
    depth = norm_mix.shape[0]
    for layer in range(depth):
        i = layer // 2
        if layer % 2 == 0:
            h = _layer0_mix(h, batch, norm_mix[layer], ab_w_in[i], ab_w_out[i], a_shift_mu[i], a_w0[i], a_w2[i],
                            a_a0[i], a_a2[i], a_g2[i], a_k_k[i], a_k_a[i], a_r_k[i], a_ln_g[i], a_ln_b[i],
                            b_conv_w[i], b_conv_b[i], b_dt_bias[i], b_a_log[i], b_d[i], b_norm_g[i])
        else:
            h = _layer1_mix(h, batch, norm_mix[layer], rel_bias, cd_w_in[i], cd_w_out[i], c_q_norm[i], c_wq_b[i],
                            c_kv_norm[i], c_wkv_b[i], c_q_gain[i], c_k_gain[i], d_q_gain[i], d_k_gain[i],
                            d_ik_gain[i])
        h = _memory_attention(h, mem2, batch, norm_mem_q[layer], norm_mem_kv[layer], xa_wq[layer], xa_wk[layer],
                              xa_wv[layer], xa_wo[layer], xa_q_gain[layer], xa_k_gain[layer])
        if layer % 2 == 0:
            h = _swiglu(h, norm_ffn[layer], ffn_w_gate[i].astype(BF16), ffn_w_up[i].astype(BF16),
                        ffn_w_down[i].astype(BF16))
        else:
            router_p = _pad_cols(moe_router[i], LANES).astype(BF16)
            h = _moe(h, norm_ffn[layer], router_p, moe_w_gate[i].astype(BF16), moe_w_up[i].astype(BF16),
                     moe_w_down[i].astype(BF16))
    return h.reshape(batch, seq, d)
```

```python
import functools
import math

import numpy as np
import jax
import jax.numpy as jnp
from jax import lax
from jax.experimental import pallas as pl
from jax.experimental.pallas import tpu as pltpu

F32 = jnp.float32
BF16 = jnp.bfloat16
I32 = jnp.int32
I16 = jnp.int16
HALF16 = 1 << 15
HIGHEST = lax.Precision.HIGHEST

V7X_VMEM_BYTES = 64 * 1024 * 1024
VMEM_LIMIT = V7X_VMEM_BYTES - 8 * 1024 * 1024
LANES = 128

NORM_EPS = 1e-6
D_MODEL = 2048
HEAD64 = 64

A_WIDTH = 1024
A_LORA = (64, 64, 160)
A_LORA_PAD = 384
A_LN_EPS = 1e-5 * (HEAD64 / 8) ** 2
B_WIDTH = 1024
B_HEADS = 16
B_GROUPS = 4
B_STATE = 128
B_CONV = 4
B_CHUNK = 128
B_CONV_CH = B_WIDTH + 2 * B_GROUPS * B_STATE
AB_R, AB_K, AB_V, AB_Z, AB_XBC, AB_LORA, AB_DT, AB_COLS_PAD = 0, 1024, 2048, 3072, 4096, 6144, 6528, 6656

C_HEADS = 8
C_NOPE = 128
C_ROPE = 64
C_QK = C_NOPE + C_ROPE
C_V = 128
C_LORA = 512
ROPE_THETA = 10000.0
D_HEADS = 8
D_KV = 2
D_DIM = 128
D_IDX_DIM = 64
D_TOPK_MAX = 256
Q_BLOCK = 128
REL_BUCKETS = 32
REL_MAX_DIST = 128
CD_QLAT, CD_KVLAT, CD_DQ, CD_DK, CD_DV, CD_IQ, CD_PEIK, CD_IW, CD_COLS_PAD = (
    0, 512, 1024, 2048, 2304, 2560, 3072, 3200, 3584)

X_HEADS = 4
X_DIM = 128
N_EXPERTS = 8

NEG_BIG = -1e30
INT_MIN = -2 ** 31


def _cparams(sem):
    return pltpu.CompilerParams(dimension_semantics=sem, vmem_limit_bytes=VMEM_LIMIT)


def _rms(x, g, eps=NORM_EPS):
    return x * lax.rsqrt(jnp.mean(x * x, axis=-1, keepdims=True) + eps) * g


def _softplus(x):
    return jnp.maximum(x, 0.0) + jnp.log(1.0 + jnp.exp(-jnp.abs(x)))


def _silu(x):
    return x * jax.nn.sigmoid(x)


def _online_softmax_step(s, m_ref, l_ref, acc_ref):
    cols = [s[:, LANES * c:LANES * (c + 1)] for c in range(s.shape[1] // LANES)]
    mx = cols[0]
    for c in cols[1:]:
        mx = jnp.maximum(mx, c)
    m_old = m_ref[...]
    m_new = jnp.maximum(m_old, jnp.max(mx, axis=-1, keepdims=True))
    alpha = jnp.exp(m_old - m_new)
    ps = [jnp.exp(c - m_new) for c in cols]
    rs = ps[0]
    for p in ps[1:]:
        rs = rs + p
    l_ref[...] = alpha * l_ref[...] + jnp.sum(rs, axis=-1, keepdims=True)
    acc_ref[...] = alpha * acc_ref[...]
    m_ref[...] = m_new
    return jnp.concatenate(ps, axis=1).astype(BF16)


def _half_sum_bcast(x):
    left = lax.broadcasted_iota(I32, x.shape, 1) < HEAD64
    s0 = jnp.sum(jnp.where(left, x, 0.0), axis=1, keepdims=True)
    s1 = jnp.sum(jnp.where(left, 0.0, x), axis=1, keepdims=True)
    return jnp.where(left, s0, s1)


def _mm_kernel(*refs, n_x, has_norm, has_res):
    x_refs = refs[:n_x]
    pos = n_x
    g_ref = refs[pos] if has_norm else None
    pos += int(has_norm)
    w_refs = refs[pos:pos + n_x]
    pos += n_x
    res_ref = refs[pos] if has_res else None
    pos += int(has_res)
    o_ref = refs[pos]
    xn_refs = refs[pos + 1:]

    @pl.when(pl.program_id(1) == 0)
    def _():
        for x_ref, xn_ref in zip(x_refs, xn_refs):
            x = x_ref[...].astype(F32)
            if has_norm:
                x = _rms(x, g_ref[...])
            xn_ref[...] = x.astype(BF16)

    acc = None
    for xn_ref, w_ref in zip(xn_refs, w_refs):
        d = jnp.dot(xn_ref[...], w_ref[...], preferred_element_type=F32)
        acc = d if acc is None else acc + d
    if has_res:
        acc = acc + res_ref[...]
    o_ref[...] = acc


def _matmul(xs, ws, *, gain=None, res=None, tm=1024, tn=512):
    m = xs[0][0].shape[0]
    n = ws[0].shape[1]
    tm = min(tm, m)
    tn = min(tn, n)
    assert m % tm == 0 and n % tn == 0
    in_specs, args, scratch = [], [], []
    for arr, cb, width in xs:
        in_specs.append(pl.BlockSpec((tm, width), lambda i, j, cb=cb: (i, cb)))
        args.append(arr)
        scratch.append(pltpu.VMEM((tm, width), BF16))
    if gain is not None:
        in_specs.append(pl.BlockSpec((1, gain.shape[-1]), lambda i, j: (0, 0)))
        args.append(gain.reshape(1, -1))
    for (arr, cb, width), w in zip(xs, ws):
        assert w.shape[0] == width
        in_specs.append(pl.BlockSpec((width, tn), lambda i, j: (0, j)))
        args.append(w)
    if res is not None:
        in_specs.append(pl.BlockSpec((tm, tn), lambda i, j: (i, j)))
        args.append(res)
    return pl.pallas_call(
        functools.partial(_mm_kernel, n_x=len(xs), has_norm=gain is not None, has_res=res is not None),
        grid=(m // tm, n // tn),
        in_specs=in_specs,
        out_specs=pl.BlockSpec((tm, tn), lambda i, j: (i, j)),
        out_shape=jax.ShapeDtypeStruct((m, n), F32),
        scratch_shapes=scratch,
        compiler_params=_cparams(("parallel", "arbitrary")),
        name="matmul",
    )(*args)


def _rwkv_pre_kernel(rkv_ref, lora_ref, rkvp_ref, lorap_ref, mu_rkv_ref, mu_lora_ref,
                     w0_ref, a0_ref, kk_ref, ka_ref, w2_ref, a2_ref, g2_ref,
                     r_o, w_o, k_o, v_o, na_o, nb_o, g_o, *, tiles_per_seq):
    first = (pl.program_id(0) % tiles_per_seq) == 0
    tm = rkv_ref.shape[0]

    def shift_mix(x, prev_rows, mu):
        prev_last = jnp.where(first, 0.0, prev_rows[7:8, :])
        xs = pltpu.roll(x, 1, axis=0)
        row = lax.broadcasted_iota(I32, x.shape, 0)
        xs = jnp.where(row == 0, prev_last, xs)
        return x + (xs - x) * mu

    lo = shift_mix(lora_ref[...], lorap_ref[...], mu_lora_ref[...])
    lane = lax.broadcasted_iota(I32, lo.shape, 1)
    act = jnp.where(lane < A_LORA[0], jnp.tanh(lo),
                    jnp.where(lane < A_LORA[0] + A_LORA[1], lo, jax.nn.sigmoid(lo))).astype(BF16)
    dw = jnp.dot(act, w2_ref[...], preferred_element_type=F32)
    da = jnp.dot(act, a2_ref[...], preferred_element_type=F32)
    g_o[...] = jnp.dot(act, g2_ref[...], preferred_element_type=F32)

    for p in range(A_WIDTH // LANES):
        sl = slice(LANES * p, LANES * (p + 1))

        def mixed(off):
            s2 = slice(off + LANES * p, off + LANES * (p + 1))
            return shift_mix(rkv_ref[:, s2], rkvp_ref[:, s2], mu_rkv_ref[:, s2])

        r_o[:, sl] = mixed(AB_R)
        v_o[:, sl] = mixed(AB_V)
        kx = mixed(AB_K)
        logw = -_softplus(-(w0_ref[:, sl] + dw[:, sl])) - 0.5
        w_o[:, sl] = jnp.exp(-jnp.exp(logw))
        a = jax.nn.sigmoid(a0_ref[:, sl] + da[:, sl])
        kk = kx * kk_ref[:, sl]
        kk = kk * lax.rsqrt(jnp.maximum(_half_sum_bcast(kk * kk), 1e-24))
        k_o[:, sl] = kx * (1.0 + (a - 1.0) * ka_ref[:, sl])
        na_o[:, sl] = -kk
        nb_o[:, sl] = kk * a


def _rwkv_pre(u, seq, mu_rkv, mu_lora, w0, a0, k_k, k_a, w2p, a2p, g2p, tm=256):
    m = u.shape[0]
    tm = min(tm, seq)
    row = lambda w: pl.BlockSpec((1, w), lambda i: (0, 0))
    full = lambda a: pl.BlockSpec(a.shape, lambda i: (0, 0))
    prev = lambda i: jnp.maximum(i * (tm // 8) - 1, 0)
    out = jax.ShapeDtypeStruct((m, A_WIDTH), F32)
    return pl.pallas_call(
        functools.partial(_rwkv_pre_kernel, tiles_per_seq=seq // tm),
        grid=(m // tm,),
        in_specs=[
            pl.BlockSpec((tm, 3 * A_WIDTH), lambda i: (i, 0)),
            pl.BlockSpec((tm, A_LORA_PAD), lambda i: (i, AB_LORA // A_LORA_PAD)),
            pl.BlockSpec((8, 3 * A_WIDTH), lambda i: (prev(i), 0)),
            pl.BlockSpec((8, A_LORA_PAD), lambda i: (prev(i), AB_LORA // A_LORA_PAD)),
            row(3 * A_WIDTH), row(A_LORA_PAD), row(A_WIDTH), row(A_WIDTH), row(A_WIDTH), row(A_WIDTH),
            full(w2p), full(a2p), full(g2p),
        ],
        out_specs=[pl.BlockSpec((tm, A_WIDTH), lambda i: (i, 0))] * 7,
        out_shape=[out] * 7,
        compiler_params=_cparams(("parallel",)),
        name="rwkv_pre",
    )(u, u, u, u, mu_rkv, mu_lora, w0, a0, k_k, k_a, w2p, a2p, g2p)


RWKV_CHUNK = 64
RWKV_PAIR_GROUP = 8


def _rwkv_scan_kernel(r_ref, w_ref, k_ref, v_ref, a_ref, b_ref, y_ref, s_ref, vt_ref, yt_ref):
    npairs = s_ref.shape[0]

    @pl.when(pl.program_id(1) == 0)
    def _():
        s_ref[...] = jnp.zeros_like(s_ref)

    lane = lax.broadcasted_iota(I32, (HEAD64, LANES), 1)
    left = lane < HEAD64
    lane64 = lane & (HEAD64 - 1)

    def pair_transpose(x):
        xt = jnp.concatenate([x, x], axis=0).T
        return jnp.where(left, xt[0:HEAD64], xt[HEAD64:2 * HEAD64])

    def two_terms(x):
        hi = x.astype(BF16)
        return jnp.concatenate([hi, (x - hi.astype(F32)).astype(BF16)], axis=1)

    for p in range(npairs):
        vt_ref[p] = pair_transpose(v_ref[:, LANES * p:LANES * (p + 1)])
    yt_ref[...] = jnp.zeros_like(yt_ref)

    ri = lax.broadcasted_iota(I32, (2 * LANES, LANES), 0)
    ci = lax.broadcasted_iota(I32, (2 * LANES, LANES), 1)
    ones_blk = (((ri // HEAD64) & 1) == (ci // HEAD64)).astype(BF16)

    def half_sum_mxu(x, split):
        if not split:
            return jnp.dot(x.astype(BF16), ones_blk[0:LANES], preferred_element_type=F32)
        return jnp.dot(two_terms(x), ones_blk, preferred_element_type=F32)

    def step8(t8, carry):
        rows8 = pl.ds(pl.multiple_of(t8 * 8, 8), 8)
        unroll = (LANES - t8 * 8) & (LANES - 1)
        for g0 in range(0, npairs, RWKV_PAIR_GROUP):
            group = range(g0, min(g0 + RWKV_PAIR_GROUP, npairs))
            rows = {p: [ref[rows8, LANES * p:LANES * (p + 1)] for ref in (a_ref, w_ref, b_ref, k_ref, r_ref)]
                    for p in group}
            s = {p: s_ref[p] for p in group}
            vt8 = {p: pltpu.roll(vt_ref[p], unroll, axis=1) for p in group}
            for j in range(8):
                sel = lane64 == t8 * 8 + j
                for p in group:
                    a_row, w_row, b_row, k_row, r_row = (x8[j:j + 1, :] for x8 in rows[p])
                    sa = half_sum_mxu(s[p] * a_row, True)
                    vc = jnp.take_along_axis(vt8[p], jnp.where(left, j, HEAD64 + j), axis=1)
                    s[p] = s[p] * w_row + sa * b_row + vc * k_row
                    yt_ref[p] = jnp.where(sel, half_sum_mxu(s[p] * r_row, False), yt_ref[p])
            for p in group:
                s_ref[p] = s[p]
        return carry

    lax.fori_loop(0, RWKV_CHUNK // 8, step8, 0)

    for p in range(npairs):
        y_ref[:, LANES * p:LANES * (p + 1)] = pair_transpose(yt_ref[p])


def _rwkv_scan(r, w, k, v, na, nb, batch):
    m = r.shape[0]
    seq = m // batch
    nchunk = seq // RWKV_CHUNK
    npairs = A_WIDTH // LANES
    spec = pl.BlockSpec((RWKV_CHUNK, A_WIDTH), lambda b, c: (b * nchunk + c, 0))
    pair_scratch = pltpu.VMEM((npairs, HEAD64, LANES), F32)
    return pl.pallas_call(
        _rwkv_scan_kernel,
        grid=(batch, nchunk),
        in_specs=[spec] * 6,
        out_specs=spec,
        out_shape=jax.ShapeDtypeStruct((m, A_WIDTH), F32),
        scratch_shapes=[pair_scratch, pair_scratch, pair_scratch],
        compiler_params=_cparams(("parallel", "arbitrary")),
        name="rwkv_scan",
    )(r, w, k, v, na, nb)


def _rwkv_post_kernel(y_ref, r_ref, k_ref, v_ref, g_ref, lng_ref, lnb_ref, rk_ref, o_ref):
    for p in range(A_WIDTH // LANES):
        sl = slice(LANES * p, LANES * (p + 1))
        y = y_ref[:, sl]
        mean = _half_sum_bcast(y) * (1.0 / HEAD64)
        d = y - mean
        var = _half_sum_bcast(d * d) * (1.0 / HEAD64)
        yn = d * lax.rsqrt(var + A_LN_EPS) * lng_ref[:, sl] + lnb_ref[:, sl]
        bonus = _half_sum_bcast(r_ref[:, sl] * k_ref[:, sl] * rk_ref[:, sl]) * v_ref[:, sl]
        o_ref[:, sl] = (yn + bonus) * g_ref[:, sl]


def _rwkv_post(y, r, k, v, g, ln_g, ln_b, r_k, tm=256):
    m = y.shape[0]
    tm = min(tm, m)
    spec = pl.BlockSpec((tm, A_WIDTH), lambda i: (i, 0))
    row = pl.BlockSpec((1, A_WIDTH), lambda i: (0, 0))
    return pl.pallas_call(
        _rwkv_post_kernel,
        grid=(m // tm,),
        in_specs=[spec] * 5 + [row] * 3,
        out_specs=spec,
        out_shape=jax.ShapeDtypeStruct((m, A_WIDTH), F32),
        compiler_params=_cparams(("parallel",)),
        name="rwkv_post",
    )(y, r, k, v, g, ln_g, ln_b, r_k)


def _ssd_kernel(z_ref, xbc_ref, dt_ref, cw_ref, cb_ref, dtb_ref, alog_ref, dskip_ref, ng_ref,
                expand_ref, o_ref, st_ref, tail_ref):
    lc = B_CHUNK

    @pl.when(pl.program_id(1) == 0)
    def _():
        st_ref[...] = jnp.zeros_like(st_ref)
        tail_ref[...] = jnp.zeros_like(tail_ref)

    x = xbc_ref[...]
    tail = tail_ref[...]
    row8 = lax.broadcasted_iota(I32, tail.shape, 0)
    conv = cb_ref[...] + cw_ref[B_CONV - 1:B_CONV, :] * x
    for j in range(1, B_CONV):
        xs = pltpu.roll(x, j, axis=0)
        top = jnp.where(row8 < j, pltpu.roll(tail, j, axis=0), xs[0:8])
        xs = jnp.concatenate([top, xs[8:]], axis=0)
        conv = conv + cw_ref[B_CONV - 1 - j:B_CONV - j, :] * xs
    tail_ref[...] = x[lc - 8:lc]
    act = _silu(conv)
    xs_in = act[:, 0:B_WIDTH]
    bm = act[:, B_WIDTH:B_WIDTH + B_GROUPS * B_STATE].astype(BF16)
    cm = act[:, B_WIDTH + B_GROUPS * B_STATE:].astype(BF16)

    dt = _softplus(dt_ref[...] + dtb_ref[...])
    a_neg = -jnp.exp(alog_ref[...])
    da = dt * a_neg
    ri = lax.broadcasted_iota(I32, (lc, lc), 0)
    ci = lax.broadcasted_iota(I32, (lc, lc), 1)
    causal = ci <= ri
    tri = causal.astype(F32)
    cum = jnp.dot(tri, da, precision=HIGHEST, preferred_element_type=F32)
    cum_t = jnp.dot(da.T, (ri <= ci).astype(F32), precision=HIGHEST,
                    preferred_element_type=F32)
    expand = expand_ref[...]
    widen = lambda t: jnp.dot(t, expand, precision=HIGHEST, preferred_element_type=F32)
    dt_full = widen(dt)
    ecum_full = widen(jnp.exp(cum))
    dte_full = widen(jnp.exp(cum[lc - 1:lc, :] - cum))
    xdt = xs_in * dt_full
    xdt_b = xdt.astype(BF16)
    xdte_b = (xdt * dte_full).astype(BF16)
    left = lax.broadcasted_iota(I32, (lc, LANES), 1) < HEAD64

    ys = []
    for g in range(B_GROUPS):
        gs = slice(B_STATE * g, B_STATE * (g + 1))
        cm_g = cm[:, gs]
        bm_g = bm[:, gs]
        cb = lax.dot_general(cm_g, bm_g, (((1,), (1,)), ((), ())), preferred_element_type=F32)
        bm_t = bm_g.T
        pairs_per_group = B_HEADS // B_GROUPS // 2
        for q in range(pairs_per_group):
            p = g * pairs_per_group + q
            sl = slice(LANES * p, LANES * (p + 1))
            yd = []
            for h in (2 * p, 2 * p + 1):
                seg = cum[:, h:h + 1] - cum_t[h:h + 1, :]
                dec = jnp.where(causal, jnp.exp(jnp.minimum(seg, 0.0)), 0.0)
                yd.append(jnp.dot((cb * dec).astype(BF16), xdt_b[:, sl], preferred_element_type=F32))
            y_diag = jnp.where(left, yd[0], yd[1])
            st = st_ref[p]
            y_off = jnp.dot(cm_g, st.astype(BF16), preferred_element_type=F32) * ecum_full[:, sl]
            ys.append(y_diag + y_off)
            st_ref[p] = st * ecum_full[lc - 1:lc, sl] + jnp.dot(bm_t, xdte_b[:, sl],
                                                                preferred_element_type=F32)
    y = jnp.concatenate(ys, axis=1)
    y = (y + dskip_ref[...] * xs_in) * _silu(z_ref[...])
    gw = B_WIDTH // B_GROUPS
    for g in range(B_GROUPS):
        gs = slice(gw * g, gw * (g + 1))
        yg = y[:, gs]
        o_ref[:, gs] = yg * lax.rsqrt(jnp.mean(yg * yg, axis=-1, keepdims=True) + NORM_EPS) * ng_ref[:, gs]


def _ssd(u, batch, conv_w, conv_b, dt_bias, a_log, d_skip, norm_g):
    m = u.shape[0]
    seq = m // batch
    nchunk = seq // B_CHUNK
    rows = lambda b, c: b * nchunk + c
    pad16 = lambda t: jnp.pad(t.reshape(1, -1), ((0, 0), (0, LANES - B_HEADS)))
    expand = (np.arange(LANES)[:, None] == (np.arange(B_WIDTH)[None, :] // HEAD64)).astype(np.float32)
    full = lambda a: pl.BlockSpec(a.shape, lambda b, c: (0,) * a.ndim)
    args = [conv_w, conv_b.reshape(1, -1), pad16(dt_bias), pad16(a_log),
            jnp.repeat(d_skip, HEAD64).reshape(1, -1), norm_g.reshape(1, -1), jnp.asarray(expand)]
    return pl.pallas_call(
        _ssd_kernel,
        grid=(batch, nchunk),
        in_specs=[
            pl.BlockSpec((B_CHUNK, B_WIDTH), lambda b, c: (rows(b, c), AB_Z // B_WIDTH)),
            pl.BlockSpec((B_CHUNK, B_CONV_CH), lambda b, c: (rows(b, c), AB_XBC // B_CONV_CH)),
            pl.BlockSpec((B_CHUNK, LANES), lambda b, c: (rows(b, c), AB_DT // LANES)),
        ] + [full(a) for a in args],
        out_specs=pl.BlockSpec((B_CHUNK, B_WIDTH), lambda b, c: (rows(b, c), 0)),
        out_shape=jax.ShapeDtypeStruct((m, B_WIDTH), F32),
        scratch_shapes=[pltpu.VMEM((B_HEADS // 2, B_STATE, LANES), F32),
                        pltpu.VMEM((8, B_CONV_CH), F32)],
        compiler_params=_cparams(("parallel", "arbitrary")),
        name="ssd",
    )(u, u, u, *args)


def _xattn_kernel(h_ref, gq_ref, wq_ref, kv_ref, qg_ref, kg_ref, wo_ref, o_ref):
    h = h_ref[...]
    hn = _rms(h, gq_ref[...]).astype(BF16)
    q = jnp.dot(hn, wq_ref[...], preferred_element_type=F32)
    outs = []
    for hd in range(X_HEADS):
        sl = slice(X_DIM * hd, X_DIM * (hd + 1))
        qh = _rms(q[:, sl], qg_ref[...]).astype(BF16)
        kh = _rms(kv_ref[:, sl], kg_ref[...]).astype(BF16)
        vh = kv_ref[:, X_HEADS * X_DIM + X_DIM * hd:X_HEADS * X_DIM + X_DIM * (hd + 1)].astype(BF16)
        s = lax.dot_general(qh, kh, (((1,), (1,)), ((), ())), preferred_element_type=F32) * X_DIM ** -0.5
        e = jnp.exp(s - jnp.max(s, axis=-1, keepdims=True))
        p = e / jnp.sum(e, axis=-1, keepdims=True)
        outs.append(jnp.dot(p.astype(BF16), vh, preferred_element_type=F32))
    o = jnp.concatenate(outs, axis=1).astype(BF16)
    o_ref[...] = h + jnp.dot(o, wo_ref[...], preferred_element_type=F32)


def _xattn(h, mem_kv, batch, gq, wq, q_gain, k_gain, wo, tm=512):
    m, d = h.shape
    seq = m // batch
    tm = min(tm, seq)
    mlen = mem_kv.shape[0] // batch
    nt = seq // tm
    full = lambda a: pl.BlockSpec(a.shape, lambda b, i: (0, 0))
    args = [gq.reshape(1, -1), wq, mem_kv, q_gain.reshape(1, -1), k_gain.reshape(1, -1), wo]
    specs = [full(a) for a in args]
    specs[2] = pl.BlockSpec((mlen, mem_kv.shape[1]), lambda b, i: (b, 0))
    return pl.pallas_call(
        _xattn_kernel,
        grid=(batch, nt),
        in_specs=[pl.BlockSpec((tm, d), lambda b, i: (b * nt + i, 0))] + specs,
        out_specs=pl.BlockSpec((tm, d), lambda b, i: (b * nt + i, 0)),
        out_shape=jax.ShapeDtypeStruct((m, d), F32),
        compiler_params=_cparams(("parallel", "parallel")),
        name="xattn",
    )(h, *args)


def _swiglu_kernel(h_ref, g_ref, wg_ref, wu_ref, wd_ref, o_ref, xn_ref, acc_ref):
    j = pl.program_id(1)

    @pl.when(j == 0)
    def _():
        xn_ref[...] = _rms(h_ref[...], g_ref[...]).astype(BF16)
        acc_ref[...] = jnp.zeros_like(acc_ref)

    xn = xn_ref[...]
    gate = jnp.dot(xn, wg_ref[...], preferred_element_type=F32)
    up = jnp.dot(xn, wu_ref[...], preferred_element_type=F32)
    acc_ref[...] += jnp.dot((_silu(gate) * up).astype(BF16), wd_ref[...], preferred_element_type=F32)

    @pl.when(j == pl.num_programs(1) - 1)
    def _():
        o_ref[...] = h_ref[...] + acc_ref[...]


def _swiglu(h, gain, wg, wu, wd, tm=512, tf=512):
    m, d = h.shape
    f = wg.shape[1]
    tm = min(tm, m)
    assert f % tf == 0
    return pl.pallas_call(
        _swiglu_kernel,
        grid=(m // tm, f // tf),
        in_specs=[
            pl.BlockSpec((tm, d), lambda i, j: (i, 0)),
            pl.BlockSpec((1, d), lambda i, j: (0, 0)),
            pl.BlockSpec((d, tf), lambda i, j: (0, j)),
            pl.BlockSpec((d, tf), lambda i, j: (0, j)),
            pl.BlockSpec((tf, d), lambda i, j: (j, 0)),
        ],
        out_specs=pl.BlockSpec((tm, d), lambda i, j: (i, 0)),
        out_shape=jax.ShapeDtypeStruct((m, d), F32),
        scratch_shapes=[pltpu.VMEM((tm, d), BF16), pltpu.VMEM((tm, d), F32)],
        compiler_params=_cparams(("parallel", "arbitrary")),
        name="swiglu",
    )(h, gain.reshape(1, -1), wg, wu, wd)


MOE_TOPK = 2
MOE_ROWS = 512
MOE_TOKENS = 256


def _moe_route_kernel(h_ref, g_ref, router_ref, eid_o, gw_o):
    xn = _rms(h_ref[...], g_ref[...]).astype(BF16)
    logits = jnp.dot(xn, router_ref[...], preferred_element_type=F32)
    lane = lax.broadcasted_iota(I32, logits.shape, 1)
    logits = jnp.where(lane < N_EXPERTS, logits, -jnp.inf)
    m1 = jnp.max(logits, axis=-1, keepdims=True)
    i1 = jnp.min(jnp.where(logits == m1, lane, LANES), axis=-1, keepdims=True)
    rest = jnp.where(lane == i1, -jnp.inf, logits)
    m2 = jnp.max(rest, axis=-1, keepdims=True)
    i2 = jnp.min(jnp.where(rest == m2, lane, LANES), axis=-1, keepdims=True)
    e2 = jnp.exp(m2 - m1)
    eid_o[...] = jnp.where(lane == 0, i1, jnp.where(lane == 1, i2, 0))
    gw_o[...] = jnp.where(lane == 0, 1.0 / (1.0 + e2), jnp.where(lane == 1, e2 / (1.0 + e2), 0.0))


def _moe_route(h, gain, router_p, tm=512):
    m, d = h.shape
    tm = min(tm, m)
    spec = pl.BlockSpec((tm, LANES), lambda i: (i, 0))
    return pl.pallas_call(
        _moe_route_kernel,
        grid=(m // tm,),
        in_specs=[pl.BlockSpec((tm, d), lambda i: (i, 0)), pl.BlockSpec((1, d), lambda i: (0, 0)),
                  pl.BlockSpec((d, LANES), lambda i: (0, 0))],
        out_specs=[spec, spec],
        out_shape=[jax.ShapeDtypeStruct((m, LANES), I32), jax.ShapeDtypeStruct((m, LANES), F32)],
        compiler_params=_cparams(("parallel",)),
        name="moe_route",
    )(h, gain.reshape(1, -1), router_p)


def _moe_plan(eid, nblocks):
    e = eid.reshape(-1)
    onehot = (e[:, None] == jnp.arange(N_EXPERTS, dtype=I32)[None, :]).astype(I32)
    csum = jnp.cumsum(onehot, axis=0)
    rank = jnp.sum(onehot * csum, axis=1) - 1
    counts = csum[-1]
    padded = (counts + MOE_ROWS - 1) // MOE_ROWS * MOE_ROWS
    gend = jnp.cumsum(padded)
    dest = jnp.sum(onehot * (gend - padded)[None, :], axis=1) + rank
    nb_used = gend[-1] // MOE_ROWS
    blk = jnp.arange(nblocks, dtype=I32)
    blk_e = jnp.minimum(jnp.sum((blk[:, None] * MOE_ROWS >= gend[None, :]).astype(I32), axis=1), N_EXPERTS - 1)
    blk_e = jnp.where(blk < nb_used, blk_e, blk_e[jnp.maximum(nb_used - 1, 0)])
    return dest.astype(I32), blk_e.astype(I32), nb_used.astype(I32).reshape(1)


def _row_copy(src_ref, src_row, dst_ref, dst_row, sem):
    return pltpu.make_async_copy(src_ref.at[pl.ds(src_row, 1), :], dst_ref.at[pl.ds(dst_row, 1), :], sem)


def _moe_dispatch_kernel(dest_ref, h_ref, xs_in_ref, xs_ref, sem):
    del xs_in_ref
    base = pl.program_id(0) * MOE_TOKENS

    def issue(r, carry):
        for c in range(MOE_TOPK):
            _row_copy(h_ref, r, xs_ref, dest_ref[MOE_TOPK * (base + r) + c], sem).start()
        return carry

    def drain(r, carry):
        for c in range(MOE_TOPK):
            _row_copy(h_ref, 0, xs_ref, 0, sem).wait()
        return carry

    lax.fori_loop(0, MOE_TOKENS, issue, 0)
    lax.fori_loop(0, MOE_TOKENS, drain, 0)


def _moe_dispatch(h, dest, rows):
    m, d = h.shape
    return pl.pallas_call(
        _moe_dispatch_kernel,
        grid_spec=pltpu.PrefetchScalarGridSpec(
            num_scalar_prefetch=1,
            grid=(m // MOE_TOKENS,),
            in_specs=[pl.BlockSpec((MOE_TOKENS, d), lambda i, dest: (i, 0)), pl.BlockSpec(memory_space=pl.ANY)],
            out_specs=pl.BlockSpec(memory_space=pl.ANY),
            scratch_shapes=[pltpu.SemaphoreType.DMA(())],
        ),
        out_shape=jax.ShapeDtypeStruct((rows, d), F32),
        input_output_aliases={2: 0},
        compiler_params=_cparams(("arbitrary",)),
        name="moe_dispatch",
    )(dest, h, jnp.zeros((rows, d), F32))


def _moe_ffn_kernel(be_ref, nb_ref, x_ref, g_ref, wg_ref, wu_ref, wd_ref, y_ref, xn_ref, acc_ref):
    del be_ref
    j = pl.program_id(1)
    live = pl.program_id(0) < nb_ref[0]
    last = j == pl.num_programs(1) - 1

    @pl.when(live & (j == 0))
    def _():
        xn_ref[...] = _rms(x_ref[...], g_ref[...]).astype(BF16)
        acc_ref[...] = jnp.zeros_like(acc_ref)

    @pl.when(live)
    def _():
        xn = xn_ref[...]
        gate = jnp.dot(xn, wg_ref[0], preferred_element_type=F32)
        up = jnp.dot(xn, wu_ref[0], preferred_element_type=F32)
        acc_ref[...] += jnp.dot((_silu(gate) * up).astype(BF16), wd_ref[0], preferred_element_type=F32)

    @pl.when(live & last)
    def _():
        y_ref[...] = acc_ref[...]

    @pl.when(jnp.logical_not(live) & last)
    def _():
        y_ref[...] = jnp.zeros_like(y_ref)


def _moe_ffn(xs, blk_e, nb_used, gain, wg, wu, wd, tf=256):
    rows, d = xs.shape
    f = wg.shape[2]
    nj = f // tf
    assert f % tf == 0 and rows % MOE_ROWS == 0
    jx = lambda b, j, nb: jnp.where(b < nb[0], j, nj - 1)
    return pl.pallas_call(
        _moe_ffn_kernel,
        grid_spec=pltpu.PrefetchScalarGridSpec(
            num_scalar_prefetch=2,
            grid=(rows // MOE_ROWS, nj),
            in_specs=[
                pl.BlockSpec((MOE_ROWS, d), lambda b, j, be, nb: (jnp.minimum(b, jnp.maximum(nb[0] - 1, 0)), 0)),
                pl.BlockSpec((1, d), lambda b, j, be, nb: (0, 0)),
                pl.BlockSpec((1, d, tf), lambda b, j, be, nb: (be[b], 0, jx(b, j, nb))),
                pl.BlockSpec((1, d, tf), lambda b, j, be, nb: (be[b], 0, jx(b, j, nb))),
                pl.BlockSpec((1, tf, d), lambda b, j, be, nb: (be[b], jx(b, j, nb), 0)),
            ],
            out_specs=pl.BlockSpec((MOE_ROWS, d), lambda b, j, be, nb: (b, 0)),
            scratch_shapes=[pltpu.VMEM((MOE_ROWS, d), BF16), pltpu.VMEM((MOE_ROWS, d), F32)],
        ),
        out_shape=jax.ShapeDtypeStruct((rows, d), F32),
        compiler_params=_cparams(("parallel", "arbitrary")),
        name="moe_ffn",
    )(blk_e, nb_used, xs, gain.reshape(1, -1), wg, wu, wd)


def _moe_combine_kernel(dest_ref, h_ref, gw_ref, ys_ref, o_ref, ybuf_ref, sem):
    base = pl.program_id(0) * MOE_TOKENS

    def issue(r, carry):
        for c in range(MOE_TOPK):
            _row_copy(ys_ref, dest_ref[MOE_TOPK * (base + r) + c], ybuf_ref.at[c], r, sem).start()
        return carry

    def drain(r, carry):
        for c in range(MOE_TOPK):
            _row_copy(ys_ref, 0, ybuf_ref.at[c], 0, sem).wait()
        return carry

    lax.fori_loop(0, MOE_TOKENS, issue, 0)
    lax.fori_loop(0, MOE_TOKENS, drain, 0)
    gw = gw_ref[...]
    o_ref[...] = h_ref[...] + (gw[:, 0:1] * ybuf_ref[0] + gw[:, 1:2] * ybuf_ref[1])


def _moe_combine(h, gw, ys, dest):
    m, d = h.shape
    tok = lambda w: pl.BlockSpec((MOE_TOKENS, w), lambda i, dest: (i, 0))
    return pl.pallas_call(
        _moe_combine_kernel,
        grid_spec=pltpu.PrefetchScalarGridSpec(
            num_scalar_prefetch=1,
            grid=(m // MOE_TOKENS,),
            in_specs=[tok(d), tok(LANES), pl.BlockSpec(memory_space=pl.ANY)],
            out_specs=tok(d),
            scratch_shapes=[pltpu.VMEM((MOE_TOPK, MOE_TOKENS, d), F32), pltpu.SemaphoreType.DMA(())],
        ),
        out_shape=jax.ShapeDtypeStruct((m, d), F32),
        compiler_params=_cparams(("arbitrary",)),
        name="moe_combine",
    )(dest, h, gw, ys)


def _moe(h, gain, router_p, wg, wu, wd):
    m = h.shape[0]
    nblocks = MOE_TOPK * m // MOE_ROWS + N_EXPERTS
    eid, gw = _moe_route(h, gain, router_p)
    dest, blk_e, nb_used = _moe_plan(eid[:, :MOE_TOPK], nblocks)
    xs = _moe_dispatch(h, dest, nblocks * MOE_ROWS)
    ys = _moe_ffn(xs, blk_e, nb_used, gain, wg, wu, wd)
    return _moe_combine(h, gw, ys, dest)


def _rope_pairs(x, cos, sin_signed):
    w = x.shape[1]
    lane = lax.broadcasted_iota(I32, x.shape, 1)
    partner = jnp.where((lane & 32) != 0, pltpu.roll(x, 32, axis=1), pltpu.roll(x, w - 32, axis=1))
    return x * cos + partner * sin_signed


def _mla_prep_kernel(ql_ref, kvl_ref, pe_ref, qn_ref, wq_ref, kvn_ref, wkv_ref, qgn_ref, qgr_ref,
                     kgn_ref, kgr_ref, cos_ref, sin_ref, qn_o, qr_o, kn_o, kr_o, v_o):
    nn = C_HEADS * C_NOPE
    left = lax.broadcasted_iota(I32, (ql_ref.shape[0], LANES), 1) < C_ROPE

    def head_norm(nope_of, rope_of, gn_ref, gr_ref, n_o, r_o):
        for p in range(C_HEADS // 2):
            pr = slice(LANES * p, LANES * (p + 1))
            rope = rope_of(p)
            r2 = rope * rope
            rs = []
            for hh, ss_rope in ((0, jnp.sum(jnp.where(left, r2, 0.0), axis=-1, keepdims=True)),
                                (1, jnp.sum(jnp.where(left, 0.0, r2), axis=-1, keepdims=True))):
                hs = slice(C_NOPE * (2 * p + hh), C_NOPE * (2 * p + hh + 1))
                nope = nope_of(hs)
                ss = jnp.sum(nope * nope, axis=-1, keepdims=True) + ss_rope
                rs.append(lax.rsqrt(ss * (1.0 / C_QK) + NORM_EPS))
                n_o[:, hs] = (nope * rs[hh] * gn_ref[:, hs]).astype(BF16)
            rope = rope * jnp.where(left, rs[0], rs[1]) * gr_ref[:, pr]
            r_o[:, pr] = _rope_pairs(rope, cos_ref[:, pr], sin_ref[:, pr]).astype(BF16)

    q = jnp.dot(_rms(ql_ref[...], qn_ref[...]).astype(BF16), wq_ref[...], preferred_element_type=F32)
    head_norm(lambda hs: q[:, hs], lambda p: q[:, nn + LANES * p:nn + LANES * (p + 1)],
              qgn_ref, qgr_ref, qn_o, qr_o)

    kv = jnp.dot(_rms(kvl_ref[...], kvn_ref[...]).astype(BF16), wkv_ref[...], preferred_element_type=F32)
    v_o[...] = kv[:, nn:].astype(BF16)
    pe = pe_ref[...]
    pe_pair = jnp.where(left, pe, pltpu.roll(pe, C_ROPE, axis=1))
    head_norm(lambda hs: kv[:, hs], lambda p: pe_pair, kgn_ref, kgr_ref, kn_o, kr_o)


def _mla_prep(u, seq, q_norm, wq_p, kv_norm, wkv_p, q_gain, k_gain, tm=256):
    m = u.shape[0]
    tm = min(tm, seq)
    nn, nr = C_HEADS * C_NOPE, C_HEADS * C_ROPE
    half = C_ROPE // 2
    freqs = ROPE_THETA ** (-jnp.arange(half, dtype=F32) / half)
    ang = jnp.arange(seq, dtype=F32)[:, None] * freqs[None, :]
    cos = jnp.tile(jnp.cos(ang), (1, 2 * C_HEADS))
    sin = jnp.tile(jnp.concatenate([-jnp.sin(ang), jnp.sin(ang)], axis=1), (1, C_HEADS))
    tile_gain = lambda g: jnp.tile(g, C_HEADS).reshape(1, -1)
    consts = [q_norm.reshape(1, -1), wq_p, kv_norm.reshape(1, -1), wkv_p,
              tile_gain(q_gain[:C_NOPE]), tile_gain(q_gain[C_NOPE:]),
              tile_gain(k_gain[:C_NOPE]), tile_gain(k_gain[C_NOPE:])]
    full = lambda a: pl.BlockSpec(a.shape, lambda i: (0, 0))
    nt = seq // tm
    tab = pl.BlockSpec((tm, nr), lambda i: (i % nt, 0))
    out = lambda w: jax.ShapeDtypeStruct((m, w), BF16)
    ospec = lambda w: pl.BlockSpec((tm, w), lambda i: (i, 0))
    return pl.pallas_call(
        _mla_prep_kernel,
        grid=(m // tm,),
        in_specs=[
            pl.BlockSpec((tm, C_LORA), lambda i: (i, CD_QLAT // C_LORA)),
            pl.BlockSpec((tm, C_LORA), lambda i: (i, CD_KVLAT // C_LORA)),
            pl.BlockSpec((tm, LANES), lambda i: (i, CD_PEIK // LANES)),
        ] + [full(a) for a in consts] + [tab, tab],
        out_specs=[ospec(nn), ospec(nr), ospec(nn), ospec(nr), ospec(nn)],
        out_shape=[out(nn), out(nr), out(nn), out(nr), out(nn)],
        compiler_params=_cparams(("parallel",)),
        name="mla_prep",
    )(u, u, u, *consts, cos, sin)


def _mla_attn_kernel(qn_ref, qr_ref, kn_ref, kr_ref, v_ref, o_ref, m_ref, l_ref, acc_ref, *, tq):
    qi = pl.program_id(2)
    lane = lax.broadcasted_iota(I32, qr_ref.shape, 1)
    qr = qr_ref[...]
    qs = []
    for hh in range(2):
        qr_h = jnp.where((lane // HEAD64) == hh, qr, jnp.zeros_like(qr))
        qs.append(jnp.concatenate([qn_ref[:, C_NOPE * hh:C_NOPE * (hh + 1)], qr_h], axis=1))
    m_ref[...] = jnp.full_like(m_ref, NEG_BIG)
    l_ref[...] = jnp.zeros_like(l_ref)
    acc_ref[...] = jnp.zeros_like(acc_ref)
    nt = (((1,), (1,)), ((), ()))

    def update(j, masked):
        rows = pl.ds(pl.multiple_of(j * tq, tq), tq)
        kr = kr_ref[rows, :]
        ss = []
        for hh in range(2):
            kc = jnp.concatenate([kn_ref[rows, C_NOPE * hh:C_NOPE * (hh + 1)], kr], axis=1)
            ss.append(lax.dot_general(qs[hh], kc, nt, preferred_element_type=F32) * C_QK ** -0.5)
        for hh in range(2):
            s = ss[hh]
            if masked:
                ri = lax.broadcasted_iota(I32, s.shape, 0)
                ci = lax.broadcasted_iota(I32, s.shape, 1)
                s = jnp.where(ci <= ri, s, NEG_BIG)
            p = _online_softmax_step(s, m_ref.at[hh], l_ref.at[hh], acc_ref.at[hh])
            acc_ref[hh] += jnp.dot(p, v_ref[rows, C_V * hh:C_V * (hh + 1)], preferred_element_type=F32)

    def body(j, carry):
        update(j, False)
        return carry

    lax.fori_loop(0, qi, body, 0)
    update(qi, True)
    for hh in range(2):
        o_ref[:, C_V * hh:C_V * (hh + 1)] = acc_ref[hh] / l_ref[hh]


def _mla_attn(qn, qr, kn, kr, v, batch, tq=512):
    m = qn.shape[0]
    seq = m // batch
    tq = min(tq, seq)
    nq = seq // tq
    qspec = lambda w: pl.BlockSpec((tq, w), lambda b, p, i: (b * nq + i, p))
    kspec = lambda w: pl.BlockSpec((seq, w), lambda b, p, i: (b, p))
    return pl.pallas_call(
        functools.partial(_mla_attn_kernel, tq=tq),
        grid=(batch, C_HEADS // 2, nq),
        in_specs=[qspec(2 * C_NOPE), qspec(2 * C_ROPE), kspec(2 * C_NOPE), kspec(2 * C_ROPE), kspec(2 * C_V)],
        out_specs=qspec(2 * C_V),
        out_shape=jax.ShapeDtypeStruct((m, C_HEADS * C_V), F32),
        scratch_shapes=[pltpu.VMEM((2, tq, LANES), F32), pltpu.VMEM((2, tq, LANES), F32),
                        pltpu.VMEM((2, tq, C_V), F32)],
        compiler_params=_cparams(("parallel", "parallel", "arbitrary")),
        name="mla_attn",
    )(qn, qr, kn, kr, v)


def _rel_bucket_table(n):
    d = np.arange(n)
    max_exact = REL_BUCKETS // 2
    nf = np.maximum(d, 1).astype(np.float32)
    large = max_exact + (np.log(nf / np.float32(max_exact)) / np.float32(math.log(REL_MAX_DIST / max_exact))
                         * np.float32(REL_BUCKETS - max_exact)).astype(np.int32)
    large = np.minimum(large, REL_BUCKETS - 1)
    return np.where(d < max_exact, d, large).astype(np.int32)


def _near_bias_kernel(bucket_ref, rel_ref, o_ref):
    for var in range(2):
        bk = bucket_ref[var]
        for hd in range(D_HEADS):
            acc = jnp.zeros(bk.shape, F32)
            for b in range(REL_BUCKETS):
                acc = jnp.where(bk == b, rel_ref[b, hd], acc)
            o_ref[var, hd] = acc - rel_ref[REL_BUCKETS - 1, hd]


def _near_bias(rel_bias):
    table = _rel_bucket_table(2 * Q_BLOCK)
    q = np.arange(Q_BLOCK)[:, None]
    j = np.arange(2 * Q_BLOCK)[None, :]
    dist0 = np.maximum(q - j, 0)
    dist1 = np.maximum(q - j + Q_BLOCK, 0)
    buckets = np.stack([table[dist0], table[dist1]]).astype(np.int32)
    return pl.pallas_call(
        _near_bias_kernel,
        in_specs=[pl.BlockSpec(memory_space=pltpu.VMEM), pl.BlockSpec(memory_space=pltpu.SMEM)],
        out_specs=pl.BlockSpec(memory_space=pltpu.VMEM),
        out_shape=jax.ShapeDtypeStruct((2, D_HEADS, Q_BLOCK, 2 * Q_BLOCK), F32),
        name="near_bias",
    )(jnp.asarray(buckets), rel_bias)


def _dsa_prep_kernel(dq_ref, dk_ref, dv_ref, iq_ref, peik_ref, iw_ref, qg_ref, kg_ref, ikg_ref,
                     q_o, k_o, v_o, iq_o, ika_o, ikb_o, iw_o):
    for hd in range(D_HEADS):
        sl = slice(D_DIM * hd, D_DIM * (hd + 1))
        q_o[:, sl] = _rms(dq_ref[:, sl], qg_ref[...]).astype(BF16)
    for hd in range(D_KV):
        sl = slice(D_DIM * hd, D_DIM * (hd + 1))
        k_o[:, sl] = _rms(dk_ref[:, sl], kg_ref[...]).astype(BF16)
    v_o[...] = dv_ref[...].astype(BF16)
    iq_o[...] = iq_ref[...].astype(BF16)
    x = peik_ref[...]
    right = lax.broadcasted_iota(I32, x.shape, 1) >= D_IDX_DIM
    x = jnp.where(right, x, 0.0)
    ms = jnp.sum(x * x, axis=-1, keepdims=True) * (1.0 / D_IDX_DIM)
    ik = (x * lax.rsqrt(ms + NORM_EPS) * ikg_ref[...]).astype(BF16)
    ikb_o[...] = ik
    ika_o[...] = pltpu.roll(ik.astype(F32), D_IDX_DIM, axis=1).astype(BF16)
    iw_o[...] = iw_ref[...] * D_HEADS ** -0.5


def _dsa_prep(u, q_gain, k_gain, ik_gain, tm=256):
    m = u.shape[0]
    tm = min(tm, m)
    blk = lambda w, off: pl.BlockSpec((tm, w), lambda i: (i, off // w))
    full = lambda a: pl.BlockSpec(a.shape, lambda i: (0, 0))
    ikg = jnp.concatenate([jnp.zeros((D_IDX_DIM,), F32), ik_gain]).reshape(1, -1)
    consts = [q_gain.reshape(1, -1), k_gain.reshape(1, -1), ikg]
    widths = [D_HEADS * D_DIM, D_KV * D_DIM, D_KV * D_DIM, D_HEADS * D_IDX_DIM, LANES, LANES, LANES]
    dtypes = [BF16] * 6 + [F32]
    return pl.pallas_call(
        _dsa_prep_kernel,
        grid=(m // tm,),
        in_specs=[blk(D_HEADS * D_DIM, CD_DQ), blk(D_KV * D_DIM, CD_DK), blk(D_KV * D_DIM, CD_DV),
                  blk(D_HEADS * D_IDX_DIM, CD_IQ), blk(LANES, CD_PEIK), blk(LANES, CD_IW)]
        + [full(a) for a in consts],
        out_specs=[pl.BlockSpec((tm, w), lambda i: (i, 0)) for w in widths],
        out_shape=[jax.ShapeDtypeStruct((m, w), dt) for w, dt in zip(widths, dtypes)],
        compiler_params=_cparams(("parallel",)),
        name="dsa_prep",
    )(u, u, u, u, u, u, *consts)


DSA_CK = 512


def _sort_key(score):
    bits = lax.bitcast_convert_type(score + 0.0, I32)
    return bits ^ ((bits >> 31) & 0x7FFFFFFF)


def _dsa_kernel(q_ref, iq_ref, iw_ref, k_ref, v_ref, ika_ref, ikb_ref, nbias_ref, o_ref,
                keys_ref, hi_ref, lo_ref, m_ref, l_ref, acc_ref, qs_ref, iqs_ref, *, topk, pos_bits):
    qb = pl.program_id(1)
    q0 = qb * Q_BLOCK
    far_end = jnp.maximum(q0 - Q_BLOCK, 0)
    nfar = (far_end + DSA_CK - 1) // DSA_CK
    near0 = pl.multiple_of(far_end, Q_BLOCK)
    nt = (((1,), (1,)), ((), ()))
    npair = D_HEADS // 2
    rep = D_HEADS // D_KV

    for p in range(npair):
        iqs_ref[Q_BLOCK * p:Q_BLOCK * (p + 1), :] = iq_ref[:, LANES * p:LANES * (p + 1)]
    for hd in range(D_HEADS):
        qs_ref[hd // rep, Q_BLOCK * (hd % rep):Q_BLOCK * (hd % rep + 1), :] = q_ref[:, D_DIM * hd:D_DIM * (hd + 1)]

    iw = iw_ref[...]
    iw_cols = [iw[:, hd:hd + 1] for hd in range(D_HEADS)]

    def index_scores(rows):
        iqs = iqs_ref[...]
        score = None
        for parity, k_ref_ in ((0, ika_ref), (1, ikb_ref)):
            d = lax.dot_general(iqs, k_ref_[rows, :], nt, preferred_element_type=F32) * D_IDX_DIM ** -0.5
            for p in range(npair):
                term = iw_cols[2 * p + parity] * jnp.maximum(d[Q_BLOCK * p:Q_BLOCK * (p + 1)], 0.0)
                score = term if score is None else score + term
        return score

    def store_keys(c, key):
        keys_ref[c] = key
        hi_ref[c] = (key >> 16).astype(I16)
        lo_ref[c] = ((key & 0xFFFF) - HALF16).astype(I16)

    def far_scores(c, carry):
        rows = pl.ds(pl.multiple_of(c * DSA_CK, DSA_CK), DSA_CK)
        key = _sort_key(index_scores(rows))
        pos = c * DSA_CK + lax.broadcasted_iota(I32, key.shape, 1)
        store_keys(c, jnp.where(pos < far_end, key, INT_MIN))
        return carry

    lax.fori_loop(0, nfar, far_scores, 0)
    near_rows = pl.ds(near0, 2 * Q_BLOCK)
    keyn = _sort_key(index_scores(near_rows))
    posn = near0 + lax.broadcasted_iota(I32, keyn.shape, 1)
    qpos = q0 + lax.broadcasted_iota(I32, keyn.shape, 0)
    keyn = jnp.where(posn <= qpos, keyn, INT_MIN)
    store_keys(nfar, jnp.concatenate([keyn, jnp.full((Q_BLOCK, DSA_CK - 2 * Q_BLOCK), INT_MIN, I32)], axis=1))

    def chunk_pos(c):
        base = jnp.where(c == nfar, near0, c * DSA_CK)
        return base + lax.broadcasted_iota(I32, (Q_BLOCK, DSA_CK), 1)

    def count(pred):
        def body(c, acc):
            hit = pred(c, keys_ref[c]).astype(F32)
            part = hit[:, 0:LANES]
            for s in range(1, DSA_CK // LANES):
                part = part + hit[:, LANES * s:LANES * (s + 1)]
            return acc + part
        acc = lax.fori_loop(0, nfar + 1, body, jnp.zeros((Q_BLOCK, LANES), F32))
        return jnp.sum(acc, axis=-1, keepdims=True)

    def count16(ref, cand, strict):
        cand16 = cand.astype(I16)

        def body(c, acc):
            x = ref[c]
            hit = jnp.where((x > cand16) if strict else (x >= cand16), jnp.int16(1), jnp.int16(0))
            part = hit[:, 0:LANES]
            for s in range(1, DSA_CK // LANES):
                part = part + hit[:, LANES * s:LANES * (s + 1)]
            return acc + part
        acc = lax.fori_loop(0, nfar + 1, body, jnp.zeros((Q_BLOCK, LANES), I16))
        return jnp.sum(acc.astype(F32), axis=-1, keepdims=True)

    def search16(ref, want):
        t = jnp.where(count16(ref, jnp.zeros((Q_BLOCK, 1), I32), False) >= want, 0, -HALF16).astype(I32)

        def bit(i, t):
            cand = t | (1 << (14 - i))
            return jnp.where(count16(ref, cand, False) >= want, cand, t)
        return lax.fori_loop(0, 15, bit, t)

    kf = float(topk)
    thr_hi = search16(hi_ref, kf)
    above = count16(hi_ref, thr_hi, True)
    thr_hi16 = thr_hi.astype(I16)

    def keep_candidates(c, carry):
        lo_ref[c] = jnp.where(hi_ref[c] == thr_hi16, lo_ref[c], jnp.int16(-HALF16))
        return carry

    lax.fori_loop(0, nfar + 1, keep_candidates, 0)
    thr_lo = search16(lo_ref, kf - above)
    thr = (thr_hi << 16) | ((thr_lo + HALF16) & 0xFFFF)
    need = kf - (above + count16(lo_ref, thr_lo, True))
    surplus = (above + count16(lo_ref, thr_lo, False) > kf) & (thr != INT_MIN)

    def tie_count(x):
        return count(lambda c, kk: (kk == thr) & (chunk_pos(c) < x))

    def cut_bit(i, cut):
        cand = cut | (1 << (pos_bits - 1 - i))
        return jnp.where(tie_count(cand) < need, cand, cut)

    take_all = jnp.full((Q_BLOCK, 1), 2 ** pos_bits - 1, I32)
    cut = lax.cond(jnp.max(surplus.astype(F32)) > 0.0,
                   lambda: lax.fori_loop(0, pos_bits, cut_bit, jnp.zeros((Q_BLOCK, 1), I32)),
                   lambda: take_all)

    def mask_bias(kk, pos):
        sel = ((kk > thr) | ((kk == thr) & (pos <= cut))) & (kk != INT_MIN)
        return jnp.where(sel, 0.0, NEG_BIG)

    m_ref[...] = jnp.full_like(m_ref, NEG_BIG)
    l_ref[...] = jnp.zeros_like(l_ref)
    acc_ref[...] = jnp.zeros_like(acc_ref)

    def attend(rows, bias_of):
        nk = bias_of(0).shape[-1]
        for g in range(D_KV):
            gs = slice(D_DIM * g, D_DIM * (g + 1))
            s = lax.dot_general(qs_ref[g], k_ref[rows, gs], nt, preferred_element_type=F32) * D_DIM ** -0.5
            s = (s.reshape(rep, Q_BLOCK, nk) + bias_of(g)).reshape(rep * Q_BLOCK, nk)
            p = _online_softmax_step(s, m_ref.at[g], l_ref.at[g], acc_ref.at[g])
            acc_ref[g] += jnp.dot(p, v_ref[rows, gs], preferred_element_type=F32)

    def far_attend(c, carry):
        rows = pl.ds(pl.multiple_of(c * DSA_CK, DSA_CK), DSA_CK)
        kk = keys_ref[c]
        pos = c * DSA_CK + lax.broadcasted_iota(I32, kk.shape, 1)
        mb = mask_bias(kk, pos)
        attend(rows, lambda g: mb[None])
        return carry

    lax.fori_loop(0, nfar, far_attend, 0)
    mbn = mask_bias(keys_ref[nfar][:, 0:2 * Q_BLOCK], posn)
    attend(near_rows, lambda g: mbn[None] + nbias_ref[0, rep * g:rep * (g + 1)])
    for hd in range(D_HEADS):
        rs = slice(Q_BLOCK * (hd % rep), Q_BLOCK * (hd % rep + 1))
        o_ref[:, D_DIM * hd:D_DIM * (hd + 1)] = acc_ref[hd // rep, rs, :] / l_ref[hd // rep, rs, :]


def _dsa(q, k, v, iq, ika, ikb, iw, near_bias, batch):
    m = q.shape[0]
    seq = m // batch
    nqb = seq // Q_BLOCK
    topk = min(D_TOPK_MAX, seq // 4)
    pos_bits = int(seq).bit_length()
    nck = max(seq // DSA_CK, 1)
    grp = D_HEADS // D_KV
    qspec = lambda w: pl.BlockSpec((Q_BLOCK, w), lambda b, i: (b * nqb + i, 0))
    kspec = lambda w: pl.BlockSpec((seq, w), lambda b, i: (b, 0))
    return pl.pallas_call(
        functools.partial(_dsa_kernel, topk=topk, pos_bits=pos_bits),
        grid=(batch, nqb),
        in_specs=[qspec(D_HEADS * D_DIM), qspec(D_HEADS * D_IDX_DIM), qspec(LANES),
                  kspec(D_KV * D_DIM), kspec(D_KV * D_DIM), kspec(LANES), kspec(LANES),
                  pl.BlockSpec((1, D_HEADS, Q_BLOCK, 2 * Q_BLOCK), lambda b, i: (jnp.minimum(i, 1), 0, 0, 0))],
        out_specs=qspec(D_HEADS * D_DIM),
        out_shape=jax.ShapeDtypeStruct((m, D_HEADS * D_DIM), F32),
        scratch_shapes=[pltpu.VMEM((nck + 1, Q_BLOCK, DSA_CK), I32),
                        pltpu.VMEM((nck + 1, Q_BLOCK, DSA_CK), I16), pltpu.VMEM((nck + 1, Q_BLOCK, DSA_CK), I16),
                        pltpu.VMEM((D_KV, grp * Q_BLOCK, LANES), F32), pltpu.VMEM((D_KV, grp * Q_BLOCK, LANES), F32),
                        pltpu.VMEM((D_KV, grp * Q_BLOCK, D_DIM), F32),
                        pltpu.VMEM((D_KV, grp * Q_BLOCK, D_DIM), BF16),
                        pltpu.VMEM((D_HEADS // 2 * Q_BLOCK, LANES), BF16)],
        compiler_params=_cparams(("parallel", "arbitrary")),
        name="dsa",
    )(q, iq, iw, k, v, ika, ikb, near_bias)


def _pad_cols(w, width):
    return jnp.pad(w, ((0, 0), (0, width - w.shape[1])))


def _pack_ab_in(w):
    a_cols = 3 * A_WIDTH + sum(A_LORA)
    wa, wb = w[:, :a_cols], w[:, a_cols:]
    rkv, lora = wa[:, :3 * A_WIDTH], wa[:, 3 * A_WIDTH:]
    z, xbc, dt = wb[:, :B_WIDTH], wb[:, B_WIDTH:B_WIDTH + B_CONV_CH], wb[:, B_WIDTH + B_CONV_CH:]
    return jnp.concatenate([rkv, z, xbc, _pad_cols(lora, A_LORA_PAD), _pad_cols(dt, LANES)], axis=1).astype(BF16)


def _pack_cd_in(w):
    c_cols = 2 * C_LORA + C_ROPE
    wc, wd = w[:, :c_cols], w[:, c_cols:]
    q_lat, kv_lat, k_pe = wc[:, :C_LORA], wc[:, C_LORA:2 * C_LORA], wc[:, 2 * C_LORA:]
    sizes = [D_HEADS * D_DIM, D_KV * D_DIM, D_KV * D_DIM, D_HEADS * D_IDX_DIM, D_IDX_DIM, D_HEADS]
    cuts = np.cumsum(sizes)[:-1]
    dq, dk, dv, iq, ik, iw = jnp.split(wd, [int(c) for c in cuts], axis=1)
    packed = jnp.concatenate([q_lat, kv_lat, dq, dk, dv, iq, k_pe, ik, iw], axis=1)
    return _pad_cols(packed, CD_COLS_PAD).astype(BF16)


def _pack_lora(w2, a2, g2):
    out, off = [], 0
    for w in (w2, a2, g2):
        out.append(jnp.pad(w, ((off, A_LORA_PAD - off - w.shape[0]), (0, 0))).astype(BF16))
        off += w.shape[0]
    return out


def _pack_mla_q(wq_b):
    w = wq_b.reshape(C_LORA, C_HEADS, C_QK)
    return jnp.concatenate([w[:, :, :C_NOPE].reshape(C_LORA, -1), w[:, :, C_NOPE:].reshape(C_LORA, -1)],
                           axis=1).astype(BF16)


def _pack_mla_kv(wkv_b):
    w = wkv_b.reshape(C_LORA, C_HEADS, C_NOPE + C_V)
    return jnp.concatenate([w[:, :, :C_NOPE].reshape(C_LORA, -1), w[:, :, C_NOPE:].reshape(C_LORA, -1)],
                           axis=1).astype(BF16)


def _layer0_mix(h, batch, norm_g, ab_w_in, ab_w_out, a_shift_mu, a_w0, a_w2, a_a0, a_a2, a_g2, a_k_k, a_k_a,
                a_r_k, a_ln_g, a_ln_b, b_conv_w, b_conv_b, b_dt_bias, b_a_log, b_d, b_norm_g):
    seq = h.shape[0] // batch
    u = _matmul([(h, 0, D_MODEL)], [_pack_ab_in(ab_w_in)], gain=norm_g)
    row = lambda t: t.reshape(1, -1)
    w2p, a2p, g2p = _pack_lora(a_w2, a_a2, a_g2)
    mu_rkv = row(a_shift_mu[:3 * A_WIDTH])
    mu_lora = _pad_cols(row(a_shift_mu[3 * A_WIDTH:]), A_LORA_PAD)
    r, w, k, v, na, nb, g = _rwkv_pre(u, seq, mu_rkv, mu_lora, row(a_w0), row(a_a0), row(a_k_k), row(a_k_a),
                                      w2p, a2p, g2p)
    y = _rwkv_scan(r, w, k, v, na, nb, batch)
    ya = _rwkv_post(y, r, k, v, g, row(a_ln_g), row(a_ln_b), row(a_r_k))
    yb = _ssd(u, batch, b_conv_w, b_conv_b, b_dt_bias, b_a_log, b_d, b_norm_g)
    w_out = ab_w_out.astype(BF16)
    return _matmul([(ya, 0, A_WIDTH), (yb, 0, B_WIDTH)], [w_out[:A_WIDTH], w_out[A_WIDTH:]], res=h)


def _layer1_mix(h, batch, norm_g, rel_bias, cd_w_in, cd_w_out, c_q_norm, c_wq_b, c_kv_norm, c_wkv_b,
                c_q_gain, c_k_gain, d_q_gain, d_k_gain, d_ik_gain):
    seq = h.shape[0] // batch
    u = _matmul([(h, 0, D_MODEL)], [_pack_cd_in(cd_w_in)], gain=norm_g)
    qn, qr, kn, kr, v = _mla_prep(u, seq, c_q_norm, _pack_mla_q(c_wq_b), c_kv_norm, _pack_mla_kv(c_wkv_b),
                                  c_q_gain, c_k_gain)
    yc = _mla_attn(qn, qr, kn, kr, v, batch)
    dq, dk, dv, iq, ika, ikb, iw = _dsa_prep(u, d_q_gain, d_k_gain, d_ik_gain)
    yd = _dsa(dq, dk, dv, iq, ika, ikb, iw, _near_bias(rel_bias), batch)
    w_out = cd_w_out.astype(BF16)
    half = C_HEADS * C_V
    return _matmul([(yc, 0, half), (yd, 0, D_HEADS * D_DIM)], [w_out[:half], w_out[half:]], res=h)


def _memory_attention(h, mem2, batch, gq, gkv, wq, wk, wv, wo, q_gain, k_gain):
    wkv = jnp.concatenate([wk, wv], axis=1).astype(BF16)
    mem_kv = _matmul([(mem2, 0, D_MODEL)], [wkv], gain=gkv)
    return _xattn(h, mem_kv, batch, gq, wq.astype(BF16), q_gain, k_gain, wo.astype(BF16))


def kernel(x, mem, rel_bias, norm_mix, norm_mem_q, norm_mem_kv, norm_ffn, xa_wq, xa_wk, xa_wv, xa_wo, xa_q_gain, xa_k_gain, ab_w_in, ab_w_out, a_shift_mu, a_w0, a_w2, a_a0, a_a2, a_g2, a_k_k, a_k_a, a_r_k, a_ln_g, a_ln_b, b_conv_w, b_conv_b, b_dt_bias, b_a_log, b_d, b_norm_g, ffn_w_gate, ffn_w_up, ffn_w_down, cd_w_in, cd_w_out, c_q_norm, c_wq_b, c_kv_norm, c_wkv_b, c_q_gain, c_k_gain, d_q_gain, d_k_gain, d_ik_gain, moe_router, moe_w_gate, moe_w_up, moe_w_down):
    batch, seq, d = x.shape
    h = x.reshape(batch * seq, d)
    mem2 = mem.reshape(-1, d)
    depth = norm_mix.shape[0]
    for layer in range(depth):
        i = layer // 2
        if layer % 2 == 0:
            h = _layer0_mix(h, batch, norm_mix[layer], ab_w_in[i], ab_w_out[i], a_shift_mu[i], a_w0[i], a_w2[i],
                            a_a0[i], a_a2[i], a_g2[i], a_k_k[i], a_k_a[i], a_r_k[i], a_ln_g[i], a_ln_b[i],
                            b_conv_w[i], b_conv_b[i], b_dt_bias[i], b_a_log[i], b_d[i], b_norm_g[i])
        else:
            h = _layer1_mix(h, batch, norm_mix[layer], rel_bias, cd_w_in[i], cd_w_out[i], c_q_norm[i], c_wq_b[i],
                            c_kv_norm[i], c_wkv_b[i], c_q_gain[i], c_k_gain[i], d_q_gain[i], d_k_gain[i],
                            d_ik_gain[i])
        h = _memory_attention(h, mem2, batch, norm_mem_q[layer], norm_mem_kv[layer], xa_wq[layer], xa_wk[layer],
                              xa_wv[layer], xa_wo[layer], xa_q_gain[layer], xa_k_gain[layer])
        if layer % 2 == 0:
            h = _swiglu(h, norm_ffn[layer], ffn_w_gate[i].astype(BF16), ffn_w_up[i].astype(BF16),
                        ffn_w_down[i].astype(BF16))
        else:
            router_p = _pad_cols(moe_router[i], LANES).astype(BF16)
            h = _moe(h, norm_ffn[layer], router_p, moe_w_gate[i].astype(BF16), moe_w_up[i].astype(BF16),
                     moe_w_down[i].astype(BF16))
    return h.reshape(batch, seq, d)
```

```python
import functools
import math

import numpy as np
import jax
import jax.numpy as jnp
from jax import lax
from jax.experimental import pallas as pl
from jax.experimental.pallas import tpu as pltpu

F32 = jnp.float32
BF16 = jnp.bfloat16
I32 = jnp.int32
I16 = jnp.int16
HALF16 = 1 << 15
HIGHEST = lax.Precision.HIGHEST

V7X_VMEM_BYTES = 64 * 1024 * 1024
VMEM_LIMIT = V7X_VMEM_BYTES - 8 * 1024 * 1024
LANES = 128

NORM_EPS = 1e-6
D_MODEL = 2048
HEAD64 = 64

A_WIDTH = 1024
A_LORA = (64, 64, 160)
A_LORA_PAD = 384
A_LN_EPS = 1e-5 * (HEAD64 / 8) ** 2
B_WIDTH = 1024
B_HEADS = 16
B_GROUPS = 4
B_STATE = 128
B_CONV = 4
B_CHUNK = 128
B_CONV_CH = B_WIDTH + 2 * B_GROUPS * B_STATE
AB_R, AB_K, AB_V, AB_Z, AB_XBC, AB_LORA, AB_DT, AB_COLS_PAD = 0, 1024, 2048, 3072, 4096, 6144, 6528, 6656

C_HEADS = 8
C_NOPE = 128
C_ROPE = 64
C_QK = C_NOPE + C_ROPE
C_V = 128
C_LORA = 512
ROPE_THETA = 10000.0
D_HEADS = 8
D_KV = 2
D_DIM = 128
D_IDX_DIM = 64
D_TOPK_MAX = 256
Q_BLOCK = 128
REL_BUCKETS = 32
REL_MAX_DIST = 128
CD_QLAT, CD_KVLAT, CD_DQ, CD_DK, CD_DV, CD_IQ, CD_PEIK, CD_IW, CD_COLS_PAD = (
    0, 512, 1024, 2048, 2304, 2560, 3072, 3200, 3584)

X_HEADS = 4
X_DIM = 128
N_EXPERTS = 8

NEG_BIG = -1e30
INT_MIN = -2 ** 31


def _cparams(sem):
    return pltpu.CompilerParams(dimension_semantics=sem, vmem_limit_bytes=VMEM_LIMIT)


def _rms(x, g, eps=NORM_EPS):
    return x * lax.rsqrt(jnp.mean(x * x, axis=-1, keepdims=True) + eps) * g


def _softplus(x):
    return jnp.maximum(x, 0.0) + jnp.log(1.0 + jnp.exp(-jnp.abs(x)))


def _silu(x):
    return x * jax.nn.sigmoid(x)


def _online_softmax_step(s, m_ref, l_ref, acc_ref):
    cols = [s[:, LANES * c:LANES * (c + 1)] for c in range(s.shape[1] // LANES)]
    mx = cols[0]
    for c in cols[1:]:
        mx = jnp.maximum(mx, c)
    m_old = m_ref[...]
    m_new = jnp.maximum(m_old, jnp.max(mx, axis=-1, keepdims=True))
    alpha = jnp.exp(m_old - m_new)
    ps = [jnp.exp(c - m_new) for c in cols]
    rs = ps[0]
    for p in ps[1:]:
        rs = rs + p
    l_ref[...] = alpha * l_ref[...] + jnp.sum(rs, axis=-1, keepdims=True)
    acc_ref[...] = alpha * acc_ref[...]
    m_ref[...] = m_new
    return jnp.concatenate(ps, axis=1).astype(BF16)


def _half_sum_bcast(x):
    left = lax.broadcasted_iota(I32, x.shape, 1) < HEAD64
    s0 = jnp.sum(jnp.where(left, x, 0.0), axis=1, keepdims=True)
    s1 = jnp.sum(jnp.where(left, 0.0, x), axis=1, keepdims=True)
    return jnp.where(left, s0, s1)


def _mm_kernel(*refs, n_x, has_norm, has_res):
    x_refs = refs[:n_x]
    pos = n_x
    g_ref = refs[pos] if has_norm else None
    pos += int(has_norm)
    w_refs = refs[pos:pos + n_x]
    pos += n_x
    res_ref = refs[pos] if has_res else None
    pos += int(has_res)
    o_ref = refs[pos]
    xn_refs = refs[pos + 1:]

    @pl.when(pl.program_id(1) == 0)
    def _():
        for x_ref, xn_ref in zip(x_refs, xn_refs):
            x = x_ref[...].astype(F32)
            if has_norm:
                x = _rms(x, g_ref[...])
            xn_ref[...] = x.astype(BF16)

    acc = None
    for xn_ref, w_ref in zip(xn_refs, w_refs):
        d = jnp.dot(xn_ref[...], w_ref[...], preferred_element_type=F32)
        acc = d if acc is None else acc + d
    if has_res:
        acc = acc + res_ref[...]
    o_ref[...] = acc


def _matmul(xs, ws, *, gain=None, res=None, tm=1024, tn=512):
    m = xs[0][0].shape[0]
    n = ws[0].shape[1]
    tm = min(tm, m)
    tn = min(tn, n)
    assert m % tm == 0 and n % tn == 0
    in_specs, args, scratch = [], [], []
    for arr, cb, width in xs:
        in_specs.append(pl.BlockSpec((tm, width), lambda i, j, cb=cb: (i, cb)))
        args.append(arr)
        scratch.append(pltpu.VMEM((tm, width), BF16))
    if gain is not None:
        in_specs.append(pl.BlockSpec((1, gain.shape[-1]), lambda i, j: (0, 0)))
        args.append(gain.reshape(1, -1))
    for (arr, cb, width), w in zip(xs, ws):
        assert w.shape[0] == width
        in_specs.append(pl.BlockSpec((width, tn), lambda i, j: (0, j)))
        args.append(w)
    if res is not None:
        in_specs.append(pl.BlockSpec((tm, tn), lambda i, j: (i, j)))
        args.append(res)
    return pl.pallas_call(
        functools.partial(_mm_kernel, n_x=len(xs), has_norm=gain is not None, has_res=res is not None),
        grid=(m // tm, n // tn),
        in_specs=in_specs,
        out_specs=pl.BlockSpec((tm, tn), lambda i, j: (i, j)),
        out_shape=jax.ShapeDtypeStruct((m, n), F32),
        scratch_shapes=scratch,
        compiler_params=_cparams(("parallel", "arbitrary")),
        name="matmul",
    )(*args)


def _rwkv_pre_kernel(rkv_ref, lora_ref, rkvp_ref, lorap_ref, mu_rkv_ref, mu_lora_ref,
                     w0_ref, a0_ref, kk_ref, ka_ref, w2_ref, a2_ref, g2_ref,
                     r_o, w_o, k_o, v_o, na_o, nb_o, g_o, *, tiles_per_seq):
    first = (pl.program_id(0) % tiles_per_seq) == 0
    tm = rkv_ref.shape[0]

    def shift_mix(x, prev_rows, mu):
        prev_last = jnp.where(first, 0.0, prev_rows[7:8, :])
        xs = pltpu.roll(x, 1, axis=0)
        row = lax.broadcasted_iota(I32, x.shape, 0)
        xs = jnp.where(row == 0, prev_last, xs)
        return x + (xs - x) * mu

    lo = shift_mix(lora_ref[...], lorap_ref[...], mu_lora_ref[...])
    lane = lax.broadcasted_iota(I32, lo.shape, 1)
    act = jnp.where(lane < A_LORA[0], jnp.tanh(lo),
                    jnp.where(lane < A_LORA[0] + A_LORA[1], lo, jax.nn.sigmoid(lo))).astype(BF16)
    dw = jnp.dot(act, w2_ref[...], preferred_element_type=F32)
    da = jnp.dot(act, a2_ref[...], preferred_element_type=F32)
    g_o[...] = jnp.dot(act, g2_ref[...], preferred_element_type=F32)

    for p in range(A_WIDTH // LANES):
        sl = slice(LANES * p, LANES * (p + 1))

        def mixed(off):
            s2 = slice(off + LANES * p, off + LANES * (p + 1))
            return shift_mix(rkv_ref[:, s2], rkvp_ref[:, s2], mu_rkv_ref[:, s2])

        r_o[:, sl] = mixed(AB_R)
        v_o[:, sl] = mixed(AB_V)
        kx = mixed(AB_K)
        logw = -_softplus(-(w0_ref[:, sl] + dw[:, sl])) - 0.5
        w_o[:, sl] = jnp.exp(-jnp.exp(logw))
        a = jax.nn.sigmoid(a0_ref[:, sl] + da[:, sl])
        kk = kx * kk_ref[:, sl]
        kk = kk * lax.rsqrt(jnp.maximum(_half_sum_bcast(kk * kk), 1e-24))
        k_o[:, sl] = kx * (1.0 + (a - 1.0) * ka_ref[:, sl])
        na_o[:, sl] = -kk
        nb_o[:, sl] = kk * a


def _rwkv_pre(u, seq, mu_rkv, mu_lora, w0, a0, k_k, k_a, w2p, a2p, g2p, tm=256):
    m = u.shape[0]
    tm = min(tm, seq)
    row = lambda w: pl.BlockSpec((1, w), lambda i: (0, 0))
    full = lambda a: pl.BlockSpec(a.shape, lambda i: (0, 0))
    prev = lambda i: jnp.maximum(i * (tm // 8) - 1, 0)
    out = jax.ShapeDtypeStruct((m, A_WIDTH), F32)
    return pl.pallas_call(
        functools.partial(_rwkv_pre_kernel, tiles_per_seq=seq // tm),
        grid=(m // tm,),
        in_specs=[
            pl.BlockSpec((tm, 3 * A_WIDTH), lambda i: (i, 0)),
            pl.BlockSpec((tm, A_LORA_PAD), lambda i: (i, AB_LORA // A_LORA_PAD)),
            pl.BlockSpec((8, 3 * A_WIDTH), lambda i: (prev(i), 0)),
            pl.BlockSpec((8, A_LORA_PAD), lambda i: (prev(i), AB_LORA // A_LORA_PAD)),
            row(3 * A_WIDTH), row(A_LORA_PAD), row(A_WIDTH), row(A_WIDTH), row(A_WIDTH), row(A_WIDTH),
            full(w2p), full(a2p), full(g2p),
        ],
        out_specs=[pl.BlockSpec((tm, A_WIDTH), lambda i: (i, 0))] * 7,
        out_shape=[out] * 7,
        compiler_params=_cparams(("parallel",)),
        name="rwkv_pre",
    )(u, u, u, u, mu_rkv, mu_lora, w0, a0, k_k, k_a, w2p, a2p, g2p)


RWKV_CHUNK = 64
RWKV_PAIR_GROUP = 8


def _rwkv_scan_kernel(r_ref, w_ref, k_ref, v_ref, a_ref, b_ref, y_ref, s_ref, vt_ref, yt_ref):
    npairs = s_ref.shape[0]

    @pl.when(pl.program_id(1) == 0)
    def _():
        s_ref[...] = jnp.zeros_like(s_ref)

    lane = lax.broadcasted_iota(I32, (HEAD64, LANES), 1)
    left = lane < HEAD64
    lane64 = lane & (HEAD64 - 1)

    def pair_transpose(x):
        xt = jnp.concatenate([x, x], axis=0).T
        return jnp.where(left, xt[0:HEAD64], xt[HEAD64:2 * HEAD64])

    def two_terms(x):
        hi = x.astype(BF16)
        return jnp.concatenate([hi, (x - hi.astype(F32)).astype(BF16)], axis=1)

    for p in range(npairs):
        vt_ref[p] = pair_transpose(v_ref[:, LANES * p:LANES * (p + 1)])
    yt_ref[...] = jnp.zeros_like(yt_ref)

    ri = lax.broadcasted_iota(I32, (2 * LANES, LANES), 0)
    ci = lax.broadcasted_iota(I32, (2 * LANES, LANES), 1)
    ones_blk = (((ri // HEAD64) & 1) == (ci // HEAD64)).astype(BF16)

    def half_sum_mxu(x, split):
        if not split:
            return jnp.dot(x.astype(BF16), ones_blk[0:LANES], preferred_element_type=F32)
        return jnp.dot(two_terms(x), ones_blk, preferred_element_type=F32)

    def step8(t8, carry):
        rows8 = pl.ds(pl.multiple_of(t8 * 8, 8), 8)
        unroll = (LANES - t8 * 8) & (LANES - 1)
        for g0 in range(0, npairs, RWKV_PAIR_GROUP):
            group = range(g0, min(g0 + RWKV_PAIR_GROUP, npairs))
            rows = {p: [ref[rows8, LANES * p:LANES * (p + 1)] for ref in (a_ref, w_ref, b_ref, k_ref, r_ref)]
                    for p in group}
            s = {p: s_ref[p] for p in group}
            vt8 = {p: pltpu.roll(vt_ref[p], unroll, axis=1) for p in group}
            for j in range(8):
                sel = lane64 == t8 * 8 + j
                for p in group:
                    a_row, w_row, b_row, k_row, r_row = (x8[j:j + 1, :] for x8 in rows[p])
                    sa = half_sum_mxu(s[p] * a_row, True)
                    vc = jnp.take_along_axis(vt8[p], jnp.where(left, j, HEAD64 + j), axis=1)
                    s[p] = s[p] * w_row + sa * b_row + vc * k_row
                    yt_ref[p] = jnp.where(sel, half_sum_mxu(s[p] * r_row, False), yt_ref[p])
            for p in group:
                s_ref[p] = s[p]
        return carry

    lax.fori_loop(0, RWKV_CHUNK // 8, step8, 0)

    for p in range(npairs):
        y_ref[:, LANES * p:LANES * (p + 1)] = pair_transpose(yt_ref[p])


def _rwkv_scan(r, w, k, v, na, nb, batch):
    m = r.shape[0]
    seq = m // batch
    nchunk = seq // RWKV_CHUNK
    npairs = A_WIDTH // LANES
    spec = pl.BlockSpec((RWKV_CHUNK, A_WIDTH), lambda b, c: (b * nchunk + c, 0))
    pair_scratch = pltpu.VMEM((npairs, HEAD64, LANES), F32)
    return pl.pallas_call(
        _rwkv_scan_kernel,
        grid=(batch, nchunk),
        in_specs=[spec] * 6,
        out_specs=spec,
        out_shape=jax.ShapeDtypeStruct((m, A_WIDTH), F32),
        scratch_shapes=[pair_scratch, pair_scratch, pair_scratch],
        compiler_params=_cparams(("parallel", "arbitrary")),
        name="rwkv_scan",
    )(r, w, k, v, na, nb)


def _rwkv_post_kernel(y_ref, r_ref, k_ref, v_ref, g_ref, lng_ref, lnb_ref, rk_ref, o_ref):
    for p in range(A_WIDTH // LANES):
        sl = slice(LANES * p, LANES * (p + 1))
        y = y_ref[:, sl]
        mean = _half_sum_bcast(y) * (1.0 / HEAD64)
        d = y - mean
        var = _half_sum_bcast(d * d) * (1.0 / HEAD64)
        yn = d * lax.rsqrt(var + A_LN_EPS) * lng_ref[:, sl] + lnb_ref[:, sl]
        bonus = _half_sum_bcast(r_ref[:, sl] * k_ref[:, sl] * rk_ref[:, sl]) * v_ref[:, sl]
        o_ref[:, sl] = (yn + bonus) * g_ref[:, sl]


def _rwkv_post(y, r, k, v, g, ln_g, ln_b, r_k, tm=256):
    m = y.shape[0]
    tm = min(tm, m)
    spec = pl.BlockSpec((tm, A_WIDTH), lambda i: (i, 0))
    row = pl.BlockSpec((1, A_WIDTH), lambda i: (0, 0))
    return pl.pallas_call(
        _rwkv_post_kernel,
        grid=(m // tm,),
        in_specs=[spec] * 5 + [row] * 3,
        out_specs=spec,
        out_shape=jax.ShapeDtypeStruct((m, A_WIDTH), F32),
        compiler_params=_cparams(("parallel",)),
        name="rwkv_post",
    )(y, r, k, v, g, ln_g, ln_b, r_k)


def _ssd_kernel(z_ref, xbc_ref, dt_ref, cw_ref, cb_ref, dtb_ref, alog_ref, dskip_ref, ng_ref,
                expand_ref, o_ref, st_ref, tail_ref):
    lc = B_CHUNK

    @pl.when(pl.program_id(1) == 0)
    def _():
        st_ref[...] = jnp.zeros_like(st_ref)
        tail_ref[...] = jnp.zeros_like(tail_ref)

    x = xbc_ref[...]
    tail = tail_ref[...]
    row8 = lax.broadcasted_iota(I32, tail.shape, 0)
    conv = cb_ref[...] + cw_ref[B_CONV - 1:B_CONV, :] * x
    for j in range(1, B_CONV):
        xs = pltpu.roll(x, j, axis=0)
        top = jnp.where(row8 < j, pltpu.roll(tail, j, axis=0), xs[0:8])
        xs = jnp.concatenate([top, xs[8:]], axis=0)
        conv = conv + cw_ref[B_CONV - 1 - j:B_CONV - j, :] * xs
    tail_ref[...] = x[lc - 8:lc]
    act = _silu(conv)
    xs_in = act[:, 0:B_WIDTH]
    bm = act[:, B_WIDTH:B_WIDTH + B_GROUPS * B_STATE].astype(BF16)
    cm = act[:, B_WIDTH + B_GROUPS * B_STATE:].astype(BF16)

    dt = _softplus(dt_ref[...] + dtb_ref[...])
    a_neg = -jnp.exp(alog_ref[...])
    da = dt * a_neg
    ri = lax.broadcasted_iota(I32, (lc, lc), 0)
    ci = lax.broadcasted_iota(I32, (lc, lc), 1)
    causal = ci <= ri
    tri = causal.astype(F32)
    cum = jnp.dot(tri, da, precision=HIGHEST, preferred_element_type=F32)
    cum_t = jnp.dot(da.T, (ri <= ci).astype(F32), precision=HIGHEST,
                    preferred_element_type=F32)
    expand = expand_ref[...]
    widen = lambda t: jnp.dot(t, expand, precision=HIGHEST, preferred_element_type=F32)
    dt_full = widen(dt)
    ecum_full = widen(jnp.exp(cum))
    dte_full = widen(jnp.exp(cum[lc - 1:lc, :] - cum))
    xdt = xs_in * dt_full
    xdt_b = xdt.astype(BF16)
    xdte_b = (xdt * dte_full).astype(BF16)
    left = lax.broadcasted_iota(I32, (lc, LANES), 1) < HEAD64

    ys = []
    for g in range(B_GROUPS):
        gs = slice(B_STATE * g, B_STATE * (g + 1))
        cm_g = cm[:, gs]
        bm_g = bm[:, gs]
        cb = lax.dot_general(cm_g, bm_g, (((1,), (1,)), ((), ())), preferred_element_type=F32)
        bm_t = bm_g.T
        pairs_per_group = B_HEADS // B_GROUPS // 2
        for q in range(pairs_per_group):
            p = g * pairs_per_group + q
            sl = slice(LANES * p, LANES * (p + 1))
            yd = []
            for h in (2 * p, 2 * p + 1):
                seg = cum[:, h:h + 1] - cum_t[h:h + 1, :]
                dec = jnp.where(causal, jnp.exp(jnp.minimum(seg, 0.0)), 0.0)
                yd.append(jnp.dot((cb * dec).astype(BF16), xdt_b[:, sl], preferred_element_type=F32))
            y_diag = jnp.where(left, yd[0], yd[1])
            st = st_ref[p]
            y_off = jnp.dot(cm_g, st.astype(BF16), preferred_element_type=F32) * ecum_full[:, sl]
            ys.append(y_diag + y_off)
            st_ref[p] = st * ecum_full[lc - 1:lc, sl] + jnp.dot(bm_t, xdte_b[:, sl],
                                                                preferred_element_type=F32)
    y = jnp.concatenate(ys, axis=1)
    y = (y + dskip_ref[...] * xs_in) * _silu(z_ref[...])
    gw = B_WIDTH // B_GROUPS
    for g in range(B_GROUPS):
        gs = slice(gw * g, gw * (g + 1))
        yg = y[:, gs]
        o_ref[:, gs] = yg * lax.rsqrt(jnp.mean(yg * yg, axis=-1, keepdims=True) + NORM_EPS) * ng_ref[:, gs]


def _ssd(u, batch, conv_w, conv_b, dt_bias, a_log, d_skip, norm_g):
    m = u.shape[0]
    seq = m // batch
    nchunk = seq // B_CHUNK
    rows = lambda b, c: b * nchunk + c
    pad16 = lambda t: jnp.pad(t.reshape(1, -1), ((0, 0), (0, LANES - B_HEADS)))
    expand = (np.arange(LANES)[:, None] == (np.arange(B_WIDTH)[None, :] // HEAD64)).astype(np.float32)
    full = lambda a: pl.BlockSpec(a.shape, lambda b, c: (0,) * a.ndim)
    args = [conv_w, conv_b.reshape(1, -1), pad16(dt_bias), pad16(a_log),
            jnp.repeat(d_skip, HEAD64).reshape(1, -1), norm_g.reshape(1, -1), jnp.asarray(expand)]
    return pl.pallas_call(
        _ssd_kernel,
        grid=(batch, nchunk),
        in_specs=[
            pl.BlockSpec((B_CHUNK, B_WIDTH), lambda b, c: (rows(b, c), AB_Z // B_WIDTH)),
            pl.BlockSpec((B_CHUNK, B_CONV_CH), lambda b, c: (rows(b, c), AB_XBC // B_CONV_CH)),
            pl.BlockSpec((B_CHUNK, LANES), lambda b, c: (rows(b, c), AB_DT // LANES)),
        ] + [full(a) for a in args],
        out_specs=pl.BlockSpec((B_CHUNK, B_WIDTH), lambda b, c: (rows(b, c), 0)),
        out_shape=jax.ShapeDtypeStruct((m, B_WIDTH), F32),
        scratch_shapes=[pltpu.VMEM((B_HEADS // 2, B_STATE, LANES), F32),
                        pltpu.VMEM((8, B_CONV_CH), F32)],
        compiler_params=_cparams(("parallel", "arbitrary")),
        name="ssd",
    )(u, u, u, *args)


def _xattn_kernel(h_ref, gq_ref, wq_ref, kv_ref, qg_ref, kg_ref, wo_ref, o_ref):
    h = h_ref[...]
    hn = _rms(h, gq_ref[...]).astype(BF16)
    q = jnp.dot(hn, wq_ref[...], preferred_element_type=F32)
    outs = []
    for hd in range(X_HEADS):
        sl = slice(X_DIM * hd, X_DIM * (hd + 1))
        qh = _rms(q[:, sl], qg_ref[...]).astype(BF16)
        kh = _rms(kv_ref[:, sl], kg_ref[...]).astype(BF16)
        vh = kv_ref[:, X_HEADS * X_DIM + X_DIM * hd:X_HEADS * X_DIM + X_DIM * (hd + 1)].astype(BF16)
        s = lax.dot_general(qh, kh, (((1,), (1,)), ((), ())), preferred_element_type=F32) * X_DIM ** -0.5
        e = jnp.exp(s - jnp.max(s, axis=-1, keepdims=True))
        p = e / jnp.sum(e, axis=-1, keepdims=True)
        outs.append(jnp.dot(p.astype(BF16), vh, preferred_element_type=F32))
    o = jnp.concatenate(outs, axis=1).astype(BF16)
    o_ref[...] = h + jnp.dot(o, wo_ref[...], preferred_element_type=F32)


def _xattn(h, mem_kv, batch, gq, wq, q_gain, k_gain, wo, tm=512):
    m, d = h.shape
    seq = m // batch
    tm = min(tm, seq)
    mlen = mem_kv.shape[0] // batch
    nt = seq // tm
    full = lambda a: pl.BlockSpec(a.shape, lambda b, i: (0, 0))
    args = [gq.reshape(1, -1), wq, mem_kv, q_gain.reshape(1, -1), k_gain.reshape(1, -1), wo]
    specs = [full(a) for a in args]
    specs[2] = pl.BlockSpec((mlen, mem_kv.shape[1]), lambda b, i: (b, 0))
    return pl.pallas_call(
        _xattn_kernel,
        grid=(batch, nt),
        in_specs=[pl.BlockSpec((tm, d), lambda b, i: (b * nt + i, 0))] + specs,
        out_specs=pl.BlockSpec((tm, d), lambda b, i: (b * nt + i, 0)),
        out_shape=jax.ShapeDtypeStruct((m, d), F32),
        compiler_params=_cparams(("parallel", "parallel")),
        name="xattn",
    )(h, *args)


def _swiglu_kernel(h_ref, g_ref, wg_ref, wu_ref, wd_ref, o_ref, xn_ref, acc_ref):
    j = pl.program_id(1)

    @pl.when(j == 0)
    def _():
        xn_ref[...] = _rms(h_ref[...], g_ref[...]).astype(BF16)
        acc_ref[...] = jnp.zeros_like(acc_ref)

    xn = xn_ref[...]
    gate = jnp.dot(xn, wg_ref[...], preferred_element_type=F32)
    up = jnp.dot(xn, wu_ref[...], preferred_element_type=F32)
    acc_ref[...] += jnp.dot((_silu(gate) * up).astype(BF16), wd_ref[...], preferred_element_type=F32)

    @pl.when(j == pl.num_programs(1) - 1)
    def _():
        o_ref[...] = h_ref[...] + acc_ref[...]


def _swiglu(h, gain, wg, wu, wd, tm=512, tf=512):
    m, d = h.shape
    f = wg.shape[1]
    tm = min(tm, m)
    assert f % tf == 0
    return pl.pallas_call(
        _swiglu_kernel,
        grid=(m // tm, f // tf),
        in_specs=[
            pl.BlockSpec((tm, d), lambda i, j: (i, 0)),
            pl.BlockSpec((1, d), lambda i, j: (0, 0)),
            pl.BlockSpec((d, tf), lambda i, j: (0, j)),
            pl.BlockSpec((d, tf), lambda i, j: (0, j)),
            pl.BlockSpec((tf, d), lambda i, j: (j, 0)),
        ],
        out_specs=pl.BlockSpec((tm, d), lambda i, j: (i, 0)),
        out_shape=jax.ShapeDtypeStruct((m, d), F32),
        scratch_shapes=[pltpu.VMEM((tm, d), BF16), pltpu.VMEM((tm, d), F32)],
        compiler_params=_cparams(("parallel", "arbitrary")),
        name="swiglu",
    )(h, gain.reshape(1, -1), wg, wu, wd)


MOE_TOPK = 2
MOE_ROWS = 512
MOE_TOKENS = 256


def _moe_route_kernel(h_ref, g_ref, router_ref, eid_o, gw_o):
    xn = _rms(h_ref[...], g_ref[...]).astype(BF16)
    logits = jnp.dot(xn, router_ref[...], preferred_element_type=F32)
    lane = lax.broadcasted_iota(I32, logits.shape, 1)
    logits = jnp.where(lane < N_EXPERTS, logits, -jnp.inf)
    m1 = jnp.max(logits, axis=-1, keepdims=True)
    i1 = jnp.min(jnp.where(logits == m1, lane, LANES), axis=-1, keepdims=True)
    rest = jnp.where(lane == i1, -jnp.inf, logits)
    m2 = jnp.max(rest, axis=-1, keepdims=True)
    i2 = jnp.min(jnp.where(rest == m2, lane, LANES), axis=-1, keepdims=True)
    e2 = jnp.exp(m2 - m1)
    eid_o[...] = jnp.where(lane == 0, i1, jnp.where(lane == 1, i2, 0))
    gw_o[...] = jnp.where(lane == 0, 1.0 / (1.0 + e2), jnp.where(lane == 1, e2 / (1.0 + e2), 0.0))


def _moe_route(h, gain, router_p, tm=512):
    m, d = h.shape
    tm = min(tm, m)
    spec = pl.BlockSpec((tm, LANES), lambda i: (i, 0))
    return pl.pallas_call(
        _moe_route_kernel,
        grid=(m // tm,),
        in_specs=[pl.BlockSpec((tm, d), lambda i: (i, 0)), pl.BlockSpec((1, d), lambda i: (0, 0)),
                  pl.BlockSpec((d, LANES), lambda i: (0, 0))],
        out_specs=[spec, spec],
        out_shape=[jax.ShapeDtypeStruct((m, LANES), I32), jax.ShapeDtypeStruct((m, LANES), F32)],
        compiler_params=_cparams(("parallel",)),
        name="moe_route",
    )(h, gain.reshape(1, -1), router_p)


def _moe_plan(eid, nblocks):
    e = eid.reshape(-1)
    onehot = (e[:, None] == jnp.arange(N_EXPERTS, dtype=I32)[None, :]).astype(I32)
    csum = jnp.cumsum(onehot, axis=0)
    rank = jnp.sum(onehot * csum, axis=1) - 1
    counts = csum[-1]
    padded = (counts + MOE_ROWS - 1) // MOE_ROWS * MOE_ROWS
    gend = jnp.cumsum(padded)
    dest = jnp.sum(onehot * (gend - padded)[None, :], axis=1) + rank
    nb_used = gend[-1] // MOE_ROWS
    blk = jnp.arange(nblocks, dtype=I32)
    blk_e = jnp.minimum(jnp.sum((blk[:, None] * MOE_ROWS >= gend[None, :]).astype(I32), axis=1), N_EXPERTS - 1)
    blk_e = jnp.where(blk < nb_used, blk_e, blk_e[jnp.maximum(nb_used - 1, 0)])
    return dest.astype(I32), blk_e.astype(I32), nb_used.astype(I32).reshape(1)


def _row_copy(src_ref, src_row, dst_ref, dst_row, sem):
    return pltpu.make_async_copy(src_ref.at[pl.ds(src_row, 1), :], dst_ref.at[pl.ds(dst_row, 1), :], sem)


def _moe_dispatch_kernel(dest_ref, h_ref, xs_in_ref, xs_ref, sem):
    del xs_in_ref
    base = pl.program_id(0) * MOE_TOKENS

    def issue(r, carry):
        for c in range(MOE_TOPK):
            _row_copy(h_ref, r, xs_ref, dest_ref[MOE_TOPK * (base + r) + c], sem).start()
        return carry

    def drain(r, carry):
        for c in range(MOE_TOPK):
            _row_copy(h_ref, 0, xs_ref, 0, sem).wait()
        return carry

    lax.fori_loop(0, MOE_TOKENS, issue, 0)
    lax.fori_loop(0, MOE_TOKENS, drain, 0)


def _moe_dispatch(h, dest, rows):
    m, d = h.shape
    return pl.pallas_call(
        _moe_dispatch_kernel,
        grid_spec=pltpu.PrefetchScalarGridSpec(
            num_scalar_prefetch=1,
            grid=(m // MOE_TOKENS,),
            in_specs=[pl.BlockSpec((MOE_TOKENS, d), lambda i, dest: (i, 0)), pl.BlockSpec(memory_space=pl.ANY)],
            out_specs=pl.BlockSpec(memory_space=pl.ANY),
            scratch_shapes=[pltpu.SemaphoreType.DMA(())],
        ),
        out_shape=jax.ShapeDtypeStruct((rows, d), F32),
        input_output_aliases={2: 0},
        compiler_params=_cparams(("arbitrary",)),
        name="moe_dispatch",
    )(dest, h, jnp.zeros((rows, d), F32))


def _moe_ffn_kernel(be_ref, nb_ref, x_ref, g_ref, wg_ref, wu_ref, wd_ref, y_ref, xn_ref, acc_ref):
    del be_ref
    j = pl.program_id(1)
    live = pl.program_id(0) < nb_ref[0]
    last = j == pl.num_programs(1) - 1

    @pl.when(live & (j == 0))
    def _():
        xn_ref[...] = _rms(x_ref[...], g_ref[...]).astype(BF16)
        acc_ref[...] = jnp.zeros_like(acc_ref)

    @pl.when(live)
    def _():
        xn = xn_ref[...]
        gate = jnp.dot(xn, wg_ref[0], preferred_element_type=F32)
        up = jnp.dot(xn, wu_ref[0], preferred_element_type=F32)
        acc_ref[...] += jnp.dot((_silu(gate) * up).astype(BF16), wd_ref[0], preferred_element_type=F32)

    @pl.when(live & last)
    def _():
        y_ref[...] = acc_ref[...]

    @pl.when(jnp.logical_not(live) & last)
    def _():
        y_ref[...] = jnp.zeros_like(y_ref)


def _moe_ffn(xs, blk_e, nb_used, gain, wg, wu, wd, tf=256):
    rows, d = xs.shape
    f = wg.shape[2]
    nj = f // tf
    assert f % tf == 0 and rows % MOE_ROWS == 0
    jx = lambda b, j, nb: jnp.where(b < nb[0], j, nj - 1)
    return pl.pallas_call(
        _moe_ffn_kernel,
        grid_spec=pltpu.PrefetchScalarGridSpec(
            num_scalar_prefetch=2,
            grid=(rows // MOE_ROWS, nj),
            in_specs=[
                pl.BlockSpec((MOE_ROWS, d), lambda b, j, be, nb: (jnp.minimum(b, jnp.maximum(nb[0] - 1, 0)), 0)),
                pl.BlockSpec((1, d), lambda b, j, be, nb: (0, 0)),
                pl.BlockSpec((1, d, tf), lambda b, j, be, nb: (be[b], 0, jx(b, j, nb))),
                pl.BlockSpec((1, d, tf), lambda b, j, be, nb: (be[b], 0, jx(b, j, nb))),
                pl.BlockSpec((1, tf, d), lambda b, j, be, nb: (be[b], jx(b, j, nb), 0)),
            ],
            out_specs=pl.BlockSpec((MOE_ROWS, d), lambda b, j, be, nb: (b, 0)),
            scratch_shapes=[pltpu.VMEM((MOE_ROWS, d), BF16), pltpu.VMEM((MOE_ROWS, d), F32)],
        ),
        out_shape=jax.ShapeDtypeStruct((rows, d), F32),
        compiler_params=_cparams(("parallel", "arbitrary")),
        name="moe_ffn",
    )(blk_e, nb_used, xs, gain.reshape(1, -1), wg, wu, wd)


def _moe_combine_kernel(dest_ref, h_ref, gw_ref, ys_ref, o_ref, ybuf_ref, sem):
    base = pl.program_id(0) * MOE_TOKENS

    def issue(r, carry):
        for c in range(MOE_TOPK):
            _row_copy(ys_ref, dest_ref[MOE_TOPK * (base + r) + c], ybuf_ref.at[c], r, sem).start()
        return carry

    def drain(r, carry):
        for c in range(MOE_TOPK):
            _row_copy(ys_ref, 0, ybuf_ref.at[c], 0, sem).wait()
        return carry

    lax.fori_loop(0, MOE_TOKENS, issue, 0)
    lax.fori_loop(0, MOE_TOKENS, drain, 0)
    gw = gw_ref[...]
    o_ref[...] = h_ref[...] + (gw[:, 0:1] * ybuf_ref[0] + gw[:, 1:2] * ybuf_ref[1])


def _moe_combine(h, gw, ys, dest):
    m, d = h.shape
    tok = lambda w: pl.BlockSpec((MOE_TOKENS, w), lambda i, dest: (i, 0))
    return pl.pallas_call(
        _moe_combine_kernel,
        grid_spec=pltpu.PrefetchScalarGridSpec(
            num_scalar_prefetch=1,
            grid=(m // MOE_TOKENS,),
            in_specs=[tok(d), tok(LANES), pl.BlockSpec(memory_space=pl.ANY)],
            out_specs=tok(d),
            scratch_shapes=[pltpu.VMEM((MOE_TOPK, MOE_TOKENS, d), F32), pltpu.SemaphoreType.DMA(())],
        ),
        out_shape=jax.ShapeDtypeStruct((m, d), F32),
        compiler_params=_cparams(("arbitrary",)),
        name="moe_combine",
    )(dest, h, gw, ys)


def _moe(h, gain, router_p, wg, wu, wd):
    m = h.shape[0]
    nblocks = MOE_TOPK * m // MOE_ROWS + N_EXPERTS
    eid, gw = _moe_route(h, gain, router_p)
    dest, blk_e, nb_used = _moe_plan(eid[:, :MOE_TOPK], nblocks)
    xs = _moe_dispatch(h, dest, nblocks * MOE_ROWS)
    ys = _moe_ffn(xs, blk_e, nb_used, gain, wg, wu, wd)
    return _moe_combine(h, gw, ys, dest)


def _rope_pairs(x, cos, sin_signed):
    w = x.shape[1]
    lane = lax.broadcasted_iota(I32, x.shape, 1)
    partner = jnp.where((lane & 32) != 0, pltpu.roll(x, 32, axis=1), pltpu.roll(x, w - 32, axis=1))
    return x * cos + partner * sin_signed


def _mla_prep_kernel(ql_ref, kvl_ref, pe_ref, qn_ref, wq_ref, kvn_ref, wkv_ref, qgn_ref, qgr_ref,
                     kgn_ref, kgr_ref, cos_ref, sin_ref, qn_o, qr_o, kn_o, kr_o, v_o):
    nn = C_HEADS * C_NOPE
    left = lax.broadcasted_iota(I32, (ql_ref.shape[0], LANES), 1) < C_ROPE

    def head_norm(nope_of, rope_of, gn_ref, gr_ref, n_o, r_o):
        for p in range(C_HEADS // 2):
            pr = slice(LANES * p, LANES * (p + 1))
            rope = rope_of(p)
            r2 = rope * rope
            rs = []
            for hh, ss_rope in ((0, jnp.sum(jnp.where(left, r2, 0.0), axis=-1, keepdims=True)),
                                (1, jnp.sum(jnp.where(left, 0.0, r2), axis=-1, keepdims=True))):
                hs = slice(C_NOPE * (2 * p + hh), C_NOPE * (2 * p + hh + 1))
                nope = nope_of(hs)
                ss = jnp.sum(nope * nope, axis=-1, keepdims=True) + ss_rope
                rs.append(lax.rsqrt(ss * (1.0 / C_QK) + NORM_EPS))
                n_o[:, hs] = (nope * rs[hh] * gn_ref[:, hs]).astype(BF16)
            rope = rope * jnp.where(left, rs[0], rs[1]) * gr_ref[:, pr]
            r_o[:, pr] = _rope_pairs(rope, cos_ref[:, pr], sin_ref[:, pr]).astype(BF16)

    q = jnp.dot(_rms(ql_ref[...], qn_ref[...]).astype(BF16), wq_ref[...], preferred_element_type=F32)
    head_norm(lambda hs: q[:, hs], lambda p: q[:, nn + LANES * p:nn + LANES * (p + 1)],
              qgn_ref, qgr_ref, qn_o, qr_o)

    kv = jnp.dot(_rms(kvl_ref[...], kvn_ref[...]).astype(BF16), wkv_ref[...], preferred_element_type=F32)
    v_o[...] = kv[:, nn:].astype(BF16)
    pe = pe_ref[...]
    pe_pair = jnp.where(left, pe, pltpu.roll(pe, C_ROPE, axis=1))
    head_norm(lambda hs: kv[:, hs], lambda p: pe_pair, kgn_ref, kgr_ref, kn_o, kr_o)


def _mla_prep(u, seq, q_norm, wq_p, kv_norm, wkv_p, q_gain, k_gain, tm=256):
    m = u.shape[0]
    tm = min(tm, seq)
    nn, nr = C_HEADS * C_NOPE, C_HEADS * C_ROPE
    half = C_ROPE // 2
    freqs = ROPE_THETA ** (-jnp.arange(half, dtype=F32) / half)
    ang = jnp.arange(seq, dtype=F32)[:, None] * freqs[None, :]
    cos = jnp.tile(jnp.cos(ang), (1, 2 * C_HEADS))
    sin = jnp.tile(jnp.concatenate([-jnp.sin(ang), jnp.sin(ang)], axis=1), (1, C_HEADS))
    tile_gain = lambda g: jnp.tile(g, C_HEADS).reshape(1, -1)
    consts = [q_norm.reshape(1, -1), wq_p, kv_norm.reshape(1, -1), wkv_p,
              tile_gain(q_gain[:C_NOPE]), tile_gain(q_gain[C_NOPE:]),
              tile_gain(k_gain[:C_NOPE]), tile_gain(k_gain[C_NOPE:])]
    full = lambda a: pl.BlockSpec(a.shape, lambda i: (0, 0))
    nt = seq // tm
    tab = pl.BlockSpec((tm, nr), lambda i: (i % nt, 0))
    out = lambda w: jax.ShapeDtypeStruct((m, w), BF16)
    ospec = lambda w: pl.BlockSpec((tm, w), lambda i: (i, 0))
    return pl.pallas_call(
        _mla_prep_kernel,
        grid=(m // tm,),
        in_specs=[
            pl.BlockSpec((tm, C_LORA), lambda i: (i, CD_QLAT // C_LORA)),
            pl.BlockSpec((tm, C_LORA), lambda i: (i, CD_KVLAT // C_LORA)),
            pl.BlockSpec((tm, LANES), lambda i: (i, CD_PEIK // LANES)),
        ] + [full(a) for a in consts] + [tab, tab],
        out_specs=[ospec(nn), ospec(nr), ospec(nn), ospec(nr), ospec(nn)],
        out_shape=[out(nn), out(nr), out(nn), out(nr), out(nn)],
        compiler_params=_cparams(("parallel",)),
        name="mla_prep",
    )(u, u, u, *consts, cos, sin)


def _mla_attn_kernel(qn_ref, qr_ref, kn_ref, kr_ref, v_ref, o_ref, m_ref, l_ref, acc_ref, *, tq):
    qi = pl.program_id(2)
    lane = lax.broadcasted_iota(I32, qr_ref.shape, 1)
    qr = qr_ref[...]
    qs = []
    for hh in range(2):
        qr_h = jnp.where((lane // HEAD64) == hh, qr, jnp.zeros_like(qr))
        qs.append(jnp.concatenate([qn_ref[:, C_NOPE * hh:C_NOPE * (hh + 1)], qr_h], axis=1))
    m_ref[...] = jnp.full_like(m_ref, NEG_BIG)
    l_ref[...] = jnp.zeros_like(l_ref)
    acc_ref[...] = jnp.zeros_like(acc_ref)
    nt = (((1,), (1,)), ((), ()))

    def update(j, masked):
        rows = pl.ds(pl.multiple_of(j * tq, tq), tq)
        kr = kr_ref[rows, :]
        ss = []
        for hh in range(2):
            kc = jnp.concatenate([kn_ref[rows, C_NOPE * hh:C_NOPE * (hh + 1)], kr], axis=1)
            ss.append(lax.dot_general(qs[hh], kc, nt, preferred_element_type=F32) * C_QK ** -0.5)
        for hh in range(2):
            s = ss[hh]
            if masked:
                ri = lax.broadcasted_iota(I32, s.shape, 0)
                ci = lax.broadcasted_iota(I32, s.shape, 1)
                s = jnp.where(ci <= ri, s, NEG_BIG)
            p = _online_softmax_step(s, m_ref.at[hh], l_ref.at[hh], acc_ref.at[hh])
            acc_ref[hh] += jnp.dot(p, v_ref[rows, C_V * hh:C_V * (hh + 1)], preferred_element_type=F32)

    def body(j, carry):
        update(j, False)
        return carry

    lax.fori_loop(0, qi, body, 0)
    update(qi, True)
    for hh in range(2):
        o_ref[:, C_V * hh:C_V * (hh + 1)] = acc_ref[hh] / l_ref[hh]


def _mla_attn(qn, qr, kn, kr, v, batch, tq=512):
    m = qn.shape[0]
    seq = m // batch
    tq = min(tq, seq)
    nq = seq // tq
    qspec = lambda w: pl.BlockSpec((tq, w), lambda b, p, i: (b * nq + i, p))
    kspec = lambda w: pl.BlockSpec((seq, w), lambda b, p, i: (b, p))
    return pl.pallas_call(
        functools.partial(_mla_attn_kernel, tq=tq),
        grid=(batch, C_HEADS // 2, nq),
        in_specs=[qspec(2 * C_NOPE), qspec(2 * C_ROPE), kspec(2 * C_NOPE), kspec(2 * C_ROPE), kspec(2 * C_V)],
        out_specs=qspec(2 * C_V),
        out_shape=jax.ShapeDtypeStruct((m, C_HEADS * C_V), F32),
        scratch_shapes=[pltpu.VMEM((2, tq, LANES), F32), pltpu.VMEM((2, tq, LANES), F32),
                        pltpu.VMEM((2, tq, C_V), F32)],
        compiler_params=_cparams(("parallel", "parallel", "arbitrary")),
        name="mla_attn",
    )(qn, qr, kn, kr, v)


def _rel_bucket_table(n):
    d = np.arange(n)
    max_exact = REL_BUCKETS // 2
    nf = np.maximum(d, 1).astype(np.float32)
    large = max_exact + (np.log(nf / np.float32(max_exact)) / np.float32(math.log(REL_MAX_DIST / max_exact))
                         * np.float32(REL_BUCKETS - max_exact)).astype(np.int32)
    large = np.minimum(large, REL_BUCKETS - 1)
    return np.where(d < max_exact, d, large).astype(np.int32)


def _near_bias_kernel(bucket_ref, rel_ref, o_ref):
    for var in range(2):
        bk = bucket_ref[var]
        for hd in range(D_HEADS):
            acc = jnp.zeros(bk.shape, F32)
            for b in range(REL_BUCKETS):
                acc = jnp.where(bk == b, rel_ref[b, hd], acc)
            o_ref[var, hd] = acc - rel_ref[REL_BUCKETS - 1, hd]


def _near_bias(rel_bias):
    table = _rel_bucket_table(2 * Q_BLOCK)
    q = np.arange(Q_BLOCK)[:, None]
    j = np.arange(2 * Q_BLOCK)[None, :]
    dist0 = np.maximum(q - j, 0)
    dist1 = np.maximum(q - j + Q_BLOCK, 0)
    buckets = np.stack([table[dist0], table[dist1]]).astype(np.int32)
    return pl.pallas_call(
        _near_bias_kernel,
        in_specs=[pl.BlockSpec(memory_space=pltpu.VMEM), pl.BlockSpec(memory_space=pltpu.SMEM)],
        out_specs=pl.BlockSpec(memory_space=pltpu.VMEM),
        out_shape=jax.ShapeDtypeStruct((2, D_HEADS, Q_BLOCK, 2 * Q_BLOCK), F32),
        name="near_bias",
    )(jnp.asarray(buckets), rel_bias)


def _dsa_prep_kernel(dq_ref, dk_ref, dv_ref, iq_ref, peik_ref, iw_ref, qg_ref, kg_ref, ikg_ref,
                     q_o, k_o, v_o, iq_o, ika_o, ikb_o, iw_o):
    for hd in range(D_HEADS):
        sl = slice(D_DIM * hd, D_DIM * (hd + 1))
        q_o[:, sl] = _rms(dq_ref[:, sl], qg_ref[...]).astype(BF16)
    for hd in range(D_KV):
        sl = slice(D_DIM * hd, D_DIM * (hd + 1))
        k_o[:, sl] = _rms(dk_ref[:, sl], kg_ref[...]).astype(BF16)
    v_o[...] = dv_ref[...].astype(BF16)
    iq_o[...] = iq_ref[...].astype(BF16)
    x = peik_ref[...]
    right = lax.broadcasted_iota(I32, x.shape, 1) >= D_IDX_DIM
    x = jnp.where(right, x, 0.0)
    ms = jnp.sum(x * x, axis=-1, keepdims=True) * (1.0 / D_IDX_DIM)
    ik = (x * lax.rsqrt(ms + NORM_EPS) * ikg_ref[...]).astype(BF16)
    ikb_o[...] = ik
    ika_o[...] = pltpu.roll(ik.astype(F32), D_IDX_DIM, axis=1).astype(BF16)
    iw_o[...] = iw_ref[...] * D_HEADS ** -0.5


def _dsa_prep(u, q_gain, k_gain, ik_gain, tm=256):
    m = u.shape[0]
    tm = min(tm, m)
    blk = lambda w, off: pl.BlockSpec((tm, w), lambda i: (i, off // w))
    full = lambda a: pl.BlockSpec(a.shape, lambda i: (0, 0))
    ikg = jnp.concatenate([jnp.zeros((D_IDX_DIM,), F32), ik_gain]).reshape(1, -1)
    consts = [q_gain.reshape(1, -1), k_gain.reshape(1, -1), ikg]
    widths = [D_HEADS * D_DIM, D_KV * D_DIM, D_KV * D_DIM, D_HEADS * D_IDX_DIM, LANES, LANES, LANES]
    dtypes = [BF16] * 6 + [F32]
    return pl.pallas_call(
        _dsa_prep_kernel,
        grid=(m // tm,),
        in_specs=[blk(D_HEADS * D_DIM, CD_DQ), blk(D_KV * D_DIM, CD_DK), blk(D_KV * D_DIM, CD_DV),
                  blk(D_HEADS * D_IDX_DIM, CD_IQ), blk(LANES, CD_PEIK), blk(LANES, CD_IW)]
        + [full(a) for a in consts],
        out_specs=[pl.BlockSpec((tm, w), lambda i: (i, 0)) for w in widths],
        out_shape=[jax.ShapeDtypeStruct((m, w), dt) for w, dt in zip(widths, dtypes)],
        compiler_params=_cparams(("parallel",)),
        name="dsa_prep",
    )(u, u, u, u, u, u, *consts)


DSA_CK = 512


def _sort_key(score):
    bits = lax.bitcast_convert_type(score + 0.0, I32)
    return bits ^ ((bits >> 31) & 0x7FFFFFFF)


def _dsa_kernel(q_ref, iq_ref, iw_ref, k_ref, v_ref, ika_ref, ikb_ref, nbias_ref, o_ref,
                keys_ref, hi_ref, lo_ref, m_ref, l_ref, acc_ref, qs_ref, iqs_ref, *, topk):
    qb = pl.program_id(1)
    q0 = qb * Q_BLOCK
    far_end = jnp.maximum(q0 - Q_BLOCK, 0)
    nfar = (far_end + DSA_CK - 1) // DSA_CK
    near0 = pl.multiple_of(far_end, Q_BLOCK)
    nt = (((1,), (1,)), ((), ()))
    npair = D_HEADS // 2
    rep = D_HEADS // D_KV

    for p in range(npair):
        iqs_ref[Q_BLOCK * p:Q_BLOCK * (p + 1), :] = iq_ref[:, LANES * p:LANES * (p + 1)]
    for hd in range(D_HEADS):
        qs_ref[hd // rep, Q_BLOCK * (hd % rep):Q_BLOCK * (hd % rep + 1), :] = q_ref[:, D_DIM * hd:D_DIM * (hd + 1)]

    iw = iw_ref[...]
    iw_cols = [iw[:, hd:hd + 1] for hd in range(D_HEADS)]

    def index_scores(rows):
        iqs = iqs_ref[...]
        score = None
        for parity, k_ref_ in ((0, ika_ref), (1, ikb_ref)):
            d = lax.dot_general(iqs, k_ref_[rows, :], nt, preferred_element_type=F32) * D_IDX_DIM ** -0.5
            for p in range(npair):
                term = iw_cols[2 * p + parity] * jnp.maximum(d[Q_BLOCK * p:Q_BLOCK * (p + 1)], 0.0)
                score = term if score is None else score + term
        return score

    def store_keys(c, key):
        keys_ref[c] = key
        hi_ref[c] = (key >> 16).astype(I16)
        lo_ref[c] = ((key & 0xFFFF) - HALF16).astype(I16)

    def far_scores(c, carry):
        rows = pl.ds(pl.multiple_of(c * DSA_CK, DSA_CK), DSA_CK)
        key = _sort_key(index_scores(rows))
        pos = c * DSA_CK + lax.broadcasted_iota(I32, key.shape, 1)
        store_keys(c, jnp.where(pos < far_end, key, INT_MIN))
        return carry

    lax.fori_loop(0, nfar, far_scores, 0)
    near_rows = pl.ds(near0, 2 * Q_BLOCK)
    keyn = _sort_key(index_scores(near_rows))
    posn = near0 + lax.broadcasted_iota(I32, keyn.shape, 1)
    qpos = q0 + lax.broadcasted_iota(I32, keyn.shape, 0)
    keyn = jnp.where(posn <= qpos, keyn, INT_MIN)
    store_keys(nfar, jnp.concatenate([keyn, jnp.full((Q_BLOCK, DSA_CK - 2 * Q_BLOCK), INT_MIN, I32)], axis=1))

    def count16(ref, cand, strict):
        cand16 = cand.astype(I16)

        def body(c, acc):
            x = ref[c]
            hit = jnp.where((x > cand16) if strict else (x >= cand16), jnp.int16(1), jnp.int16(0))
            part = hit[:, 0:LANES]
            for s in range(1, DSA_CK // LANES):
                part = part + hit[:, LANES * s:LANES * (s + 1)]
            return acc + part
        acc = lax.fori_loop(0, nfar + 1, body, jnp.zeros((Q_BLOCK, LANES), I16))
        return jnp.sum(acc.astype(F32), axis=-1, keepdims=True)

    def search16(ref, want):
        t = jnp.where(count16(ref, jnp.zeros((Q_BLOCK, 1), I32), False) >= want, 0, -HALF16).astype(I32)

        def bit(i, t):
            cand = t | (1 << (14 - i))
            return jnp.where(count16(ref, cand, False) >= want, cand, t)
        return lax.fori_loop(0, 15, bit, t)

    kf = float(topk)
    thr_hi = search16(hi_ref, kf)
    above = count16(hi_ref, thr_hi, True)
    thr_hi16 = thr_hi.astype(I16)

    def keep_candidates(c, carry):
        lo_ref[c] = jnp.where(hi_ref[c] == thr_hi16, lo_ref[c], jnp.int16(-HALF16))
        return carry

    lax.fori_loop(0, nfar + 1, keep_candidates, 0)
    thr_lo = search16(lo_ref, kf - above)
    thr = (thr_hi << 16) | ((thr_lo + HALF16) & 0xFFFF)
    need = kf - (above + count16(lo_ref, thr_lo, True))

    ti = lax.broadcasted_iota(I32, (DSA_CK, DSA_CK), 0)
    tj = lax.broadcasted_iota(I32, (DSA_CK, DSA_CK), 1)
    upto = (ti <= tj).astype(BF16)

    def mask_bias(kk, ties_before):
        nk = kk.shape[-1]
        tie = kk == thr
        rank = ties_before + jnp.dot(tie.astype(BF16), upto[0:nk, 0:nk], preferred_element_type=F32)
        sel = ((kk > thr) | (tie & (rank <= need))) & (kk != INT_MIN)
        return jnp.where(sel, 0.0, NEG_BIG), rank[:, nk - 1:nk]

    m_ref[...] = jnp.full_like(m_ref, NEG_BIG)
    l_ref[...] = jnp.zeros_like(l_ref)
    acc_ref[...] = jnp.zeros_like(acc_ref)

    def attend(rows, bias_of):
        nk = bias_of(0).shape[-1]
        for g in range(D_KV):
            gs = slice(D_DIM * g, D_DIM * (g + 1))
            s = lax.dot_general(qs_ref[g], k_ref[rows, gs], nt, preferred_element_type=F32) * D_DIM ** -0.5
            s = (s.reshape(rep, Q_BLOCK, nk) + bias_of(g)).reshape(rep * Q_BLOCK, nk)
            p = _online_softmax_step(s, m_ref.at[g], l_ref.at[g], acc_ref.at[g])
            acc_ref[g] += jnp.dot(p, v_ref[rows, gs], preferred_element_type=F32)

    def far_attend(c, ties_before):
        rows = pl.ds(pl.multiple_of(c * DSA_CK, DSA_CK), DSA_CK)
        mb, ties = mask_bias(keys_ref[c], ties_before)
        attend(rows, lambda g: mb[None])
        return ties

    ties = lax.fori_loop(0, nfar, far_attend, jnp.zeros((Q_BLOCK, 1), F32))
    mbn, _ = mask_bias(keys_ref[nfar][:, 0:2 * Q_BLOCK], ties)
    attend(near_rows, lambda g: mbn[None] + nbias_ref[0, rep * g:rep * (g + 1)])
    for hd in range(D_HEADS):
        rs = slice(Q_BLOCK * (hd % rep), Q_BLOCK * (hd % rep + 1))
        o_ref[:, D_DIM * hd:D_DIM * (hd + 1)] = acc_ref[hd // rep, rs, :] / l_ref[hd // rep, rs, :]


def _dsa(q, k, v, iq, ika, ikb, iw, near_bias, batch):
    m = q.shape[0]
    seq = m // batch
    nqb = seq // Q_BLOCK
    topk = min(D_TOPK_MAX, seq // 4)
    nck = max(seq // DSA_CK, 1)
    grp = D_HEADS // D_KV
    qspec = lambda w: pl.BlockSpec((Q_BLOCK, w), lambda b, i: (b * nqb + i, 0))
    kspec = lambda w: pl.BlockSpec((seq, w), lambda b, i: (b, 0))
    return pl.pallas_call(
        functools.partial(_dsa_kernel, topk=topk),
        grid=(batch, nqb),
        in_specs=[qspec(D_HEADS * D_DIM), qspec(D_HEADS * D_IDX_DIM), qspec(LANES),
                  kspec(D_KV * D_DIM), kspec(D_KV * D_DIM), kspec(LANES), kspec(LANES),
                  pl.BlockSpec((1, D_HEADS, Q_BLOCK, 2 * Q_BLOCK), lambda b, i: (jnp.minimum(i, 1), 0, 0, 0))],
        out_specs=qspec(D_HEADS * D_DIM),
        out_shape=jax.ShapeDtypeStruct((m, D_HEADS * D_DIM), F32),
        scratch_shapes=[pltpu.VMEM((nck + 1, Q_BLOCK, DSA_CK), I32),
                        pltpu.VMEM((nck + 1, Q_BLOCK, DSA_CK), I16), pltpu.VMEM((nck + 1, Q_BLOCK, DSA_CK), I16),
                        pltpu.VMEM((D_KV, grp * Q_BLOCK, LANES), F32), pltpu.VMEM((D_KV, grp * Q_BLOCK, LANES), F32),
                        pltpu.VMEM((D_KV, grp * Q_BLOCK, D_DIM), F32),
                        pltpu.VMEM((D_KV, grp * Q_BLOCK, D_DIM), BF16),
                        pltpu.VMEM((D_HEADS // 2 * Q_BLOCK, LANES), BF16)],
        compiler_params=_cparams(("parallel", "arbitrary")),
        name="dsa",
    )(q, iq, iw, k, v, ika, ikb, near_bias)


def _pad_cols(w, width):
    return jnp.pad(w, ((0, 0), (0, width - w.shape[1])))


def _pack_ab_in(w):
    a_cols = 3 * A_WIDTH + sum(A_LORA)
    wa, wb = w[:, :a_cols], w[:, a_cols:]
    rkv, lora = wa[:, :3 * A_WIDTH], wa[:, 3 * A_WIDTH:]
    z, xbc, dt = wb[:, :B_WIDTH], wb[:, B_WIDTH:B_WIDTH + B_CONV_CH], wb[:, B_WIDTH + B_CONV_CH:]
    return jnp.concatenate([rkv, z, xbc, _pad_cols(lora, A_LORA_PAD), _pad_cols(dt, LANES)], axis=1).astype(BF16)


def _pack_cd_in(w):
    c_cols = 2 * C_LORA + C_ROPE
    wc, wd = w[:, :c_cols], w[:, c_cols:]
    q_lat, kv_lat, k_pe = wc[:, :C_LORA], wc[:, C_LORA:2 * C_LORA], wc[:, 2 * C_LORA:]
    sizes = [D_HEADS * D_DIM, D_KV * D_DIM, D_KV * D_DIM, D_HEADS * D_IDX_DIM, D_IDX_DIM, D_HEADS]
    cuts = np.cumsum(sizes)[:-1]
    dq, dk, dv, iq, ik, iw = jnp.split(wd, [int(c) for c in cuts], axis=1)
    packed = jnp.concatenate([q_lat, kv_lat, dq, dk, dv, iq, k_pe, ik, iw], axis=1)
    return _pad_cols(packed, CD_COLS_PAD).astype(BF16)


def _pack_lora(w2, a2, g2):
    out, off = [], 0
    for w in (w2, a2, g2):
        out.append(jnp.pad(w, ((off, A_LORA_PAD - off - w.shape[0]), (0, 0))).astype(BF16))
        off += w.shape[0]
    return out


def _pack_mla_q(wq_b):
    w = wq_b.reshape(C_LORA, C_HEADS, C_QK)
    return jnp.concatenate([w[:, :, :C_NOPE].reshape(C_LORA, -1), w[:, :, C_NOPE:].reshape(C_LORA, -1)],
                           axis=1).astype(BF16)


def _pack_mla_kv(wkv_b):
    w = wkv_b.reshape(C_LORA, C_HEADS, C_NOPE + C_V)
    return jnp.concatenate([w[:, :, :C_NOPE].reshape(C_LORA, -1), w[:, :, C_NOPE:].reshape(C_LORA, -1)],
                           axis=1).astype(BF16)


def _layer0_mix(h, batch, norm_g, ab_w_in, ab_w_out, a_shift_mu, a_w0, a_w2, a_a0, a_a2, a_g2, a_k_k, a_k_a,
                a_r_k, a_ln_g, a_ln_b, b_conv_w, b_conv_b, b_dt_bias, b_a_log, b_d, b_norm_g):
    seq = h.shape[0] // batch
    u = _matmul([(h, 0, D_MODEL)], [_pack_ab_in(ab_w_in)], gain=norm_g)
    row = lambda t: t.reshape(1, -1)
    w2p, a2p, g2p = _pack_lora(a_w2, a_a2, a_g2)
    mu_rkv = row(a_shift_mu[:3 * A_WIDTH])
    mu_lora = _pad_cols(row(a_shift_mu[3 * A_WIDTH:]), A_LORA_PAD)
    r, w, k, v, na, nb, g = _rwkv_pre(u, seq, mu_rkv, mu_lora, row(a_w0), row(a_a0), row(a_k_k), row(a_k_a),
                                      w2p, a2p, g2p)
    y = _rwkv_scan(r, w, k, v, na, nb, batch)
    ya = _rwkv_post(y, r, k, v, g, row(a_ln_g), row(a_ln_b), row(a_r_k))
    yb = _ssd(u, batch, b_conv_w, b_conv_b, b_dt_bias, b_a_log, b_d, b_norm_g)
    w_out = ab_w_out.astype(BF16)
    return _matmul([(ya, 0, A_WIDTH), (yb, 0, B_WIDTH)], [w_out[:A_WIDTH], w_out[A_WIDTH:]], res=h)


def _layer1_mix(h, batch, norm_g, rel_bias, cd_w_in, cd_w_out, c_q_norm, c_wq_b, c_kv_norm, c_wkv_b,
                c_q_gain, c_k_gain, d_q_gain, d_k_gain, d_ik_gain):
    seq = h.shape[0] // batch
    u = _matmul([(h, 0, D_MODEL)], [_pack_cd_in(cd_w_in)], gain=norm_g)
    qn, qr, kn, kr, v = _mla_prep(u, seq, c_q_norm, _pack_mla_q(c_wq_b), c_kv_norm, _pack_mla_kv(c_wkv_b),
                                  c_q_gain, c_k_gain)
    yc = _mla_attn(qn, qr, kn, kr, v, batch)
    dq, dk, dv, iq, ika, ikb, iw = _dsa_prep(u, d_q_gain, d_k_gain, d_ik_gain)
    yd = _dsa(dq, dk, dv, iq, ika, ikb, iw, _near_bias(rel_bias), batch)
    w_out = cd_w_out.astype(BF16)
    half = C_HEADS * C_V
    return _matmul([(yc, 0, half), (yd, 0, D_HEADS * D_DIM)], [w_out[:half], w_out[half:]], res=h)


def _memory_attention(h, mem2, batch, gq, gkv, wq, wk, wv, wo, q_gain, k_gain):
    wkv = jnp.concatenate([wk, wv], axis=1).astype(BF16)
    mem_kv = _matmul([(mem2, 0, D_MODEL)], [wkv], gain=gkv)
    return _xattn(h, mem_kv, batch, gq, wq.astype(BF16), q_gain, k_gain, wo.astype(BF16))


def kernel(x, mem, rel_bias, norm_mix, norm_mem_q, norm_mem_kv, norm_ffn, xa_wq, xa_wk, xa_wv, xa_wo, xa_q_gain, xa_k_gain, ab_w_in, ab_w_out, a_shift_mu, a_w0, a_w2, a_a0, a_a2, a_g2, a_k_k, a_k_a, a_r_k, a_ln_g, a_ln_b, b_conv_w, b_conv_b, b_dt_bias, b_a_log, b_d, b_norm_g, ffn_w_gate, ffn_w_up, ffn_w_down, cd_w_in, cd_w_out, c_q_norm, c_wq_b, c_kv_norm, c_wkv_b, c_q_gain, c_k_gain, d_q_gain, d_k_gain, d_ik_gain, moe_router, moe_w_gate, moe_w_up, moe_w_down):
    batch, seq, d = x.shape
    h = x.reshape(batch * seq, d)
    mem2 = mem.reshape(-1, d)
    depth = norm_mix.shape[0]
    for layer in range(depth):
        i = layer // 2
        if layer % 2 == 0:
            h = _layer0_mix(h, batch, norm_mix[layer], ab_w_in[i], ab_w_out[i], a_shift_mu[i], a_w0[i], a_w2[i],
                            a_a0[i], a_a2[i], a_g2[i], a_k_k[i], a_k_a[i], a_r_k[i], a_ln_g[i], a_ln_b[i],
                            b_conv_w[i], b_conv_b[i], b_dt_bias[i], b_a_log[i], b_d[i], b_norm_g[i])
        else:
            h = _layer1_mix(h, batch, norm_mix[layer], rel_bias, cd_w_in[i], cd_w_out[i], c_q_norm[i], c_wq_b[i],
                            c_kv_norm[i], c_wkv_b[i], c_q_gain[i], c_k_gain[i], d_q_gain[i], d_k_gain[i],
                            d_ik_gain[i])
        h = _memory_attention(h, mem2, batch, norm_mem_q[layer], norm_mem_kv[layer], xa_wq[layer], xa_wk[layer],
                              xa_wv[layer], xa_wo[layer], xa_q_gain[layer], xa_k_gain[layer])
        if layer % 2 == 0:
            h = _swiglu(h, norm_ffn[layer], ffn_w_gate[i].astype(BF16), ffn_w_up[i].astype(BF16),
                        ffn_w_down[i].astype(BF16))
        else:
            router_p = _pad_cols(moe_router[i], LANES).astype(BF16)
            h = _moe(h, norm_ffn[layer], router_p, moe_w_gate[i].astype(BF16), moe_w_up[i].astype(BF16),
                     moe_w_down[i].astype(BF16))
    return h.reshape(batch, seq, d)
```

```python
import functools
import math

import numpy as np
import jax
import jax.numpy as jnp
from jax import lax
from jax.experimental import pallas as pl
from jax.experimental.pallas import tpu as pltpu

F32 = jnp.float32
BF16 = jnp.bfloat16
I32 = jnp.int32
I16 = jnp.int16
HALF16 = 1 << 15
HIGHEST = lax.Precision.HIGHEST

V7X_VMEM_BYTES = 64 * 1024 * 1024
VMEM_LIMIT = V7X_VMEM_BYTES - 8 * 1024 * 1024
LANES = 128

NORM_EPS = 1e-6
D_MODEL = 2048
HEAD64 = 64

A_WIDTH = 1024
A_LORA = (64, 64, 160)
A_LORA_PAD = 384
A_LN_EPS = 1e-5 * (HEAD64 / 8) ** 2
B_WIDTH = 1024
B_HEADS = 16
B_GROUPS = 4
B_STATE = 128
B_CONV = 4
B_CHUNK = 128
B_CONV_CH = B_WIDTH + 2 * B_GROUPS * B_STATE
AB_R, AB_K, AB_V, AB_Z, AB_XBC, AB_LORA, AB_DT, AB_COLS_PAD = 0, 1024, 2048, 3072, 4096, 6144, 6528, 6656

C_HEADS = 8
C_NOPE = 128
C_ROPE = 64
C_QK = C_NOPE + C_ROPE
C_V = 128
C_LORA = 512
ROPE_THETA = 10000.0
D_HEADS = 8
D_KV = 2
D_DIM = 128
D_IDX_DIM = 64
D_TOPK_MAX = 256
Q_BLOCK = 128
REL_BUCKETS = 32
REL_MAX_DIST = 128
CD_QLAT, CD_KVLAT, CD_DQ, CD_DK, CD_DV, CD_IQ, CD_PEIK, CD_IW, CD_COLS_PAD = (
    0, 512, 1024, 2048, 2304, 2560, 3072, 3200, 3584)

X_HEADS = 4
X_DIM = 128
N_EXPERTS = 8

NEG_BIG = -1e30
INT_MIN = -2 ** 31


def _cparams(sem):
    return pltpu.CompilerParams(dimension_semantics=sem, vmem_limit_bytes=VMEM_LIMIT)


def _rms(x, g, eps=NORM_EPS):
    return x * lax.rsqrt(jnp.mean(x * x, axis=-1, keepdims=True) + eps) * g


def _softplus(x):
    return jnp.maximum(x, 0.0) + jnp.log(1.0 + jnp.exp(-jnp.abs(x)))


def _silu(x):
    return x * jax.nn.sigmoid(x)


LOG2E = math.log2(math.e)


def _online_softmax_step(s, m_ref, l_ref, acc_ref):
    cols = [s[:, LANES * c:LANES * (c + 1)] for c in range(s.shape[1] // LANES)]
    mx = cols[0]
    for c in cols[1:]:
        mx = jnp.maximum(mx, c)
    m_old = m_ref[...]
    m_new = jnp.maximum(m_old, jnp.max(mx, axis=-1, keepdims=True))
    alpha = jnp.exp2(m_old - m_new)
    ps = [jnp.exp2(c - m_new) for c in cols]
    rs = ps[0]
    for p in ps[1:]:
        rs = rs + p
    l_ref[...] = alpha * l_ref[...] + jnp.sum(rs, axis=-1, keepdims=True)
    acc_ref[...] = alpha * acc_ref[...]
    m_ref[...] = m_new
    return jnp.concatenate(ps, axis=1).astype(BF16)


def _half_sum_bcast(x):
    left = lax.broadcasted_iota(I32, x.shape, 1) < HEAD64
    s0 = jnp.sum(jnp.where(left, x, 0.0), axis=1, keepdims=True)
    s1 = jnp.sum(jnp.where(left, 0.0, x), axis=1, keepdims=True)
    return jnp.where(left, s0, s1)


def _mm_kernel(*refs, n_x, has_norm, has_res):
    x_refs = refs[:n_x]
    pos = n_x
    g_ref = refs[pos] if has_norm else None
    pos += int(has_norm)
    w_refs = refs[pos:pos + n_x]
    pos += n_x
    res_ref = refs[pos] if has_res else None
    pos += int(has_res)
    o_ref = refs[pos]
    xn_refs = refs[pos + 1:]

    @pl.when(pl.program_id(1) == 0)
    def _():
        for x_ref, xn_ref in zip(x_refs, xn_refs):
            x = x_ref[...].astype(F32)
            if has_norm:
                x = _rms(x, g_ref[...])
            xn_ref[...] = x.astype(BF16)

    acc = None
    for xn_ref, w_ref in zip(xn_refs, w_refs):
        d = jnp.dot(xn_ref[...], w_ref[...], preferred_element_type=F32)
        acc = d if acc is None else acc + d
    if has_res:
        acc = acc + res_ref[...]
    o_ref[...] = acc


def _matmul(xs, ws, *, gain=None, res=None, tm=1024, tn=512):
    m = xs[0][0].shape[0]
    n = ws[0].shape[1]
    tm = min(tm, m)
    tn = min(tn, n)
    assert m % tm == 0 and n % tn == 0
    in_specs, args, scratch = [], [], []
    for arr, cb, width in xs:
        in_specs.append(pl.BlockSpec((tm, width), lambda i, j, cb=cb: (i, cb)))
        args.append(arr)
        scratch.append(pltpu.VMEM((tm, width), BF16))
    if gain is not None:
        in_specs.append(pl.BlockSpec((1, gain.shape[-1]), lambda i, j: (0, 0)))
        args.append(gain.reshape(1, -1))
    for (arr, cb, width), w in zip(xs, ws):
        assert w.shape[0] == width
        in_specs.append(pl.BlockSpec((width, tn), lambda i, j: (0, j)))
        args.append(w)
    if res is not None:
        in_specs.append(pl.BlockSpec((tm, tn), lambda i, j: (i, j)))
        args.append(res)
    return pl.pallas_call(
        functools.partial(_mm_kernel, n_x=len(xs), has_norm=gain is not None, has_res=res is not None),
        grid=(m // tm, n // tn),
        in_specs=in_specs,
        out_specs=pl.BlockSpec((tm, tn), lambda i, j: (i, j)),
        out_shape=jax.ShapeDtypeStruct((m, n), F32),
        scratch_shapes=scratch,
        compiler_params=_cparams(("parallel", "arbitrary")),
        name="matmul",
    )(*args)


def _rwkv_pre_kernel(rkv_ref, lora_ref, rkvp_ref, lorap_ref, mu_rkv_ref, mu_lora_ref,
                     w0_ref, a0_ref, kk_ref, ka_ref, w2_ref, a2_ref, g2_ref,
                     r_o, w_o, k_o, v_o, na_o, nb_o, g_o, *, tiles_per_seq):
    first = (pl.program_id(0) % tiles_per_seq) == 0
    tm = rkv_ref.shape[0]

    def shift_mix(x, prev_rows, mu):
        prev_last = jnp.where(first, 0.0, prev_rows[7:8, :])
        xs = pltpu.roll(x, 1, axis=0)
        row = lax.broadcasted_iota(I32, x.shape, 0)
        xs = jnp.where(row == 0, prev_last, xs)
        return x + (xs - x) * mu

    lo = shift_mix(lora_ref[...], lorap_ref[...], mu_lora_ref[...])
    lane = lax.broadcasted_iota(I32, lo.shape, 1)
    act = jnp.where(lane < A_LORA[0], jnp.tanh(lo),
                    jnp.where(lane < A_LORA[0] + A_LORA[1], lo, jax.nn.sigmoid(lo))).astype(BF16)
    dw = jnp.dot(act, w2_ref[...], preferred_element_type=F32)
    da = jnp.dot(act, a2_ref[...], preferred_element_type=F32)
    g_o[...] = jnp.dot(act, g2_ref[...], preferred_element_type=F32)

    for p in range(A_WIDTH // LANES):
        sl = slice(LANES * p, LANES * (p + 1))

        def mixed(off):
            s2 = slice(off + LANES * p, off + LANES * (p + 1))
            return shift_mix(rkv_ref[:, s2], rkvp_ref[:, s2], mu_rkv_ref[:, s2])

        r_o[:, sl] = mixed(AB_R)
        v_o[:, sl] = mixed(AB_V)
        kx = mixed(AB_K)
        logw = -_softplus(-(w0_ref[:, sl] + dw[:, sl])) - 0.5
        w_o[:, sl] = jnp.exp(-jnp.exp(logw))
        a = jax.nn.sigmoid(a0_ref[:, sl] + da[:, sl])
        kk = kx * kk_ref[:, sl]
        kk = kk * lax.rsqrt(jnp.maximum(_half_sum_bcast(kk * kk), 1e-24))
        k_o[:, sl] = kx * (1.0 + (a - 1.0) * ka_ref[:, sl])
        na_o[:, sl] = -kk
        nb_o[:, sl] = kk * a


def _rwkv_pre(u, seq, mu_rkv, mu_lora, w0, a0, k_k, k_a, w2p, a2p, g2p, tm=256):
    m = u.shape[0]
    tm = min(tm, seq)
    row = lambda w: pl.BlockSpec((1, w), lambda i: (0, 0))
    full = lambda a: pl.BlockSpec(a.shape, lambda i: (0, 0))
    prev = lambda i: jnp.maximum(i * (tm // 8) - 1, 0)
    out = jax.ShapeDtypeStruct((m, A_WIDTH), F32)
    return pl.pallas_call(
        functools.partial(_rwkv_pre_kernel, tiles_per_seq=seq // tm),
        grid=(m // tm,),
        in_specs=[
            pl.BlockSpec((tm, 3 * A_WIDTH), lambda i: (i, 0)),
            pl.BlockSpec((tm, A_LORA_PAD), lambda i: (i, AB_LORA // A_LORA_PAD)),
            pl.BlockSpec((8, 3 * A_WIDTH), lambda i: (prev(i), 0)),
            pl.BlockSpec((8, A_LORA_PAD), lambda i: (prev(i), AB_LORA // A_LORA_PAD)),
            row(3 * A_WIDTH), row(A_LORA_PAD), row(A_WIDTH), row(A_WIDTH), row(A_WIDTH), row(A_WIDTH),
            full(w2p), full(a2p), full(g2p),
        ],
        out_specs=[pl.BlockSpec((tm, A_WIDTH), lambda i: (i, 0))] * 7,
        out_shape=[out] * 7,
        compiler_params=_cparams(("parallel",)),
        name="rwkv_pre",
    )(u, u, u, u, mu_rkv, mu_lora, w0, a0, k_k, k_a, w2p, a2p, g2p)


RWKV_CHUNK = 64
RWKV_PAIR_GROUP = 8


def _rwkv_scan_kernel(r_ref, w_ref, k_ref, v_ref, a_ref, b_ref, y_ref, s_ref, vt_ref, yt_ref):
    nbatch = r_ref.shape[0]
    npairs = s_ref.shape[0]
    per_batch = npairs // nbatch
    where = lambda c: (c // per_batch, slice(LANES * (c % per_batch), LANES * (c % per_batch + 1)))

    @pl.when(pl.program_id(0) == 0)
    def _():
        s_ref[...] = jnp.zeros_like(s_ref)

    lane = lax.broadcasted_iota(I32, (HEAD64, LANES), 1)
    left = lane < HEAD64
    lane64 = lane & (HEAD64 - 1)

    def pair_transpose(x):
        xt = jnp.concatenate([x, x], axis=0).T
        return jnp.where(left, xt[0:HEAD64], xt[HEAD64:2 * HEAD64])

    def two_terms(x):
        hi = x.astype(BF16)
        return jnp.concatenate([hi, (x - hi.astype(F32)).astype(BF16)], axis=1)

    for p in range(npairs):
        bi, sl = where(p)
        vt_ref[p] = pair_transpose(v_ref[bi, :, sl])
    yt_ref[...] = jnp.zeros_like(yt_ref)

    ri = lax.broadcasted_iota(I32, (2 * LANES, LANES), 0)
    ci = lax.broadcasted_iota(I32, (2 * LANES, LANES), 1)
    ones_blk = (((ri // HEAD64) & 1) == (ci // HEAD64)).astype(BF16)

    def half_sum_mxu(x, split):
        if not split:
            return jnp.dot(x.astype(BF16), ones_blk[0:LANES], preferred_element_type=F32)
        return jnp.dot(two_terms(x), ones_blk, preferred_element_type=F32)

    def step8(t8, carry):
        rows8 = pl.ds(pl.multiple_of(t8 * 8, 8), 8)
        unroll = (LANES - t8 * 8) & (LANES - 1)
        for g0 in range(0, npairs, RWKV_PAIR_GROUP):
            group = range(g0, min(g0 + RWKV_PAIR_GROUP, npairs))
            rows = {p: [ref[where(p)[0], rows8, where(p)[1]] for ref in (a_ref, w_ref, b_ref, k_ref, r_ref)]
                    for p in group}
            s = {p: s_ref[p] for p in group}
            vt8 = {p: pltpu.roll(vt_ref[p], unroll, axis=1) for p in group}
            for j in range(8):
                sel = lane64 == t8 * 8 + j
                for p in group:
                    a_row, w_row, b_row, k_row, r_row = (x8[j:j + 1, :] for x8 in rows[p])
                    sa = half_sum_mxu(s[p] * a_row, True)
                    vc = jnp.take_along_axis(vt8[p], jnp.where(left, j, HEAD64 + j), axis=1)
                    s[p] = s[p] * w_row + sa * b_row + vc * k_row
                    yt_ref[p] = jnp.where(sel, half_sum_mxu(s[p] * r_row, False), yt_ref[p])
            for p in group:
                s_ref[p] = s[p]
        return carry

    lax.fori_loop(0, RWKV_CHUNK // 8, step8, 0)

    for p in range(npairs):
        bi, sl = where(p)
        y_ref[bi, :, sl] = pair_transpose(yt_ref[p])


def _rwkv_scan(r, w, k, v, na, nb, batch):
    m = r.shape[0]
    seq = m // batch
    npairs = batch * (A_WIDTH // LANES)
    spec = pl.BlockSpec((batch, RWKV_CHUNK, A_WIDTH), lambda c: (0, c, 0))
    pair_scratch = pltpu.VMEM((npairs, HEAD64, LANES), F32)
    by_batch = lambda t: t.reshape(batch, seq, A_WIDTH)
    y = pl.pallas_call(
        _rwkv_scan_kernel,
        grid=(seq // RWKV_CHUNK,),
        in_specs=[spec] * 6,
        out_specs=spec,
        out_shape=jax.ShapeDtypeStruct((batch, seq, A_WIDTH), F32),
        scratch_shapes=[pair_scratch, pair_scratch, pair_scratch],
        compiler_params=_cparams(("arbitrary",)),
        name="rwkv_scan",
    )(*(by_batch(t) for t in (r, w, k, v, na, nb)))
    return y.reshape(m, A_WIDTH)


def _rwkv_post_kernel(y_ref, r_ref, k_ref, v_ref, g_ref, lng_ref, lnb_ref, rk_ref, o_ref):
    for p in range(A_WIDTH // LANES):
        sl = slice(LANES * p, LANES * (p + 1))
        y = y_ref[:, sl]
        mean = _half_sum_bcast(y) * (1.0 / HEAD64)
        d = y - mean
        var = _half_sum_bcast(d * d) * (1.0 / HEAD64)
        yn = d * lax.rsqrt(var + A_LN_EPS) * lng_ref[:, sl] + lnb_ref[:, sl]
        bonus = _half_sum_bcast(r_ref[:, sl] * k_ref[:, sl] * rk_ref[:, sl]) * v_ref[:, sl]
        o_ref[:, sl] = (yn + bonus) * g_ref[:, sl]


def _rwkv_post(y, r, k, v, g, ln_g, ln_b, r_k, tm=256):
    m = y.shape[0]
    tm = min(tm, m)
    spec = pl.BlockSpec((tm, A_WIDTH), lambda i: (i, 0))
    row = pl.BlockSpec((1, A_WIDTH), lambda i: (0, 0))
    return pl.pallas_call(
        _rwkv_post_kernel,
        grid=(m // tm,),
        in_specs=[spec] * 5 + [row] * 3,
        out_specs=spec,
        out_shape=jax.ShapeDtypeStruct((m, A_WIDTH), F32),
        compiler_params=_cparams(("parallel",)),
        name="rwkv_post",
    )(y, r, k, v, g, ln_g, ln_b, r_k)


def _ssd_kernel(z_ref, xbc_ref, dt_ref, cw_ref, cb_ref, dtb_ref, alog_ref, dskip_ref, ng_ref,
                expand_ref, o_ref, st_ref, tail_ref):
    lc = B_CHUNK

    @pl.when(pl.program_id(1) == 0)
    def _():
        st_ref[...] = jnp.zeros_like(st_ref)
        tail_ref[...] = jnp.zeros_like(tail_ref)

    x = xbc_ref[...]
    tail = tail_ref[...]
    row8 = lax.broadcasted_iota(I32, tail.shape, 0)
    conv = cb_ref[...] + cw_ref[B_CONV - 1:B_CONV, :] * x
    for j in range(1, B_CONV):
        xs = pltpu.roll(x, j, axis=0)
        top = jnp.where(row8 < j, pltpu.roll(tail, j, axis=0), xs[0:8])
        xs = jnp.concatenate([top, xs[8:]], axis=0)
        conv = conv + cw_ref[B_CONV - 1 - j:B_CONV - j, :] * xs
    tail_ref[...] = x[lc - 8:lc]
    act = _silu(conv)
    xs_in = act[:, 0:B_WIDTH]
    bm = act[:, B_WIDTH:B_WIDTH + B_GROUPS * B_STATE].astype(BF16)
    cm = act[:, B_WIDTH + B_GROUPS * B_STATE:].astype(BF16)

    dt = _softplus(dt_ref[...] + dtb_ref[...])
    a_neg = -jnp.exp(alog_ref[...])
    da = dt * a_neg
    ri = lax.broadcasted_iota(I32, (lc, lc), 0)
    ci = lax.broadcasted_iota(I32, (lc, lc), 1)
    causal = ci <= ri
    tri = causal.astype(F32)
    cum = jnp.dot(tri, da, precision=HIGHEST, preferred_element_type=F32)
    cum_t = jnp.dot(da.T, (ri <= ci).astype(F32), precision=HIGHEST,
                    preferred_element_type=F32)
    expand = expand_ref[...]
    widen = lambda t: jnp.dot(t, expand, precision=HIGHEST, preferred_element_type=F32)
    dt_full = widen(dt)
    ecum_full = widen(jnp.exp(cum))
    dte_full = widen(jnp.exp(cum[lc - 1:lc, :] - cum))
    xdt = xs_in * dt_full
    xdt_b = xdt.astype(BF16)
    xdte_b = (xdt * dte_full).astype(BF16)
    left = lax.broadcasted_iota(I32, (lc, LANES), 1) < HEAD64

    ys = []
    for g in range(B_GROUPS):
        gs = slice(B_STATE * g, B_STATE * (g + 1))
        cm_g = cm[:, gs]
        bm_g = bm[:, gs]
        cb = lax.dot_general(cm_g, bm_g, (((1,), (1,)), ((), ())), preferred_element_type=F32)
        bm_t = bm_g.T
        pairs_per_group = B_HEADS // B_GROUPS // 2
        for q in range(pairs_per_group):
            p = g * pairs_per_group + q
            sl = slice(LANES * p, LANES * (p + 1))
            yd = []
            for h in (2 * p, 2 * p + 1):
                seg = cum[:, h:h + 1] - cum_t[h:h + 1, :]
                dec = jnp.where(causal, jnp.exp(jnp.minimum(seg, 0.0)), 0.0)
                yd.append(jnp.dot((cb * dec).astype(BF16), xdt_b[:, sl], preferred_element_type=F32))
            y_diag = jnp.where(left, yd[0], yd[1])
            st = st_ref[p]
            y_off = jnp.dot(cm_g, st.astype(BF16), preferred_element_type=F32) * ecum_full[:, sl]
            ys.append(y_diag + y_off)
            st_ref[p] = st * ecum_full[lc - 1:lc, sl] + jnp.dot(bm_t, xdte_b[:, sl],
                                                                preferred_element_type=F32)
    y = jnp.concatenate(ys, axis=1)
    y = (y + dskip_ref[...] * xs_in) * _silu(z_ref[...])
    gw = B_WIDTH // B_GROUPS
    for g in range(B_GROUPS):
        gs = slice(gw * g, gw * (g + 1))
        yg = y[:, gs]
        o_ref[:, gs] = yg * lax.rsqrt(jnp.mean(yg * yg, axis=-1, keepdims=True) + NORM_EPS) * ng_ref[:, gs]


def _ssd(u, batch, conv_w, conv_b, dt_bias, a_log, d_skip, norm_g):
    m = u.shape[0]
    seq = m // batch
    nchunk = seq // B_CHUNK
    rows = lambda b, c: b * nchunk + c
    pad16 = lambda t: jnp.pad(t.reshape(1, -1), ((0, 0), (0, LANES - B_HEADS)))
    expand = (np.arange(LANES)[:, None] == (np.arange(B_WIDTH)[None, :] // HEAD64)).astype(np.float32)
    full = lambda a: pl.BlockSpec(a.shape, lambda b, c: (0,) * a.ndim)
    args = [conv_w, conv_b.reshape(1, -1), pad16(dt_bias), pad16(a_log),
            jnp.repeat(d_skip, HEAD64).reshape(1, -1), norm_g.reshape(1, -1), jnp.asarray(expand)]
    return pl.pallas_call(
        _ssd_kernel,
        grid=(batch, nchunk),
        in_specs=[
            pl.BlockSpec((B_CHUNK, B_WIDTH), lambda b, c: (rows(b, c), AB_Z // B_WIDTH)),
            pl.BlockSpec((B_CHUNK, B_CONV_CH), lambda b, c: (rows(b, c), AB_XBC // B_CONV_CH)),
            pl.BlockSpec((B_CHUNK, LANES), lambda b, c: (rows(b, c), AB_DT // LANES)),
        ] + [full(a) for a in args],
        out_specs=pl.BlockSpec((B_CHUNK, B_WIDTH), lambda b, c: (rows(b, c), 0)),
        out_shape=jax.ShapeDtypeStruct((m, B_WIDTH), F32),
        scratch_shapes=[pltpu.VMEM((B_HEADS // 2, B_STATE, LANES), F32),
                        pltpu.VMEM((8, B_CONV_CH), F32)],
        compiler_params=_cparams(("parallel", "arbitrary")),
        name="ssd",
    )(u, u, u, *args)


def _xattn_kernel(h_ref, gq_ref, wq_ref, kv_ref, qg_ref, kg_ref, wo_ref, o_ref):
    h = h_ref[...]
    hn = _rms(h, gq_ref[...]).astype(BF16)
    q = jnp.dot(hn, wq_ref[...], preferred_element_type=F32)
    outs = []
    for hd in range(X_HEADS):
        sl = slice(X_DIM * hd, X_DIM * (hd + 1))
        qh = _rms(q[:, sl], qg_ref[...]).astype(BF16)
        kh = _rms(kv_ref[:, sl], kg_ref[...]).astype(BF16)
        vh = kv_ref[:, X_HEADS * X_DIM + X_DIM * hd:X_HEADS * X_DIM + X_DIM * (hd + 1)].astype(BF16)
        s = lax.dot_general(qh, kh, (((1,), (1,)), ((), ())), preferred_element_type=F32) * X_DIM ** -0.5
        e = jnp.exp(s - jnp.max(s, axis=-1, keepdims=True))
        p = e / jnp.sum(e, axis=-1, keepdims=True)
        outs.append(jnp.dot(p.astype(BF16), vh, preferred_element_type=F32))
    o = jnp.concatenate(outs, axis=1).astype(BF16)
    o_ref[...] = h + jnp.dot(o, wo_ref[...], preferred_element_type=F32)


def _xattn(h, mem_kv, batch, gq, wq, q_gain, k_gain, wo, tm=512):
    m, d = h.shape
    seq = m // batch
    tm = min(tm, seq)
    mlen = mem_kv.shape[0] // batch
    nt = seq // tm
    full = lambda a: pl.BlockSpec(a.shape, lambda b, i: (0, 0))
    args = [gq.reshape(1, -1), wq, mem_kv, q_gain.reshape(1, -1), k_gain.reshape(1, -1), wo]
    specs = [full(a) for a in args]
    specs[2] = pl.BlockSpec((mlen, mem_kv.shape[1]), lambda b, i: (b, 0))
    return pl.pallas_call(
        _xattn_kernel,
        grid=(batch, nt),
        in_specs=[pl.BlockSpec((tm, d), lambda b, i: (b * nt + i, 0))] + specs,
        out_specs=pl.BlockSpec((tm, d), lambda b, i: (b * nt + i, 0)),
        out_shape=jax.ShapeDtypeStruct((m, d), F32),
        compiler_params=_cparams(("parallel", "parallel")),
        name="xattn",
    )(h, *args)


def _swiglu_kernel(h_ref, g_ref, wg_ref, wu_ref, wd_ref, o_ref, xn_ref, acc_ref):
    j = pl.program_id(1)

    @pl.when(j == 0)
    def _():
        xn_ref[...] = _rms(h_ref[...], g_ref[...]).astype(BF16)
        acc_ref[...] = jnp.zeros_like(acc_ref)

    xn = xn_ref[...]
    gate = jnp.dot(xn, wg_ref[...], preferred_element_type=F32)
    up = jnp.dot(xn, wu_ref[...], preferred_element_type=F32)
    acc_ref[...] += jnp.dot((_silu(gate) * up).astype(BF16), wd_ref[...], preferred_element_type=F32)

    @pl.when(j == pl.num_programs(1) - 1)
    def _():
        o_ref[...] = h_ref[...] + acc_ref[...]


def _swiglu(h, gain, wg, wu, wd, tm=512, tf=512):
    m, d = h.shape
    f = wg.shape[1]
    tm = min(tm, m)
    assert f % tf == 0
    return pl.pallas_call(
        _swiglu_kernel,
        grid=(m // tm, f // tf),
        in_specs=[
            pl.BlockSpec((tm, d), lambda i, j: (i, 0)),
            pl.BlockSpec((1, d), lambda i, j: (0, 0)),
            pl.BlockSpec((d, tf), lambda i, j: (0, j)),
            pl.BlockSpec((d, tf), lambda i, j: (0, j)),
            pl.BlockSpec((tf, d), lambda i, j: (j, 0)),
        ],
        out_specs=pl.BlockSpec((tm, d), lambda i, j: (i, 0)),
        out_shape=jax.ShapeDtypeStruct((m, d), F32),
        scratch_shapes=[pltpu.VMEM((tm, d), BF16), pltpu.VMEM((tm, d), F32)],
        compiler_params=_cparams(("parallel", "arbitrary")),
        name="swiglu",
    )(h, gain.reshape(1, -1), wg, wu, wd)


MOE_TOPK = 2
MOE_ROWS = 512
MOE_TOKENS = 256


def _moe_route_kernel(h_ref, g_ref, router_ref, eid_o, gw_o):
    xn = _rms(h_ref[...], g_ref[...]).astype(BF16)
    logits = jnp.dot(xn, router_ref[...], preferred_element_type=F32)
    lane = lax.broadcasted_iota(I32, logits.shape, 1)
    logits = jnp.where(lane < N_EXPERTS, logits, -jnp.inf)
    m1 = jnp.max(logits, axis=-1, keepdims=True)
    i1 = jnp.min(jnp.where(logits == m1, lane, LANES), axis=-1, keepdims=True)
    rest = jnp.where(lane == i1, -jnp.inf, logits)
    m2 = jnp.max(rest, axis=-1, keepdims=True)
    i2 = jnp.min(jnp.where(rest == m2, lane, LANES), axis=-1, keepdims=True)
    e2 = jnp.exp(m2 - m1)
    eid_o[...] = jnp.where(lane == 0, i1, jnp.where(lane == 1, i2, 0))
    gw_o[...] = jnp.where(lane == 0, 1.0 / (1.0 + e2), jnp.where(lane == 1, e2 / (1.0 + e2), 0.0))


def _moe_route(h, gain, router_p, tm=512):
    m, d = h.shape
    tm = min(tm, m)
    spec = pl.BlockSpec((tm, LANES), lambda i: (i, 0))
    return pl.pallas_call(
        _moe_route_kernel,
        grid=(m // tm,),
        in_specs=[pl.BlockSpec((tm, d), lambda i: (i, 0)), pl.BlockSpec((1, d), lambda i: (0, 0)),
                  pl.BlockSpec((d, LANES), lambda i: (0, 0))],
        out_specs=[spec, spec],
        out_shape=[jax.ShapeDtypeStruct((m, LANES), I32), jax.ShapeDtypeStruct((m, LANES), F32)],
        compiler_params=_cparams(("parallel",)),
        name="moe_route",
    )(h, gain.reshape(1, -1), router_p)


def _moe_plan(eid, nblocks):
    e = eid.reshape(-1)
    onehot = (e[:, None] == jnp.arange(N_EXPERTS, dtype=I32)[None, :]).astype(I32)
    csum = jnp.cumsum(onehot, axis=0)
    rank = jnp.sum(onehot * csum, axis=1) - 1
    counts = csum[-1]
    padded = (counts + MOE_ROWS - 1) // MOE_ROWS * MOE_ROWS
    gend = jnp.cumsum(padded)
    dest = jnp.sum(onehot * (gend - padded)[None, :], axis=1) + rank
    nb_used = gend[-1] // MOE_ROWS
    blk = jnp.arange(nblocks, dtype=I32)
    blk_e = jnp.minimum(jnp.sum((blk[:, None] * MOE_ROWS >= gend[None, :]).astype(I32), axis=1), N_EXPERTS - 1)
    blk_e = jnp.where(blk < nb_used, blk_e, blk_e[jnp.maximum(nb_used - 1, 0)])
    return dest.astype(I32), blk_e.astype(I32), nb_used.astype(I32).reshape(1)


def _row_copy(src_ref, src_row, dst_ref, dst_row, sem):
    return pltpu.make_async_copy(src_ref.at[pl.ds(src_row, 1), :], dst_ref.at[pl.ds(dst_row, 1), :], sem)


def _moe_dispatch_kernel(dest_ref, h_ref, xs_in_ref, xs_ref, sem):
    del xs_in_ref
    base = pl.program_id(0) * MOE_TOKENS

    def issue(r, carry):
        for c in range(MOE_TOPK):
            _row_copy(h_ref, r, xs_ref, dest_ref[MOE_TOPK * (base + r) + c], sem).start()
        return carry

    def drain(r, carry):
        for c in range(MOE_TOPK):
            _row_copy(h_ref, 0, xs_ref, 0, sem).wait()
        return carry

    lax.fori_loop(0, MOE_TOKENS, issue, 0)
    lax.fori_loop(0, MOE_TOKENS, drain, 0)


def _moe_dispatch(h, dest, rows):
    m, d = h.shape
    return pl.pallas_call(
        _moe_dispatch_kernel,
        grid_spec=pltpu.PrefetchScalarGridSpec(
            num_scalar_prefetch=1,
            grid=(m // MOE_TOKENS,),
            in_specs=[pl.BlockSpec((MOE_TOKENS, d), lambda i, dest: (i, 0)), pl.BlockSpec(memory_space=pl.ANY)],
            out_specs=pl.BlockSpec(memory_space=pl.ANY),
            scratch_shapes=[pltpu.SemaphoreType.DMA(())],
        ),
        out_shape=jax.ShapeDtypeStruct((rows, d), F32),
        input_output_aliases={2: 0},
        compiler_params=_cparams(("arbitrary",)),
        name="moe_dispatch",
    )(dest, h, jnp.zeros((rows, d), F32))


def _moe_ffn_kernel(be_ref, nb_ref, x_ref, g_ref, wg_ref, wu_ref, wd_ref, y_ref, xn_ref, acc_ref):
    del be_ref
    j = pl.program_id(1)
    live = pl.program_id(0) < nb_ref[0]
    last = j == pl.num_programs(1) - 1

    @pl.when(live & (j == 0))
    def _():
        xn_ref[...] = _rms(x_ref[...], g_ref[...]).astype(BF16)
        acc_ref[...] = jnp.zeros_like(acc_ref)

    @pl.when(live)
    def _():
        xn = xn_ref[...]
        gate = jnp.dot(xn, wg_ref[0], preferred_element_type=F32)
        up = jnp.dot(xn, wu_ref[0], preferred_element_type=F32)
        acc_ref[...] += jnp.dot((_silu(gate) * up).astype(BF16), wd_ref[0], preferred_element_type=F32)

    @pl.when(live & last)
    def _():
        y_ref[...] = acc_ref[...]

    @pl.when(jnp.logical_not(live) & last)
    def _():
        y_ref[...] = jnp.zeros_like(y_ref)


def _moe_ffn(xs, blk_e, nb_used, gain, wg, wu, wd, tf=256):
    rows, d = xs.shape
    f = wg.shape[2]
    nj = f // tf
    assert f % tf == 0 and rows % MOE_ROWS == 0
    jx = lambda b, j, nb: jnp.where(b < nb[0], j, nj - 1)
    return pl.pallas_call(
        _moe_ffn_kernel,
        grid_spec=pltpu.PrefetchScalarGridSpec(
            num_scalar_prefetch=2,
            grid=(rows // MOE_ROWS, nj),
            in_specs=[
                pl.BlockSpec((MOE_ROWS, d), lambda b, j, be, nb: (jnp.minimum(b, jnp.maximum(nb[0] - 1, 0)), 0)),
                pl.BlockSpec((1, d), lambda b, j, be, nb: (0, 0)),
                pl.BlockSpec((1, d, tf), lambda b, j, be, nb: (be[b], 0, jx(b, j, nb))),
                pl.BlockSpec((1, d, tf), lambda b, j, be, nb: (be[b], 0, jx(b, j, nb))),
                pl.BlockSpec((1, tf, d), lambda b, j, be, nb: (be[b], jx(b, j, nb), 0)),
            ],
            out_specs=pl.BlockSpec((MOE_ROWS, d), lambda b, j, be, nb: (b, 0)),
            scratch_shapes=[pltpu.VMEM((MOE_ROWS, d), BF16), pltpu.VMEM((MOE_ROWS, d), F32)],
        ),
        out_shape=jax.ShapeDtypeStruct((rows, d), F32),
        compiler_params=_cparams(("parallel", "arbitrary")),
        name="moe_ffn",
    )(blk_e, nb_used, xs, gain.reshape(1, -1), wg, wu, wd)


def _moe_combine_kernel(dest_ref, h_ref, gw_ref, ys_ref, o_ref, ybuf_ref, sem):
    base = pl.program_id(0) * MOE_TOKENS

    def issue(r, carry):
        for c in range(MOE_TOPK):
            _row_copy(ys_ref, dest_ref[MOE_TOPK * (base + r) + c], ybuf_ref.at[c], r, sem).start()
        return carry

    def drain(r, carry):
        for c in range(MOE_TOPK):
            _row_copy(ys_ref, 0, ybuf_ref.at[c], 0, sem).wait()
        return carry

    lax.fori_loop(0, MOE_TOKENS, issue, 0)
    lax.fori_loop(0, MOE_TOKENS, drain, 0)
    gw = gw_ref[...]
    o_ref[...] = h_ref[...] + (gw[:, 0:1] * ybuf_ref[0] + gw[:, 1:2] * ybuf_ref[1])


def _moe_combine(h, gw, ys, dest):
    m, d = h.shape
    tok = lambda w: pl.BlockSpec((MOE_TOKENS, w), lambda i, dest: (i, 0))
    return pl.pallas_call(
        _moe_combine_kernel,
        grid_spec=pltpu.PrefetchScalarGridSpec(
            num_scalar_prefetch=1,
            grid=(m // MOE_TOKENS,),
            in_specs=[tok(d), tok(LANES), pl.BlockSpec(memory_space=pl.ANY)],
            out_specs=tok(d),
            scratch_shapes=[pltpu.VMEM((MOE_TOPK, MOE_TOKENS, d), F32), pltpu.SemaphoreType.DMA(())],
        ),
        out_shape=jax.ShapeDtypeStruct((m, d), F32),
        compiler_params=_cparams(("arbitrary",)),
        name="moe_combine",
    )(dest, h, gw, ys)


def _moe(h, gain, router_p, wg, wu, wd):
    m = h.shape[0]
    nblocks = MOE_TOPK * m // MOE_ROWS + N_EXPERTS
    eid, gw = _moe_route(h, gain, router_p)
    dest, blk_e, nb_used = _moe_plan(eid[:, :MOE_TOPK], nblocks)
    xs = _moe_dispatch(h, dest, nblocks * MOE_ROWS)
    ys = _moe_ffn(xs, blk_e, nb_used, gain, wg, wu, wd)
    return _moe_combine(h, gw, ys, dest)


def _rope_pairs(x, cos, sin_signed):
    w = x.shape[1]
    lane = lax.broadcasted_iota(I32, x.shape, 1)
    partner = jnp.where((lane & 32) != 0, pltpu.roll(x, 32, axis=1), pltpu.roll(x, w - 32, axis=1))
    return x * cos + partner * sin_signed


def _mla_prep_kernel(ql_ref, kvl_ref, pe_ref, qn_ref, wq_ref, kvn_ref, wkv_ref, qgn_ref, qgr_ref,
                     kgn_ref, kgr_ref, cos_ref, sin_ref, qn_o, qr_o, kn_o, kr_o, v_o):
    nn = C_HEADS * C_NOPE
    left = lax.broadcasted_iota(I32, (ql_ref.shape[0], LANES), 1) < C_ROPE

    def head_norm(nope_of, rope_of, gn_ref, gr_ref, n_o, r_o, post_scale):
        for p in range(C_HEADS // 2):
            pr = slice(LANES * p, LANES * (p + 1))
            rope = rope_of(p)
            r2 = rope * rope
            rs = []
            for hh, ss_rope in ((0, jnp.sum(jnp.where(left, r2, 0.0), axis=-1, keepdims=True)),
                                (1, jnp.sum(jnp.where(left, 0.0, r2), axis=-1, keepdims=True))):
                hs = slice(C_NOPE * (2 * p + hh), C_NOPE * (2 * p + hh + 1))
                nope = nope_of(hs)
                ss = jnp.sum(nope * nope, axis=-1, keepdims=True) + ss_rope
                rs.append(lax.rsqrt(ss * (1.0 / C_QK) + NORM_EPS))
                n_o[:, hs] = (nope * rs[hh] * gn_ref[:, hs] * post_scale).astype(BF16)
            rope = rope * jnp.where(left, rs[0], rs[1]) * gr_ref[:, pr]
            r_o[:, pr] = (_rope_pairs(rope, cos_ref[:, pr], sin_ref[:, pr]) * post_scale).astype(BF16)

    q = jnp.dot(_rms(ql_ref[...], qn_ref[...]).astype(BF16), wq_ref[...], preferred_element_type=F32)
    head_norm(lambda hs: q[:, hs], lambda p: q[:, nn + LANES * p:nn + LANES * (p + 1)],
              qgn_ref, qgr_ref, qn_o, qr_o, C_QK ** -0.5 * LOG2E)

    kv = jnp.dot(_rms(kvl_ref[...], kvn_ref[...]).astype(BF16), wkv_ref[...], preferred_element_type=F32)
    v_o[...] = kv[:, nn:].astype(BF16)
    pe = pe_ref[...]
    pe_pair = jnp.where(left, pe, pltpu.roll(pe, C_ROPE, axis=1))
    head_norm(lambda hs: kv[:, hs], lambda p: pe_pair, kgn_ref, kgr_ref, kn_o, kr_o, 1.0)


def _mla_prep(u, seq, q_norm, wq_p, kv_norm, wkv_p, q_gain, k_gain, tm=256):
    m = u.shape[0]
    tm = min(tm, seq)
    nn, nr = C_HEADS * C_NOPE, C_HEADS * C_ROPE
    half = C_ROPE // 2
    freqs = ROPE_THETA ** (-jnp.arange(half, dtype=F32) / half)
    ang = jnp.arange(seq, dtype=F32)[:, None] * freqs[None, :]
    cos = jnp.tile(jnp.cos(ang), (1, 2 * C_HEADS))
    sin = jnp.tile(jnp.concatenate([-jnp.sin(ang), jnp.sin(ang)], axis=1), (1, C_HEADS))
    tile_gain = lambda g: jnp.tile(g, C_HEADS).reshape(1, -1)
    consts = [q_norm.reshape(1, -1), wq_p, kv_norm.reshape(1, -1), wkv_p,
              tile_gain(q_gain[:C_NOPE]), tile_gain(q_gain[C_NOPE:]),
              tile_gain(k_gain[:C_NOPE]), tile_gain(k_gain[C_NOPE:])]
    full = lambda a: pl.BlockSpec(a.shape, lambda i: (0, 0))
    nt = seq // tm
    tab = pl.BlockSpec((tm, nr), lambda i: (i % nt, 0))
    out = lambda w: jax.ShapeDtypeStruct((m, w), BF16)
    ospec = lambda w: pl.BlockSpec((tm, w), lambda i: (i, 0))
    return pl.pallas_call(
        _mla_prep_kernel,
        grid=(m // tm,),
        in_specs=[
            pl.BlockSpec((tm, C_LORA), lambda i: (i, CD_QLAT // C_LORA)),
            pl.BlockSpec((tm, C_LORA), lambda i: (i, CD_KVLAT // C_LORA)),
            pl.BlockSpec((tm, LANES), lambda i: (i, CD_PEIK // LANES)),
        ] + [full(a) for a in consts] + [tab, tab],
        out_specs=[ospec(nn), ospec(nr), ospec(nn), ospec(nr), ospec(nn)],
        out_shape=[out(nn), out(nr), out(nn), out(nr), out(nn)],
        compiler_params=_cparams(("parallel",)),
        name="mla_prep",
    )(u, u, u, *consts, cos, sin)


def _mla_attn_kernel(qn_ref, qr_ref, kn_ref, kr_ref, v_ref, o_ref, m_ref, l_ref, acc_ref, *, tq):
    qi = pl.program_id(2)
    lane = lax.broadcasted_iota(I32, qr_ref.shape, 1)
    qr = qr_ref[...]
    qs = []
    for hh in range(2):
        qr_h = jnp.where((lane // HEAD64) == hh, qr, jnp.zeros_like(qr))
        qs.append(jnp.concatenate([qn_ref[:, C_NOPE * hh:C_NOPE * (hh + 1)], qr_h], axis=1))
    m_ref[...] = jnp.full_like(m_ref, NEG_BIG)
    l_ref[...] = jnp.zeros_like(l_ref)
    acc_ref[...] = jnp.zeros_like(acc_ref)
    nt = (((1,), (1,)), ((), ()))

    def update(j, masked):
        rows = pl.ds(pl.multiple_of(j * tq, tq), tq)
        kr = kr_ref[rows, :]
        ss = []
        for hh in range(2):
            kc = jnp.concatenate([kn_ref[rows, C_NOPE * hh:C_NOPE * (hh + 1)], kr], axis=1)
            ss.append(lax.dot_general(qs[hh], kc, nt, preferred_element_type=F32))
        for hh in range(2):
            s = ss[hh]
            if masked:
                ri = lax.broadcasted_iota(I32, s.shape, 0)
                ci = lax.broadcasted_iota(I32, s.shape, 1)
                s = jnp.where(ci <= ri, s, NEG_BIG)
            p = _online_softmax_step(s, m_ref.at[hh], l_ref.at[hh], acc_ref.at[hh])
            acc_ref[hh] += jnp.dot(p, v_ref[rows, C_V * hh:C_V * (hh + 1)], preferred_element_type=F32)

    def body(j, carry):
        update(j, False)
        return carry

    lax.fori_loop(0, qi, body, 0)
    update(qi, True)
    for hh in range(2):
        o_ref[:, C_V * hh:C_V * (hh + 1)] = acc_ref[hh] / l_ref[hh]


def _mla_attn(qn, qr, kn, kr, v, batch, tq=512):
    m = qn.shape[0]
    seq = m // batch
    tq = min(tq, seq)
    nq = seq // tq
    qspec = lambda w: pl.BlockSpec((tq, w), lambda b, p, i: (b * nq + i, p))
    kspec = lambda w: pl.BlockSpec((seq, w), lambda b, p, i: (b, p))
    return pl.pallas_call(
        functools.partial(_mla_attn_kernel, tq=tq),
        grid=(batch, C_HEADS // 2, nq),
        in_specs=[qspec(2 * C_NOPE), qspec(2 * C_ROPE), kspec(2 * C_NOPE), kspec(2 * C_ROPE), kspec(2 * C_V)],
        out_specs=qspec(2 * C_V),
        out_shape=jax.ShapeDtypeStruct((m, C_HEADS * C_V), F32),
        scratch_shapes=[pltpu.VMEM((2, tq, LANES), F32), pltpu.VMEM((2, tq, LANES), F32),
                        pltpu.VMEM((2, tq, C_V), F32)],
        compiler_params=_cparams(("parallel", "parallel", "arbitrary")),
        name="mla_attn",
    )(qn, qr, kn, kr, v)


def _rel_bucket_table(n):
    d = np.arange(n)
    max_exact = REL_BUCKETS // 2
    nf = np.maximum(d, 1).astype(np.float32)
    large = max_exact + (np.log(nf / np.float32(max_exact)) / np.float32(math.log(REL_MAX_DIST / max_exact))
                         * np.float32(REL_BUCKETS - max_exact)).astype(np.int32)
    large = np.minimum(large, REL_BUCKETS - 1)
    return np.where(d < max_exact, d, large).astype(np.int32)


def _near_bias_kernel(bucket_ref, rel_ref, o_ref):
    for var in range(2):
        bk = bucket_ref[var]
        for hd in range(D_HEADS):
            acc = jnp.zeros(bk.shape, F32)
            for b in range(REL_BUCKETS):
                acc = jnp.where(bk == b, rel_ref[b, hd], acc)
            o_ref[var, hd] = (acc - rel_ref[REL_BUCKETS - 1, hd]) * LOG2E


def _near_bias(rel_bias):
    table = _rel_bucket_table(2 * Q_BLOCK)
    q = np.arange(Q_BLOCK)[:, None]
    j = np.arange(2 * Q_BLOCK)[None, :]
    dist0 = np.maximum(q - j, 0)
    dist1 = np.maximum(q - j + Q_BLOCK, 0)
    buckets = np.stack([table[dist0], table[dist1]]).astype(np.int32)
    return pl.pallas_call(
        _near_bias_kernel,
        in_specs=[pl.BlockSpec(memory_space=pltpu.VMEM), pl.BlockSpec(memory_space=pltpu.SMEM)],
        out_specs=pl.BlockSpec(memory_space=pltpu.VMEM),
        out_shape=jax.ShapeDtypeStruct((2, D_HEADS, Q_BLOCK, 2 * Q_BLOCK), F32),
        name="near_bias",
    )(jnp.asarray(buckets), rel_bias)


def _dsa_prep_kernel(dq_ref, dk_ref, dv_ref, iq_ref, peik_ref, iw_ref, qg_ref, kg_ref, ikg_ref,
                     q_o, k_o, v_o, iq_o, ika_o, ikb_o, iw_o):
    for hd in range(D_HEADS):
        sl = slice(D_DIM * hd, D_DIM * (hd + 1))
        q_o[:, sl] = (_rms(dq_ref[:, sl], qg_ref[...]) * (D_DIM ** -0.5 * LOG2E)).astype(BF16)
    for hd in range(D_KV):
        sl = slice(D_DIM * hd, D_DIM * (hd + 1))
        k_o[:, sl] = _rms(dk_ref[:, sl], kg_ref[...]).astype(BF16)
    v_o[...] = dv_ref[...].astype(BF16)
    iq_o[...] = iq_ref[...].astype(BF16)
    x = peik_ref[...]
    right = lax.broadcasted_iota(I32, x.shape, 1) >= D_IDX_DIM
    x = jnp.where(right, x, 0.0)
    ms = jnp.sum(x * x, axis=-1, keepdims=True) * (1.0 / D_IDX_DIM)
    ik = (x * lax.rsqrt(ms + NORM_EPS) * ikg_ref[...]).astype(BF16)
    ikb_o[...] = ik
    ika_o[...] = pltpu.roll(ik.astype(F32), D_IDX_DIM, axis=1).astype(BF16)
    iw_o[...] = iw_ref[...] * D_HEADS ** -0.5


def _dsa_prep(u, q_gain, k_gain, ik_gain, tm=256):
    m = u.shape[0]
    tm = min(tm, m)
    blk = lambda w, off: pl.BlockSpec((tm, w), lambda i: (i, off // w))
    full = lambda a: pl.BlockSpec(a.shape, lambda i: (0, 0))
    ikg = jnp.concatenate([jnp.zeros((D_IDX_DIM,), F32), ik_gain]).reshape(1, -1)
    consts = [q_gain.reshape(1, -1), k_gain.reshape(1, -1), ikg]
    widths = [D_HEADS * D_DIM, D_KV * D_DIM, D_KV * D_DIM, D_HEADS * D_IDX_DIM, LANES, LANES, LANES]
    dtypes = [BF16] * 6 + [F32]
    return pl.pallas_call(
        _dsa_prep_kernel,
        grid=(m // tm,),
        in_specs=[blk(D_HEADS * D_DIM, CD_DQ), blk(D_KV * D_DIM, CD_DK), blk(D_KV * D_DIM, CD_DV),
                  blk(D_HEADS * D_IDX_DIM, CD_IQ), blk(LANES, CD_PEIK), blk(LANES, CD_IW)]
        + [full(a) for a in consts],
        out_specs=[pl.BlockSpec((tm, w), lambda i: (i, 0)) for w in widths],
        out_shape=[jax.ShapeDtypeStruct((m, w), dt) for w, dt in zip(widths, dtypes)],
        compiler_params=_cparams(("parallel",)),
        name="dsa_prep",
    )(u, u, u, u, u, u, *consts)


DSA_CK = 512


def _sort_key(score):
    bits = lax.bitcast_convert_type(score + 0.0, I32)
    return bits ^ ((bits >> 31) & 0x7FFFFFFF)


def _dsa_kernel(q_ref, iq_ref, iw_ref, k_ref, v_ref, ika_ref, ikb_ref, nbias_ref, o_ref,
                keys_ref, hi_ref, lo_ref, m_ref, l_ref, acc_ref, qs_ref, iqs_ref, *, topk):
    qb = pl.program_id(1)
    q0 = qb * Q_BLOCK
    far_end = jnp.maximum(q0 - Q_BLOCK, 0)
    nfar = (far_end + DSA_CK - 1) // DSA_CK
    near0 = pl.multiple_of(far_end, Q_BLOCK)
    nt = (((1,), (1,)), ((), ()))
    npair = D_HEADS // 2
    rep = D_HEADS // D_KV

    for p in range(npair):
        iqs_ref[Q_BLOCK * p:Q_BLOCK * (p + 1), :] = iq_ref[:, LANES * p:LANES * (p + 1)]
    for hd in range(D_HEADS):
        qs_ref[hd // rep, Q_BLOCK * (hd % rep):Q_BLOCK * (hd % rep + 1), :] = q_ref[:, D_DIM * hd:D_DIM * (hd + 1)]

    iw = iw_ref[...]
    iw_cols = [iw[:, hd:hd + 1] for hd in range(D_HEADS)]

    def index_scores(rows):
        iqs = iqs_ref[...]
        score = None
        for parity, k_ref_ in ((0, ika_ref), (1, ikb_ref)):
            d = lax.dot_general(iqs, k_ref_[rows, :], nt, preferred_element_type=F32) * D_IDX_DIM ** -0.5
            for p in range(npair):
                term = iw_cols[2 * p + parity] * jnp.maximum(d[Q_BLOCK * p:Q_BLOCK * (p + 1)], 0.0)
                score = term if score is None else score + term
        return score

    def store_keys(c, key):
        keys_ref[c] = key
        hi_ref[c] = (key >> 16).astype(I16)
        lo_ref[c] = ((key & 0xFFFF) - HALF16).astype(I16)

    def far_scores(c, carry):
        rows = pl.ds(pl.multiple_of(c * DSA_CK, DSA_CK), DSA_CK)
        key = _sort_key(index_scores(rows))
        pos = c * DSA_CK + lax.broadcasted_iota(I32, key.shape, 1)
        store_keys(c, jnp.where(pos < far_end, key, INT_MIN))
        return carry

    lax.fori_loop(0, nfar, far_scores, 0)
    near_rows = pl.ds(near0, 2 * Q_BLOCK)
    keyn = _sort_key(index_scores(near_rows))
    posn = near0 + lax.broadcasted_iota(I32, keyn.shape, 1)
    qpos = q0 + lax.broadcasted_iota(I32, keyn.shape, 0)
    keyn = jnp.where(posn <= qpos, keyn, INT_MIN)
    store_keys(nfar, jnp.concatenate([keyn, jnp.full((Q_BLOCK, DSA_CK - 2 * Q_BLOCK), INT_MIN, I32)], axis=1))

    def count16(ref, cand, strict):
        cand16 = cand.astype(I16)

        def body(c, acc):
            x = ref[c]
            hit = jnp.where((x > cand16) if strict else (x >= cand16), jnp.int16(1), jnp.int16(0))
            part = hit[:, 0:LANES]
            for s in range(1, DSA_CK // LANES):
                part = part + hit[:, LANES * s:LANES * (s + 1)]
            return acc + part
        acc = lax.fori_loop(0, nfar + 1, body, jnp.zeros((Q_BLOCK, LANES), I16))
        return jnp.sum(acc.astype(F32), axis=-1, keepdims=True)

    def search16(ref, want):
        t = jnp.where(count16(ref, jnp.zeros((Q_BLOCK, 1), I32), False) >= want, 0, -HALF16).astype(I32)

        def bit(i, t):
            cand = t | (1 << (14 - i))
            return jnp.where(count16(ref, cand, False) >= want, cand, t)
        return lax.fori_loop(0, 15, bit, t)

    kf = float(topk)
    thr_hi = search16(hi_ref, kf)
    above = count16(hi_ref, thr_hi, True)
    thr_hi16 = thr_hi.astype(I16)

    def keep_candidates(c, carry):
        lo_ref[c] = jnp.where(hi_ref[c] == thr_hi16, lo_ref[c], jnp.int16(-HALF16))
        return carry

    lax.fori_loop(0, nfar + 1, keep_candidates, 0)
    thr_lo = search16(lo_ref, kf - above)
    thr = (thr_hi << 16) | ((thr_lo + HALF16) & 0xFFFF)
    need = kf - (above + count16(lo_ref, thr_lo, True))

    ti = lax.broadcasted_iota(I32, (DSA_CK, DSA_CK), 0)
    tj = lax.broadcasted_iota(I32, (DSA_CK, DSA_CK), 1)
    upto = (ti <= tj).astype(BF16)

    def mask_bias(kk, ties_before):
        nk = kk.shape[-1]
        tie = kk == thr
        rank = ties_before + jnp.dot(tie.astype(BF16), upto[0:nk, 0:nk], preferred_element_type=F32)
        sel = ((kk > thr) | (tie & (rank <= need))) & (kk != INT_MIN)
        return jnp.where(sel, 0.0, NEG_BIG), rank[:, nk - 1:nk]

    m_ref[...] = jnp.full_like(m_ref, NEG_BIG)
    l_ref[...] = jnp.zeros_like(l_ref)
    acc_ref[...] = jnp.zeros_like(acc_ref)

    def attend(rows, bias_of):
        nk = bias_of(0).shape[-1]
        for g in range(D_KV):
            gs = slice(D_DIM * g, D_DIM * (g + 1))
            s = lax.dot_general(qs_ref[g], k_ref[rows, gs], nt, preferred_element_type=F32)
            s = (s.reshape(rep, Q_BLOCK, nk) + bias_of(g)).reshape(rep * Q_BLOCK, nk)
            p = _online_softmax_step(s, m_ref.at[g], l_ref.at[g], acc_ref.at[g])
            acc_ref[g] += jnp.dot(p, v_ref[rows, gs], preferred_element_type=F32)

    def far_attend(c, ties_before):
        rows = pl.ds(pl.multiple_of(c * DSA_CK, DSA_CK), DSA_CK)
        mb, ties = mask_bias(keys_ref[c], ties_before)
        attend(rows, lambda g: mb[None])
        return ties

    ties = lax.fori_loop(0, nfar, far_attend, jnp.zeros((Q_BLOCK, 1), F32))
    mbn, _ = mask_bias(keys_ref[nfar][:, 0:2 * Q_BLOCK], ties)
    attend(near_rows, lambda g: mbn[None] + nbias_ref[0, rep * g:rep * (g + 1)])
    for hd in range(D_HEADS):
        rs = slice(Q_BLOCK * (hd % rep), Q_BLOCK * (hd % rep + 1))
        o_ref[:, D_DIM * hd:D_DIM * (hd + 1)] = acc_ref[hd // rep, rs, :] / l_ref[hd // rep, rs, :]


def _dsa(q, k, v, iq, ika, ikb, iw, near_bias, batch):
    m = q.shape[0]
    seq = m // batch
    nqb = seq // Q_BLOCK
    topk = min(D_TOPK_MAX, seq // 4)
    nck = max(seq // DSA_CK, 1)
    grp = D_HEADS // D_KV
    qspec = lambda w: pl.BlockSpec((Q_BLOCK, w), lambda b, i: (b * nqb + i, 0))
    kspec = lambda w: pl.BlockSpec((seq, w), lambda b, i: (b, 0))
    return pl.pallas_call(
        functools.partial(_dsa_kernel, topk=topk),
        grid=(batch, nqb),
        in_specs=[qspec(D_HEADS * D_DIM), qspec(D_HEADS * D_IDX_DIM), qspec(LANES),
                  kspec(D_KV * D_DIM), kspec(D_KV * D_DIM), kspec(LANES), kspec(LANES),
                  pl.BlockSpec((1, D_HEADS, Q_BLOCK, 2 * Q_BLOCK), lambda b, i: (jnp.minimum(i, 1), 0, 0, 0))],
        out_specs=qspec(D_HEADS * D_DIM),
        out_shape=jax.ShapeDtypeStruct((m, D_HEADS * D_DIM), F32),
        scratch_shapes=[pltpu.VMEM((nck + 1, Q_BLOCK, DSA_CK), I32),
                        pltpu.VMEM((nck + 1, Q_BLOCK, DSA_CK), I16), pltpu.VMEM((nck + 1, Q_BLOCK, DSA_CK), I16),
                        pltpu.VMEM((D_KV, grp * Q_BLOCK, LANES), F32), pltpu.VMEM((D_KV, grp * Q_BLOCK, LANES), F32),
                        pltpu.VMEM((D_KV, grp * Q_BLOCK, D_DIM), F32),
                        pltpu.VMEM((D_KV, grp * Q_BLOCK, D_DIM), BF16),
                        pltpu.VMEM((D_HEADS // 2 * Q_BLOCK, LANES), BF16)],
        compiler_params=_cparams(("parallel", "arbitrary")),
        name="dsa",
    )(q, iq, iw, k, v, ika, ikb, near_bias)


def _pad_cols(w, width):
    return jnp.pad(w, ((0, 0), (0, width - w.shape[1])))


def _pack_ab_in(w):
    a_cols = 3 * A_WIDTH + sum(A_LORA)
    wa, wb = w[:, :a_cols], w[:, a_cols:]
    rkv, lora = wa[:, :3 * A_WIDTH], wa[:, 3 * A_WIDTH:]
    z, xbc, dt = wb[:, :B_WIDTH], wb[:, B_WIDTH:B_WIDTH + B_CONV_CH], wb[:, B_WIDTH + B_CONV_CH:]
    return jnp.concatenate([rkv, z, xbc, _pad_cols(lora, A_LORA_PAD), _pad_cols(dt, LANES)], axis=1).astype(BF16)


def _pack_cd_in(w):
    c_cols = 2 * C_LORA + C_ROPE
    wc, wd = w[:, :c_cols], w[:, c_cols:]
    q_lat, kv_lat, k_pe = wc[:, :C_LORA], wc[:, C_LORA:2 * C_LORA], wc[:, 2 * C_LORA:]
    sizes = [D_HEADS * D_DIM, D_KV * D_DIM, D_KV * D_DIM, D_HEADS * D_IDX_DIM, D_IDX_DIM, D_HEADS]
    cuts = np.cumsum(sizes)[:-1]
    dq, dk, dv, iq, ik, iw = jnp.split(wd, [int(c) for c in cuts], axis=1)
    packed = jnp.concatenate([q_lat, kv_lat, dq, dk, dv, iq, k_pe, ik, iw], axis=1)
    return _pad_cols(packed, CD_COLS_PAD).astype(BF16)


def _pack_lora(w2, a2, g2):
    out, off = [], 0
    for w in (w2, a2, g2):
        out.append(jnp.pad(w, ((off, A_LORA_PAD - off - w.shape[0]), (0, 0))).astype(BF16))
        off += w.shape[0]
    return out


def _pack_mla_q(wq_b):
    w = wq_b.reshape(C_LORA, C_HEADS, C_QK)
    return jnp.concatenate([w[:, :, :C_NOPE].reshape(C_LORA, -1), w[:, :, C_NOPE:].reshape(C_LORA, -1)],
                           axis=1).astype(BF16)


def _pack_mla_kv(wkv_b):
    w = wkv_b.reshape(C_LORA, C_HEADS, C_NOPE + C_V)
    return jnp.concatenate([w[:, :, :C_NOPE].reshape(C_LORA, -1), w[:, :, C_NOPE:].reshape(C_LORA, -1)],
                           axis=1).astype(BF16)


def _layer0_mix(h, batch, norm_g, ab_w_in, ab_w_out, a_shift_mu, a_w0, a_w2, a_a0, a_a2, a_g2, a_k_k, a_k_a,
                a_r_k, a_ln_g, a_ln_b, b_conv_w, b_conv_b, b_dt_bias, b_a_log, b_d, b_norm_g):
    seq = h.shape[0] // batch
    u = _matmul([(h, 0, D_MODEL)], [_pack_ab_in(ab_w_in)], gain=norm_g)
    row = lambda t: t.reshape(1, -1)
    w2p, a2p, g2p = _pack_lora(a_w2, a_a2, a_g2)
    mu_rkv = row(a_shift_mu[:3 * A_WIDTH])
    mu_lora = _pad_cols(row(a_shift_mu[3 * A_WIDTH:]), A_LORA_PAD)
    r, w, k, v, na, nb, g = _rwkv_pre(u, seq, mu_rkv, mu_lora, row(a_w0), row(a_a0), row(a_k_k), row(a_k_a),
                                      w2p, a2p, g2p)
    y = _rwkv_scan(r, w, k, v, na, nb, batch)
    ya = _rwkv_post(y, r, k, v, g, row(a_ln_g), row(a_ln_b), row(a_r_k))
    yb = _ssd(u, batch, b_conv_w, b_conv_b, b_dt_bias, b_a_log, b_d, b_norm_g)
    w_out = ab_w_out.astype(BF16)
    return _matmul([(ya, 0, A_WIDTH), (yb, 0, B_WIDTH)], [w_out[:A_WIDTH], w_out[A_WIDTH:]], res=h)


def _layer1_mix(h, batch, norm_g, rel_bias, cd_w_in, cd_w_out, c_q_norm, c_wq_b, c_kv_norm, c_wkv_b,
                c_q_gain, c_k_gain, d_q_gain, d_k_gain, d_ik_gain):
    seq = h.shape[0] // batch
    u = _matmul([(h, 0, D_MODEL)], [_pack_cd_in(cd_w_in)], gain=norm_g)
    qn, qr, kn, kr, v = _mla_prep(u, seq, c_q_norm, _pack_mla_q(c_wq_b), c_kv_norm, _pack_mla_kv(c_wkv_b),
                                  c_q_gain, c_k_gain)
    yc = _mla_attn(qn, qr, kn, kr, v, batch)
    dq, dk, dv, iq, ika, ikb, iw = _dsa_prep(u, d_q_gain, d_k_gain, d_ik_gain)
    yd = _dsa(dq, dk, dv, iq, ika, ikb, iw, _near_bias(rel_bias), batch)
    w_out = cd_w_out.astype(BF16)
    half = C_HEADS * C_V
    return _matmul([(yc, 0, half), (yd, 0, D_HEADS * D_DIM)], [w_out[:half], w_out[half:]], res=h)


def _memory_attention(h, mem2, batch, gq, gkv, wq, wk, wv, wo, q_gain, k_gain):
    wkv = jnp.concatenate([wk, wv], axis=1).astype(BF16)
    mem_kv = _matmul([(mem2, 0, D_MODEL)], [wkv], gain=gkv)
    return _xattn(h, mem_kv, batch, gq, wq.astype(BF16), q_gain, k_gain, wo.astype(BF16))


def kernel(x, mem, rel_bias, norm_mix, norm_mem_q, norm_mem_kv, norm_ffn, xa_wq, xa_wk, xa_wv, xa_wo, xa_q_gain, xa_k_gain, ab_w_in, ab_w_out, a_shift_mu, a_w0, a_w2, a_a0, a_a2, a_g2, a_k_k, a_k_a, a_r_k, a_ln_g, a_ln_b, b_conv_w, b_conv_b, b_dt_bias, b_a_log, b_d, b_norm_g, ffn_w_gate, ffn_w_up, ffn_w_down, cd_w_in, cd_w_out, c_q_norm, c_wq_b, c_kv_norm, c_wkv_b, c_q_gain, c_k_gain, d_q_gain, d_k_gain, d_ik_gain, moe_router, moe_w_gate, moe_w_up, moe_w_down):
    batch, seq, d = x.shape
    h = x.reshape(batch * seq, d)
    mem2 = mem.reshape(-1, d)
    depth = norm_mix.shape[0]
    for layer in range(depth):
        i = layer // 2
        if layer % 2 == 0:
            h = _layer0_mix(h, batch, norm_mix[layer], ab_w_in[i], ab_w_out[i], a_shift_mu[i], a_w0[i], a_w2[i],
                            a_a0[i], a_a2[i], a_g2[i], a_k_k[i], a_k_a[i], a_r_k[i], a_ln_g[i], a_ln_b[i],
                            b_conv_w[i], b_conv_b[i], b_dt_bias[i], b_a_log[i], b_d[i], b_norm_g[i])
        else:
            h = _layer1_mix(h, batch, norm_mix[layer], rel_bias, cd_w_in[i], cd_w_out[i], c_q_norm[i], c_wq_b[i],
                            c_kv_norm[i], c_wkv_b[i], c_q_gain[i], c_k_gain[i], d_q_gain[i], d_k_gain[i],
                            d_ik_gain[i])
        h = _memory_attention(h, mem2, batch, norm_mem_q[layer], norm_mem_kv[layer], xa_wq[layer], xa_wk[layer],
                              xa_wv[layer], xa_wo[layer], xa_q_gain[layer], xa_k_gain[layer])
        if layer % 2 == 0:
            h = _swiglu(h, norm_ffn[layer], ffn_w_gate[i].astype(BF16), ffn_w_up[i].astype(BF16),
                        ffn_w_down[i].astype(BF16))
        else:
            router_p = _pad_cols(moe_router[i], LANES).astype(BF16)
            h = _moe(h, norm_ffn[layer], router_p, moe_w_gate[i].astype(BF16), moe_w_up[i].astype(BF16),
                     moe_w_down[i].astype(BF16))
    return h.reshape(batch, seq, d)
```

```python
import functools
import math

import numpy as np
import jax
import jax.numpy as jnp
from jax import lax
from jax.experimental import pallas as pl
from jax.experimental.pallas import tpu as pltpu

F32 = jnp.float32
BF16 = jnp.bfloat16
I32 = jnp.int32
I16 = jnp.int16
HALF16 = 1 << 15
HIGHEST = lax.Precision.HIGHEST

V7X_VMEM_BYTES = 64 * 1024 * 1024
VMEM_LIMIT = V7X_VMEM_BYTES - 8 * 1024 * 1024
LANES = 128

NORM_EPS = 1e-6
D_MODEL = 2048
HEAD64 = 64

A_WIDTH = 1024
A_LORA = (64, 64, 160)
A_LORA_PAD = 384
A_LN_EPS = 1e-5 * (HEAD64 / 8) ** 2
B_WIDTH = 1024
B_HEADS = 16
B_GROUPS = 4
B_STATE = 128
B_CONV = 4
B_CHUNK = 128
B_CONV_CH = B_WIDTH + 2 * B_GROUPS * B_STATE
AB_R, AB_K, AB_V, AB_Z, AB_XBC, AB_LORA, AB_DT, AB_COLS_PAD = 0, 1024, 2048, 3072, 4096, 6144, 6528, 6656

C_HEADS = 8
C_NOPE = 128
C_ROPE = 64
C_QK = C_NOPE + C_ROPE
C_V = 128
C_LORA = 512
ROPE_THETA = 10000.0
D_HEADS = 8
D_KV = 2
D_DIM = 128
D_IDX_DIM = 64
D_TOPK_MAX = 256
Q_BLOCK = 128
REL_BUCKETS = 32
REL_MAX_DIST = 128
CD_QLAT, CD_KVLAT, CD_DQ, CD_DK, CD_DV, CD_IQ, CD_PEIK, CD_IW, CD_COLS_PAD = (
    0, 512, 1024, 2048, 2304, 2560, 3072, 3200, 3584)

X_HEADS = 4
X_DIM = 128
N_EXPERTS = 8

NEG_BIG = -1e30
INT_MIN = -2 ** 31


def _cparams(sem):
    return pltpu.CompilerParams(dimension_semantics=sem, vmem_limit_bytes=VMEM_LIMIT)


def _rms(x, g, eps=NORM_EPS):
    return x * lax.rsqrt(jnp.mean(x * x, axis=-1, keepdims=True) + eps) * g


def _softplus(x):
    return jnp.maximum(x, 0.0) + jnp.log(1.0 + jnp.exp(-jnp.abs(x)))


def _silu(x):
    return x * jax.nn.sigmoid(x)


LOG2E = math.log2(math.e)


def _online_softmax_step(s, m_ref, l_ref, acc_ref):
    cols = [s[:, LANES * c:LANES * (c + 1)] for c in range(s.shape[1] // LANES)]
    mx = cols[0]
    for c in cols[1:]:
        mx = jnp.maximum(mx, c)
    m_old = m_ref[...]
    m_new = jnp.maximum(m_old, jnp.max(mx, axis=-1, keepdims=True))
    alpha = jnp.exp2(m_old - m_new)
    ps = [jnp.exp2(c - m_new) for c in cols]
    rs = ps[0]
    for p in ps[1:]:
        rs = rs + p
    l_ref[...] = alpha * l_ref[...] + jnp.sum(rs, axis=-1, keepdims=True)
    acc_ref[...] = alpha * acc_ref[...]
    m_ref[...] = m_new
    return jnp.concatenate(ps, axis=1).astype(BF16)


def _half_sum_bcast(x):
    left = lax.broadcasted_iota(I32, x.shape, 1) < HEAD64
    s0 = jnp.sum(jnp.where(left, x, 0.0), axis=1, keepdims=True)
    s1 = jnp.sum(jnp.where(left, 0.0, x), axis=1, keepdims=True)
    return jnp.where(left, s0, s1)


def _mm_kernel(*refs, n_x, has_norm, has_res):
    x_refs = refs[:n_x]
    pos = n_x
    g_ref = refs[pos] if has_norm else None
    pos += int(has_norm)
    w_refs = refs[pos:pos + n_x]
    pos += n_x
    res_ref = refs[pos] if has_res else None
    pos += int(has_res)
    o_ref = refs[pos]
    xn_refs = refs[pos + 1:]

    @pl.when(pl.program_id(1) == 0)
    def _():
        for x_ref, xn_ref in zip(x_refs, xn_refs):
            x = x_ref[...].astype(F32)
            if has_norm:
                x = _rms(x, g_ref[...])
            xn_ref[...] = x.astype(BF16)

    acc = None
    for xn_ref, w_ref in zip(xn_refs, w_refs):
        d = jnp.dot(xn_ref[...], w_ref[...], preferred_element_type=F32)
        acc = d if acc is None else acc + d
    if has_res:
        acc = acc + res_ref[...]
    o_ref[...] = acc


def _matmul(xs, ws, *, gain=None, res=None, tm=1024, tn=512):
    m = xs[0][0].shape[0]
    n = ws[0].shape[1]
    tm = min(tm, m)
    tn = min(tn, n)
    assert m % tm == 0 and n % tn == 0
    in_specs, args, scratch = [], [], []
    for arr, cb, width in xs:
        in_specs.append(pl.BlockSpec((tm, width), lambda i, j, cb=cb: (i, cb)))
        args.append(arr)
        scratch.append(pltpu.VMEM((tm, width), BF16))
    if gain is not None:
        in_specs.append(pl.BlockSpec((1, gain.shape[-1]), lambda i, j: (0, 0)))
        args.append(gain.reshape(1, -1))
    for (arr, cb, width), w in zip(xs, ws):
        assert w.shape[0] == width
        in_specs.append(pl.BlockSpec((width, tn), lambda i, j: (0, j)))
        args.append(w)
    if res is not None:
        in_specs.append(pl.BlockSpec((tm, tn), lambda i, j: (i, j)))
        args.append(res)
    return pl.pallas_call(
        functools.partial(_mm_kernel, n_x=len(xs), has_norm=gain is not None, has_res=res is not None),
        grid=(m // tm, n // tn),
        in_specs=in_specs,
        out_specs=pl.BlockSpec((tm, tn), lambda i, j: (i, j)),
        out_shape=jax.ShapeDtypeStruct((m, n), F32),
        scratch_shapes=scratch,
        compiler_params=_cparams(("parallel", "arbitrary")),
        name="matmul",
    )(*args)


def _rwkv_pre_kernel(rkv_ref, lora_ref, rkvp_ref, lorap_ref, mu_rkv_ref, mu_lora_ref,
                     w0_ref, a0_ref, kk_ref, ka_ref, w2_ref, a2_ref, g2_ref,
                     r_o, w_o, k_o, v_o, na_o, nb_o, g_o, *, tiles_per_seq):
    first = (pl.program_id(0) % tiles_per_seq) == 0
    tm = rkv_ref.shape[0]

    def shift_mix(x, prev_rows, mu):
        prev_last = jnp.where(first, 0.0, prev_rows[7:8, :])
        xs = pltpu.roll(x, 1, axis=0)
        row = lax.broadcasted_iota(I32, x.shape, 0)
        xs = jnp.where(row == 0, prev_last, xs)
        return x + (xs - x) * mu

    lo = shift_mix(lora_ref[...], lorap_ref[...], mu_lora_ref[...])
    lane = lax.broadcasted_iota(I32, lo.shape, 1)
    act = jnp.where(lane < A_LORA[0], jnp.tanh(lo),
                    jnp.where(lane < A_LORA[0] + A_LORA[1], lo, jax.nn.sigmoid(lo))).astype(BF16)
    dw = jnp.dot(act, w2_ref[...], preferred_element_type=F32)
    da = jnp.dot(act, a2_ref[...], preferred_element_type=F32)
    g_o[...] = jnp.dot(act, g2_ref[...], preferred_element_type=F32)

    for p in range(A_WIDTH // LANES):
        sl = slice(LANES * p, LANES * (p + 1))

        def mixed(off):
            s2 = slice(off + LANES * p, off + LANES * (p + 1))
            return shift_mix(rkv_ref[:, s2], rkvp_ref[:, s2], mu_rkv_ref[:, s2])

        r_o[:, sl] = mixed(AB_R)
        v_o[:, sl] = mixed(AB_V)
        kx = mixed(AB_K)
        logw = -_softplus(-(w0_ref[:, sl] + dw[:, sl])) - 0.5
        w_o[:, sl] = jnp.exp(-jnp.exp(logw))
        a = jax.nn.sigmoid(a0_ref[:, sl] + da[:, sl])
        kk = kx * kk_ref[:, sl]
        kk = kk * lax.rsqrt(jnp.maximum(_half_sum_bcast(kk * kk), 1e-24))
        k_o[:, sl] = kx * (1.0 + (a - 1.0) * ka_ref[:, sl])
        na_o[:, sl] = -kk
        nb_o[:, sl] = kk * a


def _rwkv_pre(u, seq, mu_rkv, mu_lora, w0, a0, k_k, k_a, w2p, a2p, g2p, tm=256):
    m = u.shape[0]
    tm = min(tm, seq)
    row = lambda w: pl.BlockSpec((1, w), lambda i: (0, 0))
    full = lambda a: pl.BlockSpec(a.shape, lambda i: (0, 0))
    prev = lambda i: jnp.maximum(i * (tm // 8) - 1, 0)
    out = jax.ShapeDtypeStruct((m, A_WIDTH), F32)
    return pl.pallas_call(
        functools.partial(_rwkv_pre_kernel, tiles_per_seq=seq // tm),
        grid=(m // tm,),
        in_specs=[
            pl.BlockSpec((tm, 3 * A_WIDTH), lambda i: (i, 0)),
            pl.BlockSpec((tm, A_LORA_PAD), lambda i: (i, AB_LORA // A_LORA_PAD)),
            pl.BlockSpec((8, 3 * A_WIDTH), lambda i: (prev(i), 0)),
            pl.BlockSpec((8, A_LORA_PAD), lambda i: (prev(i), AB_LORA // A_LORA_PAD)),
            row(3 * A_WIDTH), row(A_LORA_PAD), row(A_WIDTH), row(A_WIDTH), row(A_WIDTH), row(A_WIDTH),
            full(w2p), full(a2p), full(g2p),
        ],
        out_specs=[pl.BlockSpec((tm, A_WIDTH), lambda i: (i, 0))] * 7,
        out_shape=[out] * 7,
        compiler_params=_cparams(("parallel",)),
        name="rwkv_pre",
    )(u, u, u, u, mu_rkv, mu_lora, w0, a0, k_k, k_a, w2p, a2p, g2p)


RWKV_CHUNK = 64
RWKV_PAIR_GROUP = 8


def _rwkv_scan_kernel(r_ref, w_ref, k_ref, v_ref, a_ref, b_ref, y_ref, s_ref, vt_ref, yt_ref):
    nbatch = r_ref.shape[0]
    npairs = s_ref.shape[0]
    per_batch = npairs // nbatch
    where = lambda c: (c // per_batch, slice(LANES * (c % per_batch), LANES * (c % per_batch + 1)))

    @pl.when(pl.program_id(0) == 0)
    def _():
        s_ref[...] = jnp.zeros_like(s_ref)

    lane = lax.broadcasted_iota(I32, (HEAD64, LANES), 1)
    left = lane < HEAD64
    lane64 = lane & (HEAD64 - 1)

    def pair_transpose(x):
        xt = jnp.concatenate([x, x], axis=0).T
        return jnp.where(left, xt[0:HEAD64], xt[HEAD64:2 * HEAD64])

    def two_terms(x):
        hi = x.astype(BF16)
        return jnp.concatenate([hi, (x - hi.astype(F32)).astype(BF16)], axis=1)

    for p in range(npairs):
        bi, sl = where(p)
        vt_ref[p] = pair_transpose(v_ref[bi, :, sl])
    yt_ref[...] = jnp.zeros_like(yt_ref)

    ri = lax.broadcasted_iota(I32, (2 * LANES, LANES), 0)
    ci = lax.broadcasted_iota(I32, (2 * LANES, LANES), 1)
    ones_blk = (((ri // HEAD64) & 1) == (ci // HEAD64)).astype(BF16)

    def half_sum_mxu(x, split):
        if not split:
            return jnp.dot(x.astype(BF16), ones_blk[0:LANES], preferred_element_type=F32)
        return jnp.dot(two_terms(x), ones_blk, preferred_element_type=F32)

    def step8(t8, carry):
        rows8 = pl.ds(pl.multiple_of(t8 * 8, 8), 8)
        unroll = (LANES - t8 * 8) & (LANES - 1)
        for g0 in range(0, npairs, RWKV_PAIR_GROUP):
            group = range(g0, min(g0 + RWKV_PAIR_GROUP, npairs))
            rows = {p: [ref[where(p)[0], rows8, where(p)[1]] for ref in (a_ref, w_ref, b_ref, k_ref, r_ref)]
                    for p in group}
            s = {p: s_ref[p] for p in group}
            vt8 = {p: pltpu.roll(vt_ref[p], unroll, axis=1) for p in group}
            for j in range(8):
                sel = lane64 == t8 * 8 + j
                for p in group:
                    a_row, w_row, b_row, k_row, r_row = (x8[j:j + 1, :] for x8 in rows[p])
                    sa = half_sum_mxu(s[p] * a_row, True)
                    vc = jnp.take_along_axis(vt8[p], jnp.where(left, j, HEAD64 + j), axis=1)
                    s[p] = s[p] * w_row + sa * b_row + vc * k_row
                    yt_ref[p] = jnp.where(sel, half_sum_mxu(s[p] * r_row, False), yt_ref[p])
            for p in group:
                s_ref[p] = s[p]
        return carry

    lax.fori_loop(0, RWKV_CHUNK // 8, step8, 0)

    for p in range(npairs):
        bi, sl = where(p)
        y_ref[bi, :, sl] = pair_transpose(yt_ref[p])


def _rwkv_scan(r, w, k, v, na, nb, batch):
    m = r.shape[0]
    seq = m // batch
    npairs = batch * (A_WIDTH // LANES)
    spec = pl.BlockSpec((batch, RWKV_CHUNK, A_WIDTH), lambda c: (0, c, 0))
    pair_scratch = pltpu.VMEM((npairs, HEAD64, LANES), F32)
    by_batch = lambda t: t.reshape(batch, seq, A_WIDTH)
    y = pl.pallas_call(
        _rwkv_scan_kernel,
        grid=(seq // RWKV_CHUNK,),
        in_specs=[spec] * 6,
        out_specs=spec,
        out_shape=jax.ShapeDtypeStruct((batch, seq, A_WIDTH), F32),
        scratch_shapes=[pair_scratch, pair_scratch, pair_scratch],
        compiler_params=_cparams(("arbitrary",)),
        name="rwkv_scan",
    )(*(by_batch(t) for t in (r, w, k, v, na, nb)))
    return y.reshape(m, A_WIDTH)


def _rwkv_post_kernel(y_ref, r_ref, k_ref, v_ref, g_ref, lng_ref, lnb_ref, rk_ref, o_ref):
    for p in range(A_WIDTH // LANES):
        sl = slice(LANES * p, LANES * (p + 1))
        y = y_ref[:, sl]
        mean = _half_sum_bcast(y) * (1.0 / HEAD64)
        d = y - mean
        var = _half_sum_bcast(d * d) * (1.0 / HEAD64)
        yn = d * lax.rsqrt(var + A_LN_EPS) * lng_ref[:, sl] + lnb_ref[:, sl]
        bonus = _half_sum_bcast(r_ref[:, sl] * k_ref[:, sl] * rk_ref[:, sl]) * v_ref[:, sl]
        o_ref[:, sl] = (yn + bonus) * g_ref[:, sl]


def _rwkv_post(y, r, k, v, g, ln_g, ln_b, r_k, tm=256):
    m = y.shape[0]
    tm = min(tm, m)
    spec = pl.BlockSpec((tm, A_WIDTH), lambda i: (i, 0))
    row = pl.BlockSpec((1, A_WIDTH), lambda i: (0, 0))
    return pl.pallas_call(
        _rwkv_post_kernel,
        grid=(m // tm,),
        in_specs=[spec] * 5 + [row] * 3,
        out_specs=spec,
        out_shape=jax.ShapeDtypeStruct((m, A_WIDTH), F32),
        compiler_params=_cparams(("parallel",)),
        name="rwkv_post",
    )(y, r, k, v, g, ln_g, ln_b, r_k)


def _ssd_kernel(z_ref, xbc_ref, dt_ref, cw_ref, cb_ref, dtb_ref, alog_ref, dskip_ref, ng_ref,
                expand_ref, o_ref, st_ref, tail_ref):
    lc = B_CHUNK

    @pl.when(pl.program_id(1) == 0)
    def _():
        st_ref[...] = jnp.zeros_like(st_ref)
        tail_ref[...] = jnp.zeros_like(tail_ref)

    x = xbc_ref[...]
    tail = tail_ref[...]
    row8 = lax.broadcasted_iota(I32, tail.shape, 0)
    conv = cb_ref[...] + cw_ref[B_CONV - 1:B_CONV, :] * x
    for j in range(1, B_CONV):
        xs = pltpu.roll(x, j, axis=0)
        top = jnp.where(row8 < j, pltpu.roll(tail, j, axis=0), xs[0:8])
        xs = jnp.concatenate([top, xs[8:]], axis=0)
        conv = conv + cw_ref[B_CONV - 1 - j:B_CONV - j, :] * xs
    tail_ref[...] = x[lc - 8:lc]
    act = _silu(conv)
    xs_in = act[:, 0:B_WIDTH]
    bm = act[:, B_WIDTH:B_WIDTH + B_GROUPS * B_STATE].astype(BF16)
    cm = act[:, B_WIDTH + B_GROUPS * B_STATE:].astype(BF16)

    dt = _softplus(dt_ref[...] + dtb_ref[...])
    a_neg = -jnp.exp(alog_ref[...])
    da = dt * a_neg
    ri = lax.broadcasted_iota(I32, (lc, lc), 0)
    ci = lax.broadcasted_iota(I32, (lc, lc), 1)
    causal = ci <= ri
    tri = causal.astype(F32)
    cum = jnp.dot(tri, da, precision=HIGHEST, preferred_element_type=F32)
    cum_t = jnp.dot(da.T, (ri <= ci).astype(F32), precision=HIGHEST,
                    preferred_element_type=F32)
    expand = expand_ref[...]
    widen = lambda t: jnp.dot(t, expand, precision=HIGHEST, preferred_element_type=F32)
    dt_full = widen(dt)
    ecum_full = widen(jnp.exp(cum))
    dte_full = widen(jnp.exp(cum[lc - 1:lc, :] - cum))
    xdt = xs_in * dt_full
    xdt_b = xdt.astype(BF16)
    xdte_b = (xdt * dte_full).astype(BF16)
    left = lax.broadcasted_iota(I32, (lc, LANES), 1) < HEAD64

    ys = []
    for g in range(B_GROUPS):
        gs = slice(B_STATE * g, B_STATE * (g + 1))
        cm_g = cm[:, gs]
        bm_g = bm[:, gs]
        cb = lax.dot_general(cm_g, bm_g, (((1,), (1,)), ((), ())), preferred_element_type=F32)
        bm_t = bm_g.T
        pairs_per_group = B_HEADS // B_GROUPS // 2
        for q in range(pairs_per_group):
            p = g * pairs_per_group + q
            sl = slice(LANES * p, LANES * (p + 1))
            yd = []
            for h in (2 * p, 2 * p + 1):
                seg = cum[:, h:h + 1] - cum_t[h:h + 1, :]
                dec = jnp.where(causal, jnp.exp(jnp.minimum(seg, 0.0)), 0.0)
                yd.append(jnp.dot((cb * dec).astype(BF16), xdt_b[:, sl], preferred_element_type=F32))
            y_diag = jnp.where(left, yd[0], yd[1])
            st = st_ref[p]
            y_off = jnp.dot(cm_g, st.astype(BF16), preferred_element_type=F32) * ecum_full[:, sl]
            ys.append(y_diag + y_off)
            st_ref[p] = st * ecum_full[lc - 1:lc, sl] + jnp.dot(bm_t, xdte_b[:, sl],
                                                                preferred_element_type=F32)
    y = jnp.concatenate(ys, axis=1)
    y = (y + dskip_ref[...] * xs_in) * _silu(z_ref[...])
    gw = B_WIDTH // B_GROUPS
    for g in range(B_GROUPS):
        gs = slice(gw * g, gw * (g + 1))
        yg = y[:, gs]
        o_ref[:, gs] = yg * lax.rsqrt(jnp.mean(yg * yg, axis=-1, keepdims=True) + NORM_EPS) * ng_ref[:, gs]


def _ssd(u, batch, conv_w, conv_b, dt_bias, a_log, d_skip, norm_g):
    m = u.shape[0]
    seq = m // batch
    nchunk = seq // B_CHUNK
    rows = lambda b, c: b * nchunk + c
    pad16 = lambda t: jnp.pad(t.reshape(1, -1), ((0, 0), (0, LANES - B_HEADS)))
    expand = (np.arange(LANES)[:, None] == (np.arange(B_WIDTH)[None, :] // HEAD64)).astype(np.float32)
    full = lambda a: pl.BlockSpec(a.shape, lambda b, c: (0,) * a.ndim)
    args = [conv_w, conv_b.reshape(1, -1), pad16(dt_bias), pad16(a_log),
            jnp.repeat(d_skip, HEAD64).reshape(1, -1), norm_g.reshape(1, -1), jnp.asarray(expand)]
    return pl.pallas_call(
        _ssd_kernel,
        grid=(batch, nchunk),
        in_specs=[
            pl.BlockSpec((B_CHUNK, B_WIDTH), lambda b, c: (rows(b, c), AB_Z // B_WIDTH)),
            pl.BlockSpec((B_CHUNK, B_CONV_CH), lambda b, c: (rows(b, c), AB_XBC // B_CONV_CH)),
            pl.BlockSpec((B_CHUNK, LANES), lambda b, c: (rows(b, c), AB_DT // LANES)),
        ] + [full(a) for a in args],
        out_specs=pl.BlockSpec((B_CHUNK, B_WIDTH), lambda b, c: (rows(b, c), 0)),
        out_shape=jax.ShapeDtypeStruct((m, B_WIDTH), F32),
        scratch_shapes=[pltpu.VMEM((B_HEADS // 2, B_STATE, LANES), F32),
                        pltpu.VMEM((8, B_CONV_CH), F32)],
        compiler_params=_cparams(("parallel", "arbitrary")),
        name="ssd",
    )(u, u, u, *args)


def _xattn_kernel(h_ref, gq_ref, wq_ref, kv_ref, qg_ref, kg_ref, wo_ref, o_ref):
    h = h_ref[...]
    hn = _rms(h, gq_ref[...]).astype(BF16)
    q = jnp.dot(hn, wq_ref[...], preferred_element_type=F32)
    outs = []
    for hd in range(X_HEADS):
        sl = slice(X_DIM * hd, X_DIM * (hd + 1))
        qh = _rms(q[:, sl], qg_ref[...]).astype(BF16)
        kh = _rms(kv_ref[:, sl], kg_ref[...]).astype(BF16)
        vh = kv_ref[:, X_HEADS * X_DIM + X_DIM * hd:X_HEADS * X_DIM + X_DIM * (hd + 1)].astype(BF16)
        s = lax.dot_general(qh, kh, (((1,), (1,)), ((), ())), preferred_element_type=F32) * X_DIM ** -0.5
        e = jnp.exp(s - jnp.max(s, axis=-1, keepdims=True))
        p = e / jnp.sum(e, axis=-1, keepdims=True)
        outs.append(jnp.dot(p.astype(BF16), vh, preferred_element_type=F32))
    o = jnp.concatenate(outs, axis=1).astype(BF16)
    o_ref[...] = h + jnp.dot(o, wo_ref[...], preferred_element_type=F32)


def _xattn(h, mem_kv, batch, gq, wq, q_gain, k_gain, wo, tm=512):
    m, d = h.shape
    seq = m // batch
    tm = min(tm, seq)
    mlen = mem_kv.shape[0] // batch
    nt = seq // tm
    full = lambda a: pl.BlockSpec(a.shape, lambda b, i: (0, 0))
    args = [gq.reshape(1, -1), wq, mem_kv, q_gain.reshape(1, -1), k_gain.reshape(1, -1), wo]
    specs = [full(a) for a in args]
    specs[2] = pl.BlockSpec((mlen, mem_kv.shape[1]), lambda b, i: (b, 0))
    return pl.pallas_call(
        _xattn_kernel,
        grid=(batch, nt),
        in_specs=[pl.BlockSpec((tm, d), lambda b, i: (b * nt + i, 0))] + specs,
        out_specs=pl.BlockSpec((tm, d), lambda b, i: (b * nt + i, 0)),
        out_shape=jax.ShapeDtypeStruct((m, d), F32),
        compiler_params=_cparams(("parallel", "parallel")),
        name="xattn",
    )(h, *args)


def _swiglu_kernel(h_ref, g_ref, wg_ref, wu_ref, wd_ref, o_ref, xn_ref, acc_ref):
    j = pl.program_id(1)

    @pl.when(j == 0)
    def _():
        xn_ref[...] = _rms(h_ref[...], g_ref[...]).astype(BF16)
        acc_ref[...] = jnp.zeros_like(acc_ref)

    xn = xn_ref[...]
    gate = jnp.dot(xn, wg_ref[...], preferred_element_type=F32)
    up = jnp.dot(xn, wu_ref[...], preferred_element_type=F32)
    acc_ref[...] += jnp.dot((_silu(gate) * up).astype(BF16), wd_ref[...], preferred_element_type=F32)

    @pl.when(j == pl.num_programs(1) - 1)
    def _():
        o_ref[...] = h_ref[...] + acc_ref[...]


def _swiglu(h, gain, wg, wu, wd, tm=512, tf=512):
    m, d = h.shape
    f = wg.shape[1]
    tm = min(tm, m)
    assert f % tf == 0
    return pl.pallas_call(
        _swiglu_kernel,
        grid=(m // tm, f // tf),
        in_specs=[
            pl.BlockSpec((tm, d), lambda i, j: (i, 0)),
            pl.BlockSpec((1, d), lambda i, j: (0, 0)),
            pl.BlockSpec((d, tf), lambda i, j: (0, j)),
            pl.BlockSpec((d, tf), lambda i, j: (0, j)),
            pl.BlockSpec((tf, d), lambda i, j: (j, 0)),
        ],
        out_specs=pl.BlockSpec((tm, d), lambda i, j: (i, 0)),
        out_shape=jax.ShapeDtypeStruct((m, d), F32),
        scratch_shapes=[pltpu.VMEM((tm, d), BF16), pltpu.VMEM((tm, d), F32)],
        compiler_params=_cparams(("parallel", "arbitrary")),
        name="swiglu",
    )(h, gain.reshape(1, -1), wg, wu, wd)


MOE_TOPK = 2
MOE_ROWS = 512
MOE_TOKENS = 256


def _moe_route_kernel(h_ref, g_ref, router_ref, eid_o, gw_o):
    xn = _rms(h_ref[...], g_ref[...]).astype(BF16)
    logits = jnp.dot(xn, router_ref[...], preferred_element_type=F32)
    lane = lax.broadcasted_iota(I32, logits.shape, 1)
    logits = jnp.where(lane < N_EXPERTS, logits, -jnp.inf)
    m1 = jnp.max(logits, axis=-1, keepdims=True)
    i1 = jnp.min(jnp.where(logits == m1, lane, LANES), axis=-1, keepdims=True)
    rest = jnp.where(lane == i1, -jnp.inf, logits)
    m2 = jnp.max(rest, axis=-1, keepdims=True)
    i2 = jnp.min(jnp.where(rest == m2, lane, LANES), axis=-1, keepdims=True)
    e2 = jnp.exp(m2 - m1)
    eid_o[...] = jnp.where(lane == 0, i1, jnp.where(lane == 1, i2, 0))
    gw_o[...] = jnp.where(lane == 0, 1.0 / (1.0 + e2), jnp.where(lane == 1, e2 / (1.0 + e2), 0.0))


def _moe_route(h, gain, router_p, tm=512):
    m, d = h.shape
    tm = min(tm, m)
    spec = pl.BlockSpec((tm, LANES), lambda i: (i, 0))
    return pl.pallas_call(
        _moe_route_kernel,
        grid=(m // tm,),
        in_specs=[pl.BlockSpec((tm, d), lambda i: (i, 0)), pl.BlockSpec((1, d), lambda i: (0, 0)),
                  pl.BlockSpec((d, LANES), lambda i: (0, 0))],
        out_specs=[spec, spec],
        out_shape=[jax.ShapeDtypeStruct((m, LANES), I32), jax.ShapeDtypeStruct((m, LANES), F32)],
        compiler_params=_cparams(("parallel",)),
        name="moe_route",
    )(h, gain.reshape(1, -1), router_p)


def _moe_plan(eid, nblocks):
    e = eid.reshape(-1)
    onehot = (e[:, None] == jnp.arange(N_EXPERTS, dtype=I32)[None, :]).astype(I32)
    csum = jnp.cumsum(onehot, axis=0)
    rank = jnp.sum(onehot * csum, axis=1) - 1
    counts = csum[-1]
    padded = (counts + MOE_ROWS - 1) // MOE_ROWS * MOE_ROWS
    gend = jnp.cumsum(padded)
    dest = jnp.sum(onehot * (gend - padded)[None, :], axis=1) + rank
    nb_used = gend[-1] // MOE_ROWS
    blk = jnp.arange(nblocks, dtype=I32)
    blk_e = jnp.minimum(jnp.sum((blk[:, None] * MOE_ROWS >= gend[None, :]).astype(I32), axis=1), N_EXPERTS - 1)
    blk_e = jnp.where(blk < nb_used, blk_e, blk_e[jnp.maximum(nb_used - 1, 0)])
    return dest.astype(I32), blk_e.astype(I32), nb_used.astype(I32).reshape(1)


def _row_copy(src_ref, src_row, dst_ref, dst_row, sem):
    return pltpu.make_async_copy(src_ref.at[pl.ds(src_row, 1), :], dst_ref.at[pl.ds(dst_row, 1), :], sem)


def _moe_dispatch_kernel(dest_ref, h_ref, xs_in_ref, xs_ref, sem):
    del xs_in_ref
    base = pl.program_id(0) * MOE_TOKENS

    def issue(r, carry):
        for c in range(MOE_TOPK):
            _row_copy(h_ref, r, xs_ref, dest_ref[MOE_TOPK * (base + r) + c], sem).start()
        return carry

    def drain(r, carry):
        for c in range(MOE_TOPK):
            _row_copy(h_ref, 0, xs_ref, 0, sem).wait()
        return carry

    lax.fori_loop(0, MOE_TOKENS, issue, 0)
    lax.fori_loop(0, MOE_TOKENS, drain, 0)


def _moe_dispatch(h, dest, rows):
    m, d = h.shape
    return pl.pallas_call(
        _moe_dispatch_kernel,
        grid_spec=pltpu.PrefetchScalarGridSpec(
            num_scalar_prefetch=1,
            grid=(m // MOE_TOKENS,),
            in_specs=[pl.BlockSpec((MOE_TOKENS, d), lambda i, dest: (i, 0)), pl.BlockSpec(memory_space=pl.ANY)],
            out_specs=pl.BlockSpec(memory_space=pl.ANY),
            scratch_shapes=[pltpu.SemaphoreType.DMA(())],
        ),
        out_shape=jax.ShapeDtypeStruct((rows, d), F32),
        input_output_aliases={2: 0},
        compiler_params=_cparams(("arbitrary",)),
        name="moe_dispatch",
    )(dest, h, jnp.zeros((rows, d), F32))


def _moe_ffn_kernel(be_ref, nb_ref, x_ref, g_ref, wg_ref, wu_ref, wd_ref, y_ref, xn_ref, acc_ref):
    del be_ref
    j = pl.program_id(1)
    live = pl.program_id(0) < nb_ref[0]
    last = j == pl.num_programs(1) - 1

    @pl.when(live & (j == 0))
    def _():
        xn_ref[...] = _rms(x_ref[...], g_ref[...]).astype(BF16)
        acc_ref[...] = jnp.zeros_like(acc_ref)

    @pl.when(live)
    def _():
        xn = xn_ref[...]
        gate = jnp.dot(xn, wg_ref[0], preferred_element_type=F32)
        up = jnp.dot(xn, wu_ref[0], preferred_element_type=F32)
        acc_ref[...] += jnp.dot((_silu(gate) * up).astype(BF16), wd_ref[0], preferred_element_type=F32)

    @pl.when(live & last)
    def _():
        y_ref[...] = acc_ref[...]

    @pl.when(jnp.logical_not(live) & last)
    def _():
        y_ref[...] = jnp.zeros_like(y_ref)


def _moe_ffn(xs, blk_e, nb_used, gain, wg, wu, wd, tf=256):
    rows, d = xs.shape
    f = wg.shape[2]
    nj = f // tf
    assert f % tf == 0 and rows % MOE_ROWS == 0
    jx = lambda b, j, nb: jnp.where(b < nb[0], j, nj - 1)
    return pl.pallas_call(
        _moe_ffn_kernel,
        grid_spec=pltpu.PrefetchScalarGridSpec(
            num_scalar_prefetch=2,
            grid=(rows // MOE_ROWS, nj),
            in_specs=[
                pl.BlockSpec((MOE_ROWS, d), lambda b, j, be, nb: (jnp.minimum(b, jnp.maximum(nb[0] - 1, 0)), 0)),
                pl.BlockSpec((1, d), lambda b, j, be, nb: (0, 0)),
                pl.BlockSpec((1, d, tf), lambda b, j, be, nb: (be[b], 0, jx(b, j, nb))),
                pl.BlockSpec((1, d, tf), lambda b, j, be, nb: (be[b], 0, jx(b, j, nb))),
                pl.BlockSpec((1, tf, d), lambda b, j, be, nb: (be[b], jx(b, j, nb), 0)),
            ],
            out_specs=pl.BlockSpec((MOE_ROWS, d), lambda b, j, be, nb: (b, 0)),
            scratch_shapes=[pltpu.VMEM((MOE_ROWS, d), BF16), pltpu.VMEM((MOE_ROWS, d), F32)],
        ),
        out_shape=jax.ShapeDtypeStruct((rows, d), F32),
        compiler_params=_cparams(("parallel", "arbitrary")),
        name="moe_ffn",
    )(blk_e, nb_used, xs, gain.reshape(1, -1), wg, wu, wd)


def _moe_combine_kernel(dest_ref, h_ref, gw_ref, ys_ref, o_ref, ybuf_ref, sem):
    base = pl.program_id(0) * MOE_TOKENS

    def issue(r, carry):
        for c in range(MOE_TOPK):
            _row_copy(ys_ref, dest_ref[MOE_TOPK * (base + r) + c], ybuf_ref.at[c], r, sem).start()
        return carry

    def drain(r, carry):
        for c in range(MOE_TOPK):
            _row_copy(ys_ref, 0, ybuf_ref.at[c], 0, sem).wait()
        return carry

    lax.fori_loop(0, MOE_TOKENS, issue, 0)
    lax.fori_loop(0, MOE_TOKENS, drain, 0)
    gw = gw_ref[...]
    o_ref[...] = h_ref[...] + (gw[:, 0:1] * ybuf_ref[0] + gw[:, 1:2] * ybuf_ref[1])


def _moe_combine(h, gw, ys, dest):
    m, d = h.shape
    tok = lambda w: pl.BlockSpec((MOE_TOKENS, w), lambda i, dest: (i, 0))
    return pl.pallas_call(
        _moe_combine_kernel,
        grid_spec=pltpu.PrefetchScalarGridSpec(
            num_scalar_prefetch=1,
            grid=(m // MOE_TOKENS,),
            in_specs=[tok(d), tok(LANES), pl.BlockSpec(memory_space=pl.ANY)],
            out_specs=tok(d),
            scratch_shapes=[pltpu.VMEM((MOE_TOPK, MOE_TOKENS, d), F32), pltpu.SemaphoreType.DMA(())],
        ),
        out_shape=jax.ShapeDtypeStruct((m, d), F32),
        compiler_params=_cparams(("arbitrary",)),
        name="moe_combine",
    )(dest, h, gw, ys)


def _moe(h, gain, router_p, wg, wu, wd):
    m = h.shape[0]
    nblocks = MOE_TOPK * m // MOE_ROWS + N_EXPERTS
    eid, gw = _moe_route(h, gain, router_p)
    dest, blk_e, nb_used = _moe_plan(eid[:, :MOE_TOPK], nblocks)
    xs = _moe_dispatch(h, dest, nblocks * MOE_ROWS)
    ys = _moe_ffn(xs, blk_e, nb_used, gain, wg, wu, wd)
    return _moe_combine(h, gw, ys, dest)


def _rope_pairs(x, cos, sin_signed):
    w = x.shape[1]
    lane = lax.broadcasted_iota(I32, x.shape, 1)
    partner = jnp.where((lane & 32) != 0, pltpu.roll(x, 32, axis=1), pltpu.roll(x, w - 32, axis=1))
    return x * cos + partner * sin_signed


def _mla_prep_kernel(ql_ref, kvl_ref, pe_ref, qn_ref, wq_ref, kvn_ref, wkv_ref, qgn_ref, qgr_ref,
                     kgn_ref, kgr_ref, cos_ref, sin_ref, qn_o, qr_o, kn_o, kr_o, v_o):
    nn = C_HEADS * C_NOPE
    left = lax.broadcasted_iota(I32, (ql_ref.shape[0], LANES), 1) < C_ROPE

    def head_norm(nope_of, rope_of, gn_ref, gr_ref, n_o, r_o, post_scale):
        for p in range(C_HEADS // 2):
            pr = slice(LANES * p, LANES * (p + 1))
            rope = rope_of(p)
            r2 = rope * rope
            rs = []
            for hh, ss_rope in ((0, jnp.sum(jnp.where(left, r2, 0.0), axis=-1, keepdims=True)),
                                (1, jnp.sum(jnp.where(left, 0.0, r2), axis=-1, keepdims=True))):
                hs = slice(C_NOPE * (2 * p + hh), C_NOPE * (2 * p + hh + 1))
                nope = nope_of(hs)
                ss = jnp.sum(nope * nope, axis=-1, keepdims=True) + ss_rope
                rs.append(lax.rsqrt(ss * (1.0 / C_QK) + NORM_EPS))
                n_o[:, hs] = (nope * rs[hh] * gn_ref[:, hs] * post_scale).astype(BF16)
            rope = rope * jnp.where(left, rs[0], rs[1]) * gr_ref[:, pr]
            r_o[:, pr] = (_rope_pairs(rope, cos_ref[:, pr], sin_ref[:, pr]) * post_scale).astype(BF16)

    q = jnp.dot(_rms(ql_ref[...], qn_ref[...]).astype(BF16), wq_ref[...], preferred_element_type=F32)
    head_norm(lambda hs: q[:, hs], lambda p: q[:, nn + LANES * p:nn + LANES * (p + 1)],
              qgn_ref, qgr_ref, qn_o, qr_o, C_QK ** -0.5 * LOG2E)

    kv = jnp.dot(_rms(kvl_ref[...], kvn_ref[...]).astype(BF16), wkv_ref[...], preferred_element_type=F32)
    v_o[...] = kv[:, nn:].astype(BF16)
    pe = pe_ref[...]
    pe_pair = jnp.where(left, pe, pltpu.roll(pe, C_ROPE, axis=1))
    head_norm(lambda hs: kv[:, hs], lambda p: pe_pair, kgn_ref, kgr_ref, kn_o, kr_o, 1.0)


def _mla_prep(u, seq, q_norm, wq_p, kv_norm, wkv_p, q_gain, k_gain, tm=256):
    m = u.shape[0]
    tm = min(tm, seq)
    nn, nr = C_HEADS * C_NOPE, C_HEADS * C_ROPE
    half = C_ROPE // 2
    freqs = ROPE_THETA ** (-jnp.arange(half, dtype=F32) / half)
    ang = jnp.arange(seq, dtype=F32)[:, None] * freqs[None, :]
    cos = jnp.tile(jnp.cos(ang), (1, 2 * C_HEADS))
    sin = jnp.tile(jnp.concatenate([-jnp.sin(ang), jnp.sin(ang)], axis=1), (1, C_HEADS))
    tile_gain = lambda g: jnp.tile(g, C_HEADS).reshape(1, -1)
    consts = [q_norm.reshape(1, -1), wq_p, kv_norm.reshape(1, -1), wkv_p,
              tile_gain(q_gain[:C_NOPE]), tile_gain(q_gain[C_NOPE:]),
              tile_gain(k_gain[:C_NOPE]), tile_gain(k_gain[C_NOPE:])]
    full = lambda a: pl.BlockSpec(a.shape, lambda i: (0, 0))
    nt = seq // tm
    tab = pl.BlockSpec((tm, nr), lambda i: (i % nt, 0))
    out = lambda w: jax.ShapeDtypeStruct((m, w), BF16)
    ospec = lambda w: pl.BlockSpec((tm, w), lambda i: (i, 0))
    return pl.pallas_call(
        _mla_prep_kernel,
        grid=(m // tm,),
        in_specs=[
            pl.BlockSpec((tm, C_LORA), lambda i: (i, CD_QLAT // C_LORA)),
            pl.BlockSpec((tm, C_LORA), lambda i: (i, CD_KVLAT // C_LORA)),
            pl.BlockSpec((tm, LANES), lambda i: (i, CD_PEIK // LANES)),
        ] + [full(a) for a in consts] + [tab, tab],
        out_specs=[ospec(nn), ospec(nr), ospec(nn), ospec(nr), ospec(nn)],
        out_shape=[out(nn), out(nr), out(nn), out(nr), out(nn)],
        compiler_params=_cparams(("parallel",)),
        name="mla_prep",
    )(u, u, u, *consts, cos, sin)


def _mla_attn_kernel(qn_ref, qr_ref, kn_ref, kr_ref, v_ref, o_ref, m_ref, l_ref, acc_ref, *, tq):
    qi = pl.program_id(2)
    lane = lax.broadcasted_iota(I32, qr_ref.shape, 1)
    qr = qr_ref[...]
    qs = []
    for hh in range(2):
        qr_h = jnp.where((lane // HEAD64) == hh, qr, jnp.zeros_like(qr))
        qs.append(jnp.concatenate([qn_ref[:, C_NOPE * hh:C_NOPE * (hh + 1)], qr_h], axis=1))
    m_ref[...] = jnp.full_like(m_ref, NEG_BIG)
    l_ref[...] = jnp.zeros_like(l_ref)
    acc_ref[...] = jnp.zeros_like(acc_ref)
    nt = (((1,), (1,)), ((), ()))

    def update(j, masked):
        rows = pl.ds(pl.multiple_of(j * tq, tq), tq)
        kr = kr_ref[rows, :]
        ss = []
        for hh in range(2):
            kc = jnp.concatenate([kn_ref[rows, C_NOPE * hh:C_NOPE * (hh + 1)], kr], axis=1)
            ss.append(lax.dot_general(qs[hh], kc, nt, preferred_element_type=F32))
        for hh in range(2):
            s = ss[hh]
            if masked:
                ri = lax.broadcasted_iota(I32, s.shape, 0)
                ci = lax.broadcasted_iota(I32, s.shape, 1)
                s = jnp.where(ci <= ri, s, NEG_BIG)
            p = _online_softmax_step(s, m_ref.at[hh], l_ref.at[hh], acc_ref.at[hh])
            acc_ref[hh] += jnp.dot(p, v_ref[rows, C_V * hh:C_V * (hh + 1)], preferred_element_type=F32)

    def body(j, carry):
        update(j, False)
        return carry

    lax.fori_loop(0, qi, body, 0)
    update(qi, True)
    for hh in range(2):
        o_ref[:, C_V * hh:C_V * (hh + 1)] = acc_ref[hh] / l_ref[hh]


def _mla_attn(qn, qr, kn, kr, v, batch, tq=512):
    m = qn.shape[0]
    seq = m // batch
    tq = min(tq, seq)
    nq = seq // tq
    qspec = lambda w: pl.BlockSpec((tq, w), lambda b, p, i: (b * nq + i, p))
    kspec = lambda w: pl.BlockSpec((seq, w), lambda b, p, i: (b, p))
    return pl.pallas_call(
        functools.partial(_mla_attn_kernel, tq=tq),
        grid=(batch, C_HEADS // 2, nq),
        in_specs=[qspec(2 * C_NOPE), qspec(2 * C_ROPE), kspec(2 * C_NOPE), kspec(2 * C_ROPE), kspec(2 * C_V)],
        out_specs=qspec(2 * C_V),
        out_shape=jax.ShapeDtypeStruct((m, C_HEADS * C_V), F32),
        scratch_shapes=[pltpu.VMEM((2, tq, LANES), F32), pltpu.VMEM((2, tq, LANES), F32),
                        pltpu.VMEM((2, tq, C_V), F32)],
        compiler_params=_cparams(("parallel", "parallel", "arbitrary")),
        name="mla_attn",
    )(qn, qr, kn, kr, v)


def _rel_bucket_table(n):
    d = np.arange(n)
    max_exact = REL_BUCKETS // 2
    nf = np.maximum(d, 1).astype(np.float32)
    large = max_exact + (np.log(nf / np.float32(max_exact)) / np.float32(math.log(REL_MAX_DIST / max_exact))
                         * np.float32(REL_BUCKETS - max_exact)).astype(np.int32)
    large = np.minimum(large, REL_BUCKETS - 1)
    return np.where(d < max_exact, d, large).astype(np.int32)


def _near_bias_kernel(bucket_ref, rel_ref, o_ref):
    for var in range(2):
        bk = bucket_ref[var]
        for hd in range(D_HEADS):
            acc = jnp.zeros(bk.shape, F32)
            for b in range(REL_BUCKETS):
                acc = jnp.where(bk == b, rel_ref[b, hd], acc)
            o_ref[var, hd] = (acc - rel_ref[REL_BUCKETS - 1, hd]) * LOG2E


def _near_bias(rel_bias):
    table = _rel_bucket_table(2 * Q_BLOCK)
    q = np.arange(Q_BLOCK)[:, None]
    j = np.arange(2 * Q_BLOCK)[None, :]
    dist0 = np.maximum(q - j, 0)
    dist1 = np.maximum(q - j + Q_BLOCK, 0)
    buckets = np.stack([table[dist0], table[dist1]]).astype(np.int32)
    return pl.pallas_call(
        _near_bias_kernel,
        in_specs=[pl.BlockSpec(memory_space=pltpu.VMEM), pl.BlockSpec(memory_space=pltpu.SMEM)],
        out_specs=pl.BlockSpec(memory_space=pltpu.VMEM),
        out_shape=jax.ShapeDtypeStruct((2, D_HEADS, Q_BLOCK, 2 * Q_BLOCK), F32),
        name="near_bias",
    )(jnp.asarray(buckets), rel_bias)


def _dsa_prep_kernel(dq_ref, dk_ref, dv_ref, iq_ref, peik_ref, iw_ref, qg_ref, kg_ref, ikg_ref,
                     q_o, k_o, v_o, iq_o, ika_o, ikb_o, iw_o):
    for hd in range(D_HEADS):
        sl = slice(D_DIM * hd, D_DIM * (hd + 1))
        q_o[:, sl] = (_rms(dq_ref[:, sl], qg_ref[...]) * (D_DIM ** -0.5 * LOG2E)).astype(BF16)
    for hd in range(D_KV):
        sl = slice(D_DIM * hd, D_DIM * (hd + 1))
        k_o[:, sl] = _rms(dk_ref[:, sl], kg_ref[...]).astype(BF16)
    v_o[...] = dv_ref[...].astype(BF16)
    iq_o[...] = iq_ref[...].astype(BF16)
    x = peik_ref[...]
    right = lax.broadcasted_iota(I32, x.shape, 1) >= D_IDX_DIM
    x = jnp.where(right, x, 0.0)
    ms = jnp.sum(x * x, axis=-1, keepdims=True) * (1.0 / D_IDX_DIM)
    ik = (x * lax.rsqrt(ms + NORM_EPS) * ikg_ref[...]).astype(BF16)
    ikb_o[...] = ik
    ika_o[...] = pltpu.roll(ik.astype(F32), D_IDX_DIM, axis=1).astype(BF16)
    iw_o[...] = iw_ref[...] * D_HEADS ** -0.5


def _dsa_prep(u, q_gain, k_gain, ik_gain, tm=256):
    m = u.shape[0]
    tm = min(tm, m)
    blk = lambda w, off: pl.BlockSpec((tm, w), lambda i: (i, off // w))
    full = lambda a: pl.BlockSpec(a.shape, lambda i: (0, 0))
    ikg = jnp.concatenate([jnp.zeros((D_IDX_DIM,), F32), ik_gain]).reshape(1, -1)
    consts = [q_gain.reshape(1, -1), k_gain.reshape(1, -1), ikg]
    widths = [D_HEADS * D_DIM, D_KV * D_DIM, D_KV * D_DIM, D_HEADS * D_IDX_DIM, LANES, LANES, LANES]
    dtypes = [BF16] * 6 + [F32]
    return pl.pallas_call(
        _dsa_prep_kernel,
        grid=(m // tm,),
        in_specs=[blk(D_HEADS * D_DIM, CD_DQ), blk(D_KV * D_DIM, CD_DK), blk(D_KV * D_DIM, CD_DV),
                  blk(D_HEADS * D_IDX_DIM, CD_IQ), blk(LANES, CD_PEIK), blk(LANES, CD_IW)]
        + [full(a) for a in consts],
        out_specs=[pl.BlockSpec((tm, w), lambda i: (i, 0)) for w in widths],
        out_shape=[jax.ShapeDtypeStruct((m, w), dt) for w, dt in zip(widths, dtypes)],
        compiler_params=_cparams(("parallel",)),
        name="dsa_prep",
    )(u, u, u, u, u, u, *consts)


DSA_CK = 512


def _sort_key(score):
    bits = lax.bitcast_convert_type(score + 0.0, I32)
    return bits ^ ((bits >> 31) & 0x7FFFFFFF)


def _dsa_kernel(q_ref, iq_ref, iw_ref, k_ref, v_ref, ika_ref, ikb_ref, nbias_ref, o_ref,
                keys_ref, hi_ref, lo_ref, m_ref, l_ref, acc_ref, qs_ref, iqs_ref, *, topk):
    qb = pl.program_id(1)
    q0 = qb * Q_BLOCK
    far_end = jnp.maximum(q0 - Q_BLOCK, 0)
    nfar = (far_end + DSA_CK - 1) // DSA_CK
    near0 = pl.multiple_of(far_end, Q_BLOCK)
    nt = (((1,), (1,)), ((), ()))
    npair = D_HEADS // 2
    rep = D_HEADS // D_KV

    for p in range(npair):
        iqs_ref[Q_BLOCK * p:Q_BLOCK * (p + 1), :] = iq_ref[:, LANES * p:LANES * (p + 1)]
    for hd in range(D_HEADS):
        qs_ref[hd // rep, Q_BLOCK * (hd % rep):Q_BLOCK * (hd % rep + 1), :] = q_ref[:, D_DIM * hd:D_DIM * (hd + 1)]

    iw = iw_ref[...]
    iw_cols = [iw[:, hd:hd + 1] for hd in range(D_HEADS)]

    def index_scores(rows):
        iqs = iqs_ref[...]
        score = None
        for parity, k_ref_ in ((0, ika_ref), (1, ikb_ref)):
            d = lax.dot_general(iqs, k_ref_[rows, :], nt, preferred_element_type=F32) * D_IDX_DIM ** -0.5
            for p in range(npair):
                term = iw_cols[2 * p + parity] * jnp.maximum(d[Q_BLOCK * p:Q_BLOCK * (p + 1)], 0.0)
                score = term if score is None else score + term
        return score

    def store_keys(c, key):
        keys_ref[c] = key
        hi_ref[c] = (key >> 16).astype(I16)
        lo_ref[c] = ((key & 0xFFFF) - HALF16).astype(I16)

    def far_scores(c, carry):
        rows = pl.ds(pl.multiple_of(c * DSA_CK, DSA_CK), DSA_CK)
        key = _sort_key(index_scores(rows))
        pos = c * DSA_CK + lax.broadcasted_iota(I32, key.shape, 1)
        store_keys(c, jnp.where(pos < far_end, key, INT_MIN))
        return carry

    lax.fori_loop(0, nfar, far_scores, 0)
    near_rows = pl.ds(near0, 2 * Q_BLOCK)
    keyn = _sort_key(index_scores(near_rows))
    posn = near0 + lax.broadcasted_iota(I32, keyn.shape, 1)
    qpos = q0 + lax.broadcasted_iota(I32, keyn.shape, 0)
    keyn = jnp.where(posn <= qpos, keyn, INT_MIN)
    store_keys(nfar, jnp.concatenate([keyn, jnp.full((Q_BLOCK, DSA_CK - 2 * Q_BLOCK), INT_MIN, I32)], axis=1))
    lowest = jnp.full((Q_BLOCK, DSA_CK), -HALF16, I16)
    hi_ref[nfar + 1] = lowest
    lo_ref[nfar + 1] = lowest
    npair_chunks = (nfar + 2) // 2

    def count16(ref, cand, strict):
        cand16 = cand.astype(I16)

        def body(c2, acc):
            for x in (ref[2 * c2], ref[2 * c2 + 1]):
                hit = jnp.where((x > cand16) if strict else (x >= cand16), jnp.int16(1), jnp.int16(0))
                for s in range(DSA_CK // LANES):
                    acc = acc + hit[:, LANES * s:LANES * (s + 1)]
            return acc
        acc = lax.fori_loop(0, npair_chunks, body, jnp.zeros((Q_BLOCK, LANES), I16))
        return jnp.sum(acc.astype(F32), axis=-1, keepdims=True)

    def search16(ref, want):
        t = jnp.where(count16(ref, jnp.zeros((Q_BLOCK, 1), I32), False) >= want, 0, -HALF16).astype(I32)

        def bit(i, t):
            cand = t | (1 << (14 - i))
            return jnp.where(count16(ref, cand, False) >= want, cand, t)
        return lax.fori_loop(0, 15, bit, t)

    kf = float(topk)
    thr_hi = search16(hi_ref, kf)
    above = count16(hi_ref, thr_hi, True)
    thr_hi16 = thr_hi.astype(I16)

    def keep_candidates(c, carry):
        lo_ref[c] = jnp.where(hi_ref[c] == thr_hi16, lo_ref[c], jnp.int16(-HALF16))
        return carry

    lax.fori_loop(0, nfar + 1, keep_candidates, 0)
    thr_lo = search16(lo_ref, kf - above)
    thr = (thr_hi << 16) | ((thr_lo + HALF16) & 0xFFFF)
    need = kf - (above + count16(lo_ref, thr_lo, True))

    ti = lax.broadcasted_iota(I32, (DSA_CK, DSA_CK), 0)
    tj = lax.broadcasted_iota(I32, (DSA_CK, DSA_CK), 1)
    upto = (ti <= tj).astype(BF16)

    def mask_bias(kk, ties_before):
        nk = kk.shape[-1]
        tie = kk == thr
        rank = ties_before + jnp.dot(tie.astype(BF16), upto[0:nk, 0:nk], preferred_element_type=F32)
        sel = ((kk > thr) | (tie & (rank <= need))) & (kk != INT_MIN)
        return jnp.where(sel, 0.0, NEG_BIG), rank[:, nk - 1:nk]

    m_ref[...] = jnp.full_like(m_ref, NEG_BIG)
    l_ref[...] = jnp.zeros_like(l_ref)
    acc_ref[...] = jnp.zeros_like(acc_ref)

    def attend(rows, bias_of):
        nk = bias_of(0).shape[-1]
        for g in range(D_KV):
            gs = slice(D_DIM * g, D_DIM * (g + 1))
            s = lax.dot_general(qs_ref[g], k_ref[rows, gs], nt, preferred_element_type=F32)
            s = (s.reshape(rep, Q_BLOCK, nk) + bias_of(g)).reshape(rep * Q_BLOCK, nk)
            p = _online_softmax_step(s, m_ref.at[g], l_ref.at[g], acc_ref.at[g])
            acc_ref[g] += jnp.dot(p, v_ref[rows, gs], preferred_element_type=F32)

    def far_attend(c, ties_before):
        rows = pl.ds(pl.multiple_of(c * DSA_CK, DSA_CK), DSA_CK)
        mb, ties = mask_bias(keys_ref[c], ties_before)
        attend(rows, lambda g: mb[None])
        return ties

    ties = lax.fori_loop(0, nfar, far_attend, jnp.zeros((Q_BLOCK, 1), F32))
    mbn, _ = mask_bias(keys_ref[nfar][:, 0:2 * Q_BLOCK], ties)
    attend(near_rows, lambda g: mbn[None] + nbias_ref[0, rep * g:rep * (g + 1)])
    for hd in range(D_HEADS):
        rs = slice(Q_BLOCK * (hd % rep), Q_BLOCK * (hd % rep + 1))
        o_ref[:, D_DIM * hd:D_DIM * (hd + 1)] = acc_ref[hd // rep, rs, :] / l_ref[hd // rep, rs, :]


def _dsa(q, k, v, iq, ika, ikb, iw, near_bias, batch):
    m = q.shape[0]
    seq = m // batch
    nqb = seq // Q_BLOCK
    topk = min(D_TOPK_MAX, seq // 4)
    nck = max(seq // DSA_CK, 1)
    grp = D_HEADS // D_KV
    qspec = lambda w: pl.BlockSpec((Q_BLOCK, w), lambda b, i: (b * nqb + i, 0))
    kspec = lambda w: pl.BlockSpec((seq, w), lambda b, i: (b, 0))
    return pl.pallas_call(
        functools.partial(_dsa_kernel, topk=topk),
        grid=(batch, nqb),
        in_specs=[qspec(D_HEADS * D_DIM), qspec(D_HEADS * D_IDX_DIM), qspec(LANES),
                  kspec(D_KV * D_DIM), kspec(D_KV * D_DIM), kspec(LANES), kspec(LANES),
                  pl.BlockSpec((1, D_HEADS, Q_BLOCK, 2 * Q_BLOCK), lambda b, i: (jnp.minimum(i, 1), 0, 0, 0))],
        out_specs=qspec(D_HEADS * D_DIM),
        out_shape=jax.ShapeDtypeStruct((m, D_HEADS * D_DIM), F32),
        scratch_shapes=[pltpu.VMEM((nck + 1, Q_BLOCK, DSA_CK), I32),
                        pltpu.VMEM((nck + 2, Q_BLOCK, DSA_CK), I16), pltpu.VMEM((nck + 2, Q_BLOCK, DSA_CK), I16),
                        pltpu.VMEM((D_KV, grp * Q_BLOCK, LANES), F32), pltpu.VMEM((D_KV, grp * Q_BLOCK, LANES), F32),
                        pltpu.VMEM((D_KV, grp * Q_BLOCK, D_DIM), F32),
                        pltpu.VMEM((D_KV, grp * Q_BLOCK, D_DIM), BF16),
                        pltpu.VMEM((D_HEADS // 2 * Q_BLOCK, LANES), BF16)],
        compiler_params=_cparams(("parallel", "arbitrary")),
        name="dsa",
    )(q, iq, iw, k, v, ika, ikb, near_bias)


def _pad_cols(w, width):
    return jnp.pad(w, ((0, 0), (0, width - w.shape[1])))


def _pack_ab_in(w):
    a_cols = 3 * A_WIDTH + sum(A_LORA)
    wa, wb = w[:, :a_cols], w[:, a_cols:]
    rkv, lora = wa[:, :3 * A_WIDTH], wa[:, 3 * A_WIDTH:]
    z, xbc, dt = wb[:, :B_WIDTH], wb[:, B_WIDTH:B_WIDTH + B_CONV_CH], wb[:, B_WIDTH + B_CONV_CH:]
    return jnp.concatenate([rkv, z, xbc, _pad_cols(lora, A_LORA_PAD), _pad_cols(dt, LANES)], axis=1).astype(BF16)


def _pack_cd_in(w):
    c_cols = 2 * C_LORA + C_ROPE
    wc, wd = w[:, :c_cols], w[:, c_cols:]
    q_lat, kv_lat, k_pe = wc[:, :C_LORA], wc[:, C_LORA:2 * C_LORA], wc[:, 2 * C_LORA:]
    sizes = [D_HEADS * D_DIM, D_KV * D_DIM, D_KV * D_DIM, D_HEADS * D_IDX_DIM, D_IDX_DIM, D_HEADS]
    cuts = np.cumsum(sizes)[:-1]
    dq, dk, dv, iq, ik, iw = jnp.split(wd, [int(c) for c in cuts], axis=1)
    packed = jnp.concatenate([q_lat, kv_lat, dq, dk, dv, iq, k_pe, ik, iw], axis=1)
    return _pad_cols(packed, CD_COLS_PAD).astype(BF16)


def _pack_lora(w2, a2, g2):
    out, off = [], 0
    for w in (w2, a2, g2):
        out.append(jnp.pad(w, ((off, A_LORA_PAD - off - w.shape[0]), (0, 0))).astype(BF16))
        off += w.shape[0]
    return out


def _pack_mla_q(wq_b):
    w = wq_b.reshape(C_LORA, C_HEADS, C_QK)
    return jnp.concatenate([w[:, :, :C_NOPE].reshape(C_LORA, -1), w[:, :, C_NOPE:].reshape(C_LORA, -1)],
                           axis=1).astype(BF16)


def _pack_mla_kv(wkv_b):
    w = wkv_b.reshape(C_LORA, C_HEADS, C_NOPE + C_V)
    return jnp.concatenate([w[:, :, :C_NOPE].reshape(C_LORA, -1), w[:, :, C_NOPE:].reshape(C_LORA, -1)],
                           axis=1).astype(BF16)


def _layer0_mix(h, batch, norm_g, ab_w_in, ab_w_out, a_shift_mu, a_w0, a_w2, a_a0, a_a2, a_g2, a_k_k, a_k_a,
                a_r_k, a_ln_g, a_ln_b, b_conv_w, b_conv_b, b_dt_bias, b_a_log, b_d, b_norm_g):
    seq = h.shape[0] // batch
    u = _matmul([(h, 0, D_MODEL)], [_pack_ab_in(ab_w_in)], gain=norm_g)
    row = lambda t: t.reshape(1, -1)
    w2p, a2p, g2p = _pack_lora(a_w2, a_a2, a_g2)
    mu_rkv = row(a_shift_mu[:3 * A_WIDTH])
    mu_lora = _pad_cols(row(a_shift_mu[3 * A_WIDTH:]), A_LORA_PAD)
    r, w, k, v, na, nb, g = _rwkv_pre(u, seq, mu_rkv, mu_lora, row(a_w0), row(a_a0), row(a_k_k), row(a_k_a),
                                      w2p, a2p, g2p)
    y = _rwkv_scan(r, w, k, v, na, nb, batch)
    ya = _rwkv_post(y, r, k, v, g, row(a_ln_g), row(a_ln_b), row(a_r_k))
    yb = _ssd(u, batch, b_conv_w, b_conv_b, b_dt_bias, b_a_log, b_d, b_norm_g)
    w_out = ab_w_out.astype(BF16)
    return _matmul([(ya, 0, A_WIDTH), (yb, 0, B_WIDTH)], [w_out[:A_WIDTH], w_out[A_WIDTH:]], res=h)


def _layer1_mix(h, batch, norm_g, rel_bias, cd_w_in, cd_w_out, c_q_norm, c_wq_b, c_kv_norm, c_wkv_b,
                c_q_gain, c_k_gain, d_q_gain, d_k_gain, d_ik_gain):
    seq = h.shape[0] // batch
    u = _matmul([(h, 0, D_MODEL)], [_pack_cd_in(cd_w_in)], gain=norm_g)
    qn, qr, kn, kr, v = _mla_prep(u, seq, c_q_norm, _pack_mla_q(c_wq_b), c_kv_norm, _pack_mla_kv(c_wkv_b),
                                  c_q_gain, c_k_gain)
    yc = _mla_attn(qn, qr, kn, kr, v, batch)
    dq, dk, dv, iq, ika, ikb, iw = _dsa_prep(u, d_q_gain, d_k_gain, d_ik_gain)
    yd = _dsa(dq, dk, dv, iq, ika, ikb, iw, _near_bias(rel_bias), batch)
    w_out = cd_w_out.astype(BF16)
    half = C_HEADS * C_V
    return _matmul([(yc, 0, half), (yd, 0, D_HEADS * D_DIM)], [w_out[:half], w_out[half:]], res=h)


def _memory_attention(h, mem2, batch, gq, gkv, wq, wk, wv, wo, q_gain, k_gain):
    wkv = jnp.concatenate([wk, wv], axis=1).astype(BF16)
    mem_kv = _matmul([(mem2, 0, D_MODEL)], [wkv], gain=gkv)
    return _xattn(h, mem_kv, batch, gq, wq.astype(BF16), q_gain, k_gain, wo.astype(BF16))


def kernel(x, mem, rel_bias, norm_mix, norm_mem_q, norm_mem_kv, norm_ffn, xa_wq, xa_wk, xa_wv, xa_wo, xa_q_gain, xa_k_gain, ab_w_in, ab_w_out, a_shift_mu, a_w0, a_w2, a_a0, a_a2, a_g2, a_k_k, a_k_a, a_r_k, a_ln_g, a_ln_b, b_conv_w, b_conv_b, b_dt_bias, b_a_log, b_d, b_norm_g, ffn_w_gate, ffn_w_up, ffn_w_down, cd_w_in, cd_w_out, c_q_norm, c_wq_b, c_kv_norm, c_wkv_b, c_q_gain, c_k_gain, d_q_gain, d_k_gain, d_ik_gain, moe_router, moe_w_gate, moe_w_up, moe_w_down):
    batch, seq, d = x.shape
    h = x.reshape(batch * seq, d)
    mem2 = mem.reshape(-1, d)
    depth = norm_mix.shape[0]
    for layer in range(depth):
        i = layer // 2
        if layer % 2 == 0:
            h = _layer0_mix(h, batch, norm_mix[layer], ab_w_in[i], ab_w_out[i], a_shift_mu[i], a_w0[i], a_w2[i],
                            a_a0[i], a_a2[i], a_g2[i], a_k_k[i], a_k_a[i], a_r_k[i], a_ln_g[i], a_ln_b[i],
                            b_conv_w[i], b_conv_b[i], b_dt_bias[i], b_a_log[i], b_d[i], b_norm_g[i])
        else:
            h = _layer1_mix(h, batch, norm_mix[layer], rel_bias, cd_w_in[i], cd_w_out[i], c_q_norm[i], c_wq_b[i],
                            c_kv_norm[i], c_wkv_b[i], c_q_gain[i], c_k_gain[i], d_q_gain[i], d_k_gain[i],
                            d_ik_gain[i])
        h = _memory_attention(h, mem2, batch, norm_mem_q[layer], norm_mem_kv[layer], xa_wq[layer], xa_wk[layer],
                              xa_wv[layer], xa_wo[layer], xa_q_gain[layer], xa_k_gain[layer])
        if layer % 2 == 0:
            h = _swiglu(h, norm_ffn[layer], ffn_w_gate[i].astype(BF16), ffn_w_up[i].astype(BF16),
                        ffn_w_down[i].astype(BF16))
        else:
            router_p = _pad_cols(moe_router[i], LANES).astype(BF16)
            h = _moe(h, norm_ffn[layer], router_p, moe_w_gate[i].astype(BF16), moe_w_up[i].astype(BF16),
                     moe_w_down[i].astype(BF16))
    return h.reshape(batch, seq, d)
```

```python
import functools
import math

import numpy as np
import jax
import jax.numpy as jnp
from jax import lax
from jax.experimental import pallas as pl
from jax.experimental.pallas import tpu as pltpu

F32 = jnp.float32
BF16 = jnp.bfloat16
I32 = jnp.int32
I16 = jnp.int16
HALF16 = 1 << 15
HIGHEST = lax.Precision.HIGHEST

V7X_VMEM_BYTES = 64 * 1024 * 1024
VMEM_LIMIT = V7X_VMEM_BYTES - 8 * 1024 * 1024
LANES = 128

NORM_EPS = 1e-6
D_MODEL = 2048
HEAD64 = 64

A_WIDTH = 1024
A_LORA = (64, 64, 160)
A_LORA_PAD = 384
A_LN_EPS = 1e-5 * (HEAD64 / 8) ** 2
B_WIDTH = 1024
B_HEADS = 16
B_GROUPS = 4
B_STATE = 128
B_CONV = 4
B_CHUNK = 128
B_CONV_CH = B_WIDTH + 2 * B_GROUPS * B_STATE
AB_R, AB_K, AB_V, AB_Z, AB_XBC, AB_LORA, AB_DT, AB_COLS_PAD = 0, 1024, 2048, 3072, 4096, 6144, 6528, 6656

C_HEADS = 8
C_NOPE = 128
C_ROPE = 64
C_QK = C_NOPE + C_ROPE
C_V = 128
C_LORA = 512
ROPE_THETA = 10000.0
D_HEADS = 8
D_KV = 2
D_DIM = 128
D_IDX_DIM = 64
D_TOPK_MAX = 256
Q_BLOCK = 128
REL_BUCKETS = 32
REL_MAX_DIST = 128
CD_QLAT, CD_KVLAT, CD_DQ, CD_DK, CD_DV, CD_IQ, CD_PEIK, CD_IW, CD_COLS_PAD = (
    0, 512, 1024, 2048, 2304, 2560, 3072, 3200, 3584)

X_HEADS = 4
X_DIM = 128
N_EXPERTS = 8

NEG_BIG = -1e30
INT_MIN = -2 ** 31


def _cparams(sem):
    return pltpu.CompilerParams(dimension_semantics=sem, vmem_limit_bytes=VMEM_LIMIT)


def _rms(x, g, eps=NORM_EPS):
    return x * lax.rsqrt(jnp.mean(x * x, axis=-1, keepdims=True) + eps) * g


def _softplus(x):
    return jnp.maximum(x, 0.0) + jnp.log(1.0 + jnp.exp(-jnp.abs(x)))


def _silu(x):
    return x * jax.nn.sigmoid(x)


LOG2E = math.log2(math.e)


def _online_softmax_step(s, m_ref, l_ref, acc_ref):
    cols = [s[:, LANES * c:LANES * (c + 1)] for c in range(s.shape[1] // LANES)]
    mx = cols[0]
    for c in cols[1:]:
        mx = jnp.maximum(mx, c)
    m_old = m_ref[...]
    m_new = jnp.maximum(m_old, jnp.max(mx, axis=-1, keepdims=True))
    alpha = jnp.exp2(m_old - m_new)
    ps = [jnp.exp2(c - m_new) for c in cols]
    rs = ps[0]
    for p in ps[1:]:
        rs = rs + p
    l_ref[...] = alpha * l_ref[...] + jnp.sum(rs, axis=-1, keepdims=True)
    acc_ref[...] = alpha * acc_ref[...]
    m_ref[...] = m_new
    return jnp.concatenate(ps, axis=1).astype(BF16)


def _half_sum_bcast(x):
    left = lax.broadcasted_iota(I32, x.shape, 1) < HEAD64
    s0 = jnp.sum(jnp.where(left, x, 0.0), axis=1, keepdims=True)
    s1 = jnp.sum(jnp.where(left, 0.0, x), axis=1, keepdims=True)
    return jnp.where(left, s0, s1)


def _mm_kernel(*refs, n_x, has_norm, has_res):
    x_refs = refs[:n_x]
    pos = n_x
    g_ref = refs[pos] if has_norm else None
    pos += int(has_norm)
    w_refs = refs[pos:pos + n_x]
    pos += n_x
    res_ref = refs[pos] if has_res else None
    pos += int(has_res)
    o_ref = refs[pos]
    xn_refs = refs[pos + 1:]

    @pl.when(pl.program_id(1) == 0)
    def _():
        for x_ref, xn_ref in zip(x_refs, xn_refs):
            x = x_ref[...].astype(F32)
            if has_norm:
                x = _rms(x, g_ref[...])
            xn_ref[...] = x.astype(BF16)

    acc = None
    for xn_ref, w_ref in zip(xn_refs, w_refs):
        d = jnp.dot(xn_ref[...], w_ref[...], preferred_element_type=F32)
        acc = d if acc is None else acc + d
    if has_res:
        acc = acc + res_ref[...]
    o_ref[...] = acc


def _matmul(xs, ws, *, gain=None, res=None, tm=1024, tn=512):
    m = xs[0][0].shape[0]
    n = ws[0].shape[1]
    tm = min(tm, m)
    tn = min(tn, n)
    assert m % tm == 0 and n % tn == 0
    in_specs, args, scratch = [], [], []
    for arr, cb, width in xs:
        in_specs.append(pl.BlockSpec((tm, width), lambda i, j, cb=cb: (i, cb)))
        args.append(arr)
        scratch.append(pltpu.VMEM((tm, width), BF16))
    if gain is not None:
        in_specs.append(pl.BlockSpec((1, gain.shape[-1]), lambda i, j: (0, 0)))
        args.append(gain.reshape(1, -1))
    for (arr, cb, width), w in zip(xs, ws):
        assert w.shape[0] == width
        in_specs.append(pl.BlockSpec((width, tn), lambda i, j: (0, j)))
        args.append(w)
    if res is not None:
        in_specs.append(pl.BlockSpec((tm, tn), lambda i, j: (i, j)))
        args.append(res)
    return pl.pallas_call(
        functools.partial(_mm_kernel, n_x=len(xs), has_norm=gain is not None, has_res=res is not None),
        grid=(m // tm, n // tn),
        in_specs=in_specs,
        out_specs=pl.BlockSpec((tm, tn), lambda i, j: (i, j)),
        out_shape=jax.ShapeDtypeStruct((m, n), F32),
        scratch_shapes=scratch,
        compiler_params=_cparams(("parallel", "arbitrary")),
        name="matmul",
    )(*args)


def _rwkv_pre_kernel(rkv_ref, lora_ref, rkvp_ref, lorap_ref, mu_rkv_ref, mu_lora_ref,
                     w0_ref, a0_ref, kk_ref, ka_ref, w2_ref, a2_ref, g2_ref,
                     r_o, w_o, k_o, v_o, na_o, nb_o, g_o, *, tiles_per_seq):
    first = (pl.program_id(0) % tiles_per_seq) == 0
    tm = rkv_ref.shape[0]

    def shift_mix(x, prev_rows, mu):
        prev_last = jnp.where(first, 0.0, prev_rows[7:8, :])
        xs = pltpu.roll(x, 1, axis=0)
        row = lax.broadcasted_iota(I32, x.shape, 0)
        xs = jnp.where(row == 0, prev_last, xs)
        return x + (xs - x) * mu

    lo = shift_mix(lora_ref[...], lorap_ref[...], mu_lora_ref[...])
    lane = lax.broadcasted_iota(I32, lo.shape, 1)
    act = jnp.where(lane < A_LORA[0], jnp.tanh(lo),
                    jnp.where(lane < A_LORA[0] + A_LORA[1], lo, jax.nn.sigmoid(lo))).astype(BF16)
    dw = jnp.dot(act, w2_ref[...], preferred_element_type=F32)
    da = jnp.dot(act, a2_ref[...], preferred_element_type=F32)
    g_o[...] = jnp.dot(act, g2_ref[...], preferred_element_type=F32)

    for p in range(A_WIDTH // LANES):
        sl = slice(LANES * p, LANES * (p + 1))

        def mixed(off):
            s2 = slice(off + LANES * p, off + LANES * (p + 1))
            return shift_mix(rkv_ref[:, s2], rkvp_ref[:, s2], mu_rkv_ref[:, s2])

        r_o[:, sl] = mixed(AB_R)
        v_o[:, sl] = mixed(AB_V)
        kx = mixed(AB_K)
        logw = -_softplus(-(w0_ref[:, sl] + dw[:, sl])) - 0.5
        w_o[:, sl] = jnp.exp(-jnp.exp(logw))
        a = jax.nn.sigmoid(a0_ref[:, sl] + da[:, sl])
        kk = kx * kk_ref[:, sl]
        kk = kk * lax.rsqrt(jnp.maximum(_half_sum_bcast(kk * kk), 1e-24))
        k_o[:, sl] = kx * (1.0 + (a - 1.0) * ka_ref[:, sl])
        na_o[:, sl] = -kk
        nb_o[:, sl] = kk * a


def _rwkv_pre(u, seq, mu_rkv, mu_lora, w0, a0, k_k, k_a, w2p, a2p, g2p, tm=256):
    m = u.shape[0]
    tm = min(tm, seq)
    row = lambda w: pl.BlockSpec((1, w), lambda i: (0, 0))
    full = lambda a: pl.BlockSpec(a.shape, lambda i: (0, 0))
    prev = lambda i: jnp.maximum(i * (tm // 8) - 1, 0)
    out = jax.ShapeDtypeStruct((m, A_WIDTH), F32)
    return pl.pallas_call(
        functools.partial(_rwkv_pre_kernel, tiles_per_seq=seq // tm),
        grid=(m // tm,),
        in_specs=[
            pl.BlockSpec((tm, 3 * A_WIDTH), lambda i: (i, 0)),
            pl.BlockSpec((tm, A_LORA_PAD), lambda i: (i, AB_LORA // A_LORA_PAD)),
            pl.BlockSpec((8, 3 * A_WIDTH), lambda i: (prev(i), 0)),
            pl.BlockSpec((8, A_LORA_PAD), lambda i: (prev(i), AB_LORA // A_LORA_PAD)),
            row(3 * A_WIDTH), row(A_LORA_PAD), row(A_WIDTH), row(A_WIDTH), row(A_WIDTH), row(A_WIDTH),
            full(w2p), full(a2p), full(g2p),
        ],
        out_specs=[pl.BlockSpec((tm, A_WIDTH), lambda i: (i, 0))] * 7,
        out_shape=[out] * 7,
        compiler_params=_cparams(("parallel",)),
        name="rwkv_pre",
    )(u, u, u, u, mu_rkv, mu_lora, w0, a0, k_k, k_a, w2p, a2p, g2p)


RWKV_CHUNK = 64
RWKV_PAIR_GROUP = 8


def _rwkv_scan_kernel(r_ref, w_ref, k_ref, v_ref, a_ref, b_ref, y_ref, s_ref, vt_ref, yt_ref):
    nbatch = r_ref.shape[0]
    npairs = s_ref.shape[0]
    per_batch = npairs // nbatch
    where = lambda c: (c // per_batch, slice(LANES * (c % per_batch), LANES * (c % per_batch + 1)))

    @pl.when(pl.program_id(0) == 0)
    def _():
        s_ref[...] = jnp.zeros_like(s_ref)

    lane = lax.broadcasted_iota(I32, (HEAD64, LANES), 1)
    left = lane < HEAD64
    lane64 = lane & (HEAD64 - 1)

    def pair_transpose(x):
        xt = jnp.concatenate([x, x], axis=0).T
        return jnp.where(left, xt[0:HEAD64], xt[HEAD64:2 * HEAD64])

    def two_terms(x):
        hi = x.astype(BF16)
        return jnp.concatenate([hi, (x - hi.astype(F32)).astype(BF16)], axis=1)

    for p in range(npairs):
        bi, sl = where(p)
        vt_ref[p] = pair_transpose(v_ref[bi, :, sl])
    yt_ref[...] = jnp.zeros_like(yt_ref)

    ri = lax.broadcasted_iota(I32, (2 * LANES, LANES), 0)
    ci = lax.broadcasted_iota(I32, (2 * LANES, LANES), 1)
    ones_blk = (((ri // HEAD64) & 1) == (ci // HEAD64)).astype(BF16)

    def half_sum_mxu(x, split):
        if not split:
            return jnp.dot(x.astype(BF16), ones_blk[0:LANES], preferred_element_type=F32)
        return jnp.dot(two_terms(x), ones_blk, preferred_element_type=F32)

    def step8(t8, carry):
        rows8 = pl.ds(pl.multiple_of(t8 * 8, 8), 8)
        unroll = (LANES - t8 * 8) & (LANES - 1)
        for g0 in range(0, npairs, RWKV_PAIR_GROUP):
            group = range(g0, min(g0 + RWKV_PAIR_GROUP, npairs))
            rows = {p: [ref[where(p)[0], rows8, where(p)[1]] for ref in (a_ref, w_ref, b_ref, k_ref, r_ref)]
                    for p in group}
            s = {p: s_ref[p] for p in group}
            vt8 = {p: pltpu.roll(vt_ref[p], unroll, axis=1) for p in group}
            for j in range(8):
                sel = lane64 == t8 * 8 + j
                for p in group:
                    a_row, w_row, b_row, k_row, r_row = (x8[j:j + 1, :] for x8 in rows[p])
                    sa = half_sum_mxu(s[p] * a_row, True)
                    vc = jnp.take_along_axis(vt8[p], jnp.where(left, j, HEAD64 + j), axis=1)
                    s[p] = s[p] * w_row + sa * b_row + vc * k_row
                    yt_ref[p] = jnp.where(sel, half_sum_mxu(s[p] * r_row, False), yt_ref[p])
            for p in group:
                s_ref[p] = s[p]
        return carry

    lax.fori_loop(0, RWKV_CHUNK // 8, step8, 0)

    for p in range(npairs):
        bi, sl = where(p)
        y_ref[bi, :, sl] = pair_transpose(yt_ref[p])


def _rwkv_scan(r, w, k, v, na, nb, batch):
    m = r.shape[0]
    seq = m // batch
    npairs = batch * (A_WIDTH // LANES)
    spec = pl.BlockSpec((batch, RWKV_CHUNK, A_WIDTH), lambda c: (0, c, 0))
    pair_scratch = pltpu.VMEM((npairs, HEAD64, LANES), F32)
    by_batch = lambda t: t.reshape(batch, seq, A_WIDTH)
    y = pl.pallas_call(
        _rwkv_scan_kernel,
        grid=(seq // RWKV_CHUNK,),
        in_specs=[spec] * 6,
        out_specs=spec,
        out_shape=jax.ShapeDtypeStruct((batch, seq, A_WIDTH), F32),
        scratch_shapes=[pair_scratch, pair_scratch, pair_scratch],
        compiler_params=_cparams(("arbitrary",)),
        name="rwkv_scan",
    )(*(by_batch(t) for t in (r, w, k, v, na, nb)))
    return y.reshape(m, A_WIDTH)


def _rwkv_post_kernel(y_ref, r_ref, k_ref, v_ref, g_ref, lng_ref, lnb_ref, rk_ref, o_ref):
    for p in range(A_WIDTH // LANES):
        sl = slice(LANES * p, LANES * (p + 1))
        y = y_ref[:, sl]
        mean = _half_sum_bcast(y) * (1.0 / HEAD64)
        d = y - mean
        var = _half_sum_bcast(d * d) * (1.0 / HEAD64)
        yn = d * lax.rsqrt(var + A_LN_EPS) * lng_ref[:, sl] + lnb_ref[:, sl]
        bonus = _half_sum_bcast(r_ref[:, sl] * k_ref[:, sl] * rk_ref[:, sl]) * v_ref[:, sl]
        o_ref[:, sl] = (yn + bonus) * g_ref[:, sl]


def _rwkv_post(y, r, k, v, g, ln_g, ln_b, r_k, tm=256):
    m = y.shape[0]
    tm = min(tm, m)
    spec = pl.BlockSpec((tm, A_WIDTH), lambda i: (i, 0))
    row = pl.BlockSpec((1, A_WIDTH), lambda i: (0, 0))
    return pl.pallas_call(
        _rwkv_post_kernel,
        grid=(m // tm,),
        in_specs=[spec] * 5 + [row] * 3,
        out_specs=spec,
        out_shape=jax.ShapeDtypeStruct((m, A_WIDTH), F32),
        compiler_params=_cparams(("parallel",)),
        name="rwkv_post",
    )(y, r, k, v, g, ln_g, ln_b, r_k)


def _ssd_kernel(z_ref, xbc_ref, dt_ref, cw_ref, cb_ref, dtb_ref, alog_ref, dskip_ref, ng_ref,
                expand_ref, o_ref, st_ref, tail_ref):
    lc = B_CHUNK

    @pl.when(pl.program_id(1) == 0)
    def _():
        st_ref[...] = jnp.zeros_like(st_ref)
        tail_ref[...] = jnp.zeros_like(tail_ref)

    x = xbc_ref[...]
    tail = tail_ref[...]
    row8 = lax.broadcasted_iota(I32, tail.shape, 0)
    conv = cb_ref[...] + cw_ref[B_CONV - 1:B_CONV, :] * x
    for j in range(1, B_CONV):
        xs = pltpu.roll(x, j, axis=0)
        top = jnp.where(row8 < j, pltpu.roll(tail, j, axis=0), xs[0:8])
        xs = jnp.concatenate([top, xs[8:]], axis=0)
        conv = conv + cw_ref[B_CONV - 1 - j:B_CONV - j, :] * xs
    tail_ref[...] = x[lc - 8:lc]
    act = _silu(conv)
    xs_in = act[:, 0:B_WIDTH]
    bm = act[:, B_WIDTH:B_WIDTH + B_GROUPS * B_STATE].astype(BF16)
    cm = act[:, B_WIDTH + B_GROUPS * B_STATE:].astype(BF16)

    dt = _softplus(dt_ref[...] + dtb_ref[...])
    a_neg = -jnp.exp(alog_ref[...])
    da = dt * a_neg
    ri = lax.broadcasted_iota(I32, (lc, lc), 0)
    ci = lax.broadcasted_iota(I32, (lc, lc), 1)
    causal = ci <= ri
    tri = causal.astype(F32)
    cum = jnp.dot(tri, da, precision=HIGHEST, preferred_element_type=F32)
    cum_t = jnp.dot(da.T, (ri <= ci).astype(F32), precision=HIGHEST,
                    preferred_element_type=F32)
    expand = expand_ref[...]
    widen = lambda t: jnp.dot(t, expand, precision=HIGHEST, preferred_element_type=F32)
    dt_full = widen(dt)
    ecum_full = widen(jnp.exp(cum))
    dte_full = widen(jnp.exp(cum[lc - 1:lc, :] - cum))
    xdt = xs_in * dt_full
    xdt_b = xdt.astype(BF16)
    xdte_b = (xdt * dte_full).astype(BF16)
    left = lax.broadcasted_iota(I32, (lc, LANES), 1) < HEAD64

    ys = []
    for g in range(B_GROUPS):
        gs = slice(B_STATE * g, B_STATE * (g + 1))
        cm_g = cm[:, gs]
        bm_g = bm[:, gs]
        cb = lax.dot_general(cm_g, bm_g, (((1,), (1,)), ((), ())), preferred_element_type=F32)
        bm_t = bm_g.T
        pairs_per_group = B_HEADS // B_GROUPS // 2
        for q in range(pairs_per_group):
            p = g * pairs_per_group + q
            sl = slice(LANES * p, LANES * (p + 1))
            yd = []
            for h in (2 * p, 2 * p + 1):
                seg = cum[:, h:h + 1] - cum_t[h:h + 1, :]
                dec = jnp.where(causal, jnp.exp(jnp.minimum(seg, 0.0)), 0.0)
                yd.append(jnp.dot((cb * dec).astype(BF16), xdt_b[:, sl], preferred_element_type=F32))
            y_diag = jnp.where(left, yd[0], yd[1])
            st = st_ref[p]
            y_off = jnp.dot(cm_g, st.astype(BF16), preferred_element_type=F32) * ecum_full[:, sl]
            ys.append(y_diag + y_off)
            st_ref[p] = st * ecum_full[lc - 1:lc, sl] + jnp.dot(bm_t, xdte_b[:, sl],
                                                                preferred_element_type=F32)
    y = jnp.concatenate(ys, axis=1)
    y = (y + dskip_ref[...] * xs_in) * _silu(z_ref[...])
    gw = B_WIDTH // B_GROUPS
    for g in range(B_GROUPS):
        gs = slice(gw * g, gw * (g + 1))
        yg = y[:, gs]
        o_ref[:, gs] = yg * lax.rsqrt(jnp.mean(yg * yg, axis=-1, keepdims=True) + NORM_EPS) * ng_ref[:, gs]


def _ssd(u, batch, conv_w, conv_b, dt_bias, a_log, d_skip, norm_g):
    m = u.shape[0]
    seq = m // batch
    nchunk = seq // B_CHUNK
    rows = lambda b, c: b * nchunk + c
    pad16 = lambda t: jnp.pad(t.reshape(1, -1), ((0, 0), (0, LANES - B_HEADS)))
    expand = (np.arange(LANES)[:, None] == (np.arange(B_WIDTH)[None, :] // HEAD64)).astype(np.float32)
    full = lambda a: pl.BlockSpec(a.shape, lambda b, c: (0,) * a.ndim)
    args = [conv_w, conv_b.reshape(1, -1), pad16(dt_bias), pad16(a_log),
            jnp.repeat(d_skip, HEAD64).reshape(1, -1), norm_g.reshape(1, -1), jnp.asarray(expand)]
    return pl.pallas_call(
        _ssd_kernel,
        grid=(batch, nchunk),
        in_specs=[
            pl.BlockSpec((B_CHUNK, B_WIDTH), lambda b, c: (rows(b, c), AB_Z // B_WIDTH)),
            pl.BlockSpec((B_CHUNK, B_CONV_CH), lambda b, c: (rows(b, c), AB_XBC // B_CONV_CH)),
            pl.BlockSpec((B_CHUNK, LANES), lambda b, c: (rows(b, c), AB_DT // LANES)),
        ] + [full(a) for a in args],
        out_specs=pl.BlockSpec((B_CHUNK, B_WIDTH), lambda b, c: (rows(b, c), 0)),
        out_shape=jax.ShapeDtypeStruct((m, B_WIDTH), F32),
        scratch_shapes=[pltpu.VMEM((B_HEADS // 2, B_STATE, LANES), F32),
                        pltpu.VMEM((8, B_CONV_CH), F32)],
        compiler_params=_cparams(("parallel", "arbitrary")),
        name="ssd",
    )(u, u, u, *args)


def _xattn_kernel(h_ref, gq_ref, wq_ref, kv_ref, qg_ref, kg_ref, wo_ref, o_ref):
    h = h_ref[...]
    hn = _rms(h, gq_ref[...]).astype(BF16)
    q = jnp.dot(hn, wq_ref[...], preferred_element_type=F32)
    outs = []
    for hd in range(X_HEADS):
        sl = slice(X_DIM * hd, X_DIM * (hd + 1))
        qh = _rms(q[:, sl], qg_ref[...]).astype(BF16)
        kh = _rms(kv_ref[:, sl], kg_ref[...]).astype(BF16)
        vh = kv_ref[:, X_HEADS * X_DIM + X_DIM * hd:X_HEADS * X_DIM + X_DIM * (hd + 1)].astype(BF16)
        s = lax.dot_general(qh, kh, (((1,), (1,)), ((), ())), preferred_element_type=F32) * X_DIM ** -0.5
        e = jnp.exp(s - jnp.max(s, axis=-1, keepdims=True))
        p = e / jnp.sum(e, axis=-1, keepdims=True)
        outs.append(jnp.dot(p.astype(BF16), vh, preferred_element_type=F32))
    o = jnp.concatenate(outs, axis=1).astype(BF16)
    o_ref[...] = h + jnp.dot(o, wo_ref[...], preferred_element_type=F32)


def _xattn(h, mem_kv, batch, gq, wq, q_gain, k_gain, wo, tm=512):
    m, d = h.shape
    seq = m // batch
    tm = min(tm, seq)
    mlen = mem_kv.shape[0] // batch
    nt = seq // tm
    full = lambda a: pl.BlockSpec(a.shape, lambda b, i: (0, 0))
    args = [gq.reshape(1, -1), wq, mem_kv, q_gain.reshape(1, -1), k_gain.reshape(1, -1), wo]
    specs = [full(a) for a in args]
    specs[2] = pl.BlockSpec((mlen, mem_kv.shape[1]), lambda b, i: (b, 0))
    return pl.pallas_call(
        _xattn_kernel,
        grid=(batch, nt),
        in_specs=[pl.BlockSpec((tm, d), lambda b, i: (b * nt + i, 0))] + specs,
        out_specs=pl.BlockSpec((tm, d), lambda b, i: (b * nt + i, 0)),
        out_shape=jax.ShapeDtypeStruct((m, d), F32),
        compiler_params=_cparams(("parallel", "parallel")),
        name="xattn",
    )(h, *args)


def _swiglu_kernel(h_ref, g_ref, wg_ref, wu_ref, wd_ref, o_ref, xn_ref, acc_ref):
    j = pl.program_id(1)

    @pl.when(j == 0)
    def _():
        xn_ref[...] = _rms(h_ref[...], g_ref[...]).astype(BF16)
        acc_ref[...] = jnp.zeros_like(acc_ref)

    xn = xn_ref[...]
    gate = jnp.dot(xn, wg_ref[...], preferred_element_type=F32)
    up = jnp.dot(xn, wu_ref[...], preferred_element_type=F32)
    acc_ref[...] += jnp.dot((_silu(gate) * up).astype(BF16), wd_ref[...], preferred_element_type=F32)

    @pl.when(j == pl.num_programs(1) - 1)
    def _():
        o_ref[...] = h_ref[...] + acc_ref[...]


def _swiglu(h, gain, wg, wu, wd, tm=512, tf=512):
    m, d = h.shape
    f = wg.shape[1]
    tm = min(tm, m)
    assert f % tf == 0
    return pl.pallas_call(
        _swiglu_kernel,
        grid=(m // tm, f // tf),
        in_specs=[
            pl.BlockSpec((tm, d), lambda i, j: (i, 0)),
            pl.BlockSpec((1, d), lambda i, j: (0, 0)),
            pl.BlockSpec((d, tf), lambda i, j: (0, j)),
            pl.BlockSpec((d, tf), lambda i, j: (0, j)),
            pl.BlockSpec((tf, d), lambda i, j: (j, 0)),
        ],
        out_specs=pl.BlockSpec((tm, d), lambda i, j: (i, 0)),
        out_shape=jax.ShapeDtypeStruct((m, d), F32),
        scratch_shapes=[pltpu.VMEM((tm, d), BF16), pltpu.VMEM((tm, d), F32)],
        compiler_params=_cparams(("parallel", "arbitrary")),
        name="swiglu",
    )(h, gain.reshape(1, -1), wg, wu, wd)


MOE_TOPK = 2
MOE_ROWS = 512
MOE_TOKENS = 256
MOE_FF_TILE = 1024


def _moe_route_kernel(h_ref, g_ref, router_ref, eid_o, gw_o):
    xn = _rms(h_ref[...], g_ref[...]).astype(BF16)
    logits = jnp.dot(xn, router_ref[...], preferred_element_type=F32)
    lane = lax.broadcasted_iota(I32, logits.shape, 1)
    logits = jnp.where(lane < N_EXPERTS, logits, -jnp.inf)
    m1 = jnp.max(logits, axis=-1, keepdims=True)
    i1 = jnp.min(jnp.where(logits == m1, lane, LANES), axis=-1, keepdims=True)
    rest = jnp.where(lane == i1, -jnp.inf, logits)
    m2 = jnp.max(rest, axis=-1, keepdims=True)
    i2 = jnp.min(jnp.where(rest == m2, lane, LANES), axis=-1, keepdims=True)
    e2 = jnp.exp(m2 - m1)
    eid_o[...] = jnp.where(lane == 0, i1, jnp.where(lane == 1, i2, 0))
    gw_o[...] = jnp.where(lane == 0, 1.0 / (1.0 + e2), jnp.where(lane == 1, e2 / (1.0 + e2), 0.0))


def _moe_route(h, gain, router_p, tm=512):
    m, d = h.shape
    tm = min(tm, m)
    spec = pl.BlockSpec((tm, LANES), lambda i: (i, 0))
    return pl.pallas_call(
        _moe_route_kernel,
        grid=(m // tm,),
        in_specs=[pl.BlockSpec((tm, d), lambda i: (i, 0)), pl.BlockSpec((1, d), lambda i: (0, 0)),
                  pl.BlockSpec((d, LANES), lambda i: (0, 0))],
        out_specs=[spec, spec],
        out_shape=[jax.ShapeDtypeStruct((m, LANES), I32), jax.ShapeDtypeStruct((m, LANES), F32)],
        compiler_params=_cparams(("parallel",)),
        name="moe_route",
    )(h, gain.reshape(1, -1), router_p)


def _moe_plan(eid, nblocks):
    e = eid.reshape(-1)
    onehot = (e[:, None] == jnp.arange(N_EXPERTS, dtype=I32)[None, :]).astype(I32)
    csum = jnp.cumsum(onehot, axis=0)
    rank = jnp.sum(onehot * csum, axis=1) - 1
    counts = csum[-1]
    padded = (counts + MOE_ROWS - 1) // MOE_ROWS * MOE_ROWS
    gend = jnp.cumsum(padded)
    dest = jnp.sum(onehot * (gend - padded)[None, :], axis=1) + rank
    nb_used = gend[-1] // MOE_ROWS
    blk = jnp.arange(nblocks, dtype=I32)
    blk_e = jnp.minimum(jnp.sum((blk[:, None] * MOE_ROWS >= gend[None, :]).astype(I32), axis=1), N_EXPERTS - 1)
    blk_e = jnp.where(blk < nb_used, blk_e, blk_e[jnp.maximum(nb_used - 1, 0)])
    return dest.astype(I32), blk_e.astype(I32), nb_used.astype(I32).reshape(1)


def _row_copy(src_ref, src_row, dst_ref, dst_row, sem):
    return pltpu.make_async_copy(src_ref.at[pl.ds(src_row, 1), :], dst_ref.at[pl.ds(dst_row, 1), :], sem)


def _moe_dispatch_kernel(dest_ref, h_ref, xs_in_ref, xs_ref, sem):
    del xs_in_ref
    base = pl.program_id(0) * MOE_TOKENS

    def issue(r, carry):
        for c in range(MOE_TOPK):
            _row_copy(h_ref, r, xs_ref, dest_ref[MOE_TOPK * (base + r) + c], sem).start()
        return carry

    def drain(r, carry):
        for c in range(MOE_TOPK):
            _row_copy(h_ref, 0, xs_ref, 0, sem).wait()
        return carry

    lax.fori_loop(0, MOE_TOKENS, issue, 0)
    lax.fori_loop(0, MOE_TOKENS, drain, 0)


def _moe_dispatch(h, dest, rows):
    m, d = h.shape
    return pl.pallas_call(
        _moe_dispatch_kernel,
        grid_spec=pltpu.PrefetchScalarGridSpec(
            num_scalar_prefetch=1,
            grid=(m // MOE_TOKENS,),
            in_specs=[pl.BlockSpec((MOE_TOKENS, d), lambda i, dest: (i, 0)), pl.BlockSpec(memory_space=pl.ANY)],
            out_specs=pl.BlockSpec(memory_space=pl.ANY),
            scratch_shapes=[pltpu.SemaphoreType.DMA(())],
        ),
        out_shape=jax.ShapeDtypeStruct((rows, d), F32),
        input_output_aliases={2: 0},
        compiler_params=_cparams(("arbitrary",)),
        name="moe_dispatch",
    )(dest, h, jnp.zeros((rows, d), F32))


def _moe_ffn_kernel(be_ref, nb_ref, x_ref, g_ref, wg_ref, wu_ref, wd_ref, y_ref, xn_ref, acc_ref):
    del be_ref
    j = pl.program_id(1)
    live = pl.program_id(0) < nb_ref[0]
    last = j == pl.num_programs(1) - 1

    @pl.when(live & (j == 0))
    def _():
        xn_ref[...] = _rms(x_ref[...], g_ref[...]).astype(BF16)
        acc_ref[...] = jnp.zeros_like(acc_ref)

    @pl.when(live)
    def _():
        xn = xn_ref[...]
        gate = jnp.dot(xn, wg_ref[0], preferred_element_type=F32)
        up = jnp.dot(xn, wu_ref[0], preferred_element_type=F32)
        acc_ref[...] += jnp.dot((_silu(gate) * up).astype(BF16), wd_ref[0], preferred_element_type=F32)

    @pl.when(live & last)
    def _():
        y_ref[...] = acc_ref[...]

    @pl.when(jnp.logical_not(live) & last)
    def _():
        y_ref[...] = jnp.zeros_like(y_ref)


def _moe_ffn(xs, blk_e, nb_used, gain, wg, wu, wd, tf=MOE_FF_TILE):
    rows, d = xs.shape
    f = wg.shape[2]
    nj = f // tf
    assert f % tf == 0 and rows % MOE_ROWS == 0
    jx = lambda b, j, nb: jnp.where(b < nb[0], j, nj - 1)
    return pl.pallas_call(
        _moe_ffn_kernel,
        grid_spec=pltpu.PrefetchScalarGridSpec(
            num_scalar_prefetch=2,
            grid=(rows // MOE_ROWS, nj),
            in_specs=[
                pl.BlockSpec((MOE_ROWS, d), lambda b, j, be, nb: (jnp.minimum(b, jnp.maximum(nb[0] - 1, 0)), 0)),
                pl.BlockSpec((1, d), lambda b, j, be, nb: (0, 0)),
                pl.BlockSpec((1, d, tf), lambda b, j, be, nb: (be[b], 0, jx(b, j, nb))),
                pl.BlockSpec((1, d, tf), lambda b, j, be, nb: (be[b], 0, jx(b, j, nb))),
                pl.BlockSpec((1, tf, d), lambda b, j, be, nb: (be[b], jx(b, j, nb), 0)),
            ],
            out_specs=pl.BlockSpec((MOE_ROWS, d), lambda b, j, be, nb: (b, 0)),
            scratch_shapes=[pltpu.VMEM((MOE_ROWS, d), BF16), pltpu.VMEM((MOE_ROWS, d), F32)],
        ),
        out_shape=jax.ShapeDtypeStruct((rows, d), F32),
        compiler_params=_cparams(("parallel", "arbitrary")),
        name="moe_ffn",
    )(blk_e, nb_used, xs, gain.reshape(1, -1), wg, wu, wd)


def _moe_combine_kernel(dest_ref, h_ref, gw_ref, ys_ref, o_ref, ybuf_ref, sem):
    base = pl.program_id(0) * MOE_TOKENS

    def issue(r, carry):
        for c in range(MOE_TOPK):
            _row_copy(ys_ref, dest_ref[MOE_TOPK * (base + r) + c], ybuf_ref.at[c], r, sem).start()
        return carry

    def drain(r, carry):
        for c in range(MOE_TOPK):
            _row_copy(ys_ref, 0, ybuf_ref.at[c], 0, sem).wait()
        return carry

    lax.fori_loop(0, MOE_TOKENS, issue, 0)
    lax.fori_loop(0, MOE_TOKENS, drain, 0)
    gw = gw_ref[...]
    o_ref[...] = h_ref[...] + (gw[:, 0:1] * ybuf_ref[0] + gw[:, 1:2] * ybuf_ref[1])


def _moe_combine(h, gw, ys, dest):
    m, d = h.shape
    tok = lambda w: pl.BlockSpec((MOE_TOKENS, w), lambda i, dest: (i, 0))
    return pl.pallas_call(
        _moe_combine_kernel,
        grid_spec=pltpu.PrefetchScalarGridSpec(
            num_scalar_prefetch=1,
            grid=(m // MOE_TOKENS,),
            in_specs=[tok(d), tok(LANES), pl.BlockSpec(memory_space=pl.ANY)],
            out_specs=tok(d),
            scratch_shapes=[pltpu.VMEM((MOE_TOPK, MOE_TOKENS, d), F32), pltpu.SemaphoreType.DMA(())],
        ),
        out_shape=jax.ShapeDtypeStruct((m, d), F32),
        compiler_params=_cparams(("arbitrary",)),
        name="moe_combine",
    )(dest, h, gw, ys)


def _moe(h, gain, router_p, wg, wu, wd):
    m = h.shape[0]
    nblocks = MOE_TOPK * m // MOE_ROWS + N_EXPERTS
    eid, gw = _moe_route(h, gain, router_p)
    dest, blk_e, nb_used = _moe_plan(eid[:, :MOE_TOPK], nblocks)
    xs = _moe_dispatch(h, dest, nblocks * MOE_ROWS)
    ys = _moe_ffn(xs, blk_e, nb_used, gain, wg, wu, wd)
    return _moe_combine(h, gw, ys, dest)


def _rope_pairs(x, cos, sin_signed):
    w = x.shape[1]
    lane = lax.broadcasted_iota(I32, x.shape, 1)
    partner = jnp.where((lane & 32) != 0, pltpu.roll(x, 32, axis=1), pltpu.roll(x, w - 32, axis=1))
    return x * cos + partner * sin_signed


def _mla_prep_kernel(ql_ref, kvl_ref, pe_ref, qn_ref, wq_ref, kvn_ref, wkv_ref, qgn_ref, qgr_ref,
                     kgn_ref, kgr_ref, cos_ref, sin_ref, qn_o, qr_o, kn_o, kr_o, v_o):
    nn = C_HEADS * C_NOPE
    left = lax.broadcasted_iota(I32, (ql_ref.shape[0], LANES), 1) < C_ROPE

    def head_norm(nope_of, rope_of, gn_ref, gr_ref, n_o, r_o, post_scale):
        for p in range(C_HEADS // 2):
            pr = slice(LANES * p, LANES * (p + 1))
            rope = rope_of(p)
            r2 = rope * rope
            rs = []
            for hh, ss_rope in ((0, jnp.sum(jnp.where(left, r2, 0.0), axis=-1, keepdims=True)),
                                (1, jnp.sum(jnp.where(left, 0.0, r2), axis=-1, keepdims=True))):
                hs = slice(C_NOPE * (2 * p + hh), C_NOPE * (2 * p + hh + 1))
                nope = nope_of(hs)
                ss = jnp.sum(nope * nope, axis=-1, keepdims=True) + ss_rope
                rs.append(lax.rsqrt(ss * (1.0 / C_QK) + NORM_EPS))
                n_o[:, hs] = (nope * rs[hh] * gn_ref[:, hs] * post_scale).astype(BF16)
            rope = rope * jnp.where(left, rs[0], rs[1]) * gr_ref[:, pr]
            r_o[:, pr] = (_rope_pairs(rope, cos_ref[:, pr], sin_ref[:, pr]) * post_scale).astype(BF16)

    q = jnp.dot(_rms(ql_ref[...], qn_ref[...]).astype(BF16), wq_ref[...], preferred_element_type=F32)
    head_norm(lambda hs: q[:, hs], lambda p: q[:, nn + LANES * p:nn + LANES * (p + 1)],
              qgn_ref, qgr_ref, qn_o, qr_o, C_QK ** -0.5 * LOG2E)

    kv = jnp.dot(_rms(kvl_ref[...], kvn_ref[...]).astype(BF16), wkv_ref[...], preferred_element_type=F32)
    v_o[...] = kv[:, nn:].astype(BF16)
    pe = pe_ref[...]
    pe_pair = jnp.where(left, pe, pltpu.roll(pe, C_ROPE, axis=1))
    head_norm(lambda hs: kv[:, hs], lambda p: pe_pair, kgn_ref, kgr_ref, kn_o, kr_o, 1.0)


def _mla_prep(u, seq, q_norm, wq_p, kv_norm, wkv_p, q_gain, k_gain, tm=256):
    m = u.shape[0]
    tm = min(tm, seq)
    nn, nr = C_HEADS * C_NOPE, C_HEADS * C_ROPE
    half = C_ROPE // 2
    freqs = ROPE_THETA ** (-jnp.arange(half, dtype=F32) / half)
    ang = jnp.arange(seq, dtype=F32)[:, None] * freqs[None, :]
    cos = jnp.tile(jnp.cos(ang), (1, 2 * C_HEADS))
    sin = jnp.tile(jnp.concatenate([-jnp.sin(ang), jnp.sin(ang)], axis=1), (1, C_HEADS))
    tile_gain = lambda g: jnp.tile(g, C_HEADS).reshape(1, -1)
    consts = [q_norm.reshape(1, -1), wq_p, kv_norm.reshape(1, -1), wkv_p,
              tile_gain(q_gain[:C_NOPE]), tile_gain(q_gain[C_NOPE:]),
              tile_gain(k_gain[:C_NOPE]), tile_gain(k_gain[C_NOPE:])]
    full = lambda a: pl.BlockSpec(a.shape, lambda i: (0, 0))
    nt = seq // tm
    tab = pl.BlockSpec((tm, nr), lambda i: (i % nt, 0))
    out = lambda w: jax.ShapeDtypeStruct((m, w), BF16)
    ospec = lambda w: pl.BlockSpec((tm, w), lambda i: (i, 0))
    return pl.pallas_call(
        _mla_prep_kernel,
        grid=(m // tm,),
        in_specs=[
            pl.BlockSpec((tm, C_LORA), lambda i: (i, CD_QLAT // C_LORA)),
            pl.BlockSpec((tm, C_LORA), lambda i: (i, CD_KVLAT // C_LORA)),
            pl.BlockSpec((tm, LANES), lambda i: (i, CD_PEIK // LANES)),
        ] + [full(a) for a in consts] + [tab, tab],
        out_specs=[ospec(nn), ospec(nr), ospec(nn), ospec(nr), ospec(nn)],
        out_shape=[out(nn), out(nr), out(nn), out(nr), out(nn)],
        compiler_params=_cparams(("parallel",)),
        name="mla_prep",
    )(u, u, u, *consts, cos, sin)


def _mla_attn_kernel(qn_ref, qr_ref, kn_ref, kr_ref, v_ref, o_ref, m_ref, l_ref, acc_ref, *, tq):
    qi = pl.program_id(2)
    lane = lax.broadcasted_iota(I32, qr_ref.shape, 1)
    qr = qr_ref[...]
    qs = []
    for hh in range(2):
        qr_h = jnp.where((lane // HEAD64) == hh, qr, jnp.zeros_like(qr))
        qs.append(jnp.concatenate([qn_ref[:, C_NOPE * hh:C_NOPE * (hh + 1)], qr_h], axis=1))
    m_ref[...] = jnp.full_like(m_ref, NEG_BIG)
    l_ref[...] = jnp.zeros_like(l_ref)
    acc_ref[...] = jnp.zeros_like(acc_ref)
    nt = (((1,), (1,)), ((), ()))

    def update(j, masked):
        rows = pl.ds(pl.multiple_of(j * tq, tq), tq)
        kr = kr_ref[rows, :]
        ss = []
        for hh in range(2):
            kc = jnp.concatenate([kn_ref[rows, C_NOPE * hh:C_NOPE * (hh + 1)], kr], axis=1)
            ss.append(lax.dot_general(qs[hh], kc, nt, preferred_element_type=F32))
        for hh in range(2):
            s = ss[hh]
            if masked:
                ri = lax.broadcasted_iota(I32, s.shape, 0)
                ci = lax.broadcasted_iota(I32, s.shape, 1)
                s = jnp.where(ci <= ri, s, NEG_BIG)
            p = _online_softmax_step(s, m_ref.at[hh], l_ref.at[hh], acc_ref.at[hh])
            acc_ref[hh] += jnp.dot(p, v_ref[rows, C_V * hh:C_V * (hh + 1)], preferred_element_type=F32)

    def body(j, carry):
        update(j, False)
        return carry

    lax.fori_loop(0, qi, body, 0)
    update(qi, True)
    for hh in range(2):
        o_ref[:, C_V * hh:C_V * (hh + 1)] = acc_ref[hh] / l_ref[hh]


def _mla_attn(qn, qr, kn, kr, v, batch, tq=512):
    m = qn.shape[0]
    seq = m // batch
    tq = min(tq, seq)
    nq = seq // tq
    qspec = lambda w: pl.BlockSpec((tq, w), lambda b, p, i: (b * nq + i, p))
    kspec = lambda w: pl.BlockSpec((seq, w), lambda b, p, i: (b, p))
    return pl.pallas_call(
        functools.partial(_mla_attn_kernel, tq=tq),
        grid=(batch, C_HEADS // 2, nq),
        in_specs=[qspec(2 * C_NOPE), qspec(2 * C_ROPE), kspec(2 * C_NOPE), kspec(2 * C_ROPE), kspec(2 * C_V)],
        out_specs=qspec(2 * C_V),
        out_shape=jax.ShapeDtypeStruct((m, C_HEADS * C_V), F32),
        scratch_shapes=[pltpu.VMEM((2, tq, LANES), F32), pltpu.VMEM((2, tq, LANES), F32),
                        pltpu.VMEM((2, tq, C_V), F32)],
        compiler_params=_cparams(("parallel", "parallel", "arbitrary")),
        name="mla_attn",
    )(qn, qr, kn, kr, v)


def _rel_bucket_table(n):
    d = np.arange(n)
    max_exact = REL_BUCKETS // 2
    nf = np.maximum(d, 1).astype(np.float32)
    large = max_exact + (np.log(nf / np.float32(max_exact)) / np.float32(math.log(REL_MAX_DIST / max_exact))
                         * np.float32(REL_BUCKETS - max_exact)).astype(np.int32)
    large = np.minimum(large, REL_BUCKETS - 1)
    return np.where(d < max_exact, d, large).astype(np.int32)


def _near_bias_kernel(bucket_ref, rel_ref, o_ref):
    for var in range(2):
        bk = bucket_ref[var]
        for hd in range(D_HEADS):
            acc = jnp.zeros(bk.shape, F32)
            for b in range(REL_BUCKETS):
                acc = jnp.where(bk == b, rel_ref[b, hd], acc)
            o_ref[var, hd] = (acc - rel_ref[REL_BUCKETS - 1, hd]) * LOG2E


def _near_bias(rel_bias):
    table = _rel_bucket_table(2 * Q_BLOCK)
    q = np.arange(Q_BLOCK)[:, None]
    j = np.arange(2 * Q_BLOCK)[None, :]
    dist0 = np.maximum(q - j, 0)
    dist1 = np.maximum(q - j + Q_BLOCK, 0)
    buckets = np.stack([table[dist0], table[dist1]]).astype(np.int32)
    return pl.pallas_call(
        _near_bias_kernel,
        in_specs=[pl.BlockSpec(memory_space=pltpu.VMEM), pl.BlockSpec(memory_space=pltpu.SMEM)],
        out_specs=pl.BlockSpec(memory_space=pltpu.VMEM),
        out_shape=jax.ShapeDtypeStruct((2, D_HEADS, Q_BLOCK, 2 * Q_BLOCK), F32),
        name="near_bias",
    )(jnp.asarray(buckets), rel_bias)


def _dsa_prep_kernel(dq_ref, dk_ref, dv_ref, iq_ref, peik_ref, iw_ref, qg_ref, kg_ref, ikg_ref,
                     q_o, k_o, v_o, iq_o, ika_o, ikb_o, iw_o):
    for hd in range(D_HEADS):
        sl = slice(D_DIM * hd, D_DIM * (hd + 1))
        q_o[:, sl] = (_rms(dq_ref[:, sl], qg_ref[...]) * (D_DIM ** -0.5 * LOG2E)).astype(BF16)
    for hd in range(D_KV):
        sl = slice(D_DIM * hd, D_DIM * (hd + 1))
        k_o[:, sl] = _rms(dk_ref[:, sl], kg_ref[...]).astype(BF16)
    v_o[...] = dv_ref[...].astype(BF16)
    iq_o[...] = iq_ref[...].astype(BF16)
    x = peik_ref[...]
    right = lax.broadcasted_iota(I32, x.shape, 1) >= D_IDX_DIM
    x = jnp.where(right, x, 0.0)
    ms = jnp.sum(x * x, axis=-1, keepdims=True) * (1.0 / D_IDX_DIM)
    ik = (x * lax.rsqrt(ms + NORM_EPS) * ikg_ref[...]).astype(BF16)
    ikb_o[...] = ik
    ika_o[...] = pltpu.roll(ik.astype(F32), D_IDX_DIM, axis=1).astype(BF16)
    iw_o[...] = iw_ref[...] * D_HEADS ** -0.5


def _dsa_prep(u, q_gain, k_gain, ik_gain, tm=256):
    m = u.shape[0]
    tm = min(tm, m)
    blk = lambda w, off: pl.BlockSpec((tm, w), lambda i: (i, off // w))
    full = lambda a: pl.BlockSpec(a.shape, lambda i: (0, 0))
    ikg = jnp.concatenate([jnp.zeros((D_IDX_DIM,), F32), ik_gain]).reshape(1, -1)
    consts = [q_gain.reshape(1, -1), k_gain.reshape(1, -1), ikg]
    widths = [D_HEADS * D_DIM, D_KV * D_DIM, D_KV * D_DIM, D_HEADS * D_IDX_DIM, LANES, LANES, LANES]
    dtypes = [BF16] * 6 + [F32]
    return pl.pallas_call(
        _dsa_prep_kernel,
        grid=(m // tm,),
        in_specs=[blk(D_HEADS * D_DIM, CD_DQ), blk(D_KV * D_DIM, CD_DK), blk(D_KV * D_DIM, CD_DV),
                  blk(D_HEADS * D_IDX_DIM, CD_IQ), blk(LANES, CD_PEIK), blk(LANES, CD_IW)]
        + [full(a) for a in consts],
        out_specs=[pl.BlockSpec((tm, w), lambda i: (i, 0)) for w in widths],
        out_shape=[jax.ShapeDtypeStruct((m, w), dt) for w, dt in zip(widths, dtypes)],
        compiler_params=_cparams(("parallel",)),
        name="dsa_prep",
    )(u, u, u, u, u, u, *consts)


DSA_CK = 512


def _sort_key(score):
    bits = lax.bitcast_convert_type(score + 0.0, I32)
    return bits ^ ((bits >> 31) & 0x7FFFFFFF)


def _dsa_kernel(q_ref, iq_ref, iw_ref, k_ref, v_ref, ika_ref, ikb_ref, nbias_ref, o_ref,
                keys_ref, hi_ref, lo_ref, m_ref, l_ref, acc_ref, qs_ref, iqs_ref, *, topk):
    qb = pl.program_id(1)
    q0 = qb * Q_BLOCK
    far_end = jnp.maximum(q0 - Q_BLOCK, 0)
    nfar = (far_end + DSA_CK - 1) // DSA_CK
    near0 = pl.multiple_of(far_end, Q_BLOCK)
    nt = (((1,), (1,)), ((), ()))
    npair = D_HEADS // 2
    rep = D_HEADS // D_KV

    for p in range(npair):
        iqs_ref[Q_BLOCK * p:Q_BLOCK * (p + 1), :] = iq_ref[:, LANES * p:LANES * (p + 1)]
    for hd in range(D_HEADS):
        qs_ref[hd // rep, Q_BLOCK * (hd % rep):Q_BLOCK * (hd % rep + 1), :] = q_ref[:, D_DIM * hd:D_DIM * (hd + 1)]

    iw = iw_ref[...]
    iw_cols = [iw[:, hd:hd + 1] for hd in range(D_HEADS)]

    def index_scores(rows):
        iqs = iqs_ref[...]
        score = None
        for parity, k_ref_ in ((0, ika_ref), (1, ikb_ref)):
            d = lax.dot_general(iqs, k_ref_[rows, :], nt, preferred_element_type=F32) * D_IDX_DIM ** -0.5
            for p in range(npair):
                term = iw_cols[2 * p + parity] * jnp.maximum(d[Q_BLOCK * p:Q_BLOCK * (p + 1)], 0.0)
                score = term if score is None else score + term
        return score

    def store_keys(c, key):
        keys_ref[c] = key
        hi_ref[c] = (key >> 16).astype(I16)
        lo_ref[c] = ((key & 0xFFFF) - HALF16).astype(I16)

    def far_scores(c, carry):
        rows = pl.ds(pl.multiple_of(c * DSA_CK, DSA_CK), DSA_CK)
        key = _sort_key(index_scores(rows))
        pos = c * DSA_CK + lax.broadcasted_iota(I32, key.shape, 1)
        store_keys(c, jnp.where(pos < far_end, key, INT_MIN))
        return carry

    lax.fori_loop(0, nfar, far_scores, 0)
    near_rows = pl.ds(near0, 2 * Q_BLOCK)
    keyn = _sort_key(index_scores(near_rows))
    posn = near0 + lax.broadcasted_iota(I32, keyn.shape, 1)
    qpos = q0 + lax.broadcasted_iota(I32, keyn.shape, 0)
    keyn = jnp.where(posn <= qpos, keyn, INT_MIN)
    store_keys(nfar, jnp.concatenate([keyn, jnp.full((Q_BLOCK, DSA_CK - 2 * Q_BLOCK), INT_MIN, I32)], axis=1))
    lowest = jnp.full((Q_BLOCK, DSA_CK), -HALF16, I16)
    hi_ref[nfar + 1] = lowest
    lo_ref[nfar + 1] = lowest
    npair_chunks = (nfar + 2) // 2

    def count16(ref, cand, strict):
        cand16 = cand.astype(I16)

        def body(c2, acc):
            for x in (ref[2 * c2], ref[2 * c2 + 1]):
                hit = jnp.where((x > cand16) if strict else (x >= cand16), jnp.int16(1), jnp.int16(0))
                for s in range(DSA_CK // LANES):
                    acc = acc + hit[:, LANES * s:LANES * (s + 1)]
            return acc
        acc = lax.fori_loop(0, npair_chunks, body, jnp.zeros((Q_BLOCK, LANES), I16))
        return jnp.sum(acc.astype(F32), axis=-1, keepdims=True)

    def search16(ref, want):
        t = jnp.where(count16(ref, jnp.zeros((Q_BLOCK, 1), I32), False) >= want, 0, -HALF16).astype(I32)

        def bit(i, t):
            cand = t | (1 << (14 - i))
            return jnp.where(count16(ref, cand, False) >= want, cand, t)
        return lax.fori_loop(0, 15, bit, t)

    kf = float(topk)
    thr_hi = search16(hi_ref, kf)
    above = count16(hi_ref, thr_hi, True)
    thr_hi16 = thr_hi.astype(I16)

    def keep_candidates(c, carry):
        lo_ref[c] = jnp.where(hi_ref[c] == thr_hi16, lo_ref[c], jnp.int16(-HALF16))
        return carry

    lax.fori_loop(0, nfar + 1, keep_candidates, 0)
    thr_lo = search16(lo_ref, kf - above)
    thr = (thr_hi << 16) | ((thr_lo + HALF16) & 0xFFFF)
    need = kf - (above + count16(lo_ref, thr_lo, True))

    ti = lax.broadcasted_iota(I32, (DSA_CK, DSA_CK), 0)
    tj = lax.broadcasted_iota(I32, (DSA_CK, DSA_CK), 1)
    upto = (ti <= tj).astype(BF16)

    def mask_bias(kk, ties_before):
        nk = kk.shape[-1]
        tie = kk == thr
        rank = ties_before + jnp.dot(tie.astype(BF16), upto[0:nk, 0:nk], preferred_element_type=F32)
        sel = ((kk > thr) | (tie & (rank <= need))) & (kk != INT_MIN)
        return jnp.where(sel, 0.0, NEG_BIG), rank[:, nk - 1:nk]

    m_ref[...] = jnp.full_like(m_ref, NEG_BIG)
    l_ref[...] = jnp.zeros_like(l_ref)
    acc_ref[...] = jnp.zeros_like(acc_ref)

    def attend(rows, bias_of):
        nk = bias_of(0).shape[-1]
        for g in range(D_KV):
            gs = slice(D_DIM * g, D_DIM * (g + 1))
            s = lax.dot_general(qs_ref[g], k_ref[rows, gs], nt, preferred_element_type=F32)
            s = (s.reshape(rep, Q_BLOCK, nk) + bias_of(g)).reshape(rep * Q_BLOCK, nk)
            p = _online_softmax_step(s, m_ref.at[g], l_ref.at[g], acc_ref.at[g])
            acc_ref[g] += jnp.dot(p, v_ref[rows, gs], preferred_element_type=F32)

    def far_attend(c, ties_before):
        rows = pl.ds(pl.multiple_of(c * DSA_CK, DSA_CK), DSA_CK)
        mb, ties = mask_bias(keys_ref[c], ties_before)
        attend(rows, lambda g: mb[None])
        return ties

    ties = lax.fori_loop(0, nfar, far_attend, jnp.zeros((Q_BLOCK, 1), F32))
    mbn, _ = mask_bias(keys_ref[nfar][:, 0:2 * Q_BLOCK], ties)
    attend(near_rows, lambda g: mbn[None] + nbias_ref[0, rep * g:rep * (g + 1)])
    for hd in range(D_HEADS):
        rs = slice(Q_BLOCK * (hd % rep), Q_BLOCK * (hd % rep + 1))
        o_ref[:, D_DIM * hd:D_DIM * (hd + 1)] = acc_ref[hd // rep, rs, :] / l_ref[hd // rep, rs, :]


def _dsa(q, k, v, iq, ika, ikb, iw, near_bias, batch):
    m = q.shape[0]
    seq = m // batch
    nqb = seq // Q_BLOCK
    topk = min(D_TOPK_MAX, seq // 4)
    nck = max(seq // DSA_CK, 1)
    grp = D_HEADS // D_KV
    qspec = lambda w: pl.BlockSpec((Q_BLOCK, w), lambda b, i: (b * nqb + i, 0))
    kspec = lambda w: pl.BlockSpec((seq, w), lambda b, i: (b, 0))
    return pl.pallas_call(
        functools.partial(_dsa_kernel, topk=topk),
        grid=(batch, nqb),
        in_specs=[qspec(D_HEADS * D_DIM), qspec(D_HEADS * D_IDX_DIM), qspec(LANES),
                  kspec(D_KV * D_DIM), kspec(D_KV * D_DIM), kspec(LANES), kspec(LANES),
                  pl.BlockSpec((1, D_HEADS, Q_BLOCK, 2 * Q_BLOCK), lambda b, i: (jnp.minimum(i, 1), 0, 0, 0))],
        out_specs=qspec(D_HEADS * D_DIM),
        out_shape=jax.ShapeDtypeStruct((m, D_HEADS * D_DIM), F32),
        scratch_shapes=[pltpu.VMEM((nck + 1, Q_BLOCK, DSA_CK), I32),
                        pltpu.VMEM((nck + 2, Q_BLOCK, DSA_CK), I16), pltpu.VMEM((nck + 2, Q_BLOCK, DSA_CK), I16),
                        pltpu.VMEM((D_KV, grp * Q_BLOCK, LANES), F32), pltpu.VMEM((D_KV, grp * Q_BLOCK, LANES), F32),
                        pltpu.VMEM((D_KV, grp * Q_BLOCK, D_DIM), F32),
                        pltpu.VMEM((D_KV, grp * Q_BLOCK, D_DIM), BF16),
                        pltpu.VMEM((D_HEADS // 2 * Q_BLOCK, LANES), BF16)],
        compiler_params=_cparams(("parallel", "arbitrary")),
        name="dsa",
    )(q, iq, iw, k, v, ika, ikb, near_bias)


def _pad_cols(w, width):
    return jnp.pad(w, ((0, 0), (0, width - w.shape[1])))


def _pack_ab_in(w):
    a_cols = 3 * A_WIDTH + sum(A_LORA)
    wa, wb = w[:, :a_cols], w[:, a_cols:]
    rkv, lora = wa[:, :3 * A_WIDTH], wa[:, 3 * A_WIDTH:]
    z, xbc, dt = wb[:, :B_WIDTH], wb[:, B_WIDTH:B_WIDTH + B_CONV_CH], wb[:, B_WIDTH + B_CONV_CH:]
    return jnp.concatenate([rkv, z, xbc, _pad_cols(lora, A_LORA_PAD), _pad_cols(dt, LANES)], axis=1).astype(BF16)


def _pack_cd_in(w):
    c_cols = 2 * C_LORA + C_ROPE
    wc, wd = w[:, :c_cols], w[:, c_cols:]
    q_lat, kv_lat, k_pe = wc[:, :C_LORA], wc[:, C_LORA:2 * C_LORA], wc[:, 2 * C_LORA:]
    sizes = [D_HEADS * D_DIM, D_KV * D_DIM, D_KV * D_DIM, D_HEADS * D_IDX_DIM, D_IDX_DIM, D_HEADS]
    cuts = np.cumsum(sizes)[:-1]
    dq, dk, dv, iq, ik, iw = jnp.split(wd, [int(c) for c in cuts], axis=1)
    packed = jnp.concatenate([q_lat, kv_lat, dq, dk, dv, iq, k_pe, ik, iw], axis=1)
    return _pad_cols(packed, CD_COLS_PAD).astype(BF16)


def _pack_lora(w2, a2, g2):
    out, off = [], 0
    for w in (w2, a2, g2):
        out.append(jnp.pad(w, ((off, A_LORA_PAD - off - w.shape[0]), (0, 0))).astype(BF16))
        off += w.shape[0]
    return out


def _pack_mla_q(wq_b):
    w = wq_b.reshape(C_LORA, C_HEADS, C_QK)
    return jnp.concatenate([w[:, :, :C_NOPE].reshape(C_LORA, -1), w[:, :, C_NOPE:].reshape(C_LORA, -1)],
                           axis=1).astype(BF16)


def _pack_mla_kv(wkv_b):
    w = wkv_b.reshape(C_LORA, C_HEADS, C_NOPE + C_V)
    return jnp.concatenate([w[:, :, :C_NOPE].reshape(C_LORA, -1), w[:, :, C_NOPE:].reshape(C_LORA, -1)],
                           axis=1).astype(BF16)


def _layer0_mix(h, batch, norm_g, ab_w_in, ab_w_out, a_shift_mu, a_w0, a_w2, a_a0, a_a2, a_g2, a_k_k, a_k_a,
                a_r_k, a_ln_g, a_ln_b, b_conv_w, b_conv_b, b_dt_bias, b_a_log, b_d, b_norm_g):
    seq = h.shape[0] // batch
    u = _matmul([(h, 0, D_MODEL)], [_pack_ab_in(ab_w_in)], gain=norm_g)
    row = lambda t: t.reshape(1, -1)
    w2p, a2p, g2p = _pack_lora(a_w2, a_a2, a_g2)
    mu_rkv = row(a_shift_mu[:3 * A_WIDTH])
    mu_lora = _pad_cols(row(a_shift_mu[3 * A_WIDTH:]), A_LORA_PAD)
    r, w, k, v, na, nb, g = _rwkv_pre(u, seq, mu_rkv, mu_lora, row(a_w0), row(a_a0), row(a_k_k), row(a_k_a),
                                      w2p, a2p, g2p)
    y = _rwkv_scan(r, w, k, v, na, nb, batch)
    ya = _rwkv_post(y, r, k, v, g, row(a_ln_g), row(a_ln_b), row(a_r_k))
    yb = _ssd(u, batch, b_conv_w, b_conv_b, b_dt_bias, b_a_log, b_d, b_norm_g)
    w_out = ab_w_out.astype(BF16)
    return _matmul([(ya, 0, A_WIDTH), (yb, 0, B_WIDTH)], [w_out[:A_WIDTH], w_out[A_WIDTH:]], res=h)


def _layer1_mix(h, batch, norm_g, rel_bias, cd_w_in, cd_w_out, c_q_norm, c_wq_b, c_kv_norm, c_wkv_b,
                c_q_gain, c_k_gain, d_q_gain, d_k_gain, d_ik_gain):
    seq = h.shape[0] // batch
    u = _matmul([(h, 0, D_MODEL)], [_pack_cd_in(cd_w_in)], gain=norm_g)
    qn, qr, kn, kr, v = _mla_prep(u, seq, c_q_norm, _pack_mla_q(c_wq_b), c_kv_norm, _pack_mla_kv(c_wkv_b),
                                  c_q_gain, c_k_gain)
    yc = _mla_attn(qn, qr, kn, kr, v, batch)
    dq, dk, dv, iq, ika, ikb, iw = _dsa_prep(u, d_q_gain, d_k_gain, d_ik_gain)
    yd = _dsa(dq, dk, dv, iq, ika, ikb, iw, _near_bias(rel_bias), batch)
    w_out = cd_w_out.astype(BF16)
    half = C_HEADS * C_V
    return _matmul([(yc, 0, half), (yd, 0, D_HEADS * D_DIM)], [w_out[:half], w_out[half:]], res=h)


def _memory_attention(h, mem2, batch, gq, gkv, wq, wk, wv, wo, q_gain, k_gain):
    wkv = jnp.concatenate([wk, wv], axis=1).astype(BF16)
    mem_kv = _matmul([(mem2, 0, D_MODEL)], [wkv], gain=gkv)
    return _xattn(h, mem_kv, batch, gq, wq.astype(BF16), q_gain, k_gain, wo.astype(BF16))


def kernel(x, mem, rel_bias, norm_mix, norm_mem_q, norm_mem_kv, norm_ffn, xa_wq, xa_wk, xa_wv, xa_wo, xa_q_gain, xa_k_gain, ab_w_in, ab_w_out, a_shift_mu, a_w0, a_w2, a_a0, a_a2, a_g2, a_k_k, a_k_a, a_r_k, a_ln_g, a_ln_b, b_conv_w, b_conv_b, b_dt_bias, b_a_log, b_d, b_norm_g, ffn_w_gate, ffn_w_up, ffn_w_down, cd_w_in, cd_w_out, c_q_norm, c_wq_b, c_kv_norm, c_wkv_b, c_q_gain, c_k_gain, d_q_gain, d_k_gain, d_ik_gain, moe_router, moe_w_gate, moe_w_up, moe_w_down):
    batch, seq, d = x.shape
    h = x.reshape(batch * seq, d)
    mem2 = mem.reshape(-1, d)
    depth = norm_mix.shape[0]
    for layer in range(depth):
        i = layer // 2
        if layer % 2 == 0:
            h = _layer0_mix(h, batch, norm_mix[layer], ab_w_in[i], ab_w_out[i], a_shift_mu[i], a_w0[i], a_w2[i],
                            a_a0[i], a_a2[i], a_g2[i], a_k_k[i], a_k_a[i], a_r_k[i], a_ln_g[i], a_ln_b[i],
                            b_conv_w[i], b_conv_b[i], b_dt_bias[i], b_a_log[i], b_d[i], b_norm_g[i])
        else:
            h = _layer1_mix(h, batch, norm_mix[layer], rel_bias, cd_w_in[i], cd_w_out[i], c_q_norm[i], c_wq_b[i],
                            c_kv_norm[i], c_wkv_b[i], c_q_gain[i], c_k_gain[i], d_q_gain[i], d_k_gain[i],
                            d_ik_gain[i])
        h = _memory_attention(h, mem2, batch, norm_mem_q[layer], norm_mem_kv[layer], xa_wq[layer], xa_wk[layer],
                              xa_wv[layer], xa_wo[layer], xa_q_gain[layer], xa_k_gain[layer])
        if layer % 2 == 0:
            h = _swiglu(h, norm_ffn[layer], ffn_w_gate[i].astype(BF16), ffn_w_up[i].astype(BF16),
                        ffn_w_down[i].astype(BF16))
        else:
            router_p = _pad_cols(moe_router[i], LANES).astype(BF16)
            pad = -moe_w_gate.shape[-1] % MOE_FF_TILE
            wide = lambda w: jnp.pad(w, ((0, 0), (0, 0), (0, pad))).astype(BF16)
            h = _moe(h, norm_ffn[layer], router_p, wide(moe_w_gate[i]), wide(moe_w_up[i]),
                     jnp.pad(moe_w_down[i], ((0, 0), (0, pad), (0, 0))).astype(BF16))
    return h.reshape(batch, seq, d)
```
